```python
import math
import jax
import jax.numpy as jnp
from jax import lax
import numpy as np

D_MODEL = 1024
BATCH = 8
SEQ = 2048
DEPTH = 2
DEC_BATCH = 128
DEC_SEQ = 4
PAST_LEN = 16384
PAGE_SIZE = 128

D_MIX = D_MODEL
SSD_HEADS = 8
SSD_WIDTH = D_MIX // 2
SSD_HEAD_DIM = SSD_WIDTH // SSD_HEADS
SSD_GROUPS = 2
SSD_STATE = 64
SSD_CONV = 4
SSD_XBC = SSD_WIDTH + 2 * SSD_GROUPS * SSD_STATE
RET_HEADS = 4
RET_WIDTH = D_MIX // 4
RET_HEAD_DIM = RET_WIDTH // RET_HEADS
ML_HEADS = 4
ML_WIDTH = D_MIX - SSD_WIDTH - RET_WIDTH
ML_HEAD_DIM = ML_WIDTH // ML_HEADS
D_IN_PROJ = SSD_WIDTH + SSD_XBC + SSD_HEADS + 4 * RET_WIDTH + 4 * ML_WIDTH + 2 * ML_HEADS
CHUNK = 128
ROPE_BASE = 10000.0
EPS = 1e-6
EXPERT_GROUPS = 4
EXPERTS_PER_GROUP = 8
N_EXPERTS = EXPERT_GROUPS * EXPERTS_PER_GROUP
TOP_K_INNER = 2
EXPERT_FF = D_MODEL // 4

kernel_name = 'hymba_ssd_retention_mlstm_hmoe_adaln_step'

F32 = jnp.float32


def grouped_rms_norm(x, g, n_groups):
    xf = x.astype(F32)
    xg = xf.reshape(x.shape[:-1] + (n_groups, x.shape[-1] // n_groups))
    yg = xg * lax.rsqrt(jnp.mean(xg * xg, axis=-1, keepdims=True) + EPS)
    return (yg.reshape(x.shape) * g.astype(F32)).astype(x.dtype)


def rms_norm(x, g):
    return grouped_rms_norm(x, g, 1)


def rope(x, pos):
    half = x.shape[-1] // 2
    inv = ROPE_BASE ** (-jnp.arange(half, dtype=F32) / half)
    ang = pos.astype(F32)[:, None] * inv[None, :]
    cos = jnp.cos(ang)[:, None, :]
    sin = jnp.sin(ang)[:, None, :]
    x1 = x[..., :half].astype(F32)
    x2 = x[..., half:].astype(F32)
    return jnp.concatenate([x1 * cos - x2 * sin, x1 * sin + x2 * cos], axis=-1).astype(x.dtype)


def to_chunks(a, L):
    B, T = a.shape[0], a.shape[1]
    return a.reshape((B, T // L, L) + a.shape[2:]).swapaxes(0, 1)


def from_chunks(a):
    nC, B, L = a.shape[0], a.shape[1], a.shape[2]
    return a.swapaxes(0, 1).reshape((B, nC * L) + a.shape[3:])


def decayed_linear_scan(q, k, v, log_a, s0):
    L = math.gcd(q.shape[1], CHUNK)
    mask = jnp.tril(jnp.ones((L, L), dtype=bool))

    def step(s, inp):
        qc, kc, vc, lac = inp
        dt = qc.dtype
        a = jnp.cumsum(lac.astype(F32), axis=1)
        a_h = a.swapaxes(1, 2)
        diff = a_h[..., :, None] - a_h[..., None, :]
        decay = jnp.exp(jnp.where(mask, diff, -jnp.inf)).astype(dt)
        scores = jnp.einsum('bihk,bjhk->bhij', qc, kc) * decay
        o = jnp.einsum('bhij,bjhv->bihv', scores, vc)
        o = o + jnp.einsum('bihk,bhkv->bihv', qc, s) * jnp.exp(a).astype(dt)[..., None]
        a_last = a[:, -1]
        w = jnp.exp(a_last[:, None, :] - a).astype(dt)
        s_new = s * jnp.exp(a_last).astype(s.dtype)[..., None, None] + jnp.einsum('blhk,blhv->bhkv', kc * w[..., None], vc)
        return s_new.astype(s.dtype), o.astype(dt)

    s_fin, o = lax.scan(step, s0, (to_chunks(q, L), to_chunks(k, L), to_chunks(v, L), to_chunks(log_a, L)))
    return from_chunks(o), s_fin


def mlstm_scan(q, k, v, i_pre, log_f, c0, n0, m0):
    L = math.gcd(q.shape[1], CHUNK)
    mask = jnp.tril(jnp.ones((L, L), dtype=bool))

    def step(carry, inp):
        c, n, m = carry
        qc, kc, vc, ic, fc = inp
        dt = qc.dtype
        b = jnp.cumsum(fc.astype(F32), axis=1)
        ic32 = ic.astype(F32)
        m_prev = m.astype(F32)
        m_t = b + jnp.maximum(m_prev[:, None, :], lax.cummax(ic32 - b, axis=1))
        inter = jnp.exp(b + m_prev[:, None, :] - m_t)
        bh = b.swapaxes(1, 2)
        ih = ic32.swapaxes(1, 2)
        mh = m_t.swapaxes(1, 2)
        logd = ih[..., None, :] + bh[..., :, None] - bh[..., None, :] - mh[..., :, None]
        dmat = jnp.exp(jnp.where(mask, logd, -jnp.inf)).astype(dt)
        s = jnp.einsum('bihk,bjhk->bhij', qc, kc) * dmat
        inter_d = inter.astype(dt)
        num = jnp.einsum('bhij,bjhv->bihv', s, vc) + jnp.einsum('bihk,bhkv->bihv', qc, c) * inter_d[..., None]
        den = s.sum(-1).swapaxes(1, 2) + jnp.einsum('bihk,bhk->bih', qc, n) * inter_d
        denom = jnp.maximum(jnp.abs(den.astype(F32)), jnp.exp(-m_t))
        h = (num.astype(F32) / denom[..., None]).astype(dt)
        m_last = m_t[:, -1]
        w_end = jnp.exp(ic32 + b[:, -1:, :] - b - m_last[:, None, :]).astype(dt)
        dec = jnp.exp(b[:, -1] + m_prev - m_last)
        c_new = c * dec.astype(c.dtype)[..., None, None] + jnp.einsum('blhk,blhv->bhkv', kc * w_end[..., None], vc)
        n_new = n * dec.astype(n.dtype)[..., None] + jnp.einsum('blhk,blh->bhk', kc, w_end)
        return (c_new.astype(c.dtype), n_new.astype(n.dtype), m_last.astype(m.dtype)), h

    (c_f, n_f, m_f), h = lax.scan(step, (c0, n0, m0), (to_chunks(q, L), to_chunks(k, L), to_chunks(v, L), to_chunks(i_pre, L), to_chunks(log_f, L)))
    return from_chunks(h), c_f, n_f, m_f


def causal_dwconv(u, buf, w, b):
    T = u.shape[1]
    full = jnp.concatenate([buf.astype(u.dtype), u], axis=1)
    out = b
    for tap in range(SSD_CONV):
        out = out + full[:, tap:tap + T] * w[tap]
    return out, full[:, T:]


def mixer_layer(h, pos, st, P, l):
    ssd_s, conv_buf, ret_s, ml_c, ml_n, ml_m = st
    B, T, _ = h.shape
    sizes = [SSD_WIDTH, SSD_XBC, SSD_HEADS, RET_WIDTH, RET_WIDTH, RET_WIDTH, RET_WIDTH,
             ML_WIDTH, ML_WIDTH, ML_WIDTH, ML_WIDTH, ML_HEADS, ML_HEADS]
    offsets = np.cumsum(sizes)[:-1].tolist()
    proj = h @ P['w_in'][l]
    z, xbc, dt_pre, rq, rk, rv, rg, mq, mk, mv, mo, mi, mf = jnp.split(proj, offsets, axis=-1)

    xbc, conv_new = causal_dwconv(xbc, conv_buf, P['conv_w'][l], P['conv_b'][l])
    xbc = jax.nn.silu(xbc)
    xs, Bm, Cm = jnp.split(xbc, [SSD_WIDTH, SSD_WIDTH + SSD_GROUPS * SSD_STATE], axis=-1)
    xs = xs.reshape(B, T, SSD_HEADS, SSD_HEAD_DIM)
    rep = SSD_HEADS // SSD_GROUPS
    Bm = jnp.repeat(Bm.reshape(B, T, SSD_GROUPS, SSD_STATE), rep, axis=2)
    Cm = jnp.repeat(Cm.reshape(B, T, SSD_GROUPS, SSD_STATE), rep, axis=2)
    dt = jax.nn.softplus(dt_pre + P['dt_bias'][l])
    A = -jnp.exp(P['a_log'][l])
    y, ssd_new = decayed_linear_scan(Cm, Bm, xs * dt[..., None], dt * A, ssd_s)
    y = y + P['d_skip'][l][:, None] * xs
    y = y.reshape(B, T, SSD_WIDTH) * jax.nn.silu(z)
    y_ssd = grouped_rms_norm(y, P['g_ssd_norm'][l], SSD_GROUPS)

    q = rope(rq.reshape(B, T, RET_HEADS, RET_HEAD_DIM), pos)
    k = rope(rk.reshape(B, T, RET_HEADS, RET_HEAD_DIM), pos) * (RET_HEAD_DIM ** -0.5)
    v = rv.reshape(B, T, RET_HEADS, RET_HEAD_DIM)
    log_gamma = jnp.log(1.0 - 2.0 ** (-5.0 - jnp.arange(RET_HEADS, dtype=F32)))
    la = jnp.broadcast_to(log_gamma, (B, T, RET_HEADS)).astype(h.dtype)
    o, ret_new = decayed_linear_scan(q, k, v, la, ret_s)
    y_ret = grouped_rms_norm(o.reshape(B, T, RET_WIDTH), P['g_ret_norm'][l], RET_HEADS) * jax.nn.silu(rg)

    q = mq.reshape(B, T, ML_HEADS, ML_HEAD_DIM)
    k = mk.reshape(B, T, ML_HEADS, ML_HEAD_DIM) * (ML_HEAD_DIM ** -0.5)
    v = mv.reshape(B, T, ML_HEADS, ML_HEAD_DIM)
    i_pre = mi + P['b_mlstm_i'][l]
    log_f = jax.nn.log_sigmoid((mf + P['b_mlstm_f'][l]).astype(F32))
    hh, c_new, n_new, m_new = mlstm_scan(q, k, v, i_pre, log_f, ml_c, ml_n, ml_m)
    y_ml = grouped_rms_norm(hh.reshape(B, T, ML_WIDTH), P['g_mlstm_norm'][l], ML_HEADS) * jax.nn.sigmoid(mo)

    out = jnp.concatenate([y_ssd, y_ret, y_ml], axis=-1) @ P['w_out'][l]
    return out, (ssd_new, conv_new, ret_new, c_new, n_new, m_new)


def hier_moe(h, P, l):
    B, T, D = h.shape
    hf = h.reshape(B * T, D)
    g_logits = (hf @ P['w_router_group'][l] + P['b_router_group'][l]).astype(F32)
    g_prob = jax.nn.softmax(g_logits, axis=-1)
    g_w, g_idx = lax.top_k(g_prob, 1)
    e_logits = (hf @ P['w_router_expert'][l] + P['b_router_expert'][l]).astype(F32)
    e_logits = e_logits.reshape(-1, EXPERT_GROUPS, EXPERTS_PER_GROUP)
    e_logits = jnp.take_along_axis(e_logits, g_idx[:, :, None], axis=1)[:, 0]
    e_prob = jax.nn.softmax(e_logits, axis=-1)
    top_p, top_i = lax.top_k(e_prob, TOP_K_INNER)
    top_p = top_p / jnp.sum(top_p, axis=-1, keepdims=True)
    expert_id = g_idx * EXPERTS_PER_GROUP + top_i
    weights = g_w * top_p
    gates = jnp.sum(jax.nn.one_hot(expert_id, N_EXPERTS, dtype=F32) * weights[..., None], axis=1).astype(h.dtype)
    out = jnp.zeros_like(hf)
    for e in range(N_EXPERTS):
        a, u = jnp.split(hf @ P['w_gate_up'][l, e], 2, axis=-1)
        out = out + gates[:, e:e + 1] * ((jax.nn.silu(a) * u) @ P['w_down'][l, e])
    return out.reshape(B, T, D)


def trunk(x, c, pos, init_states, P):
    new = []
    for l in range(DEPTH):
        mod = jax.nn.silu(c) @ P['w_ada'][l] + P['b_ada'][l]
        sh1, sc1, gt1, sh2, sc2, gt2 = [m[:, None, :] for m in jnp.split(mod, 6, axis=-1)]
        h = rms_norm(x, P['g_norm1'][l]) * (1.0 + sc1) + sh1
        mix, st = mixer_layer(h, pos, init_states[l], P, l)
        x = x + gt1 * mix
        h = rms_norm(x, P['g_norm2'][l]) * (1.0 + sc2) + sh2
        x = x + gt2 * hier_moe(h, P, l)
        new.append(st)
    y = rms_norm(x, P['g_final'])
    stacked = [jnp.stack([s[i] for s in new], axis=0) for i in range(6)]
    return y, stacked


def setup_inputs(seed: int = 0) -> dict:
    key = jax.random.key(seed)
    ks = iter(jax.random.split(key, 48))

    def nrm(shape, s):
        return jax.random.normal(next(ks), shape, F32) * s

    dt0 = jax.random.uniform(next(ks), (DEPTH, SSD_HEADS), F32, 0.001, 0.1)
    return {
        'x_prompt': nrm((BATCH, SEQ, D_MODEL), 1.0),
        'x_sample': nrm((DEC_BATCH, DEC_SEQ, D_MODEL), 1.0),
        'state_ssd': nrm((DEPTH, DEC_BATCH, SSD_HEADS, SSD_STATE, SSD_HEAD_DIM), 0.5),
        'state_ssd_conv': nrm((DEPTH, DEC_BATCH, SSD_CONV - 1, SSD_XBC), 1.0),
        'state_ret': nrm((DEPTH, DEC_BATCH, RET_HEADS, RET_HEAD_DIM, RET_HEAD_DIM), 1.0),
        'state_mlstm_c': nrm((DEPTH, DEC_BATCH, ML_HEADS, ML_HEAD_DIM, ML_HEAD_DIM), 0.5),
        'state_mlstm_n': nrm((DEPTH, DEC_BATCH, ML_HEADS, ML_HEAD_DIM), 0.5),
        'state_mlstm_m': nrm((DEPTH, DEC_BATCH, ML_HEADS), 1.0),
        'c_prompt': nrm((BATCH, D_MODEL), 1.0),
        'c_sample': nrm((DEC_BATCH, D_MODEL), 1.0),
        'w_ada': nrm((DEPTH, D_MODEL, 6 * D_MODEL), 0.5 * D_MODEL ** -0.5),
        'b_ada': nrm((DEPTH, 6 * D_MODEL), 0.02),
        'g_norm1': 1.0 + nrm((DEPTH, D_MODEL), 0.01),
        'g_norm2': 1.0 + nrm((DEPTH, D_MODEL), 0.01),
        'w_in': nrm((DEPTH, D_MODEL, D_IN_PROJ), D_MODEL ** -0.5),
        'conv_w': nrm((DEPTH, SSD_CONV, SSD_XBC), SSD_CONV ** -0.5),
        'conv_b': nrm((DEPTH, SSD_XBC), 0.02),
        'dt_bias': jnp.log(jnp.expm1(dt0)),
        'a_log': jnp.log(jax.random.uniform(next(ks), (DEPTH, SSD_HEADS), F32, 1.0, 16.0)),
        'd_skip': 1.0 + nrm((DEPTH, SSD_HEADS), 0.1),
        'g_ssd_norm': 1.0 + nrm((DEPTH, SSD_WIDTH), 0.01),
        'g_ret_norm': 1.0 + nrm((DEPTH, RET_WIDTH), 0.01),
        'b_mlstm_i': nrm((DEPTH, ML_HEADS), 0.1),
        'b_mlstm_f': jnp.linspace(3.0, 6.0, ML_HEADS, dtype=F32)[None, :] + nrm((DEPTH, ML_HEADS), 0.1),
        'g_mlstm_norm': 1.0 + nrm((DEPTH, ML_WIDTH), 0.01),
        'w_out': nrm((DEPTH, D_MIX, D_MODEL), D_MIX ** -0.5),
        'w_router_group': nrm((DEPTH, D_MODEL, EXPERT_GROUPS), D_MODEL ** -0.5),
        'b_router_group': nrm((DEPTH, EXPERT_GROUPS), 0.01),
        'w_router_expert': nrm((DEPTH, D_MODEL, N_EXPERTS), D_MODEL ** -0.5),
        'b_router_expert': nrm((DEPTH, N_EXPERTS), 0.01),
        'w_gate_up': nrm((DEPTH, N_EXPERTS, D_MODEL, 2 * EXPERT_FF), D_MODEL ** -0.5),
        'w_down': nrm((DEPTH, N_EXPERTS, EXPERT_FF, D_MODEL), EXPERT_FF ** -0.5),
        'g_final': 1.0 + nrm((D_MODEL,), 0.01),
    }


def reference(x_prompt, x_sample, state_ssd, state_ssd_conv, state_ret, state_mlstm_c, state_mlstm_n,
              state_mlstm_m, c_prompt, c_sample, w_ada, b_ada, g_norm1, g_norm2, w_in, conv_w, conv_b,
              dt_bias, a_log, d_skip, g_ssd_norm, g_ret_norm, b_mlstm_i, b_mlstm_f, g_mlstm_norm, w_out,
              w_router_group, b_router_group, w_router_expert, b_router_expert, w_gate_up, w_down, g_final):
    P = dict(w_ada=w_ada, b_ada=b_ada, g_norm1=g_norm1, g_norm2=g_norm2, w_in=w_in, conv_w=conv_w,
             conv_b=conv_b, dt_bias=dt_bias, a_log=a_log, d_skip=d_skip, g_ssd_norm=g_ssd_norm,
             g_ret_norm=g_ret_norm, b_mlstm_i=b_mlstm_i, b_mlstm_f=b_mlstm_f, g_mlstm_norm=g_mlstm_norm,
             w_out=w_out, w_router_group=w_router_group, b_router_group=b_router_group,
             w_router_expert=w_router_expert, b_router_expert=b_router_expert, w_gate_up=w_gate_up,
             w_down=w_down, g_final=g_final)
    Bp, Tp = x_prompt.shape[0], x_prompt.shape[1]
    dtp = x_prompt.dtype
    zero_state = (jnp.zeros((Bp, SSD_HEADS, SSD_STATE, SSD_HEAD_DIM), dtp),
                  jnp.zeros((Bp, SSD_CONV - 1, SSD_XBC), dtp),
                  jnp.zeros((Bp, RET_HEADS, RET_HEAD_DIM, RET_HEAD_DIM), dtp),
                  jnp.zeros((Bp, ML_HEADS, ML_HEAD_DIM, ML_HEAD_DIM), dtp),
                  jnp.zeros((Bp, ML_HEADS, ML_HEAD_DIM), dtp),
                  jnp.zeros((Bp, ML_HEADS), dtp))
    pos_p = jnp.arange(Tp, dtype=jnp.int32)
    y_prompt, p_states = trunk(x_prompt, c_prompt, pos_p, [zero_state] * DEPTH, P)
    p_ssd, p_conv, p_ret, p_c, p_n, p_m = p_states
    Ts = x_sample.shape[1]
    s_init = [(state_ssd[l], state_ssd_conv[l], state_ret[l], state_mlstm_c[l], state_mlstm_n[l], state_mlstm_m[l])
              for l in range(DEPTH)]
    pos_s = PAST_LEN + jnp.arange(Ts, dtype=jnp.int32)
    y_sample, s_states = trunk(x_sample, c_sample, pos_s, s_init, P)
    s_ssd, s_conv, s_ret, s_c, s_n, s_m = s_states
    return (y_prompt, y_sample, p_ssd, p_conv, p_ret, p_c, p_n, p_m, s_ssd, s_conv, s_ret, s_c, s_n, s_m)
```

```python
import functools
import math

import jax
import jax.numpy as jnp
from jax import lax
from jax.experimental import pallas as pl
from jax.experimental.pallas import tpu as pltpu

F32 = jnp.float32
BF16 = jnp.bfloat16

D_MODEL = 1024
DEPTH = 2
PAST_LEN = 16384
SSD_HEADS = 8
SSD_WIDTH = 512
SSD_GROUPS = 2
SSD_STATE = 64
SSD_CONV = 4
SSD_XBC = 768
RET_HEADS = 4
RET_WIDTH = 256
ML_HEADS = 4
ML_WIDTH = 256
HEAD_DIM = 64
ROPE_BASE = 10000.0
EPS = 1e-6
EXPERT_GROUPS = 4
EXPERTS_PER_GROUP = 8
N_EXPERTS = 32
EXPERT_FF = 256

LANES = 128
CHUNK = 128
P_Z, P_XBC, P_RET, P_ML, P_SM, P_W = 0, 512, 1280, 2304, 3328, 3456
C_SSD, C_RET, C_ML = 0, 8, 12
N_PAIRS = 8
NEG = -1e30
TOKEN_TILE = 512
VMEM_LIMIT = 56 * 1024 * 1024
S_ROWS, S_FIRST, S_LAST = 8, 3, 6
S_BLOCK = 16


def _cparams(sem):
    return pltpu.CompilerParams(dimension_semantics=sem, vmem_limit_bytes=VMEM_LIMIT)


def _split3(x):
    x1 = x.astype(BF16)
    r = x - x1.astype(F32)
    x2 = r.astype(BF16)
    r = r - x2.astype(F32)
    return x1, x2, r.astype(BF16)


def _dot01(m01, x):
    return sum(jnp.dot(m01, p, preferred_element_type=F32) for p in _split3(x))


def _dot(a, b):
    return jnp.dot(a, b, preferred_element_type=F32)


def _dot_nt(a, b):
    return lax.dot_general(a, b, (((1,), (1,)), ((), ())), preferred_element_type=F32)


def _dot_tn(a, b):
    return lax.dot_general(a, b, (((0,), (0,)), ((), ())), preferred_element_type=F32)


def _softplus(x):
    return jnp.maximum(x, 0.0) + jnp.log1p(jnp.exp(-jnp.abs(x)))


def _silu(x):
    return x * jax.nn.sigmoid(x)


def _rms(x):
    return x * lax.rsqrt(jnp.mean(x * x, axis=-1, keepdims=True) + EPS)


def _cummax_rows(x, seg):
    t = lax.broadcasted_iota(jnp.int32, x.shape, 0) & (seg - 1)
    s = 1
    while s < seg:
        x = jnp.maximum(x, jnp.where(t >= s, pltpu.roll(x, s, 0), NEG))
        s *= 2
    return x


def _rope(x, cos, sin_signed, lane):
    swapped = jnp.where((lane & 63) < 32, pltpu.roll(x, 96, 1), pltpu.roll(x, 32, 1))
    return x * cos + swapped * sin_signed


def _ada_kernel(c_ref, w_ref, b_ref, o_ref):
    c = c_ref[...]
    o_ref[0] = _dot(_silu(c).astype(BF16), w_ref[0].astype(BF16)) + b_ref[0]


def _ada(c_all, w_ada, b_ada):
    nb = c_all.shape[0]
    tn = 1536
    return pl.pallas_call(
        _ada_kernel,
        grid=(DEPTH, 6 * D_MODEL // tn),
        in_specs=[pl.BlockSpec((nb, D_MODEL), lambda l, j: (0, 0)),
                  pl.BlockSpec((1, D_MODEL, tn), lambda l, j: (l, 0, j)),
                  pl.BlockSpec((1, 1, tn), lambda l, j: (l, 0, j))],
        out_specs=pl.BlockSpec((1, nb, tn), lambda l, j: (l, 0, j)),
        out_shape=jax.ShapeDtypeStruct((DEPTH, nb, 6 * D_MODEL), F32),
        compiler_params=_cparams(("arbitrary", "arbitrary")),
        name="ada_mod",
    )(c_all, w_ada, b_ada.reshape(DEPTH, 1, 6 * D_MODEL))


def _inproj_kernel(x_ref, sc_ref, sh_ref, g_ref, w_ref, o_ref):
    h = _rms(x_ref[0]) * g_ref[...] * (1.0 + sc_ref[0]) + sh_ref[0]
    o_ref[0] = _dot(h.astype(BF16), w_ref[...])


def _mod_spec(mod, tm):
    if mod.shape[1] == 1:
        return pl.BlockSpec((1, 1, D_MODEL), lambda b, i: (b, 0, 0))
    return pl.BlockSpec((1, tm, D_MODEL), lambda b, i: (b, i, 0))


def _inproj(x, sc, sh, g, w):
    nb, t, _ = x.shape
    tm = min(TOKEN_TILE, t)
    return pl.pallas_call(
        _inproj_kernel,
        grid=(nb, t // tm),
        in_specs=[pl.BlockSpec((1, tm, D_MODEL), lambda b, i: (b, i, 0)),
                  _mod_spec(sc, tm), _mod_spec(sh, tm),
                  pl.BlockSpec((1, D_MODEL), lambda b, i: (0, 0)),
                  pl.BlockSpec((D_MODEL, P_W), lambda b, i: (0, 0))],
        out_specs=pl.BlockSpec((1, tm, P_W), lambda b, i: (b, i, 0)),
        out_shape=jax.ShapeDtypeStruct((nb, t, P_W), F32),
        compiler_params=_cparams(("arbitrary", "arbitrary")),
        name="norm_inproj",
    )(x, sc, sh, g, w)


def _mixer_core(z, us, retb, mlb, small, cos, sin, ptab, cw, cb, dsk, gs, gr, gm,
                mask, tril, valid, seg, mprev, last_fn, st, y_ref):
    rows = small.shape[0]
    lane = lax.broadcasted_iota(jnp.int32, (rows, LANES), 1)
    lm0 = lane < HEAD_DIM

    pre = small + ptab[0:1]
    a_neg = -jnp.exp(ptab[1:2])
    dt = _softplus(pre)
    logf = -_softplus(-pre)
    la = jnp.where(lane < C_RET, dt * a_neg,
                   jnp.where(lane < C_ML, ptab[2:3], jnp.where(lane < C_ML + 4, logf, 0.0)))
    cum = _dot01(tril, la)
    ic = pltpu.roll(pre, 4, 1)
    mlm = (lane >= C_ML) & (lane < C_ML + 4)
    d = jnp.where(mlm if valid is None else (mlm & valid), ic - cum, NEG)
    m_t = cum + jnp.maximum(mprev, _cummax_rows(d, seg))
    inter = jnp.exp(cum + mprev - m_t)
    xt = jnp.where(mlm, d, cum).T
    colv = cum - m_t
    cum_last, m_last = last_fn(cum), last_fn(m_t)
    wq = jnp.where(mlm, jnp.exp(ic + cum_last - cum - m_last), jnp.exp(cum_last - cum))
    if valid is not None:
        wq = jnp.where(valid, wq, 0.0)
    decq = jnp.where(mlm, jnp.exp(cum_last + mprev - m_last), jnp.exp(cum_last))
    eq = jnp.where(mlm, inter, jnp.exp(cum))

    def dec_mat(c, ml):
        lg = (colv[:, c:c + 1] + xt[c:c + 1, :]) if ml else (cum[:, c:c + 1] - xt[c:c + 1, :])
        return jnp.exp(jnp.where(mask, lg, -jnp.inf))

    def pair(idx, qp, kp, vp, c0, c1, ml):
        q0 = jnp.where(lm0, qp, 0.0)
        q1 = jnp.where(lm0, 0.0, qp)
        kb = kp.astype(BF16)
        s0 = _dot_nt(q0.astype(BF16), kb) * dec_mat(c0, ml)
        s1 = _dot_nt(q1.astype(BF16), kb) * dec_mat(c1, ml)
        v0 = jnp.where(lm0, vp, 0.0).astype(BF16)
        v1 = jnp.where(lm0, 0.0, vp).astype(BF16)
        intra = _dot(s0.astype(BF16), v0) + _dot(s1.astype(BF16), v1)
        eqp = jnp.where(lm0, eq[:, c0:c0 + 1], eq[:, c1:c1 + 1])
        kw = kp * jnp.where(lm0, wq[:, c0:c0 + 1], wq[:, c1:c1 + 1])
        carried, qn = st.step(idx, qp, eqp, kw, vp, decq, c0, c1, ml)
        return intra + carried, s0, s1, qn

    def head_norm(o):
        o2 = o * o
        ms0 = jnp.sum(jnp.where(lm0, o2, 0.0), axis=-1, keepdims=True) * (1.0 / HEAD_DIM)
        ms1 = jnp.sum(jnp.where(lm0, 0.0, o2), axis=-1, keepdims=True) * (1.0 / HEAD_DIM)
        return o * jnp.where(lm0, lax.rsqrt(ms0 + EPS), lax.rsqrt(ms1 + EPS))

    conv = cb + us[0] * cw[0:1] + us[1] * cw[1:2] + us[2] * cw[2:3] + us[3] * cw[3:4]
    xc = _silu(conv)
    bb = xc[:, SSD_WIDTH:SSD_WIDTH + LANES]
    cc = xc[:, SSD_WIDTH + LANES:SSD_WIDTH + 2 * LANES]
    br = pltpu.roll(bb, HEAD_DIM, 1)
    cr = pltpu.roll(cc, HEAD_DIM, 1)
    ys = []
    for p in range(4):
        if p < 2:
            kp, qp = jnp.where(lm0, bb, br), jnp.where(lm0, cc, cr)
        else:
            kp, qp = jnp.where(lm0, br, bb), jnp.where(lm0, cr, cc)
        c0, c1 = C_SSD + 2 * p, C_SSD + 2 * p + 1
        sl = slice(LANES * p, LANES * (p + 1))
        xsp = xc[:, sl]
        dtp = jnp.where(lm0, dt[:, c0:c0 + 1], dt[:, c1:c1 + 1])
        o, _, _, _ = pair(p, qp, kp, xsp * dtp, c0, c1, False)
        ys.append((o + dsk[:, sl] * xsp) * _silu(z[:, sl]))
    for g in range(SSD_GROUPS):
        ya, yb = ys[2 * g], ys[2 * g + 1]
        ms = (jnp.sum(ya * ya, axis=-1, keepdims=True)
              + jnp.sum(yb * yb, axis=-1, keepdims=True)) * (1.0 / (2 * LANES))
        r = lax.rsqrt(ms + EPS)
        for j, yv in ((2 * g, ya), (2 * g + 1, yb)):
            sl = slice(LANES * j, LANES * (j + 1))
            y_ref[:, sl] = (yv * r * gs[:, sl]).astype(y_ref.dtype)

    for p in range(2):
        sl = slice(LANES * p, LANES * (p + 1))
        qp = _rope(retb[:, LANES * p:LANES * (p + 1)], cos[:, sl], sin[:, sl], lane)
        kp = _rope(retb[:, RET_WIDTH + LANES * p:RET_WIDTH + LANES * (p + 1)], cos[:, sl], sin[:, sl], lane)
        kp = kp * (HEAD_DIM ** -0.5)
        vp = retb[:, 2 * RET_WIDTH + LANES * p:2 * RET_WIDTH + LANES * (p + 1)]
        gp = retb[:, 3 * RET_WIDTH + LANES * p:3 * RET_WIDTH + LANES * (p + 1)]
        c0, c1 = C_RET + 2 * p, C_RET + 2 * p + 1
        o, _, _, _ = pair(4 + p, qp, kp, vp, c0, c1, False)
        y = head_norm(o) * gr[:, sl] * _silu(gp)
        y_ref[:, SSD_WIDTH + LANES * p:SSD_WIDTH + LANES * (p + 1)] = y.astype(y_ref.dtype)

    for p in range(2):
        sl = slice(LANES * p, LANES * (p + 1))
        qp = mlb[:, LANES * p:LANES * (p + 1)]
        kp = mlb[:, ML_WIDTH + LANES * p:ML_WIDTH + LANES * (p + 1)] * (HEAD_DIM ** -0.5)
        vp = mlb[:, 2 * ML_WIDTH + LANES * p:2 * ML_WIDTH + LANES * (p + 1)]
        op = mlb[:, 3 * ML_WIDTH + LANES * p:3 * ML_WIDTH + LANES * (p + 1)]
        c0, c1 = C_ML + 2 * p, C_ML + 2 * p + 1
        num, s0, s1, (qn0, qn1) = pair(6 + p, qp, kp, vp, c0, c1, True)
        den0 = jnp.sum(s0, axis=-1, keepdims=True) + qn0 * eq[:, c0:c0 + 1]
        den1 = jnp.sum(s1, axis=-1, keepdims=True) + qn1 * eq[:, c1:c1 + 1]
        dn0 = jnp.maximum(jnp.abs(den0), jnp.exp(-m_t[:, c0:c0 + 1]))
        dn1 = jnp.maximum(jnp.abs(den1), jnp.exp(-m_t[:, c1:c1 + 1]))
        hh = num / jnp.where(lm0, dn0, dn1)
        y = head_norm(hh) * gm[:, sl] * jax.nn.sigmoid(op)
        off = SSD_WIDTH + RET_WIDTH + LANES * p
        y_ref[:, off:off + LANES] = y.astype(y_ref.dtype)
    return m_t


def _half_rows():
    return lax.broadcasted_iota(jnp.int32, (LANES, HEAD_DIM), 0) < HEAD_DIM


class _CarriedState:
    def __init__(self, sv, nrow):
        self.sv, self.nrow = sv, nrow

    def step(self, idx, qp, eqp, kw, vp, decq, c0, c1, ml):
        lane = lax.broadcasted_iota(jnp.int32, (1, LANES), 1)
        lm0 = lane < HEAD_DIM
        decq = decq[0:1, :]
        s_old = self.sv[idx]
        sb = s_old.astype(BF16)
        q0 = jnp.where(lm0, qp, 0.0).astype(BF16)
        q1 = jnp.where(lm0, 0.0, qp).astype(BF16)
        carried = jnp.concatenate([_dot(q0, sb), _dot(q1, sb)], axis=1) * eqp
        kwb = kw.astype(BF16)
        u0 = _dot_tn(kwb, vp[:, :HEAD_DIM].astype(BF16))
        u1 = _dot_tn(kwb, vp[:, HEAD_DIM:].astype(BF16))
        top = _half_rows()
        dcol = jnp.where(top, decq[:, c0:c0 + 1], decq[:, c1:c1 + 1])
        self.sv[idx] = s_old * dcol + jnp.where(top, u0, u1)
        qn = None
        if ml:
            p = idx - 6
            n_old = self.nrow[p:p + 1, :]
            qn_l = qp * n_old
            qn = (jnp.sum(jnp.where(lm0, qn_l, 0.0), axis=-1, keepdims=True),
                  jnp.sum(jnp.where(lm0, 0.0, qn_l), axis=-1, keepdims=True))
            drow = jnp.where(lm0, decq[:, c0:c0 + 1], decq[:, c1:c1 + 1])
            self.nrow[p:p + 1, :] = n_old * drow + jnp.sum(kw, axis=0, keepdims=True)
        return carried, qn


def _mixer_prompt_kernel(proj_ref, cos_ref, sin_ref, ptab_ref, cw_ref, cb_ref, dsk_ref, gs_ref,
                         gr_ref, gm_ref, y_ref, sv_o, conv_o, n_o, m_o, sv, nrow, mrow, cbuf):
    ci = pl.program_id(1)
    rows = proj_ref.shape[1]

    @pl.when(ci == 0)
    def _():
        sv[...] = jnp.zeros_like(sv)
        nrow[...] = jnp.zeros_like(nrow)
        mrow[...] = jnp.zeros_like(mrow)
        cbuf[0:8, :] = jnp.zeros((8, SSD_XBC), F32)

    cbuf[8:8 + rows, :] = proj_ref[0, :, P_XBC:P_RET]
    us = [cbuf[pl.ds(5 + k, rows), :] for k in range(SSD_CONV)]
    ri = lax.broadcasted_iota(jnp.int32, (rows, rows), 0)
    cj = lax.broadcasted_iota(jnp.int32, (rows, rows), 1)
    mask = cj <= ri
    tril = jnp.where(mask, 1.0, 0.0).astype(BF16)
    m_t = _mixer_core(
        proj_ref[0, :, P_Z:P_XBC], us, proj_ref[0, :, P_RET:P_ML], proj_ref[0, :, P_ML:P_SM],
        proj_ref[0, :, P_SM:P_W], cos_ref[...], sin_ref[...], ptab_ref[...], cw_ref[...], cb_ref[...],
        dsk_ref[...], gs_ref[...], gr_ref[...], gm_ref[...],
        mask, tril, None, rows, mrow[0:1, :], lambda a: a[rows - 1:rows, :],
        _CarriedState(sv, nrow), y_ref.at[0])
    mrow[0:1, :] = m_t[rows - 1:rows, :]
    cbuf[0:8, :] = cbuf[rows:rows + 8, :]

    @pl.when(ci == pl.num_programs(1) - 1)
    def _():
        sv_o[0] = sv[...]
        conv_o[0] = cbuf[0:8, :]
        n_o[0] = nrow[...]
        m_o[0] = mrow[...]


def _const_spec(a):
    nd = a.ndim
    return pl.BlockSpec(a.shape, lambda *_: (0,) * nd)


def _mixer_prompt(proj, cos, sin, consts):
    nb, t, _ = proj.shape
    rows = math.gcd(t, CHUNK)
    outs = pl.pallas_call(
        _mixer_prompt_kernel,
        grid=(nb, t // rows),
        in_specs=[pl.BlockSpec((1, rows, P_W), lambda b, c: (b, c, 0)),
                  pl.BlockSpec((rows, RET_WIDTH), lambda b, c: (c, 0)),
                  pl.BlockSpec((rows, RET_WIDTH), lambda b, c: (c, 0))] + [_const_spec(a) for a in consts],
        out_specs=[pl.BlockSpec((1, rows, D_MODEL), lambda b, c: (b, c, 0)),
                   pl.BlockSpec((1, N_PAIRS, LANES, HEAD_DIM), lambda b, c: (b, 0, 0, 0)),
                   pl.BlockSpec((1, 8, SSD_XBC), lambda b, c: (b, 0, 0)),
                   pl.BlockSpec((1, 8, LANES), lambda b, c: (b, 0, 0)),
                   pl.BlockSpec((1, 8, LANES), lambda b, c: (b, 0, 0))],
        out_shape=[jax.ShapeDtypeStruct((nb, t, D_MODEL), BF16),
                   jax.ShapeDtypeStruct((nb, N_PAIRS, LANES, HEAD_DIM), F32),
                   jax.ShapeDtypeStruct((nb, 8, SSD_XBC), F32),
                   jax.ShapeDtypeStruct((nb, 8, LANES), F32),
                   jax.ShapeDtypeStruct((nb, 8, LANES), F32)],
        scratch_shapes=[pltpu.VMEM((N_PAIRS, LANES, HEAD_DIM), F32),
                        pltpu.VMEM((8, LANES), F32),
                        pltpu.VMEM((8, LANES), F32),
                        pltpu.VMEM((rows + 8, SSD_XBC), F32)],
        compiler_params=_cparams(("arbitrary", "arbitrary")),
        name="mixer_prompt",
    )(proj, cos, sin, *consts)
    y, sv, conv, n, m = outs
    sv = sv.reshape(nb, 2 * N_PAIRS, HEAD_DIM, HEAD_DIM)
    states = (sv[:, :8], conv[:, 5:8], sv[:, 8:12], sv[:, 12:16],
              n[:, 0:2].reshape(nb, ML_HEADS, HEAD_DIM), m[:, 0, C_ML:C_ML + 4])
    return y, states


class _PerSequenceState:
    def __init__(self, s_in, s_out, n_in, n_out, q_s, e_s, k_s, v_s, d_s, o_s, qn_s):
        self.s_in, self.s_out, self.n_in, self.n_out = s_in, s_out, n_in, n_out
        self.q_s, self.e_s, self.k_s, self.v_s, self.d_s, self.o_s, self.qn_s = q_s, e_s, k_s, v_s, d_s, o_s, qn_s

    def step(self, idx, qp, eqp, kw, vp, decq, c0, c1, ml):
        lane = lax.broadcasted_iota(jnp.int32, (1, LANES), 1)
        lm0 = lane < HEAD_DIM
        self.q_s[...] = qp
        self.e_s[...] = eqp
        self.k_s[...] = kw
        self.v_s[...] = vp
        self.d_s[...] = jnp.where(lm0, decq[:, c0:c0 + 1], decq[:, c1:c1 + 1])
        top = _half_rows()
        nseq = self.q_s.shape[0] // S_ROWS

        def body(b, carry):
            r0 = pl.multiple_of(b * S_ROWS, S_ROWS)
            rs = pl.ds(r0, S_ROWS)
            s_old = self.s_in[b, idx]
            sb = s_old.astype(BF16)
            q = self.q_s[rs, :]
            q0 = jnp.where(lm0, q, 0.0)
            q1 = jnp.where(lm0, 0.0, q)
            carried = jnp.concatenate([_dot(q0.astype(BF16), sb), _dot(q1.astype(BF16), sb)], axis=1)
            self.o_s[rs, :] = carried * self.e_s[rs, :]
            kwb = self.k_s[rs, :]
            v = self.v_s[rs, :]
            u0 = _dot_tn(kwb.astype(BF16), v[:, :HEAD_DIM].astype(BF16))
            u1 = _dot_tn(kwb.astype(BF16), v[:, HEAD_DIM:].astype(BF16))
            drow = self.d_s[pl.ds(r0, 1), :]
            dcol = jnp.where(top, drow[:, 0:1], drow[:, HEAD_DIM:HEAD_DIM + 1])
            self.s_out[b, idx] = s_old * dcol + jnp.where(top, u0, u1)
            if ml:
                p = idx - 6
                n_old = self.n_in[b, p]
                qn_l = q * n_old
                qn0 = jnp.sum(jnp.where(lm0, qn_l, 0.0), axis=-1, keepdims=True)
                qn1 = jnp.sum(jnp.where(lm0, 0.0, qn_l), axis=-1, keepdims=True)
                self.qn_s[rs, :] = jnp.where(lm0, qn0, qn1)
                self.n_out[b, p] = n_old * drow + jnp.sum(kwb, axis=0, keepdims=True)
            return carry

        lax.fori_loop(0, nseq, body, 0)
        qn = None
        if ml:
            qn = (self.qn_s[:, 0:1], self.qn_s[:, HEAD_DIM:HEAD_DIM + 1])
        return self.o_s[...], qn


def _mixer_sample_kernel(proj_ref, tail_ref, mprev_ref, cos_ref, sin_ref, s_in, n_in, ptab_ref, cw_ref,
                         cb_ref, dsk_ref, gs_ref, gr_ref, gm_ref, y_ref, u_o, m_o, s_out, n_out,
                         q_s, e_s, k_s, v_s, d_s, o_s, qn_s):
    rows = proj_ref.shape[0]
    ri = lax.broadcasted_iota(jnp.int32, (rows, rows), 0)
    cj = lax.broadcasted_iota(jnp.int32, (rows, rows), 1)
    tj = cj & (S_ROWS - 1)
    same_seq = (ri & ~(S_ROWS - 1)) == (cj & ~(S_ROWS - 1))
    mask = same_seq & (tj >= S_FIRST) & (tj <= S_LAST) & (cj <= ri)
    tril = jnp.where(mask, 1.0, 0.0).astype(BF16)
    last_sel = jnp.where(cj == ((ri & ~(S_ROWS - 1)) + S_LAST), 1.0, 0.0).astype(BF16)
    tr = lax.broadcasted_iota(jnp.int32, (rows, 1), 0) & (S_ROWS - 1)
    valid = (tr >= S_FIRST) & (tr <= S_LAST)
    u_full = jnp.where(tr < S_FIRST, tail_ref[...], proj_ref[:, P_XBC:P_RET])
    u_o[...] = u_full
    us = [pltpu.roll(u_full, SSD_CONV - 1 - k, 0) for k in range(SSD_CONV - 1)] + [u_full]
    st = _PerSequenceState(s_in, s_out, n_in, n_out, q_s, e_s, k_s, v_s, d_s, o_s, qn_s)
    m_o[...] = _mixer_core(
        proj_ref[:, P_Z:P_XBC], us, proj_ref[:, P_RET:P_ML], proj_ref[:, P_ML:P_SM], proj_ref[:, P_SM:P_W],
        cos_ref[...], sin_ref[...], ptab_ref[...], cw_ref[...], cb_ref[...], dsk_ref[...], gs_ref[...],
        gr_ref[...], gm_ref[...], mask, tril, valid, S_ROWS, mprev_ref[...],
        lambda a: _dot01(last_sel, a), st, y_ref)


def _mixer_sample(proj, tail, mprev, cos, sin, s_in, n_in, consts):
    n_rows = proj.shape[0]
    rows = S_BLOCK * S_ROWS
    nseq = n_rows // S_ROWS

    def rowspec(w):
        return pl.BlockSpec((rows, w), lambda i: (i, 0))

    s_spec = pl.BlockSpec((S_BLOCK, N_PAIRS, LANES, HEAD_DIM), lambda i: (i, 0, 0, 0))
    n_spec = pl.BlockSpec((S_BLOCK, 2, 1, LANES), lambda i: (i, 0, 0, 0))
    return pl.pallas_call(
        _mixer_sample_kernel,
        grid=(n_rows // rows,),
        in_specs=[rowspec(P_W), rowspec(SSD_XBC), rowspec(LANES), _const_spec(cos), _const_spec(sin),
                  s_spec, n_spec] + [_const_spec(a) for a in consts],
        out_specs=[rowspec(D_MODEL), rowspec(SSD_XBC), rowspec(LANES), s_spec, n_spec],
        out_shape=[jax.ShapeDtypeStruct((n_rows, D_MODEL), BF16),
                   jax.ShapeDtypeStruct((n_rows, SSD_XBC), F32),
                   jax.ShapeDtypeStruct((n_rows, LANES), F32),
                   jax.ShapeDtypeStruct((nseq, N_PAIRS, LANES, HEAD_DIM), F32),
                   jax.ShapeDtypeStruct((nseq, 2, 1, LANES), F32)],
        scratch_shapes=[pltpu.VMEM((rows, LANES), F32) for _ in range(7)],
        compiler_params=_cparams(("arbitrary",)),
        name="mixer_sample",
    )(proj, tail, mprev, cos, sin, s_in, n_in, *consts)


def _route(logits):
    lane = lax.broadcasted_iota(jnp.int32, logits.shape, 1)
    gmask = (lane >= N_EXPERTS) & (lane < N_EXPERTS + EXPERT_GROUPS)
    gl = jnp.where(gmask, logits, -jnp.inf)
    ge = jnp.exp(gl - jnp.max(gl, axis=-1, keepdims=True))
    gprob = ge / jnp.sum(ge, axis=-1, keepdims=True)
    g_w = jnp.max(gprob, axis=-1, keepdims=True)
    g_idx = jnp.min(jnp.where(gmask & (gprob == g_w), lane - N_EXPERTS, LANES), axis=-1, keepdims=True)
    emask = (lane < N_EXPERTS) & ((lane >> 3) == g_idx)
    el = jnp.where(emask, logits, -jnp.inf)
    ee = jnp.exp(el - jnp.max(el, axis=-1, keepdims=True))
    eprob = ee / jnp.sum(ee, axis=-1, keepdims=True)
    p1 = jnp.max(jnp.where(emask, eprob, -1.0), axis=-1, keepdims=True)
    i1 = jnp.min(jnp.where(emask & (eprob == p1), lane, LANES), axis=-1, keepdims=True)
    rest = emask & (lane != i1)
    p2 = jnp.max(jnp.where(rest, eprob, -1.0), axis=-1, keepdims=True)
    i2 = jnp.min(jnp.where(rest & (eprob == p2), lane, LANES), axis=-1, keepdims=True)
    tot = p1 + p2
    return jnp.where(lane == i1, g_w * (p1 / tot), 0.0) + jnp.where(lane == i2, g_w * (p2 / tot), 0.0)


def _outproj_kernel(y_ref, x_ref, gt_ref, sc_ref, sh_ref, g_ref, w_ref, wr_ref, br_ref,
                    x1_ref, h2_ref, gates_ref):
    x1 = x_ref[0] + gt_ref[0] * _dot(y_ref[0], w_ref[...])
    x1_ref[0] = x1
    h2 = (_rms(x1) * g_ref[...] * (1.0 + sc_ref[0]) + sh_ref[0]).astype(BF16)
    h2_ref[0] = h2
    gates_ref[0] = _route(_dot(h2, wr_ref[...]) + br_ref[...])


def _outproj(y, x, gt, sc, sh, g, w, wr, br):
    nb, t, _ = x.shape
    tm = min(TOKEN_TILE, t)
    tok = lambda w_: pl.BlockSpec((1, tm, w_), lambda b, i: (b, i, 0))
    return pl.pallas_call(
        _outproj_kernel,
        grid=(nb, t // tm),
        in_specs=[tok(D_MODEL), tok(D_MODEL), _mod_spec(gt, tm), _mod_spec(sc, tm), _mod_spec(sh, tm),
                  _const_spec(g), _const_spec(w), _const_spec(wr), _const_spec(br)],
        out_specs=[tok(D_MODEL), tok(D_MODEL), tok(LANES)],
        out_shape=[jax.ShapeDtypeStruct((nb, t, D_MODEL), F32),
                   jax.ShapeDtypeStruct((nb, t, D_MODEL), BF16),
                   jax.ShapeDtypeStruct((nb, t, LANES), F32)],
        compiler_params=_cparams(("arbitrary", "arbitrary")),
        name="outproj_router",
    )(y, x, gt, sc, sh, g, w, wr, br)


def _moe_kernel(h_ref, gates_ref, x1_ref, gt_ref, gf_ref, wgu_ref, wd_ref, o_ref, acc_ref, *, final):
    g = pl.program_id(2)

    @pl.when(g == 0)
    def _():
        acc_ref[...] = jnp.zeros_like(acc_ref)

    h = h_ref[0]
    gates = gates_ref[0]
    lane = lax.broadcasted_iota(jnp.int32, gates.shape, 1)
    for e in range(EXPERTS_PER_GROUP):
        au = _dot(h, wgu_ref[0, e])
        act = _silu(au[:, :EXPERT_FF]) * au[:, EXPERT_FF:]
        ye = _dot(act.astype(BF16), wd_ref[0, e])
        ge = jnp.sum(jnp.where(lane == g * EXPERTS_PER_GROUP + e, gates, 0.0), axis=-1, keepdims=True)
        acc_ref[...] += ge * ye

    @pl.when(g == pl.num_programs(2) - 1)
    def _():
        x2 = x1_ref[0] + gt_ref[0] * acc_ref[...]
        o_ref[0] = _rms(x2) * gf_ref[...] if final else x2


def _moe(h2, gates, x1, gt, gf, wgu, wd, final):
    nb, t, _ = x1.shape
    tm = min(TOKEN_TILE, t)
    tok = lambda w_: pl.BlockSpec((1, tm, w_), lambda b, i, g: (b, i, 0))
    gt_spec = (pl.BlockSpec((1, 1, D_MODEL), lambda b, i, g: (b, 0, 0)) if gt.shape[1] == 1
               else pl.BlockSpec((1, tm, D_MODEL), lambda b, i, g: (b, i, 0)))
    return pl.pallas_call(
        functools.partial(_moe_kernel, final=final),
        grid=(nb, t // tm, EXPERT_GROUPS),
        in_specs=[tok(D_MODEL), tok(LANES), tok(D_MODEL), gt_spec,
                  pl.BlockSpec((1, D_MODEL), lambda b, i, g: (0, 0)),
                  pl.BlockSpec((1, EXPERTS_PER_GROUP, D_MODEL, 2 * EXPERT_FF), lambda b, i, g: (g, 0, 0, 0)),
                  pl.BlockSpec((1, EXPERTS_PER_GROUP, EXPERT_FF, D_MODEL), lambda b, i, g: (g, 0, 0, 0))],
        out_specs=tok(D_MODEL),
        out_shape=jax.ShapeDtypeStruct((nb, t, D_MODEL), F32),
        scratch_shapes=[pltpu.VMEM((tm, D_MODEL), F32)],
        compiler_params=_cparams(("arbitrary", "arbitrary", "arbitrary")),
        name="moe_experts",
    )(h2, gates, x1, gt, gf, wgu, wd)


def _rope_tables(pos):
    half = HEAD_DIM // 2
    inv = ROPE_BASE ** (-jnp.arange(half, dtype=F32) / half)
    ang = pos.astype(F32)[:, None] * inv[None, :]
    cos, sin = jnp.cos(ang), jnp.sin(ang)
    cos_t = jnp.tile(jnp.concatenate([cos, cos], axis=-1), (1, RET_HEADS))
    sin_t = jnp.tile(jnp.concatenate([-sin, sin], axis=-1), (1, RET_HEADS))
    return cos_t, sin_t


def _layer_consts(l, w_in, conv_w, conv_b, dt_bias, a_log, d_skip, g_ssd_norm, g_ret_norm,
                  b_mlstm_i, b_mlstm_f, g_mlstm_norm):
    w = w_in[l]
    w_p = jnp.concatenate([w[:, 0:1280], w[:, 1288:3336], w[:, 1280:1288], w[:, 3336:3344],
                           jnp.zeros((D_MODEL, LANES - 16), F32)], axis=1).astype(BF16)
    pad = lambda v, n: jnp.concatenate([v, jnp.zeros((n,), F32)])
    log_gamma = jnp.log(1.0 - 2.0 ** (-5.0 - jnp.arange(RET_HEADS, dtype=F32)))
    ptab = jnp.stack([pad(jnp.concatenate([dt_bias[l], b_mlstm_i[l], b_mlstm_f[l]]), LANES - 16),
                      pad(a_log[l], LANES - 8),
                      pad(jnp.concatenate([jnp.zeros((8,), F32), log_gamma]), LANES - 12)]
                     + [jnp.zeros((LANES,), F32)] * 5)
    consts = (ptab, conv_w[l], conv_b[l][None, :], jnp.repeat(d_skip[l], HEAD_DIM)[None, :],
              g_ssd_norm[l][None, :], g_ret_norm[l][None, :], g_mlstm_norm[l][None, :])
    return w_p, consts


def kernel(x_prompt, x_sample, state_ssd, state_ssd_conv, state_ret, state_mlstm_c, state_mlstm_n,
           state_mlstm_m, c_prompt, c_sample, w_ada, b_ada, g_norm1, g_norm2, w_in, conv_w, conv_b,
           dt_bias, a_log, d_skip, g_ssd_norm, g_ret_norm, b_mlstm_i, b_mlstm_f, g_mlstm_norm, w_out,
           w_router_group, b_router_group, w_router_expert, b_router_expert, w_gate_up, w_down, g_final):
    bp, tp, _ = x_prompt.shape
    bs, ts, _ = x_sample.shape
    assert ts == S_LAST - S_FIRST + 1
    n_srows = bs * S_ROWS

    mod = _ada(jnp.concatenate([c_prompt, c_sample], axis=0), w_ada, b_ada)

    def mods(l, lo, hi, per_row):
        m = mod[l, lo:hi].reshape(hi - lo, 6, D_MODEL)
        if per_row:
            return [jnp.repeat(m[:, k], S_ROWS, axis=0)[None] for k in range(6)]
        return [m[:, k][:, None, :] for k in range(6)]

    cos_p, sin_p = _rope_tables(jnp.arange(tp, dtype=jnp.int32))
    t8 = jnp.clip(jnp.arange(S_ROWS, dtype=jnp.int32) - S_FIRST, 0, ts - 1)
    cos_s, sin_s = _rope_tables(PAST_LEN + t8)
    cos_s, sin_s = jnp.tile(cos_s, (S_BLOCK, 1)), jnp.tile(sin_s, (S_BLOCK, 1))

    xp = x_prompt
    xs = jnp.pad(x_sample, ((0, 0), (S_FIRST, S_ROWS - 1 - S_LAST), (0, 0))).reshape(1, n_srows, D_MODEL)
    p_states, s_states = [], []
    yp = ys = None
    for l in range(DEPTH):
        w_p, consts = _layer_consts(l, w_in, conv_w, conv_b, dt_bias, a_log, d_skip, g_ssd_norm,
                                    g_ret_norm, b_mlstm_i, b_mlstm_f, g_mlstm_norm)
        w_o = w_out[l].astype(BF16)
        wr = jnp.concatenate([w_router_expert[l], w_router_group[l],
                              jnp.zeros((D_MODEL, LANES - N_EXPERTS - EXPERT_GROUPS), F32)], axis=1).astype(BF16)
        br = jnp.concatenate([b_router_expert[l], b_router_group[l],
                              jnp.zeros((LANES - N_EXPERTS - EXPERT_GROUPS,), F32)])[None, :]
        wgu = w_gate_up[l].astype(BF16).reshape(EXPERT_GROUPS, EXPERTS_PER_GROUP, D_MODEL, 2 * EXPERT_FF)
        wd = w_down[l].astype(BF16).reshape(EXPERT_GROUPS, EXPERTS_PER_GROUP, EXPERT_FF, D_MODEL)
        g1, g2, gf = g_norm1[l][None, :], g_norm2[l][None, :], g_final[None, :]
        final = l == DEPTH - 1

        sh1, sc1, gt1, sh2, sc2, gt2 = mods(l, 0, bp, False)
        proj = _inproj(xp, sc1, sh1, g1, w_p)
        ycat, st = _mixer_prompt(proj, cos_p, sin_p, consts)
        p_states.append(st)
        x1, h2, gates = _outproj(ycat, xp, gt1, sc2, sh2, g2, w_o, wr, br)
        xp = _moe(h2, gates, x1, gt2, gf, wgu, wd, final)

        sh1, sc1, gt1, sh2, sc2, gt2 = mods(l, bp, bp + bs, True)
        proj = _inproj(xs, sc1, sh1, g1, w_p)[0]
        tail = jnp.pad(state_ssd_conv[l], ((0, 0), (0, S_ROWS - SSD_CONV + 1), (0, 0))).reshape(n_srows, SSD_XBC)
        mprev = jnp.pad(jnp.repeat(state_mlstm_m[l], S_ROWS, axis=0), ((0, 0), (C_ML, LANES - C_ML - ML_HEADS)))
        s_in = jnp.concatenate([state_ssd[l], state_ret[l], state_mlstm_c[l]], axis=1)
        s_in = s_in.reshape(bs, N_PAIRS, LANES, HEAD_DIM)
        n_in = state_mlstm_n[l].reshape(bs, 2, 1, LANES)
        ycat, u_full, m_rows, s_new, n_new = _mixer_sample(proj, tail, mprev, cos_s, sin_s, s_in, n_in, consts)
        s_new = s_new.reshape(bs, 2 * N_PAIRS, HEAD_DIM, HEAD_DIM)
        s_states.append((s_new[:, :8], u_full.reshape(bs, S_ROWS, SSD_XBC)[:, S_LAST - 2:S_LAST + 1],
                         s_new[:, 8:12], s_new[:, 12:16], n_new.reshape(bs, ML_HEADS, HEAD_DIM),
                         m_rows.reshape(bs, S_ROWS, LANES)[:, S_LAST, C_ML:C_ML + ML_HEADS]))
        x1, h2, gates = _outproj(ycat[None], xs, gt1, sc2, sh2, g2, w_o, wr, br)
        xs = _moe(h2, gates, x1, gt2, gf, wgu, wd, final)

    y_prompt = xp
    y_sample = xs.reshape(bs, S_ROWS, D_MODEL)[:, S_FIRST:S_LAST + 1]
    p_st = [jnp.stack([s[i] for s in p_states], axis=0) for i in range(6)]
    s_st = [jnp.stack([s[i] for s in s_states], axis=0) for i in range(6)]
    return (y_prompt, y_sample, *p_st, *s_st)
```

```python
import functools
import math

import jax
import jax.numpy as jnp
from jax import lax
from jax.experimental import pallas as pl
from jax.experimental.pallas import tpu as pltpu

F32 = jnp.float32
BF16 = jnp.bfloat16

D_MODEL = 1024
DEPTH = 2
PAST_LEN = 16384
SSD_HEADS = 8
SSD_WIDTH = 512
SSD_GROUPS = 2
SSD_STATE = 64
SSD_CONV = 4
SSD_XBC = 768
RET_HEADS = 4
RET_WIDTH = 256
ML_HEADS = 4
ML_WIDTH = 256
HEAD_DIM = 64
ROPE_BASE = 10000.0
EPS = 1e-6
EXPERT_GROUPS = 4
EXPERTS_PER_GROUP = 8
N_EXPERTS = 32
EXPERT_FF = 256

LANES = 128
CHUNK = 128
P_Z, P_XBC, P_RET, P_ML, P_SM, P_W = 0, 512, 1280, 2304, 3328, 3456
C_SSD, C_RET, C_ML = 0, 8, 12
N_PAIRS = 8
NEG = -1e30
TOKEN_TILE = 512
MOE_TILE = 256
VMEM_LIMIT = 56 * 1024 * 1024
S_ROWS, S_FIRST, S_LAST = 8, 3, 6
S_BLOCK = 16


def _cparams(sem):
    return pltpu.CompilerParams(dimension_semantics=sem, vmem_limit_bytes=VMEM_LIMIT)


def _split3(x):
    x1 = x.astype(BF16)
    r = x - x1.astype(F32)
    x2 = r.astype(BF16)
    r = r - x2.astype(F32)
    return x1, x2, r.astype(BF16)


def _dot01(m01, x):
    return sum(jnp.dot(m01, p, preferred_element_type=F32) for p in _split3(x))


def _dot(a, b):
    return jnp.dot(a, b, preferred_element_type=F32)


def _dot_nt(a, b):
    return lax.dot_general(a, b, (((1,), (1,)), ((), ())), preferred_element_type=F32)


def _dot_tn(a, b):
    return lax.dot_general(a, b, (((0,), (0,)), ((), ())), preferred_element_type=F32)


def _softplus(x):
    return jnp.maximum(x, 0.0) + jnp.log1p(jnp.exp(-jnp.abs(x)))


def _silu(x):
    return x * jax.nn.sigmoid(x)


def _rms(x):
    return x * lax.rsqrt(jnp.mean(x * x, axis=-1, keepdims=True) + EPS)


def _cummax_rows(x, seg):
    t = lax.broadcasted_iota(jnp.int32, x.shape, 0) & (seg - 1)
    s = 1
    while s < seg:
        x = jnp.maximum(x, jnp.where(t >= s, pltpu.roll(x, s, 0), NEG))
        s *= 2
    return x


def _rope(x, cos, sin_signed, lane):
    swapped = jnp.where((lane & 63) < 32, pltpu.roll(x, 96, 1), pltpu.roll(x, 32, 1))
    return x * cos + swapped * sin_signed


def _ada_kernel(c_ref, w_ref, b_ref, o_ref):
    c = c_ref[...]
    o_ref[0] = _dot(_silu(c).astype(BF16), w_ref[0].astype(BF16)) + b_ref[0]


def _ada(c_all, w_ada, b_ada):
    nb = c_all.shape[0]
    tn = 1536
    return pl.pallas_call(
        _ada_kernel,
        grid=(DEPTH, 6 * D_MODEL // tn),
        in_specs=[pl.BlockSpec((nb, D_MODEL), lambda l, j: (0, 0)),
                  pl.BlockSpec((1, D_MODEL, tn), lambda l, j: (l, 0, j)),
                  pl.BlockSpec((1, 1, tn), lambda l, j: (l, 0, j))],
        out_specs=pl.BlockSpec((1, nb, tn), lambda l, j: (l, 0, j)),
        out_shape=jax.ShapeDtypeStruct((DEPTH, nb, 6 * D_MODEL), F32),
        compiler_params=_cparams(("arbitrary", "arbitrary")),
        name="ada_mod",
    )(c_all, w_ada, b_ada.reshape(DEPTH, 1, 6 * D_MODEL))


def _inproj_kernel(x_ref, scp, scs, shp, shs, g_ref, w_ref, o_ref, *, np_tiles):
    is_p = pl.program_id(0) < np_tiles
    h = _rms(x_ref[...]) * g_ref[...] * (1.0 + _mod_val(scp, scs, is_p)) + _mod_val(shp, shs, is_p)
    o_ref[...] = _dot(h.astype(BF16), w_ref[...])


def _mod_specs(tm, n_prompt, tp):
    np_tiles, last_b = n_prompt // tm, n_prompt // tp - 1
    return [pl.BlockSpec((1, 1, D_MODEL), lambda i, *_: (jnp.minimum(i * tm // tp, last_b), 0, 0)),
            pl.BlockSpec((tm, D_MODEL), lambda i, *_: (jnp.maximum(i - np_tiles, 0), 0))]


def _mod_val(p_ref, s_ref, is_prompt):
    return jnp.where(is_prompt, p_ref[0], s_ref[...])


def _row_spec(tm, width):
    return pl.BlockSpec((tm, width), lambda i, *_: (i, 0))


def _const_spec(a):
    nd = a.ndim
    return pl.BlockSpec(a.shape, lambda *_: (0,) * nd)


def _inproj(x, sc, sh, g, w, n_prompt, tp):
    n = x.shape[0]
    tm = TOKEN_TILE
    ms = _mod_specs(tm, n_prompt, tp)
    return pl.pallas_call(
        functools.partial(_inproj_kernel, np_tiles=n_prompt // tm),
        grid=(n // tm,),
        in_specs=[_row_spec(tm, D_MODEL)] + ms + ms + [_const_spec(g), _const_spec(w)],
        out_specs=_row_spec(tm, P_W),
        out_shape=jax.ShapeDtypeStruct((n, P_W), F32),
        compiler_params=_cparams(("arbitrary",)),
        name="norm_inproj",
    )(x, *sc, *sh, g, w)


def _mixer_core(z, us, retb, mlb, small, cos, sin, ptab, cw, cb, dsk, gs, gr, gm,
                mask, tril, valid, seg, mprev, last_fn, st, y_ref):
    rows = small.shape[0]
    lane = lax.broadcasted_iota(jnp.int32, (rows, LANES), 1)
    lm0 = lane < HEAD_DIM

    pre = small + ptab[0:1]
    a_neg = -jnp.exp(ptab[1:2])
    dt = _softplus(pre)
    logf = -_softplus(-pre)
    la = jnp.where(lane < C_RET, dt * a_neg,
                   jnp.where(lane < C_ML, ptab[2:3], jnp.where(lane < C_ML + 4, logf, 0.0)))
    cum = _dot01(tril, la)
    ic = pltpu.roll(pre, 4, 1)
    mlm = (lane >= C_ML) & (lane < C_ML + 4)
    d = jnp.where(mlm if valid is None else (mlm & valid), ic - cum, NEG)
    m_t = cum + jnp.maximum(mprev, _cummax_rows(d, seg))
    inter = jnp.exp(cum + mprev - m_t)
    xt = jnp.where(mlm, d, cum).T
    colv = cum - m_t
    cum_last, m_last = last_fn(cum), last_fn(m_t)
    wq = jnp.where(mlm, jnp.exp(ic + cum_last - cum - m_last), jnp.exp(cum_last - cum))
    if valid is not None:
        wq = jnp.where(valid, wq, 0.0)
    decq = jnp.where(mlm, jnp.exp(cum_last + mprev - m_last), jnp.exp(cum_last))
    eq = jnp.where(mlm, inter, jnp.exp(cum))

    def dec_mat(c, ml):
        lg = (colv[:, c:c + 1] + xt[c:c + 1, :]) if ml else (cum[:, c:c + 1] - xt[c:c + 1, :])
        return jnp.exp(jnp.where(mask, lg, -jnp.inf))

    def pair(idx, qp, kp, vp, c0, c1, ml):
        q0 = jnp.where(lm0, qp, 0.0)
        q1 = jnp.where(lm0, 0.0, qp)
        kb = kp.astype(BF16)
        s0 = _dot_nt(q0.astype(BF16), kb) * dec_mat(c0, ml)
        s1 = _dot_nt(q1.astype(BF16), kb) * dec_mat(c1, ml)
        v0 = jnp.where(lm0, vp, 0.0).astype(BF16)
        v1 = jnp.where(lm0, 0.0, vp).astype(BF16)
        intra = _dot(s0.astype(BF16), v0) + _dot(s1.astype(BF16), v1)
        eqp = jnp.where(lm0, eq[:, c0:c0 + 1], eq[:, c1:c1 + 1])
        kw = kp * jnp.where(lm0, wq[:, c0:c0 + 1], wq[:, c1:c1 + 1])
        carried, qn = st.step(idx, qp, eqp, kw, vp, decq, c0, c1, ml)
        return intra + carried, s0, s1, qn

    def head_norm(o):
        o2 = o * o
        ms0 = jnp.sum(jnp.where(lm0, o2, 0.0), axis=-1, keepdims=True) * (1.0 / HEAD_DIM)
        ms1 = jnp.sum(jnp.where(lm0, 0.0, o2), axis=-1, keepdims=True) * (1.0 / HEAD_DIM)
        return o * jnp.where(lm0, lax.rsqrt(ms0 + EPS), lax.rsqrt(ms1 + EPS))

    conv = cb + us[0] * cw[0:1] + us[1] * cw[1:2] + us[2] * cw[2:3] + us[3] * cw[3:4]
    xc = _silu(conv)
    bb = xc[:, SSD_WIDTH:SSD_WIDTH + LANES]
    cc = xc[:, SSD_WIDTH + LANES:SSD_WIDTH + 2 * LANES]
    br = pltpu.roll(bb, HEAD_DIM, 1)
    cr = pltpu.roll(cc, HEAD_DIM, 1)
    ys = []
    for p in range(4):
        if p < 2:
            kp, qp = jnp.where(lm0, bb, br), jnp.where(lm0, cc, cr)
        else:
            kp, qp = jnp.where(lm0, br, bb), jnp.where(lm0, cr, cc)
        c0, c1 = C_SSD + 2 * p, C_SSD + 2 * p + 1
        sl = slice(LANES * p, LANES * (p + 1))
        xsp = xc[:, sl]
        dtp = jnp.where(lm0, dt[:, c0:c0 + 1], dt[:, c1:c1 + 1])
        o, _, _, _ = pair(p, qp, kp, xsp * dtp, c0, c1, False)
        ys.append((o + dsk[:, sl] * xsp) * _silu(z[:, sl]))
    for g in range(SSD_GROUPS):
        ya, yb = ys[2 * g], ys[2 * g + 1]
        ms = (jnp.sum(ya * ya, axis=-1, keepdims=True)
              + jnp.sum(yb * yb, axis=-1, keepdims=True)) * (1.0 / (2 * LANES))
        r = lax.rsqrt(ms + EPS)
        for j, yv in ((2 * g, ya), (2 * g + 1, yb)):
            sl = slice(LANES * j, LANES * (j + 1))
            y_ref[:, sl] = (yv * r * gs[:, sl]).astype(y_ref.dtype)

    for p in range(2):
        sl = slice(LANES * p, LANES * (p + 1))
        qp = _rope(retb[:, LANES * p:LANES * (p + 1)], cos[:, sl], sin[:, sl], lane)
        kp = _rope(retb[:, RET_WIDTH + LANES * p:RET_WIDTH + LANES * (p + 1)], cos[:, sl], sin[:, sl], lane)
        kp = kp * (HEAD_DIM ** -0.5)
        vp = retb[:, 2 * RET_WIDTH + LANES * p:2 * RET_WIDTH + LANES * (p + 1)]
        gp = retb[:, 3 * RET_WIDTH + LANES * p:3 * RET_WIDTH + LANES * (p + 1)]
        c0, c1 = C_RET + 2 * p, C_RET + 2 * p + 1
        o, _, _, _ = pair(4 + p, qp, kp, vp, c0, c1, False)
        y = head_norm(o) * gr[:, sl] * _silu(gp)
        y_ref[:, SSD_WIDTH + LANES * p:SSD_WIDTH + LANES * (p + 1)] = y.astype(y_ref.dtype)

    for p in range(2):
        sl = slice(LANES * p, LANES * (p + 1))
        qp = mlb[:, LANES * p:LANES * (p + 1)]
        kp = mlb[:, ML_WIDTH + LANES * p:ML_WIDTH + LANES * (p + 1)] * (HEAD_DIM ** -0.5)
        vp = mlb[:, 2 * ML_WIDTH + LANES * p:2 * ML_WIDTH + LANES * (p + 1)]
        op = mlb[:, 3 * ML_WIDTH + LANES * p:3 * ML_WIDTH + LANES * (p + 1)]
        c0, c1 = C_ML + 2 * p, C_ML + 2 * p + 1
        num, s0, s1, (qn0, qn1) = pair(6 + p, qp, kp, vp, c0, c1, True)
        den0 = jnp.sum(s0, axis=-1, keepdims=True) + qn0 * eq[:, c0:c0 + 1]
        den1 = jnp.sum(s1, axis=-1, keepdims=True) + qn1 * eq[:, c1:c1 + 1]
        dn0 = jnp.maximum(jnp.abs(den0), jnp.exp(-m_t[:, c0:c0 + 1]))
        dn1 = jnp.maximum(jnp.abs(den1), jnp.exp(-m_t[:, c1:c1 + 1]))
        hh = num / jnp.where(lm0, dn0, dn1)
        y = head_norm(hh) * gm[:, sl] * jax.nn.sigmoid(op)
        off = SSD_WIDTH + RET_WIDTH + LANES * p
        y_ref[:, off:off + LANES] = y.astype(y_ref.dtype)
    return m_t


def _half_rows():
    return lax.broadcasted_iota(jnp.int32, (LANES, HEAD_DIM), 0) < HEAD_DIM


class _CarriedState:
    def __init__(self, sv, nrow):
        self.sv, self.nrow = sv, nrow

    def step(self, idx, qp, eqp, kw, vp, decq, c0, c1, ml):
        lane = lax.broadcasted_iota(jnp.int32, (1, LANES), 1)
        lm0 = lane < HEAD_DIM
        decq = decq[0:1, :]
        s_old = self.sv[idx]
        sb = s_old.astype(BF16)
        q0 = jnp.where(lm0, qp, 0.0).astype(BF16)
        q1 = jnp.where(lm0, 0.0, qp).astype(BF16)
        carried = jnp.concatenate([_dot(q0, sb), _dot(q1, sb)], axis=1) * eqp
        kwb = kw.astype(BF16)
        u0 = _dot_tn(kwb, vp[:, :HEAD_DIM].astype(BF16))
        u1 = _dot_tn(kwb, vp[:, HEAD_DIM:].astype(BF16))
        top = _half_rows()
        dcol = jnp.where(top, decq[:, c0:c0 + 1], decq[:, c1:c1 + 1])
        self.sv[idx] = s_old * dcol + jnp.where(top, u0, u1)
        qn = None
        if ml:
            p = idx - 6
            n_old = self.nrow[p:p + 1, :]
            qn_l = qp * n_old
            qn = (jnp.sum(jnp.where(lm0, qn_l, 0.0), axis=-1, keepdims=True),
                  jnp.sum(jnp.where(lm0, 0.0, qn_l), axis=-1, keepdims=True))
            drow = jnp.where(lm0, decq[:, c0:c0 + 1], decq[:, c1:c1 + 1])
            self.nrow[p:p + 1, :] = n_old * drow + jnp.sum(kw, axis=0, keepdims=True)
        return carried, qn


def _mixer_prompt_kernel(proj_ref, cos_ref, sin_ref, ptab_ref, cw_ref, cb_ref, dsk_ref, gs_ref,
                         gr_ref, gm_ref, y_ref, sv_o, conv_o, n_o, m_o, sv, nrow, mrow, cbuf):
    ci = pl.program_id(1)
    rows = proj_ref.shape[1]

    @pl.when(ci == 0)
    def _():
        sv[...] = jnp.zeros_like(sv)
        nrow[...] = jnp.zeros_like(nrow)
        mrow[...] = jnp.zeros_like(mrow)
        cbuf[0:8, :] = jnp.zeros((8, SSD_XBC), F32)

    cbuf[8:8 + rows, :] = proj_ref[0, :, P_XBC:P_RET]
    us = [cbuf[pl.ds(5 + k, rows), :] for k in range(SSD_CONV)]
    ri = lax.broadcasted_iota(jnp.int32, (rows, rows), 0)
    cj = lax.broadcasted_iota(jnp.int32, (rows, rows), 1)
    mask = cj <= ri
    tril = jnp.where(mask, 1.0, 0.0).astype(BF16)
    m_t = _mixer_core(
        proj_ref[0, :, P_Z:P_XBC], us, proj_ref[0, :, P_RET:P_ML], proj_ref[0, :, P_ML:P_SM],
        proj_ref[0, :, P_SM:P_W], cos_ref[...], sin_ref[...], ptab_ref[...], cw_ref[...], cb_ref[...],
        dsk_ref[...], gs_ref[...], gr_ref[...], gm_ref[...],
        mask, tril, None, rows, mrow[0:1, :], lambda a: a[rows - 1:rows, :],
        _CarriedState(sv, nrow), y_ref.at[0])
    mrow[0:1, :] = m_t[rows - 1:rows, :]
    cbuf[0:8, :] = cbuf[rows:rows + 8, :]

    @pl.when(ci == pl.num_programs(1) - 1)
    def _():
        sv_o[0] = sv[...]
        conv_o[0] = cbuf[0:8, :]
        n_o[0] = nrow[...]
        m_o[0] = mrow[...]


def _mixer_prompt(proj, nb, t, cos, sin, consts):
    rows = math.gcd(t, CHUNK)
    nc = t // rows
    outs = pl.pallas_call(
        _mixer_prompt_kernel,
        grid=(nb, nc),
        in_specs=[pl.BlockSpec((1, rows, P_W), lambda b, c: (0, b * nc + c, 0)),
                  pl.BlockSpec((rows, RET_WIDTH), lambda b, c: (c, 0)),
                  pl.BlockSpec((rows, RET_WIDTH), lambda b, c: (c, 0))] + [_const_spec(a) for a in consts],
        out_specs=[pl.BlockSpec((1, rows, D_MODEL), lambda b, c: (b, c, 0)),
                   pl.BlockSpec((1, N_PAIRS, LANES, HEAD_DIM), lambda b, c: (b, 0, 0, 0)),
                   pl.BlockSpec((1, 8, SSD_XBC), lambda b, c: (b, 0, 0)),
                   pl.BlockSpec((1, 8, LANES), lambda b, c: (b, 0, 0)),
                   pl.BlockSpec((1, 8, LANES), lambda b, c: (b, 0, 0))],
        out_shape=[jax.ShapeDtypeStruct((nb, t, D_MODEL), BF16),
                   jax.ShapeDtypeStruct((nb, N_PAIRS, LANES, HEAD_DIM), F32),
                   jax.ShapeDtypeStruct((nb, 8, SSD_XBC), F32),
                   jax.ShapeDtypeStruct((nb, 8, LANES), F32),
                   jax.ShapeDtypeStruct((nb, 8, LANES), F32)],
        scratch_shapes=[pltpu.VMEM((N_PAIRS, LANES, HEAD_DIM), F32),
                        pltpu.VMEM((8, LANES), F32),
                        pltpu.VMEM((8, LANES), F32),
                        pltpu.VMEM((rows + 8, SSD_XBC), F32)],
        compiler_params=_cparams(("arbitrary", "arbitrary")),
        name="mixer_prompt",
    )(proj[None], cos, sin, *consts)
    y, sv, conv, n, m = outs
    sv = sv.reshape(nb, 2 * N_PAIRS, HEAD_DIM, HEAD_DIM)
    states = (sv[:, :8], conv[:, 5:8], sv[:, 8:12], sv[:, 12:16],
              n[:, 0:2].reshape(nb, ML_HEADS, HEAD_DIM), m[:, 0, C_ML:C_ML + 4])
    return y, states


class _PerSequenceState:
    def __init__(self, s_in, s_out, n_in, n_out, q_s, e_s, k_s, v_s, d_s, o_s, qn_s):
        self.s_in, self.s_out, self.n_in, self.n_out = s_in, s_out, n_in, n_out
        self.q_s, self.e_s, self.k_s, self.v_s, self.d_s, self.o_s, self.qn_s = q_s, e_s, k_s, v_s, d_s, o_s, qn_s

    @staticmethod
    def _pair_ref(refs, idx):
        return (refs[0], idx) if idx < 4 else ((refs[1], idx - 4) if idx < 6 else (refs[2], idx - 6))

    def step(self, idx, qp, eqp, kw, vp, decq, c0, c1, ml):
        lane = lax.broadcasted_iota(jnp.int32, (1, LANES), 1)
        lm0 = lane < HEAD_DIM
        sin_ref, pi = self._pair_ref(self.s_in, idx)
        sout_ref, _ = self._pair_ref(self.s_out, idx)
        self.q_s[...] = qp
        self.e_s[...] = eqp
        self.k_s[...] = kw
        self.v_s[...] = vp
        self.d_s[...] = jnp.where(lm0, decq[:, c0:c0 + 1], decq[:, c1:c1 + 1])
        top = _half_rows()
        nseq = self.q_s.shape[0] // S_ROWS

        def body(b, carry):
            r0 = pl.multiple_of(b * S_ROWS, S_ROWS)
            rs = pl.ds(r0, S_ROWS)
            s_old = sin_ref[b, pi]
            sb = s_old.astype(BF16)
            q = self.q_s[rs, :]
            q0 = jnp.where(lm0, q, 0.0)
            q1 = jnp.where(lm0, 0.0, q)
            carried = jnp.concatenate([_dot(q0.astype(BF16), sb), _dot(q1.astype(BF16), sb)], axis=1)
            self.o_s[rs, :] = carried * self.e_s[rs, :]
            kwb = self.k_s[rs, :]
            v = self.v_s[rs, :]
            u0 = _dot_tn(kwb.astype(BF16), v[:, :HEAD_DIM].astype(BF16))
            u1 = _dot_tn(kwb.astype(BF16), v[:, HEAD_DIM:].astype(BF16))
            drow = self.d_s[pl.ds(r0, 1), :]
            dcol = jnp.where(top, drow[:, 0:1], drow[:, HEAD_DIM:HEAD_DIM + 1])
            sout_ref[b, pi] = s_old * dcol + jnp.where(top, u0, u1)
            if ml:
                p = idx - 6
                n_old = self.n_in[b, p]
                qn_l = q * n_old
                qn0 = jnp.sum(jnp.where(lm0, qn_l, 0.0), axis=-1, keepdims=True)
                qn1 = jnp.sum(jnp.where(lm0, 0.0, qn_l), axis=-1, keepdims=True)
                self.qn_s[rs, :] = jnp.where(lm0, qn0, qn1)
                self.n_out[b, p] = n_old * drow + jnp.sum(kwb, axis=0, keepdims=True)
            return carry

        lax.fori_loop(0, nseq, body, 0)
        qn = None
        if ml:
            qn = (self.qn_s[:, 0:1], self.qn_s[:, HEAD_DIM:HEAD_DIM + 1])
        return self.o_s[...], qn


def _mixer_sample_kernel(proj_ref, tail_ref, mprev_ref, cos_ref, sin_ref, ssd_in, ret_in, c_in, n_in,
                         ptab_ref, cw_ref, cb_ref, dsk_ref, gs_ref, gr_ref, gm_ref,
                         y_ref, u_o, m_o, ssd_out, ret_out, c_out, n_out,
                         q_s, e_s, k_s, v_s, d_s, o_s, qn_s):
    s_in, s_out = (ssd_in, ret_in, c_in), (ssd_out, ret_out, c_out)
    rows = proj_ref.shape[0]
    ri = lax.broadcasted_iota(jnp.int32, (rows, rows), 0)
    cj = lax.broadcasted_iota(jnp.int32, (rows, rows), 1)
    tj = cj & (S_ROWS - 1)
    same_seq = (ri & ~(S_ROWS - 1)) == (cj & ~(S_ROWS - 1))
    mask = same_seq & (tj >= S_FIRST) & (tj <= S_LAST) & (cj <= ri)
    tril = jnp.where(mask, 1.0, 0.0).astype(BF16)
    last_sel = jnp.where(cj == ((ri & ~(S_ROWS - 1)) + S_LAST), 1.0, 0.0).astype(BF16)
    tr = lax.broadcasted_iota(jnp.int32, (rows, 1), 0) & (S_ROWS - 1)
    valid = (tr >= S_FIRST) & (tr <= S_LAST)
    u_full = jnp.where(tr < S_FIRST, tail_ref[...], proj_ref[:, P_XBC:P_RET])
    u_o[...] = u_full
    us = [pltpu.roll(u_full, SSD_CONV - 1 - k, 0) for k in range(SSD_CONV - 1)] + [u_full]
    st = _PerSequenceState(s_in, s_out, n_in, n_out, q_s, e_s, k_s, v_s, d_s, o_s, qn_s)
    m_o[...] = _mixer_core(
        proj_ref[:, P_Z:P_XBC], us, proj_ref[:, P_RET:P_ML], proj_ref[:, P_ML:P_SM], proj_ref[:, P_SM:P_W],
        cos_ref[...], sin_ref[...], ptab_ref[...], cw_ref[...], cb_ref[...], dsk_ref[...], gs_ref[...],
        gr_ref[...], gm_ref[...], mask, tril, valid, S_ROWS, mprev_ref[...],
        lambda a: _dot01(last_sel, a), st, y_ref)


def _mixer_sample(proj, row0, tail, mprev, cos, sin, ssd, ret, mlc, mln, consts):
    n_rows = tail.shape[0]
    rows = S_BLOCK * S_ROWS
    nseq = n_rows // S_ROWS
    off = row0 // rows

    def rowspec(w):
        return pl.BlockSpec((rows, w), lambda i: (i, 0))

    def sspec(a):
        return pl.BlockSpec((S_BLOCK,) + a.shape[1:], lambda i: (i, 0, 0, 0))

    states = (ssd, ret, mlc, mln)
    return pl.pallas_call(
        _mixer_sample_kernel,
        grid=(n_rows // rows,),
        in_specs=[pl.BlockSpec((rows, P_W), lambda i: (i + off, 0)), rowspec(SSD_XBC), rowspec(LANES),
                  _const_spec(cos), _const_spec(sin)] + [sspec(a) for a in states]
                 + [_const_spec(a) for a in consts],
        out_specs=[rowspec(D_MODEL), rowspec(SSD_XBC), rowspec(LANES)] + [sspec(a) for a in states],
        out_shape=[jax.ShapeDtypeStruct((n_rows, D_MODEL), BF16),
                   jax.ShapeDtypeStruct((n_rows, SSD_XBC), F32),
                   jax.ShapeDtypeStruct((n_rows, LANES), F32)]
                  + [jax.ShapeDtypeStruct(a.shape, F32) for a in states],
        scratch_shapes=[pltpu.VMEM((rows, LANES), F32) for _ in range(7)],
        compiler_params=_cparams(("arbitrary",)),
        name="mixer_sample",
    )(proj, tail, mprev, cos, sin, *states, *consts)


def _route(logits):
    lane = lax.broadcasted_iota(jnp.int32, logits.shape, 1)
    gmask = (lane >= N_EXPERTS) & (lane < N_EXPERTS + EXPERT_GROUPS)
    gl = jnp.where(gmask, logits, -jnp.inf)
    ge = jnp.exp(gl - jnp.max(gl, axis=-1, keepdims=True))
    gprob = ge / jnp.sum(ge, axis=-1, keepdims=True)
    g_w = jnp.max(gprob, axis=-1, keepdims=True)
    g_idx = jnp.min(jnp.where(gmask & (gprob == g_w), lane - N_EXPERTS, LANES), axis=-1, keepdims=True)
    emask = (lane < N_EXPERTS) & ((lane >> 3) == g_idx)
    el = jnp.where(emask, logits, -jnp.inf)
    ee = jnp.exp(el - jnp.max(el, axis=-1, keepdims=True))
    eprob = ee / jnp.sum(ee, axis=-1, keepdims=True)
    p1 = jnp.max(jnp.where(emask, eprob, -1.0), axis=-1, keepdims=True)
    i1 = jnp.min(jnp.where(emask & (eprob == p1), lane, LANES), axis=-1, keepdims=True)
    rest = emask & (lane != i1)
    p2 = jnp.max(jnp.where(rest, eprob, -1.0), axis=-1, keepdims=True)
    i2 = jnp.min(jnp.where(rest & (eprob == p2), lane, LANES), axis=-1, keepdims=True)
    tot = p1 + p2
    w1, w2 = g_w * (p1 / tot), g_w * (p2 / tot)
    key = g_idx * 64 + (jnp.minimum(i1, i2) & 7) * 8 + (jnp.maximum(i1, i2) & 7)
    cols = (w1, w2, i1.astype(F32), i2.astype(F32), key.astype(F32))
    out = jnp.zeros(logits.shape, F32)
    for c, v in enumerate(cols):
        out = jnp.where(lane == c, v, out)
    return out


def _outproj_kernel(y_ref, x_ref, gtp, gts, scp, scs, shp, shs, g_ref, w_ref, wr_ref, br_ref,
                    x1_ref, h2_ref, route_ref, *, np_tiles):
    is_p = pl.program_id(0) < np_tiles
    x1 = x_ref[...] + _mod_val(gtp, gts, is_p) * _dot(y_ref[...], w_ref[...])
    x1_ref[...] = x1
    h2 = _rms(x1) * g_ref[...] * (1.0 + _mod_val(scp, scs, is_p)) + _mod_val(shp, shs, is_p)
    h2b = h2.astype(BF16)
    h2_ref[...] = h2b.astype(F32)
    route_ref[...] = _route(_dot(h2b, wr_ref[...]) + br_ref[...])


def _outproj(y, x, gt, sc, sh, g, w, wr, br, n_prompt, tp):
    n = x.shape[0]
    tm = TOKEN_TILE
    ms = _mod_specs(tm, n_prompt, tp)
    return pl.pallas_call(
        functools.partial(_outproj_kernel, np_tiles=n_prompt // tm),
        grid=(n // tm,),
        in_specs=[_row_spec(tm, D_MODEL), _row_spec(tm, D_MODEL)] + ms + ms + ms
                 + [_const_spec(g), _const_spec(w), _const_spec(wr), _const_spec(br)],
        out_specs=[_row_spec(tm, D_MODEL), _row_spec(tm, D_MODEL), _row_spec(tm, LANES)],
        out_shape=[jax.ShapeDtypeStruct((n, D_MODEL), F32),
                   jax.ShapeDtypeStruct((n, D_MODEL), F32),
                   jax.ShapeDtypeStruct((n, LANES), F32)],
        compiler_params=_cparams(("arbitrary",)),
        name="outproj_router",
    )(y, x, *gt, *sc, *sh, g, w, wr, br)


def _gather_tiles(idx_ref, src_hbm, buf, sem, tm):
    i, n = pl.program_id(0), pl.num_programs(0)

    def issue(tile, slot):
        def body(r, c):
            row = idx_ref[tile * tm + r]
            pltpu.make_async_copy(src_hbm.at[pl.ds(row, 1), :], buf.at[slot, pl.ds(r, 1), :],
                                  sem.at[slot]).start()
            return c
        lax.fori_loop(0, tm, body, 0, unroll=8)

    @pl.when(i == 0)
    def _():
        issue(0, 0)

    @pl.when(i + 1 < n)
    def _():
        issue(i + 1, (i + 1) % 2)

    slot = i % 2
    pltpu.make_async_copy(src_hbm.at[pl.ds(0, tm), :], buf.at[slot], sem.at[slot]).wait()
    return slot


def _gather_kernel(idx_ref, src_hbm, o_ref, buf, sem):
    slot = _gather_tiles(idx_ref, src_hbm, buf, sem, o_ref.shape[0])
    o_ref[...] = buf[slot]


def _gather_rows(src, idx):
    n, width = idx.shape[0], src.shape[1]
    tm = MOE_TILE
    return pl.pallas_call(
        _gather_kernel,
        grid_spec=pltpu.PrefetchScalarGridSpec(
            num_scalar_prefetch=1, grid=(n // tm,),
            in_specs=[pl.BlockSpec(memory_space=pl.ANY)],
            out_specs=_row_spec(tm, width),
            scratch_shapes=[pltpu.VMEM((2, tm, width), src.dtype), pltpu.SemaphoreType.DMA((2,))]),
        out_shape=jax.ShapeDtypeStruct((n, width), src.dtype),
        compiler_params=_cparams(("arbitrary",)),
        name="gather_rows",
    )(idx, src)


def _experts_kernel(tile_ref, grp_ref, flag_ref, h_ref, r_ref, wgu_ref, wd_ref, o_ref):
    j = pl.program_id(0)
    flags = flag_ref[j]
    valid = (flags & 1) != 0

    @pl.when(valid & ((flags & 2) != 0))
    def _():
        o_ref[...] = jnp.zeros_like(o_ref)

    @pl.when(valid)
    def _():
        h = h_ref[...].astype(BF16)
        r = r_ref[...]
        w1, w2, i1, i2 = r[:, 0:1], r[:, 1:2], r[:, 2:3], r[:, 3:4]
        base = grp_ref[j] * EXPERTS_PER_GROUP
        for e in range(EXPERTS_PER_GROUP):
            @pl.when(((flags >> (8 + e)) & 1) != 0)
            def _():
                eid = (base + e).astype(F32)
                ge = jnp.where(i1 == eid, w1, 0.0) + jnp.where(i2 == eid, w2, 0.0)
                au = _dot(h, wgu_ref[0, e])
                act = _silu(au[:, :EXPERT_FF]) * au[:, EXPERT_FF:]
                o_ref[...] += ge * _dot(act.astype(BF16), wd_ref[0, e])


def _experts(h_sorted, r_sorted, tile, grp, flags, wgu, wd):
    n = h_sorted.shape[0]
    tm = MOE_TILE
    return pl.pallas_call(
        _experts_kernel,
        grid_spec=pltpu.PrefetchScalarGridSpec(
            num_scalar_prefetch=3, grid=(tile.shape[0],),
            in_specs=[pl.BlockSpec((tm, D_MODEL), lambda j, t, g, f: (t[j], 0)),
                      pl.BlockSpec((tm, 8), lambda j, t, g, f: (t[j], 0)),
                      pl.BlockSpec((1, EXPERTS_PER_GROUP, D_MODEL, 2 * EXPERT_FF), lambda j, t, g, f: (g[j], 0, 0, 0)),
                      pl.BlockSpec((1, EXPERTS_PER_GROUP, EXPERT_FF, D_MODEL), lambda j, t, g, f: (g[j], 0, 0, 0))],
            out_specs=pl.BlockSpec((tm, D_MODEL), lambda j, t, g, f: (t[j], 0))),
        out_shape=jax.ShapeDtypeStruct((n, D_MODEL), F32),
        compiler_params=_cparams(("arbitrary",)),
        name="moe_experts",
    )(tile, grp, flags, h_sorted, r_sorted, wgu, wd)


def _combine_kernel(pos_ref, y_hbm, x1_ref, gtp, gts, gf_ref, o_ref, buf, sem, *, np_tiles, final):
    slot = _gather_tiles(pos_ref, y_hbm, buf, sem, o_ref.shape[0])
    is_p = pl.program_id(0) < np_tiles
    x2 = x1_ref[...] + _mod_val(gtp, gts, is_p) * buf[slot]
    o_ref[...] = _rms(x2) * gf_ref[...] if final else x2


def _combine(y_sorted, pos, x1, gt, gf, n_prompt, tp, final):
    n = x1.shape[0]
    tm = MOE_TILE
    return pl.pallas_call(
        functools.partial(_combine_kernel, np_tiles=n_prompt // tm, final=final),
        grid_spec=pltpu.PrefetchScalarGridSpec(
            num_scalar_prefetch=1, grid=(n // tm,),
            in_specs=[pl.BlockSpec(memory_space=pl.ANY), _row_spec(tm, D_MODEL)]
                     + _mod_specs(tm, n_prompt, tp) + [_const_spec(gf)],
            out_specs=_row_spec(tm, D_MODEL),
            scratch_shapes=[pltpu.VMEM((2, tm, D_MODEL), F32), pltpu.SemaphoreType.DMA((2,))]),
        out_shape=jax.ShapeDtypeStruct((n, D_MODEL), F32),
        compiler_params=_cparams(("arbitrary",)),
        name="moe_combine",
    )(pos, y_sorted, x1, *gt, gf)


def _work_items(key_sorted, i1s, i2s, n_tiles, tm):
    n_items = n_tiles + EXPERT_GROUPS - 1
    gs = key_sorted >> 6
    gf, gl = gs[::tm], gs[tm - 1::tm]
    cnt = gl - gf + 1
    start = jnp.cumsum(cnt) - cnt
    total = jnp.sum(cnt)
    j = jnp.arange(n_items, dtype=jnp.int32)
    valid = j < total
    tile = jnp.clip(jnp.searchsorted(start, j, side="right").astype(jnp.int32) - 1, 0, n_tiles - 1)
    tile = jnp.where(valid, tile, n_tiles - 1)
    grp = jnp.where(valid, gf[tile] + (j - start[tile]), gl[n_tiles - 1])
    first = valid & (j == start[tile])
    ex = jnp.arange(N_EXPERTS, dtype=jnp.int32)
    present = ((i1s[:, None] == ex) | (i2s[:, None] == ex)).reshape(n_tiles, tm, N_EXPERTS).any(axis=1)
    pres = present[tile].reshape(n_items, EXPERT_GROUPS, EXPERTS_PER_GROUP)[j, grp]
    bits = jnp.sum(pres.astype(jnp.int32) << (8 + jnp.arange(EXPERTS_PER_GROUP, dtype=jnp.int32)), axis=1)
    flags = valid.astype(jnp.int32) | (first.astype(jnp.int32) << 1) | bits
    return tile, grp.astype(jnp.int32), flags


def _moe(h2, route, x1, gt, gf, wgu, wd, n_prompt, tp, final):
    n = h2.shape[0]
    tm = MOE_TILE
    key = route[:, 4].astype(jnp.int32)
    perm = jnp.argsort(key).astype(jnp.int32)
    pos = jnp.zeros((n,), jnp.int32).at[perm].set(jnp.arange(n, dtype=jnp.int32))
    r_sorted = route[:, :8][perm]
    tile, grp, flags = _work_items(key[perm], r_sorted[:, 2].astype(jnp.int32),
                                   r_sorted[:, 3].astype(jnp.int32), n // tm, tm)
    y_sorted = _experts(_gather_rows(h2, perm), r_sorted, tile, grp, flags, wgu, wd)
    return _combine(y_sorted, pos, x1, gt, gf, n_prompt, tp, final)


def _rope_tables(pos):
    half = HEAD_DIM // 2
    inv = ROPE_BASE ** (-jnp.arange(half, dtype=F32) / half)
    ang = pos.astype(F32)[:, None] * inv[None, :]
    cos, sin = jnp.cos(ang), jnp.sin(ang)
    cos_t = jnp.tile(jnp.concatenate([cos, cos], axis=-1), (1, RET_HEADS))
    sin_t = jnp.tile(jnp.concatenate([-sin, sin], axis=-1), (1, RET_HEADS))
    return cos_t, sin_t


def _layer_consts(l, w_in, conv_w, conv_b, dt_bias, a_log, d_skip, g_ssd_norm, g_ret_norm,
                  b_mlstm_i, b_mlstm_f, g_mlstm_norm):
    w = w_in[l]
    w_p = jnp.concatenate([w[:, 0:1280], w[:, 1288:3336], w[:, 1280:1288], w[:, 3336:3344],
                           jnp.zeros((D_MODEL, LANES - 16), F32)], axis=1).astype(BF16)
    pad = lambda v, n: jnp.concatenate([v, jnp.zeros((n,), F32)])
    log_gamma = jnp.log(1.0 - 2.0 ** (-5.0 - jnp.arange(RET_HEADS, dtype=F32)))
    ptab = jnp.stack([pad(jnp.concatenate([dt_bias[l], b_mlstm_i[l], b_mlstm_f[l]]), LANES - 16),
                      pad(a_log[l], LANES - 8),
                      pad(jnp.concatenate([jnp.zeros((8,), F32), log_gamma]), LANES - 12)]
                     + [jnp.zeros((LANES,), F32)] * 5)
    consts = (ptab, conv_w[l], conv_b[l][None, :], jnp.repeat(d_skip[l], HEAD_DIM)[None, :],
              g_ssd_norm[l][None, :], g_ret_norm[l][None, :], g_mlstm_norm[l][None, :])
    return w_p, consts


def kernel(x_prompt, x_sample, state_ssd, state_ssd_conv, state_ret, state_mlstm_c, state_mlstm_n,
           state_mlstm_m, c_prompt, c_sample, w_ada, b_ada, g_norm1, g_norm2, w_in, conv_w, conv_b,
           dt_bias, a_log, d_skip, g_ssd_norm, g_ret_norm, b_mlstm_i, b_mlstm_f, g_mlstm_norm, w_out,
           w_router_group, b_router_group, w_router_expert, b_router_expert, w_gate_up, w_down, g_final):
    bp, tp, _ = x_prompt.shape
    bs, ts, _ = x_sample.shape
    assert ts == S_LAST - S_FIRST + 1
    n_srows = bs * S_ROWS

    n_prompt = bp * tp
    mod = _ada(jnp.concatenate([c_prompt, c_sample], axis=0), w_ada, b_ada)

    def mods(l):
        m = mod[l].reshape(bp + bs, 6, D_MODEL)
        return [(m[:bp, k][:, None, :], jnp.repeat(m[bp:, k], S_ROWS, axis=0)) for k in range(6)]

    cos_p, sin_p = _rope_tables(jnp.arange(tp, dtype=jnp.int32))
    t8 = jnp.clip(jnp.arange(S_ROWS, dtype=jnp.int32) - S_FIRST, 0, ts - 1)
    cos_s, sin_s = _rope_tables(PAST_LEN + t8)
    cos_s, sin_s = jnp.tile(cos_s, (S_BLOCK, 1)), jnp.tile(sin_s, (S_BLOCK, 1))

    xs = jnp.pad(x_sample, ((0, 0), (S_FIRST, S_ROWS - 1 - S_LAST), (0, 0))).reshape(n_srows, D_MODEL)
    x = jnp.concatenate([x_prompt.reshape(n_prompt, D_MODEL), xs], axis=0)
    p_states, s_states = [], []
    for l in range(DEPTH):
        w_p, consts = _layer_consts(l, w_in, conv_w, conv_b, dt_bias, a_log, d_skip, g_ssd_norm,
                                    g_ret_norm, b_mlstm_i, b_mlstm_f, g_mlstm_norm)
        w_o = w_out[l].astype(BF16)
        wr = jnp.concatenate([w_router_expert[l], w_router_group[l],
                              jnp.zeros((D_MODEL, LANES - N_EXPERTS - EXPERT_GROUPS), F32)], axis=1).astype(BF16)
        br = jnp.concatenate([b_router_expert[l], b_router_group[l],
                              jnp.zeros((LANES - N_EXPERTS - EXPERT_GROUPS,), F32)])[None, :]
        wgu = w_gate_up[l].astype(BF16).reshape(EXPERT_GROUPS, EXPERTS_PER_GROUP, D_MODEL, 2 * EXPERT_FF)
        wd = w_down[l].astype(BF16).reshape(EXPERT_GROUPS, EXPERTS_PER_GROUP, EXPERT_FF, D_MODEL)
        g1, g2, gf = g_norm1[l][None, :], g_norm2[l][None, :], g_final[None, :]
        final = l == DEPTH - 1

        sh1, sc1, gt1, sh2, sc2, gt2 = mods(l)
        proj = _inproj(x, sc1, sh1, g1, w_p, n_prompt, tp)

        ycat_p, st = _mixer_prompt(proj, bp, tp, cos_p, sin_p, consts)
        p_states.append(st)

        tail = jnp.pad(state_ssd_conv[l], ((0, 0), (0, S_ROWS - SSD_CONV + 1), (0, 0))).reshape(n_srows, SSD_XBC)
        mprev = jnp.pad(jnp.repeat(state_mlstm_m[l], S_ROWS, axis=0), ((0, 0), (C_ML, LANES - C_ML - ML_HEADS)))
        ycat_s, u_full, m_rows, ssd_n, ret_n, c_n, n_n = _mixer_sample(
            proj, n_prompt, tail, mprev, cos_s, sin_s,
            state_ssd[l].reshape(bs, 4, LANES, HEAD_DIM), state_ret[l].reshape(bs, 2, LANES, HEAD_DIM),
            state_mlstm_c[l].reshape(bs, 2, LANES, HEAD_DIM), state_mlstm_n[l].reshape(bs, 2, 1, LANES), consts)
        s_states.append((ssd_n.reshape(state_ssd.shape[1:]),
                         u_full.reshape(bs, S_ROWS, SSD_XBC)[:, S_LAST - 2:S_LAST + 1],
                         ret_n.reshape(state_ret.shape[1:]), c_n.reshape(state_mlstm_c.shape[1:]),
                         n_n.reshape(bs, ML_HEADS, HEAD_DIM),
                         m_rows.reshape(bs, S_ROWS, LANES)[:, S_LAST, C_ML:C_ML + ML_HEADS]))

        ycat = jnp.concatenate([ycat_p.reshape(n_prompt, D_MODEL), ycat_s], axis=0)
        x1, h2, route = _outproj(ycat, x, gt1, sc2, sh2, g2, w_o, wr, br, n_prompt, tp)
        x = _moe(h2, route, x1, gt2, gf, wgu, wd, n_prompt, tp, final)

    y_prompt = x[:n_prompt].reshape(bp, tp, D_MODEL)
    y_sample = x[n_prompt:].reshape(bs, S_ROWS, D_MODEL)[:, S_FIRST:S_LAST + 1]
    p_st = [jnp.stack([s[i] for s in p_states], axis=0) for i in range(6)]
    s_st = [jnp.stack([s[i] for s in s_states], axis=0) for i in range(6)]
    return (y_prompt, y_sample, *p_st, *s_st)
```

```python
import functools
import math

import jax
import jax.numpy as jnp
from jax import lax
from jax.experimental import pallas as pl
from jax.experimental.pallas import tpu as pltpu

F32 = jnp.float32
BF16 = jnp.bfloat16
I32 = jnp.int32

D_MODEL = 1024
DEPTH = 2
PAST_LEN = 16384
SSD_HEADS = 8
SSD_WIDTH = 512
SSD_GROUPS = 2
SSD_STATE = 64
SSD_CONV = 4
SSD_XBC = 768
RET_HEADS = 4
RET_WIDTH = 256
ML_HEADS = 4
ML_WIDTH = 256
HEAD_DIM = 64
ROPE_BASE = 10000.0
EPS = 1e-6
EXPERT_GROUPS = 4
EXPERTS_PER_GROUP = 8
N_EXPERTS = 32
EXPERT_FF = 256

LANES = 128
CHUNK = 128
SEQ_PER_STEP = 2
P_Z, P_XBC, P_RET, P_ML, P_SM, P_W = 0, 512, 1280, 2304, 3328, 3456
C_SSD, C_RET, C_ML = 0, 8, 12
N_PAIRS = 8
NEG = -1e30
TOKEN_TILE = 512
MOE_TILE = 256
VMEM_LIMIT = 56 * 1024 * 1024
S_ROWS, S_FIRST, S_LAST = 8, 3, 6
S_BLOCK = 16
S_UNROLL = 4
H_EXT = D_MODEL + LANES
N_CLASS = 256


def _cparams(sem):
    return pltpu.CompilerParams(dimension_semantics=sem, vmem_limit_bytes=VMEM_LIMIT)


def _split3(x):
    x1 = x.astype(BF16)
    r = x - x1.astype(F32)
    x2 = r.astype(BF16)
    r = r - x2.astype(F32)
    return x1, x2, r.astype(BF16)


def _dot01(m01, x):
    return sum(jnp.dot(m01, p, preferred_element_type=F32) for p in _split3(x))


def _dot(a, b):
    return jnp.dot(a, b, preferred_element_type=F32)


def _dot_nt(a, b):
    return lax.dot_general(a, b, (((1,), (1,)), ((), ())), preferred_element_type=F32)


def _dot_tn(a, b):
    return lax.dot_general(a, b, (((0,), (0,)), ((), ())), preferred_element_type=F32)


def _softplus(x):
    return jnp.maximum(x, 0.0) + jnp.log1p(jnp.exp(-jnp.abs(x)))


def _silu(x):
    return x * jax.nn.sigmoid(x)


def _rms(x):
    return x * lax.rsqrt(jnp.mean(x * x, axis=-1, keepdims=True) + EPS)


def _cummax_rows(x, seg):
    t = lax.broadcasted_iota(I32, x.shape, 0) & (seg - 1)
    s = 1
    while s < seg:
        x = jnp.maximum(x, jnp.where(t >= s, pltpu.roll(x, s, 0), NEG))
        s *= 2
    return x


def _rope(x, cos, sin_signed, lane):
    swapped = jnp.where((lane & 63) < 32, pltpu.roll(x, 96, 1), pltpu.roll(x, 32, 1))
    return x * cos + swapped * sin_signed


def _row_spec(tm, width):
    return pl.BlockSpec((tm, width), lambda i, *_: (i, 0))


def _const_spec(a):
    nd = a.ndim
    return pl.BlockSpec(a.shape, lambda *_: (0,) * nd)


def _layer_spec(a, l):
    nd = a.ndim
    return pl.BlockSpec((1,) + a.shape[1:], lambda *_: (l,) + (0,) * (nd - 1))


def _dual_specs(tm, np_tiles, width):
    return [pl.BlockSpec((tm, width), lambda i, *_: (jnp.minimum(i, np_tiles - 1), 0)),
            pl.BlockSpec((tm, width), lambda i, *_: (jnp.maximum(i - np_tiles, 0), 0))]


def _mod_specs(tm, n_prompt, tp):
    np_tiles, last_b = n_prompt // tm, n_prompt // tp - 1
    return [pl.BlockSpec((1, 1, D_MODEL), lambda i, *_: (jnp.minimum(i * tm // tp, last_b), 0, 0)),
            pl.BlockSpec((tm, D_MODEL), lambda i, *_: (jnp.maximum(i - np_tiles, 0), 0))]


def _mod_val(p_ref, s_ref, is_prompt):
    return jnp.where(is_prompt, p_ref[0], s_ref[...])


def _dual_val(p_ref, s_ref, is_prompt):
    return jnp.where(is_prompt, p_ref[...], s_ref[...])


def _ada_kernel(c_ref, w_ref, b_ref, o_ref):
    c = c_ref[...]
    o_ref[0] = _dot(_silu(c).astype(BF16), w_ref[0].astype(BF16)) + b_ref[0]


def _ada(c_all, w_ada, b_ada):
    nb = c_all.shape[0]
    tn = 1536
    return pl.pallas_call(
        _ada_kernel,
        grid=(DEPTH, 6 * D_MODEL // tn),
        in_specs=[pl.BlockSpec((nb, D_MODEL), lambda l, j: (0, 0)),
                  pl.BlockSpec((1, D_MODEL, tn), lambda l, j: (l, 0, j)),
                  pl.BlockSpec((1, 1, tn), lambda l, j: (l, 0, j))],
        out_specs=pl.BlockSpec((1, nb, tn), lambda l, j: (l, 0, j)),
        out_shape=jax.ShapeDtypeStruct((DEPTH, nb, 6 * D_MODEL), F32),
        compiler_params=_cparams(("arbitrary", "arbitrary")),
        name="ada_mod",
    )(c_all, w_ada, b_ada.reshape(DEPTH, 1, 6 * D_MODEL))


def _inproj_kernel(xp, xs, scp, scs, shp, shs, g_ref, w_ref, o_ref, *, np_tiles):
    is_p = pl.program_id(0) < np_tiles
    h = _rms(_dual_val(xp, xs, is_p)) * g_ref[0] * (1.0 + _mod_val(scp, scs, is_p)) + _mod_val(shp, shs, is_p)
    o_ref[...] = _dot(h.astype(BF16), w_ref[0])


def _inproj(x, sc, sh, g, w, l, tp):
    n_prompt, ns = x[0].shape[0], x[1].shape[0]
    tm = TOKEN_TILE
    ms = _mod_specs(tm, n_prompt, tp)
    return pl.pallas_call(
        functools.partial(_inproj_kernel, np_tiles=n_prompt // tm),
        grid=((n_prompt + ns) // tm,),
        in_specs=_dual_specs(tm, n_prompt // tm, D_MODEL) + ms + ms + [_layer_spec(g, l), _layer_spec(w, l)],
        out_specs=_row_spec(tm, P_W),
        out_shape=jax.ShapeDtypeStruct((n_prompt + ns, P_W), F32),
        compiler_params=_cparams(("arbitrary",)),
        name="norm_inproj",
    )(*x, *sc, *sh, g, w)


def _mixer_core(z, us, retb, mlb, small, cos, sin, ptab, cw, cb, dsk, gs, gr, gm,
                mask, tril, valid, seg, mprev, last_fn, st, y_ref):
    rows = small.shape[0]
    lane = lax.broadcasted_iota(I32, (rows, LANES), 1)
    lm0 = lane < HEAD_DIM

    pre = small + ptab[0:1]
    a_neg = -jnp.exp(ptab[1:2])
    dt = _softplus(pre)
    logf = -_softplus(-pre)
    la = jnp.where(lane < C_RET, dt * a_neg,
                   jnp.where(lane < C_ML, ptab[2:3], jnp.where(lane < C_ML + 4, logf, 0.0)))
    cum = _dot01(tril, la)
    ic = pltpu.roll(pre, 4, 1)
    mlm = (lane >= C_ML) & (lane < C_ML + 4)
    d = jnp.where(mlm if valid is None else (mlm & valid), ic - cum, NEG)
    m_t = cum + jnp.maximum(mprev, _cummax_rows(d, seg))
    inter = jnp.exp(cum + mprev - m_t)
    xt = jnp.where(mlm, d, cum).T
    colv = cum - m_t
    cum_last, m_last = last_fn(cum), last_fn(m_t)
    wq = jnp.where(mlm, jnp.exp(ic + cum_last - cum - m_last), jnp.exp(cum_last - cum))
    if valid is not None:
        wq = jnp.where(valid, wq, 0.0)
    decq = jnp.where(mlm, jnp.exp(cum_last + mprev - m_last), jnp.exp(cum_last))
    eq = jnp.where(mlm, inter, jnp.exp(cum))

    def dec_mat(c, ml):
        lg = (colv[:, c:c + 1] + xt[c:c + 1, :]) if ml else (cum[:, c:c + 1] - xt[c:c + 1, :])
        return jnp.exp(jnp.where(mask, lg, -jnp.inf))

    def pair(idx, qp, kp, vp, c0, c1, ml):
        q0 = jnp.where(lm0, qp, 0.0)
        q1 = jnp.where(lm0, 0.0, qp)
        kb = kp.astype(BF16)
        s0 = _dot_nt(q0.astype(BF16), kb) * dec_mat(c0, ml)
        s1 = _dot_nt(q1.astype(BF16), kb) * dec_mat(c1, ml)
        v0 = jnp.where(lm0, vp, 0.0).astype(BF16)
        v1 = jnp.where(lm0, 0.0, vp).astype(BF16)
        intra = _dot(s0.astype(BF16), v0) + _dot(s1.astype(BF16), v1)
        eqp = jnp.where(lm0, eq[:, c0:c0 + 1], eq[:, c1:c1 + 1])
        kw = kp * jnp.where(lm0, wq[:, c0:c0 + 1], wq[:, c1:c1 + 1])
        carried, qn = st.step(idx, qp, eqp, kw, vp, decq, c0, c1, ml)
        return intra + carried, s0, s1, qn

    def head_norm(o):
        o2 = o * o
        ms0 = jnp.sum(jnp.where(lm0, o2, 0.0), axis=-1, keepdims=True) * (1.0 / HEAD_DIM)
        ms1 = jnp.sum(jnp.where(lm0, 0.0, o2), axis=-1, keepdims=True) * (1.0 / HEAD_DIM)
        return o * jnp.where(lm0, lax.rsqrt(ms0 + EPS), lax.rsqrt(ms1 + EPS))

    conv = cb + us[0] * cw[0:1] + us[1] * cw[1:2] + us[2] * cw[2:3] + us[3] * cw[3:4]
    xc = _silu(conv)
    bb = xc[:, SSD_WIDTH:SSD_WIDTH + LANES]
    cc = xc[:, SSD_WIDTH + LANES:SSD_WIDTH + 2 * LANES]
    br = pltpu.roll(bb, HEAD_DIM, 1)
    cr = pltpu.roll(cc, HEAD_DIM, 1)
    ys = []
    for p in range(4):
        if p < 2:
            kp, qp = jnp.where(lm0, bb, br), jnp.where(lm0, cc, cr)
        else:
            kp, qp = jnp.where(lm0, br, bb), jnp.where(lm0, cr, cc)
        c0, c1 = C_SSD + 2 * p, C_SSD + 2 * p + 1
        sl = slice(LANES * p, LANES * (p + 1))
        xsp = xc[:, sl]
        dtp = jnp.where(lm0, dt[:, c0:c0 + 1], dt[:, c1:c1 + 1])
        o, _, _, _ = pair(p, qp, kp, xsp * dtp, c0, c1, False)
        ys.append((o + dsk[:, sl] * xsp) * _silu(z[:, sl]))
    for g in range(SSD_GROUPS):
        ya, yb = ys[2 * g], ys[2 * g + 1]
        ms = (jnp.sum(ya * ya, axis=-1, keepdims=True)
              + jnp.sum(yb * yb, axis=-1, keepdims=True)) * (1.0 / (2 * LANES))
        r = lax.rsqrt(ms + EPS)
        for j, yv in ((2 * g, ya), (2 * g + 1, yb)):
            sl = slice(LANES * j, LANES * (j + 1))
            y_ref[:, sl] = (yv * r * gs[:, sl]).astype(y_ref.dtype)

    for p in range(2):
        sl = slice(LANES * p, LANES * (p + 1))
        qp = _rope(retb[:, LANES * p:LANES * (p + 1)], cos[:, sl], sin[:, sl], lane)
        kp = _rope(retb[:, RET_WIDTH + LANES * p:RET_WIDTH + LANES * (p + 1)], cos[:, sl], sin[:, sl], lane)
        kp = kp * (HEAD_DIM ** -0.5)
        vp = retb[:, 2 * RET_WIDTH + LANES * p:2 * RET_WIDTH + LANES * (p + 1)]
        gp = retb[:, 3 * RET_WIDTH + LANES * p:3 * RET_WIDTH + LANES * (p + 1)]
        c0, c1 = C_RET + 2 * p, C_RET + 2 * p + 1
        o, _, _, _ = pair(4 + p, qp, kp, vp, c0, c1, False)
        y = head_norm(o) * gr[:, sl] * _silu(gp)
        y_ref[:, SSD_WIDTH + LANES * p:SSD_WIDTH + LANES * (p + 1)] = y.astype(y_ref.dtype)

    for p in range(2):
        sl = slice(LANES * p, LANES * (p + 1))
        qp = mlb[:, LANES * p:LANES * (p + 1)]
        kp = mlb[:, ML_WIDTH + LANES * p:ML_WIDTH + LANES * (p + 1)] * (HEAD_DIM ** -0.5)
        vp = mlb[:, 2 * ML_WIDTH + LANES * p:2 * ML_WIDTH + LANES * (p + 1)]
        op = mlb[:, 3 * ML_WIDTH + LANES * p:3 * ML_WIDTH + LANES * (p + 1)]
        c0, c1 = C_ML + 2 * p, C_ML + 2 * p + 1
        num, s0, s1, (qn0, qn1) = pair(6 + p, qp, kp, vp, c0, c1, True)
        den0 = jnp.sum(s0, axis=-1, keepdims=True) + qn0 * eq[:, c0:c0 + 1]
        den1 = jnp.sum(s1, axis=-1, keepdims=True) + qn1 * eq[:, c1:c1 + 1]
        dn0 = jnp.maximum(jnp.abs(den0), jnp.exp(-m_t[:, c0:c0 + 1]))
        dn1 = jnp.maximum(jnp.abs(den1), jnp.exp(-m_t[:, c1:c1 + 1]))
        hh = num / jnp.where(lm0, dn0, dn1)
        y = head_norm(hh) * gm[:, sl] * jax.nn.sigmoid(op)
        off = SSD_WIDTH + RET_WIDTH + LANES * p
        y_ref[:, off:off + LANES] = y.astype(y_ref.dtype)
    return m_t


def _half_rows():
    return lax.broadcasted_iota(I32, (LANES, HEAD_DIM), 0) < HEAD_DIM


class _CarriedState:
    def __init__(self, sv, nrow):
        self.sv, self.nrow = sv, nrow

    def step(self, idx, qp, eqp, kw, vp, decq, c0, c1, ml):
        lane = lax.broadcasted_iota(I32, (1, LANES), 1)
        lm0 = lane < HEAD_DIM
        decq = decq[0:1, :]
        s_old = self.sv[idx]
        sb = s_old.astype(BF16)
        q0 = jnp.where(lm0, qp, 0.0).astype(BF16)
        q1 = jnp.where(lm0, 0.0, qp).astype(BF16)
        carried = jnp.concatenate([_dot(q0, sb), _dot(q1, sb)], axis=1) * eqp
        kwb = kw.astype(BF16)
        u0 = _dot_tn(kwb, vp[:, :HEAD_DIM].astype(BF16))
        u1 = _dot_tn(kwb, vp[:, HEAD_DIM:].astype(BF16))
        top = _half_rows()
        dcol = jnp.where(top, decq[:, c0:c0 + 1], decq[:, c1:c1 + 1])
        self.sv[idx] = s_old * dcol + jnp.where(top, u0, u1)
        qn = None
        if ml:
            p = idx - 6
            n_old = self.nrow[p:p + 1, :]
            qn_l = qp * n_old
            qn = (jnp.sum(jnp.where(lm0, qn_l, 0.0), axis=-1, keepdims=True),
                  jnp.sum(jnp.where(lm0, 0.0, qn_l), axis=-1, keepdims=True))
            drow = jnp.where(lm0, decq[:, c0:c0 + 1], decq[:, c1:c1 + 1])
            self.nrow[p:p + 1, :] = n_old * drow + jnp.sum(kw, axis=0, keepdims=True)
        return carried, qn


def _mixer_prompt_kernel(*refs):
    proj_refs = refs[:SEQ_PER_STEP]
    (cos_ref, sin_ref, ptab_ref, cw_ref, cb_ref, dsk_ref, gs_ref, gr_ref, gm_ref,
     y_ref, sv_o, conv_o, n_o, m_o, sv, nrow, mrow, cbuf) = refs[SEQ_PER_STEP:]
    ci = pl.program_id(1)
    rows = proj_refs[0].shape[1]

    @pl.when(ci == 0)
    def _():
        sv[...] = jnp.zeros_like(sv)
        nrow[...] = jnp.zeros_like(nrow)
        mrow[...] = jnp.zeros_like(mrow)
        cbuf[:, 0:8, :] = jnp.zeros((SEQ_PER_STEP, 8, SSD_XBC), F32)

    ri = lax.broadcasted_iota(I32, (rows, rows), 0)
    cj = lax.broadcasted_iota(I32, (rows, rows), 1)
    mask = cj <= ri
    tril = jnp.where(mask, 1.0, 0.0).astype(BF16)
    for s in range(SEQ_PER_STEP):
        proj_ref = proj_refs[s]
        cbuf[s, 8:8 + rows, :] = proj_ref[0, :, P_XBC:P_RET]
        us = [cbuf[s, pl.ds(5 + k, rows), :] for k in range(SSD_CONV)]
        m_t = _mixer_core(
            proj_ref[0, :, P_Z:P_XBC], us, proj_ref[0, :, P_RET:P_ML], proj_ref[0, :, P_ML:P_SM],
            proj_ref[0, :, P_SM:P_W], cos_ref[...], sin_ref[...], ptab_ref[0], cw_ref[0], cb_ref[0],
            dsk_ref[0], gs_ref[0], gr_ref[0], gm_ref[0],
            mask, tril, None, rows, mrow[s, 0:1, :], lambda a: a[rows - 1:rows, :],
            _CarriedState(sv.at[s], nrow.at[s]), y_ref.at[s])
        mrow[s, 0:1, :] = m_t[rows - 1:rows, :]
        cbuf[s, 0:8, :] = cbuf[s, rows:rows + 8, :]

    @pl.when(ci == pl.num_programs(1) - 1)
    def _():
        sv_o[...] = sv[...]
        conv_o[...] = cbuf[:, 0:8, :]
        n_o[...] = nrow[...]
        m_o[...] = mrow[...]


def _mixer_prompt(proj, nb, t, cos, sin, consts, l):
    rows = math.gcd(t, CHUNK)
    nc = t // rows
    sq = SEQ_PER_STEP
    assert nb % sq == 0

    def proj_spec(s):
        return pl.BlockSpec((1, rows, P_W), lambda b, c: (0, (sq * b + s) * nc + c, 0))

    def seq_spec(*tail):
        return pl.BlockSpec((sq,) + tail, lambda b, c: (b,) + (0,) * len(tail))

    outs = pl.pallas_call(
        _mixer_prompt_kernel,
        grid=(nb // sq, nc),
        in_specs=[proj_spec(s) for s in range(sq)]
                 + [pl.BlockSpec((rows, RET_WIDTH), lambda b, c: (c, 0)),
                    pl.BlockSpec((rows, RET_WIDTH), lambda b, c: (c, 0))] + [_layer_spec(a, l) for a in consts],
        out_specs=[pl.BlockSpec((sq, rows, D_MODEL), lambda b, c: (b, c, 0)),
                   seq_spec(N_PAIRS, LANES, HEAD_DIM), seq_spec(8, SSD_XBC), seq_spec(8, LANES), seq_spec(8, LANES)],
        out_shape=[jax.ShapeDtypeStruct((nb, t, D_MODEL), BF16),
                   jax.ShapeDtypeStruct((nb, N_PAIRS, LANES, HEAD_DIM), F32),
                   jax.ShapeDtypeStruct((nb, 8, SSD_XBC), F32),
                   jax.ShapeDtypeStruct((nb, 8, LANES), F32),
                   jax.ShapeDtypeStruct((nb, 8, LANES), F32)],
        scratch_shapes=[pltpu.VMEM((sq, N_PAIRS, LANES, HEAD_DIM), F32),
                        pltpu.VMEM((sq, 8, LANES), F32),
                        pltpu.VMEM((sq, 8, LANES), F32),
                        pltpu.VMEM((sq, rows + 8, SSD_XBC), F32)],
        compiler_params=_cparams(("arbitrary", "arbitrary")),
        name="mixer_prompt",
    )(*([proj[None]] * sq), cos, sin, *consts)
    y, sv, conv, n, m = outs
    sv = sv.reshape(nb, 2 * N_PAIRS, HEAD_DIM, HEAD_DIM)
    states = (sv[:, :8], conv[:, 5:8], sv[:, 8:12], sv[:, 12:16],
              n[:, 0:2].reshape(nb, ML_HEADS, HEAD_DIM), m[:, 0, C_ML:C_ML + 4])
    return y.reshape(nb * t, D_MODEL), states


class _PerSequenceState:
    def __init__(self, s_in, s_out, n_in, n_out, q_s, e_s, k_s, v_s, d_s, o_s, qn_s):
        self.s_in, self.s_out, self.n_in, self.n_out = s_in, s_out, n_in, n_out
        self.q_s, self.e_s, self.k_s, self.v_s, self.d_s, self.o_s, self.qn_s = q_s, e_s, k_s, v_s, d_s, o_s, qn_s

    @staticmethod
    def _pair_ref(refs, idx):
        return (refs[0], idx) if idx < 4 else ((refs[1], idx - 4) if idx < 6 else (refs[2], idx - 6))

    def step(self, idx, qp, eqp, kw, vp, decq, c0, c1, ml):
        lane = lax.broadcasted_iota(I32, (1, LANES), 1)
        lm0 = lane < HEAD_DIM
        sin_ref, pi = self._pair_ref(self.s_in, idx)
        sout_ref, _ = self._pair_ref(self.s_out, idx)
        self.q_s[...] = qp
        self.e_s[...] = eqp
        self.k_s[...] = kw
        self.v_s[...] = vp
        self.d_s[...] = jnp.where(lm0, decq[:, c0:c0 + 1], decq[:, c1:c1 + 1])
        top = _half_rows()
        nseq = self.q_s.shape[0] // S_ROWS

        def body(b, carry):
            r0 = pl.multiple_of(b * S_ROWS, S_ROWS)
            rs = pl.ds(r0, S_ROWS)
            s_old = sin_ref[b, pi]
            sb = s_old.astype(BF16)
            q = self.q_s[rs, :]
            q0 = jnp.where(lm0, q, 0.0)
            q1 = jnp.where(lm0, 0.0, q)
            carried = jnp.concatenate([_dot(q0.astype(BF16), sb), _dot(q1.astype(BF16), sb)], axis=1)
            self.o_s[rs, :] = carried * self.e_s[rs, :]
            kwb = self.k_s[rs, :]
            v = self.v_s[rs, :]
            u0 = _dot_tn(kwb.astype(BF16), v[:, :HEAD_DIM].astype(BF16))
            u1 = _dot_tn(kwb.astype(BF16), v[:, HEAD_DIM:].astype(BF16))
            drow = self.d_s[pl.ds(r0, 1), :]
            dcol = jnp.where(top, drow[:, 0:1], drow[:, HEAD_DIM:HEAD_DIM + 1])
            sout_ref[b, pi] = s_old * dcol + jnp.where(top, u0, u1)
            if ml:
                p = idx - 6
                n_old = self.n_in[b, p]
                qn_l = q * n_old
                qn0 = jnp.sum(jnp.where(lm0, qn_l, 0.0), axis=-1, keepdims=True)
                qn1 = jnp.sum(jnp.where(lm0, 0.0, qn_l), axis=-1, keepdims=True)
                self.qn_s[rs, :] = jnp.where(lm0, qn0, qn1)
                self.n_out[b, p] = n_old * drow + jnp.sum(kwb, axis=0, keepdims=True)
            return carry

        lax.fori_loop(0, nseq, body, 0, unroll=S_UNROLL)
        qn = None
        if ml:
            qn = (self.qn_s[:, 0:1], self.qn_s[:, HEAD_DIM:HEAD_DIM + 1])
        return self.o_s[...], qn


def _mixer_sample_kernel(proj_ref, tail_ref, mprev_ref, cos_ref, sin_ref, ssd_in, ret_in, c_in, n_in,
                         ptab_ref, cw_ref, cb_ref, dsk_ref, gs_ref, gr_ref, gm_ref,
                         y_ref, u_o, m_o, ssd_out, ret_out, c_out, n_out,
                         q_s, e_s, k_s, v_s, d_s, o_s, qn_s):
    s_in, s_out = (ssd_in.at[0], ret_in.at[0], c_in.at[0]), (ssd_out, ret_out, c_out)
    rows = proj_ref.shape[0]
    ri = lax.broadcasted_iota(I32, (rows, rows), 0)
    cj = lax.broadcasted_iota(I32, (rows, rows), 1)
    tj = cj & (S_ROWS - 1)
    same_seq = (ri & ~(S_ROWS - 1)) == (cj & ~(S_ROWS - 1))
    mask = same_seq & (tj >= S_FIRST) & (tj <= S_LAST) & (cj <= ri)
    tril = jnp.where(mask, 1.0, 0.0).astype(BF16)
    last_sel = jnp.where(cj == ((ri & ~(S_ROWS - 1)) + S_LAST), 1.0, 0.0).astype(BF16)
    tr = lax.broadcasted_iota(I32, (rows, 1), 0) & (S_ROWS - 1)
    valid = (tr >= S_FIRST) & (tr <= S_LAST)
    u_full = jnp.where(tr < S_FIRST, tail_ref[...], proj_ref[:, P_XBC:P_RET])
    u_o[...] = u_full
    us = [pltpu.roll(u_full, SSD_CONV - 1 - k, 0) for k in range(SSD_CONV - 1)] + [u_full]
    st = _PerSequenceState(s_in, s_out, n_in.at[0], n_out, q_s, e_s, k_s, v_s, d_s, o_s, qn_s)
    m_o[...] = _mixer_core(
        proj_ref[:, P_Z:P_XBC], us, proj_ref[:, P_RET:P_ML], proj_ref[:, P_ML:P_SM], proj_ref[:, P_SM:P_W],
        cos_ref[...], sin_ref[...], ptab_ref[0], cw_ref[0], cb_ref[0], dsk_ref[0], gs_ref[0],
        gr_ref[0], gm_ref[0], mask, tril, valid, S_ROWS, mprev_ref[...],
        lambda a: _dot01(last_sel, a), st, y_ref)


def _mixer_sample(proj, row0, tail, mprev, cos, sin, ssd, ret, mlc, mln, consts, l):
    n_rows = tail.shape[0]
    rows = S_BLOCK * S_ROWS
    off = row0 // rows

    def rowspec(w):
        return pl.BlockSpec((rows, w), lambda i: (i, 0))

    def sspec_in(a):
        return pl.BlockSpec((1, S_BLOCK) + a.shape[2:], lambda i: (l, i, 0, 0, 0))

    def sspec_out(a):
        return pl.BlockSpec((S_BLOCK,) + a.shape[2:], lambda i: (i, 0, 0, 0))

    states = (ssd, ret, mlc, mln)
    return pl.pallas_call(
        _mixer_sample_kernel,
        grid=(n_rows // rows,),
        in_specs=[pl.BlockSpec((rows, P_W), lambda i: (i + off, 0)), rowspec(SSD_XBC), rowspec(LANES),
                  _const_spec(cos), _const_spec(sin)] + [sspec_in(a) for a in states]
                 + [_layer_spec(a, l) for a in consts],
        out_specs=[rowspec(D_MODEL), rowspec(SSD_XBC), rowspec(LANES)] + [sspec_out(a) for a in states],
        out_shape=[jax.ShapeDtypeStruct((n_rows, D_MODEL), BF16),
                   jax.ShapeDtypeStruct((n_rows, SSD_XBC), F32),
                   jax.ShapeDtypeStruct((n_rows, LANES), F32)]
                  + [jax.ShapeDtypeStruct(a.shape[1:], F32) for a in states],
        scratch_shapes=[pltpu.VMEM((rows, LANES), F32) for _ in range(7)],
        compiler_params=_cparams(("arbitrary",)),
        name="mixer_sample",
    )(proj, tail, mprev, cos, sin, *states, *consts)


def _route(logits):
    lane = lax.broadcasted_iota(I32, logits.shape, 1)
    gmask = (lane >= N_EXPERTS) & (lane < N_EXPERTS + EXPERT_GROUPS)
    gl = jnp.where(gmask, logits, -jnp.inf)
    ge = jnp.exp(gl - jnp.max(gl, axis=-1, keepdims=True))
    gprob = ge / jnp.sum(ge, axis=-1, keepdims=True)
    g_w = jnp.max(gprob, axis=-1, keepdims=True)
    g_idx = jnp.min(jnp.where(gmask & (gprob == g_w), lane - N_EXPERTS, LANES), axis=-1, keepdims=True)
    emask = (lane < N_EXPERTS) & ((lane >> 3) == g_idx)
    el = jnp.where(emask, logits, -jnp.inf)
    ee = jnp.exp(el - jnp.max(el, axis=-1, keepdims=True))
    eprob = ee / jnp.sum(ee, axis=-1, keepdims=True)
    p1 = jnp.max(jnp.where(emask, eprob, -1.0), axis=-1, keepdims=True)
    i1 = jnp.min(jnp.where(emask & (eprob == p1), lane, LANES), axis=-1, keepdims=True)
    rest = emask & (lane != i1)
    p2 = jnp.max(jnp.where(rest, eprob, -1.0), axis=-1, keepdims=True)
    i2 = jnp.min(jnp.where(rest & (eprob == p2), lane, LANES), axis=-1, keepdims=True)
    tot = p1 + p2
    key = g_idx * 64 + (jnp.minimum(i1, i2) & 7) * 8 + (jnp.maximum(i1, i2) & 7)
    return g_w * (p1 / tot), g_w * (p2 / tot), i1, i2, key


def _outproj_kernel(yp, ys, xp, xs, gtp, gts, scp, scs, shp, shs, g_ref, w_ref, wr_ref, br_ref,
                    x1_ref, hx_ref, info_ref, cnt_ref, carry, *, np_tiles):
    i = pl.program_id(0)
    is_p = i < np_tiles

    @pl.when(i == 0)
    def _():
        carry[...] = jnp.zeros_like(carry)

    x1 = _dual_val(xp, xs, is_p) + _mod_val(gtp, gts, is_p) * _dot(_dual_val(yp, ys, is_p), w_ref[0])
    x1_ref[...] = x1
    h2 = _rms(x1) * g_ref[0] * (1.0 + _mod_val(scp, scs, is_p)) + _mod_val(shp, shs, is_p)
    h2b = h2.astype(BF16)
    w1, w2, i1, i2, key = _route(_dot(h2b, wr_ref[0]) + br_ref[0])

    tm = x1.shape[0]
    onehot = lax.broadcasted_iota(I32, (tm, N_CLASS), 1) == key
    tril = (lax.broadcasted_iota(I32, (tm, tm), 1) <= lax.broadcasted_iota(I32, (tm, tm), 0))
    upto = _dot(jnp.where(tril, 1.0, 0.0).astype(BF16), jnp.where(onehot, 1.0, 0.0).astype(BF16))
    seen = carry[0:1, :]
    rank = jnp.sum(jnp.where(onehot, upto - 1.0 + seen, 0.0), axis=-1, keepdims=True)
    seen = seen + upto[tm - 1:tm, :]
    carry[0:1, :] = seen
    cnt_ref[...] = jnp.broadcast_to(seen, cnt_ref.shape)

    lane = lax.broadcasted_iota(I32, (tm, LANES), 1)
    info = jnp.zeros((tm, LANES), F32)
    for c, v in enumerate((w1, w2, i1.astype(F32), i2.astype(F32), key.astype(F32), rank)):
        info = jnp.where(lane == c, v, info)
    hx_ref[:, 0:D_MODEL] = h2b.astype(F32)
    hx_ref[:, D_MODEL:H_EXT] = info
    info_ref[...] = info[:, 0:8]


def _outproj(y, x, gt, sc, sh, g, w, wr, br, l, tp):
    n_prompt, ns = x[0].shape[0], x[1].shape[0]
    n = n_prompt + ns
    tm = TOKEN_TILE
    npt = n_prompt // tm
    ms = _mod_specs(tm, n_prompt, tp)
    return pl.pallas_call(
        functools.partial(_outproj_kernel, np_tiles=npt),
        grid=(n // tm,),
        in_specs=_dual_specs(tm, npt, D_MODEL) + _dual_specs(tm, npt, D_MODEL) + ms + ms + ms
                 + [_layer_spec(a, l) for a in (g, w, wr, br)],
        out_specs=[_row_spec(tm, D_MODEL), _row_spec(tm, H_EXT), _row_spec(tm, 8),
                   pl.BlockSpec((8, N_CLASS), lambda i: (0, 0))],
        out_shape=[jax.ShapeDtypeStruct((n, D_MODEL), F32),
                   jax.ShapeDtypeStruct((n, H_EXT), F32),
                   jax.ShapeDtypeStruct((n, 8), F32),
                   jax.ShapeDtypeStruct((8, N_CLASS), F32)],
        scratch_shapes=[pltpu.VMEM((8, N_CLASS), F32)],
        compiler_params=_cparams(("arbitrary",)),
        name="outproj_router",
    )(*y, *x, *gt, *sc, *sh, g, w, wr, br)


def _scatter_kernel(pos_ref, x_ref, o_hbm, buf, sem):
    i, n = pl.program_id(0), pl.num_programs(0)
    tm = x_ref.shape[0]
    slot = i % 2

    def wait_slot(s):
        pltpu.make_async_copy(buf.at[s], o_hbm.at[pl.ds(0, tm), :], sem.at[s]).wait()

    @pl.when(i >= 2)
    def _():
        wait_slot(slot)

    buf[slot] = x_ref[...]

    def body(r, c):
        dst = pos_ref[i * tm + r]
        pltpu.make_async_copy(buf.at[slot, pl.ds(r, 1), :], o_hbm.at[pl.ds(dst, 1), :], sem.at[slot]).start()
        return c
    lax.fori_loop(0, tm, body, 0, unroll=8)

    @pl.when(i == n - 1)
    def _():
        wait_slot(slot)

        @pl.when(n >= 2)
        def _():
            wait_slot(1 - slot)


def _scatter_rows(x, pos):
    n, width = x.shape
    tm = MOE_TILE
    return pl.pallas_call(
        _scatter_kernel,
        grid_spec=pltpu.PrefetchScalarGridSpec(
            num_scalar_prefetch=1, grid=(n // tm,),
            in_specs=[_row_spec(tm, width)],
            out_specs=pl.BlockSpec(memory_space=pl.ANY),
            scratch_shapes=[pltpu.VMEM((2, tm, width), x.dtype), pltpu.SemaphoreType.DMA((2,))]),
        out_shape=jax.ShapeDtypeStruct((n, width), x.dtype),
        compiler_params=_cparams(("arbitrary",)),
        name="scatter_rows",
    )(pos, x)


def _gather_tiles(idx_ref, src_hbm, buf, sem, tm):
    i, n = pl.program_id(0), pl.num_programs(0)

    def issue(tile, slot):
        def body(r, c):
            row = idx_ref[tile * tm + r]
            pltpu.make_async_copy(src_hbm.at[pl.ds(row, 1), :], buf.at[slot, pl.ds(r, 1), :],
                                  sem.at[slot]).start()
            return c
        lax.fori_loop(0, tm, body, 0, unroll=8)

    @pl.when(i == 0)
    def _():
        issue(0, 0)

    @pl.when(i + 1 < n)
    def _():
        issue(i + 1, (i + 1) % 2)

    slot = i % 2
    pltpu.make_async_copy(src_hbm.at[pl.ds(0, tm), :], buf.at[slot], sem.at[slot]).wait()
    return slot


def _experts_kernel(tile_ref, grp_ref, flag_ref, hx_ref, wgu_ref, wd_ref, o_ref):
    j = pl.program_id(0)
    flags = flag_ref[j]
    valid = (flags & 1) != 0

    @pl.when(valid & ((flags & 2) != 0))
    def _():
        o_ref[...] = jnp.zeros_like(o_ref)

    @pl.when(valid)
    def _():
        h = hx_ref[:, 0:D_MODEL].astype(BF16)
        r = hx_ref[:, D_MODEL:H_EXT]
        w1, w2, i1, i2 = r[:, 0:1], r[:, 1:2], r[:, 2:3], r[:, 3:4]
        base = grp_ref[j] * EXPERTS_PER_GROUP
        for e in range(EXPERTS_PER_GROUP):
            @pl.when(((flags >> (8 + e)) & 1) != 0)
            def _():
                eid = (base + e).astype(F32)
                ge = jnp.where(i1 == eid, w1, 0.0) + jnp.where(i2 == eid, w2, 0.0)
                au = _dot(h, wgu_ref[0, 0, e])
                act = _silu(au[:, :EXPERT_FF]) * au[:, EXPERT_FF:]
                o_ref[...] += ge * _dot(act.astype(BF16), wd_ref[0, 0, e])


def _experts(hx_sorted, tile, grp, flags, wgu, wd, l):
    n = hx_sorted.shape[0]
    tm = MOE_TILE
    return pl.pallas_call(
        _experts_kernel,
        grid_spec=pltpu.PrefetchScalarGridSpec(
            num_scalar_prefetch=3, grid=(tile.shape[0],),
            in_specs=[pl.BlockSpec((tm, H_EXT), lambda j, t, g, f: (t[j], 0)),
                      pl.BlockSpec((1, 1, EXPERTS_PER_GROUP, D_MODEL, 2 * EXPERT_FF),
                                   lambda j, t, g, f: (l, g[j], 0, 0, 0)),
                      pl.BlockSpec((1, 1, EXPERTS_PER_GROUP, EXPERT_FF, D_MODEL),
                                   lambda j, t, g, f: (l, g[j], 0, 0, 0))],
            out_specs=pl.BlockSpec((tm, D_MODEL), lambda j, t, g, f: (t[j], 0))),
        out_shape=jax.ShapeDtypeStruct((n, D_MODEL), F32),
        compiler_params=_cparams(("arbitrary",)),
        name="moe_experts",
    )(tile, grp, flags, hx_sorted, wgu, wd)


def _combine_kernel(pos_ref, y_hbm, x1_ref, gtp, gts, gf_ref, op_ref, os_ref, buf, sem, *, np_tiles, final):
    slot = _gather_tiles(pos_ref, y_hbm, buf, sem, x1_ref.shape[0])
    is_p = pl.program_id(0) < np_tiles
    x2 = x1_ref[...] + _mod_val(gtp, gts, is_p) * buf[slot]
    out = _rms(x2) * gf_ref[...] if final else x2

    @pl.when(is_p)
    def _():
        op_ref[...] = out

    @pl.when(jnp.logical_not(is_p))
    def _():
        os_ref[...] = out


def _combine(y_sorted, pos, x1, gt, gf, n_prompt, tp, final):
    n = x1.shape[0]
    tm = MOE_TILE
    npt = n_prompt // tm
    return pl.pallas_call(
        functools.partial(_combine_kernel, np_tiles=npt, final=final),
        grid_spec=pltpu.PrefetchScalarGridSpec(
            num_scalar_prefetch=1, grid=(n // tm,),
            in_specs=[pl.BlockSpec(memory_space=pl.ANY), _row_spec(tm, D_MODEL)]
                     + _mod_specs(tm, n_prompt, tp) + [_const_spec(gf)],
            out_specs=_dual_specs(tm, npt, D_MODEL),
            scratch_shapes=[pltpu.VMEM((2, tm, D_MODEL), F32), pltpu.SemaphoreType.DMA((2,))]),
        out_shape=[jax.ShapeDtypeStruct((n_prompt, D_MODEL), F32),
                   jax.ShapeDtypeStruct((n - n_prompt, D_MODEL), F32)],
        compiler_params=_cparams(("arbitrary",)),
        name="moe_combine",
    )(pos, y_sorted, x1, *gt, gf)


def _routing_tables(cnt, key, rank, n_tiles, tm):
    n_items = n_tiles + EXPERT_GROUPS - 1
    c = jnp.arange(N_CLASS, dtype=I32)
    start_c = jnp.sum(jnp.where(c[:, None] < c[None, :], cnt[:, None], 0), axis=0)
    pos = jnp.sum(jnp.where(key[:, None] == c[None, :], start_c[None, :], 0), axis=1) + rank

    g = jnp.arange(EXPERT_GROUPS, dtype=I32)
    gend = jnp.sum(jnp.where(c[None, :] < 64 * (g[:, None] + 1), cnt[None, :], 0), axis=1)
    t0 = jnp.arange(n_tiles, dtype=I32) * tm
    t1 = t0 + (tm - 1)
    gfirst = jnp.sum((gend[None, :] <= t0[:, None]).astype(I32), axis=1)
    glast = jnp.sum((gend[None, :] <= t1[:, None]).astype(I32), axis=1)
    per_tile = glast - gfirst + 1
    tt = jnp.arange(n_tiles, dtype=I32)
    start_t = jnp.sum(jnp.where(tt[:, None] < tt[None, :], per_tile[:, None], 0), axis=0)
    total = jnp.sum(per_tile)

    j = jnp.arange(n_items, dtype=I32)
    valid = j < total
    tile = jnp.sum((start_t[None, :] <= j[:, None]).astype(I32), axis=1) - 1
    tile = jnp.where(valid, tile, n_tiles - 1)
    sel = tile[:, None] == tt[None, :]
    pick = lambda v: jnp.sum(jnp.where(sel, v[None, :], 0), axis=1)
    grp = jnp.where(valid, pick(gfirst) + (j - pick(start_t)), glast[n_tiles - 1])
    first = valid & (j == pick(start_t))

    in_tile = (cnt[None, :] > 0) & (start_c[None, :] <= t1[:, None]) & ((start_c + cnt)[None, :] > t0[:, None])
    e = jnp.arange(N_EXPERTS, dtype=I32)
    member = (((c >> 6) * 8 + ((c >> 3) & 7))[:, None] == e[None, :]) | (((c >> 6) * 8 + (c & 7))[:, None] == e[None, :])
    present_t = jnp.any(in_tile[:, :, None] & member[None, :, :], axis=1)
    present_j = jnp.any(sel[:, :, None] & present_t[None, :, :], axis=1)
    eg = e[None, :] - grp[:, None] * EXPERTS_PER_GROUP
    bits = jnp.sum(jnp.where(present_j & (eg >= 0) & (eg < EXPERTS_PER_GROUP),
                             1 << (8 + jnp.clip(eg, 0, EXPERTS_PER_GROUP - 1)), 0), axis=1)
    flags = valid.astype(I32) | (first.astype(I32) << 1) | bits
    return pos.astype(I32), tile.astype(I32), grp.astype(I32), flags.astype(I32)


def _moe(hx, info, counts, x1, gt, gf, wgu, wd, l, n_prompt, tp, final):
    n = hx.shape[0]
    tm = MOE_TILE
    pos, tile, grp, flags = _routing_tables(counts[0].astype(I32), info[:, 4].astype(I32),
                                            info[:, 5].astype(I32), n // tm, tm)
    y_sorted = _experts(_scatter_rows(hx, pos), tile, grp, flags, wgu, wd, l)
    return _combine(y_sorted, pos, x1, gt, gf, n_prompt, tp, final)


def _rope_tables(pos):
    half = HEAD_DIM // 2
    inv = ROPE_BASE ** (-jnp.arange(half, dtype=F32) / half)
    ang = pos.astype(F32)[:, None] * inv[None, :]
    cos, sin = jnp.cos(ang), jnp.sin(ang)
    cos_t = jnp.tile(jnp.concatenate([cos, cos], axis=-1), (1, RET_HEADS))
    sin_t = jnp.tile(jnp.concatenate([-sin, sin], axis=-1), (1, RET_HEADS))
    return cos_t, sin_t


def _mixer_consts(w_in, conv_w, conv_b, dt_bias, a_log, d_skip, g_ssd_norm, g_ret_norm,
                  b_mlstm_i, b_mlstm_f, g_mlstm_norm):
    nl = w_in.shape[0]
    w_p = jnp.concatenate([w_in[:, :, 0:1280], w_in[:, :, 1288:3336], w_in[:, :, 1280:1288],
                           w_in[:, :, 3336:3344], jnp.zeros((nl, D_MODEL, LANES - 16), F32)], axis=2).astype(BF16)
    pad = lambda v: jnp.pad(v, ((0, 0), (0, LANES - v.shape[1])))
    log_gamma = jnp.log(1.0 - 2.0 ** (-5.0 - jnp.arange(RET_HEADS, dtype=F32)))
    lg = jnp.broadcast_to(jnp.concatenate([jnp.zeros((8,), F32), log_gamma])[None, :], (nl, 12))
    ptab = jnp.stack([pad(jnp.concatenate([dt_bias, b_mlstm_i, b_mlstm_f], axis=1)), pad(a_log), pad(lg)]
                     + [jnp.zeros((nl, LANES), F32)] * 5, axis=1)
    consts = (ptab, conv_w, conv_b[:, None, :], jnp.repeat(d_skip, HEAD_DIM, axis=1)[:, None, :],
              g_ssd_norm[:, None, :], g_ret_norm[:, None, :], g_mlstm_norm[:, None, :])
    return w_p, consts


def kernel(x_prompt, x_sample, state_ssd, state_ssd_conv, state_ret, state_mlstm_c, state_mlstm_n,
           state_mlstm_m, c_prompt, c_sample, w_ada, b_ada, g_norm1, g_norm2, w_in, conv_w, conv_b,
           dt_bias, a_log, d_skip, g_ssd_norm, g_ret_norm, b_mlstm_i, b_mlstm_f, g_mlstm_norm, w_out,
           w_router_group, b_router_group, w_router_expert, b_router_expert, w_gate_up, w_down, g_final):
    bp, tp, _ = x_prompt.shape
    bs, ts, _ = x_sample.shape
    assert ts == S_LAST - S_FIRST + 1
    n_srows = bs * S_ROWS
    n_prompt = bp * tp

    mod = _ada(jnp.concatenate([c_prompt, c_sample], axis=0), w_ada, b_ada)

    def mods(l):
        m = mod[l].reshape(bp + bs, 6, D_MODEL)
        return [(m[:bp, k][:, None, :], jnp.repeat(m[bp:, k], S_ROWS, axis=0)) for k in range(6)]

    cos_p, sin_p = _rope_tables(jnp.arange(tp, dtype=I32))
    t8 = jnp.clip(jnp.arange(S_ROWS, dtype=I32) - S_FIRST, 0, ts - 1)
    cos_s, sin_s = _rope_tables(PAST_LEN + t8)
    cos_s, sin_s = jnp.tile(cos_s, (S_BLOCK, 1)), jnp.tile(sin_s, (S_BLOCK, 1))

    w_p, consts = _mixer_consts(w_in, conv_w, conv_b, dt_bias, a_log, d_skip, g_ssd_norm, g_ret_norm,
                                b_mlstm_i, b_mlstm_f, g_mlstm_norm)
    w_o = w_out.astype(BF16)
    zpad = LANES - N_EXPERTS - EXPERT_GROUPS
    wr = jnp.concatenate([w_router_expert, w_router_group, jnp.zeros((DEPTH, D_MODEL, zpad), F32)],
                         axis=2).astype(BF16)
    br = jnp.concatenate([b_router_expert, b_router_group, jnp.zeros((DEPTH, zpad), F32)], axis=1)[:, None, :]
    wgu = w_gate_up.astype(BF16).reshape(DEPTH, EXPERT_GROUPS, EXPERTS_PER_GROUP, D_MODEL, 2 * EXPERT_FF)
    wd = w_down.astype(BF16).reshape(DEPTH, EXPERT_GROUPS, EXPERTS_PER_GROUP, EXPERT_FF, D_MODEL)
    g1, g2, gf = g_norm1[:, None, :], g_norm2[:, None, :], g_final[None, :]
    ssd_all = state_ssd.reshape(DEPTH, bs, 4, LANES, HEAD_DIM)
    ret_all = state_ret.reshape(DEPTH, bs, 2, LANES, HEAD_DIM)
    c_all = state_mlstm_c.reshape(DEPTH, bs, 2, LANES, HEAD_DIM)
    n_all = state_mlstm_n.reshape(DEPTH, bs, 2, 1, LANES)

    x = (x_prompt.reshape(n_prompt, D_MODEL),
         jnp.pad(x_sample, ((0, 0), (S_FIRST, S_ROWS - 1 - S_LAST), (0, 0))).reshape(n_srows, D_MODEL))
    p_states, s_states = [], []
    for l in range(DEPTH):
        final = l == DEPTH - 1
        sh1, sc1, gt1, sh2, sc2, gt2 = mods(l)
        proj = _inproj(x, sc1, sh1, g1, w_p, l, tp)

        ycat_p, st = _mixer_prompt(proj, bp, tp, cos_p, sin_p, consts, l)
        p_states.append(st)

        tail = jnp.pad(state_ssd_conv[l], ((0, 0), (0, S_ROWS - SSD_CONV + 1), (0, 0))).reshape(n_srows, SSD_XBC)
        mprev = jnp.pad(jnp.repeat(state_mlstm_m[l], S_ROWS, axis=0), ((0, 0), (C_ML, LANES - C_ML - ML_HEADS)))
        ycat_s, u_full, m_rows, ssd_n, ret_n, c_n, n_n = _mixer_sample(
            proj, n_prompt, tail, mprev, cos_s, sin_s, ssd_all, ret_all, c_all, n_all, consts, l)
        s_states.append((ssd_n.reshape(state_ssd.shape[1:]),
                         u_full.reshape(bs, S_ROWS, SSD_XBC)[:, S_LAST - 2:S_LAST + 1],
                         ret_n.reshape(state_ret.shape[1:]), c_n.reshape(state_mlstm_c.shape[1:]),
                         n_n.reshape(bs, ML_HEADS, HEAD_DIM),
                         m_rows.reshape(bs, S_ROWS, LANES)[:, S_LAST, C_ML:C_ML + ML_HEADS]))

        x1, hx, info, counts = _outproj((ycat_p, ycat_s), x, gt1, sc2, sh2, g2, w_o, wr, br, l, tp)
        x = _moe(hx, info, counts, x1, gt2, gf, wgu, wd, l, n_prompt, tp, final)

    y_prompt = x[0].reshape(bp, tp, D_MODEL)
    y_sample = x[1].reshape(bs, S_ROWS, D_MODEL)[:, S_FIRST:S_LAST + 1]
    p_st = [jnp.stack([s[i] for s in p_states], axis=0) for i in range(6)]
    s_st = [jnp.stack([s[i] for s in s_states], axis=0) for i in range(6)]
    return (y_prompt, y_sample, *p_st, *s_st)
```

```python
import functools
import math

import jax
import jax.numpy as jnp
from jax import lax
from jax.experimental import pallas as pl
from jax.experimental.pallas import tpu as pltpu

F32 = jnp.float32
BF16 = jnp.bfloat16
I32 = jnp.int32

D_MODEL = 1024
DEPTH = 2
PAST_LEN = 16384
SSD_HEADS = 8
SSD_WIDTH = 512
SSD_GROUPS = 2
SSD_STATE = 64
SSD_CONV = 4
SSD_XBC = 768
RET_HEADS = 4
RET_WIDTH = 256
ML_HEADS = 4
ML_WIDTH = 256
HEAD_DIM = 64
ROPE_BASE = 10000.0
EPS = 1e-6
EXPERT_GROUPS = 4
EXPERTS_PER_GROUP = 8
N_EXPERTS = 32
EXPERT_FF = 256

LANES = 128
CHUNK = 128
SEQ_PER_STEP = 2
P_Z, P_XBC, P_RET, P_ML, P_SM, P_W = 0, 512, 1280, 2304, 3328, 3456
C_SSD, C_RET, C_ML = 0, 8, 12
N_PAIRS = 8
NEG = -1e30
TOKEN_TILE = 512
MOE_TILE = 256
VMEM_LIMIT = 56 * 1024 * 1024
S_ROWS, S_FIRST, S_LAST = 8, 3, 6
S_BLOCK = 16
S_UNROLL = 8
H_EXT = D_MODEL + LANES
N_CLASS = 256


def _cparams(sem):
    return pltpu.CompilerParams(dimension_semantics=sem, vmem_limit_bytes=VMEM_LIMIT)


def _split3(x):
    x1 = x.astype(BF16)
    r = x - x1.astype(F32)
    x2 = r.astype(BF16)
    r = r - x2.astype(F32)
    return x1, x2, r.astype(BF16)


def _dot01(m01, x):
    return sum(jnp.dot(m01, p, preferred_element_type=F32) for p in _split3(x))


def _dot(a, b):
    return jnp.dot(a, b, preferred_element_type=F32)


def _dot_nt(a, b):
    return lax.dot_general(a, b, (((1,), (1,)), ((), ())), preferred_element_type=F32)


def _dot_tn(a, b):
    return lax.dot_general(a, b, (((0,), (0,)), ((), ())), preferred_element_type=F32)


def _softplus(x):
    return jnp.maximum(x, 0.0) + jnp.log1p(jnp.exp(-jnp.abs(x)))


def _silu(x):
    return x * jax.nn.sigmoid(x)


def _rms(x):
    return x * lax.rsqrt(jnp.mean(x * x, axis=-1, keepdims=True) + EPS)


def _cummax_rows(x, seg):
    t = lax.broadcasted_iota(I32, x.shape, 0) & (seg - 1)
    s = 1
    while s < seg:
        x = jnp.maximum(x, jnp.where(t >= s, pltpu.roll(x, s, 0), NEG))
        s *= 2
    return x


def _rope(x, cos, sin_signed, lane):
    swapped = jnp.where((lane & 63) < 32, pltpu.roll(x, 96, 1), pltpu.roll(x, 32, 1))
    return x * cos + swapped * sin_signed


def _row_spec(tm, width):
    return pl.BlockSpec((tm, width), lambda i, *_: (i, 0))


def _const_spec(a):
    nd = a.ndim
    return pl.BlockSpec(a.shape, lambda *_: (0,) * nd)


def _layer_spec(a, l):
    nd = a.ndim
    return pl.BlockSpec((1,) + a.shape[1:], lambda *_: (l,) + (0,) * (nd - 1))


def _dual_specs(tm, np_tiles, width):
    return [pl.BlockSpec((tm, width), lambda i, *_: (jnp.minimum(i, np_tiles - 1), 0)),
            pl.BlockSpec((tm, width), lambda i, *_: (jnp.maximum(i - np_tiles, 0), 0))]


def _mod_specs(tm, n_prompt, tp):
    np_tiles, last_b = n_prompt // tm, n_prompt // tp - 1
    return [pl.BlockSpec((1, 1, D_MODEL), lambda i, *_: (jnp.minimum(i * tm // tp, last_b), 0, 0)),
            pl.BlockSpec((tm, D_MODEL), lambda i, *_: (jnp.maximum(i - np_tiles, 0), 0))]


def _mod_val(p_ref, s_ref, is_prompt):
    return jnp.where(is_prompt, p_ref[0], s_ref[...])


def _dual_val(p_ref, s_ref, is_prompt):
    return jnp.where(is_prompt, p_ref[...], s_ref[...])


def _ada_kernel(c_ref, w_ref, b_ref, o_ref):
    c = c_ref[...]
    o_ref[0] = _dot(_silu(c).astype(BF16), w_ref[0].astype(BF16)) + b_ref[0]


def _ada(c_all, w_ada, b_ada):
    nb = c_all.shape[0]
    tn = 1536
    return pl.pallas_call(
        _ada_kernel,
        grid=(DEPTH, 6 * D_MODEL // tn),
        in_specs=[pl.BlockSpec((nb, D_MODEL), lambda l, j: (0, 0)),
                  pl.BlockSpec((1, D_MODEL, tn), lambda l, j: (l, 0, j)),
                  pl.BlockSpec((1, 1, tn), lambda l, j: (l, 0, j))],
        out_specs=pl.BlockSpec((1, nb, tn), lambda l, j: (l, 0, j)),
        out_shape=jax.ShapeDtypeStruct((DEPTH, nb, 6 * D_MODEL), F32),
        compiler_params=_cparams(("arbitrary", "arbitrary")),
        name="ada_mod",
    )(c_all, w_ada, b_ada.reshape(DEPTH, 1, 6 * D_MODEL))


def _inproj_kernel(xp, xs, scp, scs, shp, shs, g_ref, w_ref, o_ref, *, np_tiles):
    is_p = pl.program_id(0) < np_tiles
    h = _rms(_dual_val(xp, xs, is_p)) * g_ref[0] * (1.0 + _mod_val(scp, scs, is_p)) + _mod_val(shp, shs, is_p)
    o_ref[...] = _dot(h.astype(BF16), w_ref[0])


def _inproj(x, sc, sh, g, w, l, tp):
    n_prompt, ns = x[0].shape[0], x[1].shape[0]
    tm = TOKEN_TILE
    ms = _mod_specs(tm, n_prompt, tp)
    return pl.pallas_call(
        functools.partial(_inproj_kernel, np_tiles=n_prompt // tm),
        grid=((n_prompt + ns) // tm,),
        in_specs=_dual_specs(tm, n_prompt // tm, D_MODEL) + ms + ms + [_layer_spec(g, l), _layer_spec(w, l)],
        out_specs=_row_spec(tm, P_W),
        out_shape=jax.ShapeDtypeStruct((n_prompt + ns, P_W), F32),
        compiler_params=_cparams(("arbitrary",)),
        name="norm_inproj",
    )(*x, *sc, *sh, g, w)


def _lane_bcast(a, c):
    return jnp.broadcast_to(a[:, c:c + 1], a.shape)


def _pair_lanes(a, c0, c1):
    lm0 = lax.broadcasted_iota(I32, (a.shape[0], LANES), 1) < HEAD_DIM
    return jnp.where(lm0, a[:, c0:c0 + 1], a[:, c1:c1 + 1])


def _ret_factors(ptab, tril, mask, valid, last_fn):
    rows = tril.shape[0]
    lane = lax.broadcasted_iota(I32, (rows, LANES), 1)
    lm0 = lane < HEAD_DIM
    cum = _dot01(tril, jnp.where((lane >= C_RET) & (lane < C_ML), ptab[2:3], 0.0))
    xt = cum.T
    cum_last = last_fn(cum)
    out = []
    for p in range(RET_HEADS // 2):
        c0, c1 = C_RET + 2 * p, C_RET + 2 * p + 1
        b0, b1 = _lane_bcast(cum, c0), _lane_bcast(cum, c1)
        bp = jnp.where(lm0, b0, b1)
        wp = jnp.exp(_pair_lanes(cum_last, c0, c1) - bp)
        if valid is not None:
            wp = jnp.where(valid, wp, 0.0)
        out.append((jnp.exp(jnp.where(mask, b0 - xt[c0:c0 + 1, :], -jnp.inf)),
                    jnp.exp(jnp.where(mask, b1 - xt[c1:c1 + 1, :], -jnp.inf)), jnp.exp(bp), wp))
    return out


def _mixer_core(z, us, retb, mlb, small, cos, sin, ptab, cw, cb, dsk, gs, gr, gm,
                mask, tril, valid, seg, mprev, last_fn, st, y_ref, ret_factors):
    rows = small.shape[0]
    lane = lax.broadcasted_iota(I32, (rows, LANES), 1)
    lm0 = lane < HEAD_DIM

    pre = small + ptab[0:1]
    a_neg = -jnp.exp(ptab[1:2])
    dt = _softplus(pre)
    logf = -_softplus(-pre)
    la = jnp.where(lane < C_RET, dt * a_neg,
                   jnp.where(lane < C_ML, ptab[2:3], jnp.where(lane < C_ML + 4, logf, 0.0)))
    cum = _dot01(tril, la)
    ic = pltpu.roll(pre, 4, 1)
    mlm = (lane >= C_ML) & (lane < C_ML + 4)
    d = jnp.where(mlm if valid is None else (mlm & valid), ic - cum, NEG)
    m_t = cum + jnp.maximum(mprev, _cummax_rows(d, seg))
    xt = jnp.where(mlm, d, cum).T
    colv = cum - m_t
    cum_last, m_last = last_fn(cum), last_fn(m_t)
    decq = jnp.where(mlm, jnp.exp(cum_last + mprev - m_last), jnp.exp(cum_last))
    yield

    def factors(c0, c1, kind):
        if kind == "ret":
            return ret_factors((c0 - C_RET) // 2)
        if kind == "ssd":
            b0, b1 = _lane_bcast(cum, c0), _lane_bcast(cum, c1)
            bp = jnp.where(lm0, b0, b1)
            wp = jnp.exp(_pair_lanes(cum_last, c0, c1) - bp)
            if valid is not None:
                wp = jnp.where(valid, wp, 0.0)
            return (jnp.exp(jnp.where(mask, b0 - xt[c0:c0 + 1, :], -jnp.inf)),
                    jnp.exp(jnp.where(mask, b1 - xt[c1:c1 + 1, :], -jnp.inf)), jnp.exp(bp), wp)
        a0, a1 = _lane_bcast(colv, c0), _lane_bcast(colv, c1)
        eqp = jnp.exp(jnp.where(lm0, a0, a1) + _pair_lanes(mprev, c0, c1))
        wp = jnp.exp(jnp.where(lm0, _lane_bcast(d, c0), _lane_bcast(d, c1))
                     + _pair_lanes(cum_last, c0, c1) - _pair_lanes(m_last, c0, c1))
        return (jnp.exp(jnp.where(mask, a0 + xt[c0:c0 + 1, :], -jnp.inf)),
                jnp.exp(jnp.where(mask, a1 + xt[c1:c1 + 1, :], -jnp.inf)), eqp, wp)

    def pair(idx, qp, kp, vp, c0, c1, kind):
        d0, d1, eqp, wp = factors(c0, c1, kind)
        q0 = jnp.where(lm0, qp, 0.0)
        q1 = jnp.where(lm0, 0.0, qp)
        sc = _dot_nt(jnp.concatenate([q0, q1], axis=0).astype(BF16), kp.astype(BF16))
        s0, s1 = sc[0:rows] * d0, sc[rows:2 * rows] * d1
        v01 = jnp.concatenate([jnp.where(lm0, vp, 0.0), jnp.where(lm0, 0.0, vp)], axis=0).astype(BF16)
        intra = _dot(jnp.concatenate([s0, s1], axis=1).astype(BF16), v01)
        kw = kp * wp
        carried, qn = st.step(idx, qp, eqp, kw, vp, decq, c0, c1, kind == "ml")
        return intra + carried, s0, s1, qn

    def head_norm(o):
        o2 = o * o
        ms0 = jnp.sum(jnp.where(lm0, o2, 0.0), axis=-1, keepdims=True) * (1.0 / HEAD_DIM)
        ms1 = jnp.sum(jnp.where(lm0, 0.0, o2), axis=-1, keepdims=True) * (1.0 / HEAD_DIM)
        return o * jnp.where(lm0, lax.rsqrt(ms0 + EPS), lax.rsqrt(ms1 + EPS))

    conv = cb + us[0] * cw[0:1] + us[1] * cw[1:2] + us[2] * cw[2:3] + us[3] * cw[3:4]
    xc = _silu(conv)
    bb = xc[:, SSD_WIDTH:SSD_WIDTH + LANES]
    cc = xc[:, SSD_WIDTH + LANES:SSD_WIDTH + 2 * LANES]
    br = pltpu.roll(bb, HEAD_DIM, 1)
    cr = pltpu.roll(cc, HEAD_DIM, 1)
    ys = []
    for p in range(4):
        if p < 2:
            kp, qp = jnp.where(lm0, bb, br), jnp.where(lm0, cc, cr)
        else:
            kp, qp = jnp.where(lm0, br, bb), jnp.where(lm0, cr, cc)
        c0, c1 = C_SSD + 2 * p, C_SSD + 2 * p + 1
        sl = slice(LANES * p, LANES * (p + 1))
        xsp = xc[:, sl]
        dtp = jnp.where(lm0, dt[:, c0:c0 + 1], dt[:, c1:c1 + 1])
        o, _, _, _ = pair(p, qp, kp, xsp * dtp, c0, c1, "ssd")
        ys.append((o + dsk[:, sl] * xsp) * _silu(z[:, sl]))
        yield
    for g in range(SSD_GROUPS):
        ya, yb = ys[2 * g], ys[2 * g + 1]
        ms = (jnp.sum(ya * ya, axis=-1, keepdims=True)
              + jnp.sum(yb * yb, axis=-1, keepdims=True)) * (1.0 / (2 * LANES))
        r = lax.rsqrt(ms + EPS)
        for j, yv in ((2 * g, ya), (2 * g + 1, yb)):
            sl = slice(LANES * j, LANES * (j + 1))
            y_ref[:, sl] = (yv * r * gs[:, sl]).astype(y_ref.dtype)

    for p in range(2):
        sl = slice(LANES * p, LANES * (p + 1))
        qp = _rope(retb[:, LANES * p:LANES * (p + 1)], cos[:, sl], sin[:, sl], lane)
        kp = _rope(retb[:, RET_WIDTH + LANES * p:RET_WIDTH + LANES * (p + 1)], cos[:, sl], sin[:, sl], lane)
        kp = kp * (HEAD_DIM ** -0.5)
        vp = retb[:, 2 * RET_WIDTH + LANES * p:2 * RET_WIDTH + LANES * (p + 1)]
        gp = retb[:, 3 * RET_WIDTH + LANES * p:3 * RET_WIDTH + LANES * (p + 1)]
        c0, c1 = C_RET + 2 * p, C_RET + 2 * p + 1
        o, _, _, _ = pair(4 + p, qp, kp, vp, c0, c1, "ret")
        y = head_norm(o) * gr[:, sl] * _silu(gp)
        y_ref[:, SSD_WIDTH + LANES * p:SSD_WIDTH + LANES * (p + 1)] = y.astype(y_ref.dtype)
        yield

    for p in range(2):
        sl = slice(LANES * p, LANES * (p + 1))
        qp = mlb[:, LANES * p:LANES * (p + 1)]
        kp = mlb[:, ML_WIDTH + LANES * p:ML_WIDTH + LANES * (p + 1)] * (HEAD_DIM ** -0.5)
        vp = mlb[:, 2 * ML_WIDTH + LANES * p:2 * ML_WIDTH + LANES * (p + 1)]
        op = mlb[:, 3 * ML_WIDTH + LANES * p:3 * ML_WIDTH + LANES * (p + 1)]
        c0, c1 = C_ML + 2 * p, C_ML + 2 * p + 1
        num, s0, s1, (qn0, qn1) = pair(6 + p, qp, kp, vp, c0, c1, "ml")
        inter0 = jnp.exp(colv[:, c0:c0 + 1] + mprev[:, c0:c0 + 1])
        inter1 = jnp.exp(colv[:, c1:c1 + 1] + mprev[:, c1:c1 + 1])
        den0 = jnp.sum(s0, axis=-1, keepdims=True) + qn0 * inter0
        den1 = jnp.sum(s1, axis=-1, keepdims=True) + qn1 * inter1
        dn0 = jnp.maximum(jnp.abs(den0), jnp.exp(-m_t[:, c0:c0 + 1]))
        dn1 = jnp.maximum(jnp.abs(den1), jnp.exp(-m_t[:, c1:c1 + 1]))
        hh = num / jnp.where(lm0, dn0, dn1)
        y = head_norm(hh) * gm[:, sl] * jax.nn.sigmoid(op)
        off = SSD_WIDTH + RET_WIDTH + LANES * p
        y_ref[:, off:off + LANES] = y.astype(y_ref.dtype)
        yield
    return m_t


def _run_interleaved(gens):
    out = [None] * len(gens)
    live = list(range(len(gens)))
    while live:
        for k in list(live):
            try:
                next(gens[k])
            except StopIteration as stop:
                out[k] = stop.value
                live.remove(k)
    return out


def _half_rows():
    return lax.broadcasted_iota(I32, (LANES, HEAD_DIM), 0) < HEAD_DIM


class _CarriedState:
    def __init__(self, sv, nrow):
        self.sv, self.nrow = sv, nrow

    def step(self, idx, qp, eqp, kw, vp, decq, c0, c1, ml):
        lm0 = lax.broadcasted_iota(I32, (1, LANES), 1) < HEAD_DIM
        drow = _pair_lanes(decq[0:1, :], c0, c1)
        s_old = self.sv[idx]
        carried = _dot(qp.astype(BF16), s_old.astype(BF16)) * eqp
        u = _dot_tn(kw.astype(BF16), vp.astype(BF16))
        same_head = ((lax.broadcasted_iota(I32, (LANES, LANES), 0) < HEAD_DIM)
                     == (lax.broadcasted_iota(I32, (LANES, LANES), 1) < HEAD_DIM))
        self.sv[idx] = s_old * drow + jnp.where(same_head, u, 0.0)
        qn = None
        if ml:
            p = idx - 6
            n_old = self.nrow[p:p + 1, :]
            qn_l = qp * n_old
            qn = (jnp.sum(jnp.where(lm0, qn_l, 0.0), axis=-1, keepdims=True),
                  jnp.sum(jnp.where(lm0, 0.0, qn_l), axis=-1, keepdims=True))
            self.nrow[p:p + 1, :] = n_old * drow + jnp.sum(kw, axis=0, keepdims=True)
        return carried, qn


def _mixer_prompt_kernel(*refs):
    sq = SEQ_PER_STEP
    proj_refs = refs[:sq]
    (cos_ref, sin_ref, ptab_ref, cw_ref, cb_ref, dsk_ref, gs_ref, gr_ref, gm_ref,
     y_ref, sv_o, conv_o, n_o, m_o) = refs[sq:sq + 14]
    scr = refs[sq + 14:]
    sv, nrow, mrow, cbuf = scr[0:sq], scr[sq:2 * sq], scr[2 * sq:3 * sq], scr[3 * sq:4 * sq]
    ret_cache = scr[4 * sq]
    ci = pl.program_id(1)
    rows = proj_refs[0].shape[1]
    ri = lax.broadcasted_iota(I32, (rows, rows), 0)
    cj = lax.broadcasted_iota(I32, (rows, rows), 1)
    mask = cj <= ri
    tril = jnp.where(mask, 1.0, 0.0).astype(BF16)
    last_row = lambda a: a[rows - 1:rows, :]

    @pl.when(ci == 0)
    def _():
        for s in range(sq):
            sv[s][...] = jnp.zeros_like(sv[s])
            nrow[s][...] = jnp.zeros_like(nrow[s])
            mrow[s][...] = jnp.zeros_like(mrow[s])
            cbuf[s][0:8, :] = jnp.zeros((8, SSD_XBC), F32)
        for p, fs in enumerate(_ret_factors(ptab_ref[0], tril, mask, None, last_row)):
            for k, f in enumerate(fs):
                ret_cache[4 * p + k] = f

    ret_factors = lambda p: tuple(ret_cache[4 * p + k] for k in range(4))
    gens = []
    for s in range(sq):
        proj_ref = proj_refs[s]
        cbuf[s][8:8 + rows, :] = proj_ref[0, :, P_XBC:P_RET]
        us = [cbuf[s][pl.ds(5 + k, rows), :] for k in range(SSD_CONV)]
        gens.append(_mixer_core(
            proj_ref[0, :, P_Z:P_XBC], us, proj_ref[0, :, P_RET:P_ML], proj_ref[0, :, P_ML:P_SM],
            proj_ref[0, :, P_SM:P_W], cos_ref[...], sin_ref[...], ptab_ref[0], cw_ref[0], cb_ref[0],
            dsk_ref[0], gs_ref[0], gr_ref[0], gm_ref[0],
            mask, tril, None, rows, mrow[s][0:1, :], last_row,
            _CarriedState(sv[s], nrow[s]), y_ref.at[s], ret_factors))
    for s, m_t in enumerate(_run_interleaved(gens)):
        mrow[s][0:1, :] = m_t[rows - 1:rows, :]
        cbuf[s][0:8, :] = cbuf[s][rows:rows + 8, :]

    @pl.when(ci == pl.num_programs(1) - 1)
    def _():
        top = _half_rows()
        for s in range(sq):
            for idx in range(N_PAIRS):
                bd = sv[s][idx]
                sv_o[s, idx] = jnp.where(top, bd[:, :HEAD_DIM], bd[:, HEAD_DIM:])
            conv_o[s] = cbuf[s][0:8, :]
            n_o[s] = nrow[s][...]
            m_o[s] = mrow[s][...]


def _mixer_prompt(proj, nb, t, cos, sin, consts, l):
    rows = math.gcd(t, CHUNK)
    nc = t // rows
    sq = SEQ_PER_STEP
    assert nb % sq == 0

    def proj_spec(s):
        return pl.BlockSpec((1, rows, P_W), lambda b, c: (0, (sq * b + s) * nc + c, 0))

    def seq_spec(*tail):
        return pl.BlockSpec((sq,) + tail, lambda b, c: (b,) + (0,) * len(tail))

    outs = pl.pallas_call(
        _mixer_prompt_kernel,
        grid=(nb // sq, nc),
        in_specs=[proj_spec(s) for s in range(sq)]
                 + [pl.BlockSpec((rows, RET_WIDTH), lambda b, c: (c, 0)),
                    pl.BlockSpec((rows, RET_WIDTH), lambda b, c: (c, 0))] + [_layer_spec(a, l) for a in consts],
        out_specs=[pl.BlockSpec((sq, rows, D_MODEL), lambda b, c: (b, c, 0)),
                   seq_spec(N_PAIRS, LANES, HEAD_DIM), seq_spec(8, SSD_XBC), seq_spec(8, LANES), seq_spec(8, LANES)],
        out_shape=[jax.ShapeDtypeStruct((nb, t, D_MODEL), BF16),
                   jax.ShapeDtypeStruct((nb, N_PAIRS, LANES, HEAD_DIM), F32),
                   jax.ShapeDtypeStruct((nb, 8, SSD_XBC), F32),
                   jax.ShapeDtypeStruct((nb, 8, LANES), F32),
                   jax.ShapeDtypeStruct((nb, 8, LANES), F32)],
        scratch_shapes=[pltpu.VMEM((N_PAIRS, LANES, LANES), F32) for _ in range(sq)]
                       + [pltpu.VMEM((8, LANES), F32) for _ in range(2 * sq)]
                       + [pltpu.VMEM((rows + 8, SSD_XBC), F32) for _ in range(sq)]
                       + [pltpu.VMEM((2 * RET_HEADS, rows, LANES), F32)],
        compiler_params=_cparams(("arbitrary", "arbitrary")),
        name="mixer_prompt",
    )(*([proj[None]] * sq), cos, sin, *consts)
    y, sv, conv, n, m = outs
    sv = sv.reshape(nb, 2 * N_PAIRS, HEAD_DIM, HEAD_DIM)
    states = (sv[:, :8], conv[:, 5:8], sv[:, 8:12], sv[:, 12:16],
              n[:, 0:2].reshape(nb, ML_HEADS, HEAD_DIM), m[:, 0, C_ML:C_ML + 4])
    return y.reshape(nb * t, D_MODEL), states


class _PerSequenceState:
    def __init__(self, s_in, s_out, n_in, n_out, q_s, e_s, k_s, v_s, d_s, o_s, qn_s):
        self.s_in, self.s_out, self.n_in, self.n_out = s_in, s_out, n_in, n_out
        self.q_s, self.e_s, self.k_s, self.v_s, self.d_s, self.o_s, self.qn_s = q_s, e_s, k_s, v_s, d_s, o_s, qn_s

    @staticmethod
    def _pair_ref(refs, idx):
        return (refs[0], 2 * idx) if idx < 4 else ((refs[1], 2 * idx - 8) if idx < 6 else (refs[2], 2 * idx - 12))

    def step(self, idx, qp, eqp, kw, vp, decq, c0, c1, ml):
        lane = lax.broadcasted_iota(I32, (1, LANES), 1)
        lm0 = lane < HEAD_DIM
        sin_ref, pi = self._pair_ref(self.s_in, idx)
        sout_ref, _ = self._pair_ref(self.s_out, idx)
        self.q_s[...] = qp
        self.e_s[...] = eqp
        self.k_s[...] = kw
        self.v_s[...] = vp
        self.d_s[...] = jnp.where(lm0, decq[:, c0:c0 + 1], decq[:, c1:c1 + 1])
        top = _half_rows()
        nseq = self.q_s.shape[0] // S_ROWS

        def body(b, carry):
            r0 = pl.multiple_of(b * S_ROWS, S_ROWS)
            rs = pl.ds(r0, S_ROWS)
            s_old = jnp.concatenate([sin_ref[b, pi], sin_ref[b, pi + 1]], axis=0)
            sb = s_old.astype(BF16)
            q = self.q_s[rs, :]
            q0 = jnp.where(lm0, q, 0.0)
            q1 = jnp.where(lm0, 0.0, q)
            carried = jnp.concatenate([_dot(q0.astype(BF16), sb), _dot(q1.astype(BF16), sb)], axis=1)
            self.o_s[rs, :] = carried * self.e_s[rs, :]
            kwb = self.k_s[rs, :]
            v = self.v_s[rs, :]
            u0 = _dot_tn(kwb.astype(BF16), v[:, :HEAD_DIM].astype(BF16))
            u1 = _dot_tn(kwb.astype(BF16), v[:, HEAD_DIM:].astype(BF16))
            drow = self.d_s[pl.ds(r0, 1), :]
            dcol = jnp.where(top, drow[:, 0:1], drow[:, HEAD_DIM:HEAD_DIM + 1])
            s_new = s_old * dcol + jnp.where(top, u0, u1)
            sout_ref[b, pi] = s_new[:HEAD_DIM]
            sout_ref[b, pi + 1] = s_new[HEAD_DIM:]
            if ml:
                p = idx - 6
                n_old = self.n_in[b, p]
                qn_l = q * n_old
                qn0 = jnp.sum(jnp.where(lm0, qn_l, 0.0), axis=-1, keepdims=True)
                qn1 = jnp.sum(jnp.where(lm0, 0.0, qn_l), axis=-1, keepdims=True)
                self.qn_s[rs, :] = jnp.where(lm0, qn0, qn1)
                self.n_out[b, p] = n_old * drow + jnp.sum(kwb, axis=0, keepdims=True)
            return carry

        lax.fori_loop(0, nseq, body, 0, unroll=S_UNROLL)
        qn = None
        if ml:
            qn = (self.qn_s[:, 0:1], self.qn_s[:, HEAD_DIM:HEAD_DIM + 1])
        return self.o_s[...], qn


def _mixer_sample_kernel(proj_ref, tail_ref, mprev_ref, cos_ref, sin_ref, ssd_in, ret_in, c_in, n_in,
                         ptab_ref, cw_ref, cb_ref, dsk_ref, gs_ref, gr_ref, gm_ref,
                         y_ref, u_o, m_o, ssd_out, ret_out, c_out, n_out,
                         q_s, e_s, k_s, v_s, d_s, o_s, qn_s):
    s_in, s_out = (ssd_in.at[0], ret_in.at[0], c_in.at[0]), (ssd_out, ret_out, c_out)
    rows = proj_ref.shape[0]
    ri = lax.broadcasted_iota(I32, (rows, rows), 0)
    cj = lax.broadcasted_iota(I32, (rows, rows), 1)
    tj = cj & (S_ROWS - 1)
    same_seq = (ri & ~(S_ROWS - 1)) == (cj & ~(S_ROWS - 1))
    mask = same_seq & (tj >= S_FIRST) & (tj <= S_LAST) & (cj <= ri)
    tril = jnp.where(mask, 1.0, 0.0).astype(BF16)
    last_sel = jnp.where(cj == ((ri & ~(S_ROWS - 1)) + S_LAST), 1.0, 0.0).astype(BF16)
    tr = lax.broadcasted_iota(I32, (rows, 1), 0) & (S_ROWS - 1)
    valid = (tr >= S_FIRST) & (tr <= S_LAST)
    u_full = jnp.where(tr < S_FIRST, tail_ref[...], proj_ref[:, P_XBC:P_RET])
    u_o[...] = u_full
    us = [pltpu.roll(u_full, SSD_CONV - 1 - k, 0) for k in range(SSD_CONV - 1)] + [u_full]
    st = _PerSequenceState(s_in, s_out, n_in.at[0], n_out, q_s, e_s, k_s, v_s, d_s, o_s, qn_s)
    last_fn = lambda a: _dot01(last_sel, a)
    rf = _ret_factors(ptab_ref[0], tril, mask, valid, last_fn)
    m_o[...] = _run_interleaved([_mixer_core(
        proj_ref[:, P_Z:P_XBC], us, proj_ref[:, P_RET:P_ML], proj_ref[:, P_ML:P_SM], proj_ref[:, P_SM:P_W],
        cos_ref[...], sin_ref[...], ptab_ref[0], cw_ref[0], cb_ref[0], dsk_ref[0], gs_ref[0],
        gr_ref[0], gm_ref[0], mask, tril, valid, S_ROWS, mprev_ref[...],
        last_fn, st, y_ref, lambda p: rf[p])])[0]


def _mixer_sample(proj, row0, tail, mprev, cos, sin, ssd, ret, mlc, mln, consts, l):
    n_rows = tail.shape[0]
    rows = S_BLOCK * S_ROWS
    off = row0 // rows

    def rowspec(w):
        return pl.BlockSpec((rows, w), lambda i: (i, 0))

    def sspec_in(a):
        return pl.BlockSpec((1, S_BLOCK) + a.shape[2:], lambda i: (l, i, 0, 0, 0))

    def sspec_out(a):
        return pl.BlockSpec((S_BLOCK,) + a.shape[2:], lambda i: (i, 0, 0, 0))

    states = (ssd, ret, mlc, mln)
    return pl.pallas_call(
        _mixer_sample_kernel,
        grid=(n_rows // rows,),
        in_specs=[pl.BlockSpec((rows, P_W), lambda i: (i + off, 0)), rowspec(SSD_XBC), rowspec(LANES),
                  _const_spec(cos), _const_spec(sin)] + [sspec_in(a) for a in states]
                 + [_layer_spec(a, l) for a in consts],
        out_specs=[rowspec(D_MODEL), rowspec(SSD_XBC), rowspec(LANES)] + [sspec_out(a) for a in states],
        out_shape=[jax.ShapeDtypeStruct((n_rows, D_MODEL), BF16),
                   jax.ShapeDtypeStruct((n_rows, SSD_XBC), F32),
                   jax.ShapeDtypeStruct((n_rows, LANES), F32)]
                  + [jax.ShapeDtypeStruct(a.shape[1:], F32) for a in states],
        scratch_shapes=[pltpu.VMEM((rows, LANES), F32) for _ in range(7)],
        compiler_params=_cparams(("arbitrary",)),
        name="mixer_sample",
    )(proj, tail, mprev, cos, sin, *states, *consts)


def _route(logits):
    lane = lax.broadcasted_iota(I32, logits.shape, 1)
    gmask = (lane >= N_EXPERTS) & (lane < N_EXPERTS + EXPERT_GROUPS)
    gl = jnp.where(gmask, logits, -jnp.inf)
    ge = jnp.exp(gl - jnp.max(gl, axis=-1, keepdims=True))
    gprob = ge / jnp.sum(ge, axis=-1, keepdims=True)
    g_w = jnp.max(gprob, axis=-1, keepdims=True)
    g_idx = jnp.min(jnp.where(gmask & (gprob == g_w), lane - N_EXPERTS, LANES), axis=-1, keepdims=True)
    emask = (lane < N_EXPERTS) & ((lane >> 3) == g_idx)
    el = jnp.where(emask, logits, -jnp.inf)
    ee = jnp.exp(el - jnp.max(el, axis=-1, keepdims=True))
    eprob = ee / jnp.sum(ee, axis=-1, keepdims=True)
    p1 = jnp.max(jnp.where(emask, eprob, -1.0), axis=-1, keepdims=True)
    i1 = jnp.min(jnp.where(emask & (eprob == p1), lane, LANES), axis=-1, keepdims=True)
    rest = emask & (lane != i1)
    p2 = jnp.max(jnp.where(rest, eprob, -1.0), axis=-1, keepdims=True)
    i2 = jnp.min(jnp.where(rest & (eprob == p2), lane, LANES), axis=-1, keepdims=True)
    tot = p1 + p2
    key = g_idx * 64 + (jnp.minimum(i1, i2) & 7) * 8 + (jnp.maximum(i1, i2) & 7)
    return g_w * (p1 / tot), g_w * (p2 / tot), i1, i2, key


def _outproj_kernel(yp, ys, xp, xs, gtp, gts, scp, scs, shp, shs, g_ref, w_ref, wr_ref, br_ref,
                    x1_ref, hx_ref, info_ref, cnt_ref, carry, *, np_tiles):
    i = pl.program_id(0)
    is_p = i < np_tiles

    @pl.when(i == 0)
    def _():
        carry[...] = jnp.zeros_like(carry)

    x1 = _dual_val(xp, xs, is_p) + _mod_val(gtp, gts, is_p) * _dot(_dual_val(yp, ys, is_p), w_ref[0])
    x1_ref[...] = x1
    h2 = _rms(x1) * g_ref[0] * (1.0 + _mod_val(scp, scs, is_p)) + _mod_val(shp, shs, is_p)
    h2b = h2.astype(BF16)
    w1, w2, i1, i2, key = _route(_dot(h2b, wr_ref[0]) + br_ref[0])

    tm = x1.shape[0]
    onehot = lax.broadcasted_iota(I32, (tm, N_CLASS), 1) == key
    tril = (lax.broadcasted_iota(I32, (tm, tm), 1) <= lax.broadcasted_iota(I32, (tm, tm), 0))
    upto = _dot(jnp.where(tril, 1.0, 0.0).astype(BF16), jnp.where(onehot, 1.0, 0.0).astype(BF16))
    seen = carry[0:1, :]
    rank = jnp.sum(jnp.where(onehot, upto - 1.0 + seen, 0.0), axis=-1, keepdims=True)
    seen = seen + upto[tm - 1:tm, :]
    carry[0:1, :] = seen
    cnt_ref[...] = jnp.broadcast_to(seen, cnt_ref.shape)

    lane = lax.broadcasted_iota(I32, (tm, LANES), 1)
    info = jnp.zeros((tm, LANES), F32)
    for c, v in enumerate((w1, w2, i1.astype(F32), i2.astype(F32), key.astype(F32), rank)):
        info = jnp.where(lane == c, v, info)
    hx_ref[:, 0:D_MODEL] = h2b.astype(F32)
    hx_ref[:, D_MODEL:H_EXT] = info
    info_ref[...] = info[:, 0:8]


def _outproj(y, x, gt, sc, sh, g, w, wr, br, l, tp):
    n_prompt, ns = x[0].shape[0], x[1].shape[0]
    n = n_prompt + ns
    tm = TOKEN_TILE
    npt = n_prompt // tm
    ms = _mod_specs(tm, n_prompt, tp)
    return pl.pallas_call(
        functools.partial(_outproj_kernel, np_tiles=npt),
        grid=(n // tm,),
        in_specs=_dual_specs(tm, npt, D_MODEL) + _dual_specs(tm, npt, D_MODEL) + ms + ms + ms
                 + [_layer_spec(a, l) for a in (g, w, wr, br)],
        out_specs=[_row_spec(tm, D_MODEL), _row_spec(tm, H_EXT), _row_spec(tm, 8),
                   pl.BlockSpec((8, N_CLASS), lambda i: (0, 0))],
        out_shape=[jax.ShapeDtypeStruct((n, D_MODEL), F32),
                   jax.ShapeDtypeStruct((n, H_EXT), F32),
                   jax.ShapeDtypeStruct((n, 8), F32),
                   jax.ShapeDtypeStruct((8, N_CLASS), F32)],
        scratch_shapes=[pltpu.VMEM((8, N_CLASS), F32)],
        compiler_params=_cparams(("arbitrary",)),
        name="outproj_router",
    )(*y, *x, *gt, *sc, *sh, g, w, wr, br)


def _scatter_kernel(pos_ref, x_ref, o_hbm, buf, sem):
    i, n = pl.program_id(0), pl.num_programs(0)
    tm = x_ref.shape[0]
    slot = i % 2

    def wait_slot(s):
        pltpu.make_async_copy(buf.at[s], o_hbm.at[pl.ds(0, tm), :], sem.at[s]).wait()

    @pl.when(i >= 2)
    def _():
        wait_slot(slot)

    buf[slot] = x_ref[...]

    def body(r, c):
        dst = pos_ref[i * tm + r]
        pltpu.make_async_copy(buf.at[slot, pl.ds(r, 1), :], o_hbm.at[pl.ds(dst, 1), :], sem.at[slot]).start()
        return c
    lax.fori_loop(0, tm, body, 0, unroll=8)

    @pl.when(i == n - 1)
    def _():
        wait_slot(slot)

        @pl.when(n >= 2)
        def _():
            wait_slot(1 - slot)


def _scatter_rows(x, pos):
    n, width = x.shape
    tm = MOE_TILE
    return pl.pallas_call(
        _scatter_kernel,
        grid_spec=pltpu.PrefetchScalarGridSpec(
            num_scalar_prefetch=1, grid=(n // tm,),
            in_specs=[_row_spec(tm, width)],
            out_specs=pl.BlockSpec(memory_space=pl.ANY),
            scratch_shapes=[pltpu.VMEM((2, tm, width), x.dtype), pltpu.SemaphoreType.DMA((2,))]),
        out_shape=jax.ShapeDtypeStruct((n, width), x.dtype),
        compiler_params=_cparams(("arbitrary",)),
        name="scatter_rows",
    )(pos, x)


def _gather_tiles(idx_ref, src_hbm, buf, sem, tm):
    i, n = pl.program_id(0), pl.num_programs(0)

    def issue(tile, slot):
        def body(r, c):
            row = idx_ref[tile * tm + r]
            pltpu.make_async_copy(src_hbm.at[pl.ds(row, 1), :], buf.at[slot, pl.ds(r, 1), :],
                                  sem.at[slot]).start()
            return c
        lax.fori_loop(0, tm, body, 0, unroll=8)

    @pl.when(i == 0)
    def _():
        issue(0, 0)

    @pl.when(i + 1 < n)
    def _():
        issue(i + 1, (i + 1) % 2)

    slot = i % 2
    pltpu.make_async_copy(src_hbm.at[pl.ds(0, tm), :], buf.at[slot], sem.at[slot]).wait()
    return slot


def _experts_kernel(tile_ref, grp_ref, flag_ref, hx_ref, wgu_ref, wd_ref, o_ref):
    j = pl.program_id(0)
    flags = flag_ref[j]
    valid = (flags & 1) != 0

    @pl.when(valid & ((flags & 2) != 0))
    def _():
        o_ref[...] = jnp.zeros_like(o_ref)

    @pl.when(valid)
    def _():
        h = hx_ref[:, 0:D_MODEL].astype(BF16)
        r = hx_ref[:, D_MODEL:H_EXT]
        w1, w2, i1, i2 = r[:, 0:1], r[:, 1:2], r[:, 2:3], r[:, 3:4]
        base = grp_ref[j] * EXPERTS_PER_GROUP
        for e in range(EXPERTS_PER_GROUP):
            @pl.when(((flags >> (8 + e)) & 1) != 0)
            def _():
                eid = (base + e).astype(F32)
                ge = jnp.where(i1 == eid, w1, 0.0) + jnp.where(i2 == eid, w2, 0.0)
                au = _dot(h, wgu_ref[0, 0, e])
                act = _silu(au[:, :EXPERT_FF]) * au[:, EXPERT_FF:]
                o_ref[...] += ge * _dot(act.astype(BF16), wd_ref[0, 0, e])


def _experts(hx_sorted, tile, grp, flags, wgu, wd, l):
    n = hx_sorted.shape[0]
    tm = MOE_TILE
    return pl.pallas_call(
        _experts_kernel,
        grid_spec=pltpu.PrefetchScalarGridSpec(
            num_scalar_prefetch=3, grid=(tile.shape[0],),
            in_specs=[pl.BlockSpec((tm, H_EXT), lambda j, t, g, f: (t[j], 0)),
                      pl.BlockSpec((1, 1, EXPERTS_PER_GROUP, D_MODEL, 2 * EXPERT_FF),
                                   lambda j, t, g, f: (l, g[j], 0, 0, 0)),
                      pl.BlockSpec((1, 1, EXPERTS_PER_GROUP, EXPERT_FF, D_MODEL),
                                   lambda j, t, g, f: (l, g[j], 0, 0, 0))],
            out_specs=pl.BlockSpec((tm, D_MODEL), lambda j, t, g, f: (t[j], 0))),
        out_shape=jax.ShapeDtypeStruct((n, D_MODEL), F32),
        compiler_params=_cparams(("arbitrary",)),
        name="moe_experts",
    )(tile, grp, flags, hx_sorted, wgu, wd)


def _combine_kernel(pos_ref, y_hbm, x1_ref, gtp, gts, gf_ref, op_ref, os_ref, buf, sem, *, np_tiles, final):
    slot = _gather_tiles(pos_ref, y_hbm, buf, sem, x1_ref.shape[0])
    is_p = pl.program_id(0) < np_tiles
    x2 = x1_ref[...] + _mod_val(gtp, gts, is_p) * buf[slot]
    out = _rms(x2) * gf_ref[...] if final else x2

    @pl.when(is_p)
    def _():
        op_ref[...] = out

    @pl.when(jnp.logical_not(is_p))
    def _():
        os_ref[...] = out


def _combine(y_sorted, pos, x1, gt, gf, n_prompt, tp, final):
    n = x1.shape[0]
    tm = MOE_TILE
    npt = n_prompt // tm
    return pl.pallas_call(
        functools.partial(_combine_kernel, np_tiles=npt, final=final),
        grid_spec=pltpu.PrefetchScalarGridSpec(
            num_scalar_prefetch=1, grid=(n // tm,),
            in_specs=[pl.BlockSpec(memory_space=pl.ANY), _row_spec(tm, D_MODEL)]
                     + _mod_specs(tm, n_prompt, tp) + [_const_spec(gf)],
            out_specs=_dual_specs(tm, npt, D_MODEL),
            scratch_shapes=[pltpu.VMEM((2, tm, D_MODEL), F32), pltpu.SemaphoreType.DMA((2,))]),
        out_shape=[jax.ShapeDtypeStruct((n_prompt, D_MODEL), F32),
                   jax.ShapeDtypeStruct((n - n_prompt, D_MODEL), F32)],
        compiler_params=_cparams(("arbitrary",)),
        name="moe_combine",
    )(pos, y_sorted, x1, *gt, gf)


def _routing_tables(cnt, key, rank, n_tiles, tm):
    n_items = n_tiles + EXPERT_GROUPS - 1
    c = jnp.arange(N_CLASS, dtype=I32)
    start_c = jnp.sum(jnp.where(c[:, None] < c[None, :], cnt[:, None], 0), axis=0)
    pos = jnp.sum(jnp.where(key[:, None] == c[None, :], start_c[None, :], 0), axis=1) + rank

    g = jnp.arange(EXPERT_GROUPS, dtype=I32)
    gend = jnp.sum(jnp.where(c[None, :] < 64 * (g[:, None] + 1), cnt[None, :], 0), axis=1)
    t0 = jnp.arange(n_tiles, dtype=I32) * tm
    t1 = t0 + (tm - 1)
    gfirst = jnp.sum((gend[None, :] <= t0[:, None]).astype(I32), axis=1)
    glast = jnp.sum((gend[None, :] <= t1[:, None]).astype(I32), axis=1)
    per_tile = glast - gfirst + 1
    tt = jnp.arange(n_tiles, dtype=I32)
    start_t = jnp.sum(jnp.where(tt[:, None] < tt[None, :], per_tile[:, None], 0), axis=0)
    total = jnp.sum(per_tile)

    j = jnp.arange(n_items, dtype=I32)
    valid = j < total
    tile = jnp.sum((start_t[None, :] <= j[:, None]).astype(I32), axis=1) - 1
    tile = jnp.where(valid, tile, n_tiles - 1)
    sel = tile[:, None] == tt[None, :]
    pick = lambda v: jnp.sum(jnp.where(sel, v[None, :], 0), axis=1)
    grp = jnp.where(valid, pick(gfirst) + (j - pick(start_t)), glast[n_tiles - 1])
    first = valid & (j == pick(start_t))

    in_tile = (cnt[None, :] > 0) & (start_c[None, :] <= t1[:, None]) & ((start_c + cnt)[None, :] > t0[:, None])
    e = jnp.arange(N_EXPERTS, dtype=I32)
    member = (((c >> 6) * 8 + ((c >> 3) & 7))[:, None] == e[None, :]) | (((c >> 6) * 8 + (c & 7))[:, None] == e[None, :])
    present_t = jnp.any(in_tile[:, :, None] & member[None, :, :], axis=1)
    present_j = jnp.any(sel[:, :, None] & present_t[None, :, :], axis=1)
    eg = e[None, :] - grp[:, None] * EXPERTS_PER_GROUP
    bits = jnp.sum(jnp.where(present_j & (eg >= 0) & (eg < EXPERTS_PER_GROUP),
                             1 << (8 + jnp.clip(eg, 0, EXPERTS_PER_GROUP - 1)), 0), axis=1)
    flags = valid.astype(I32) | (first.astype(I32) << 1) | bits
    return pos.astype(I32), tile.astype(I32), grp.astype(I32), flags.astype(I32)


def _moe(hx, info, counts, x1, gt, gf, wgu, wd, l, n_prompt, tp, final):
    n = hx.shape[0]
    tm = MOE_TILE
    pos, tile, grp, flags = _routing_tables(counts[0].astype(I32), info[:, 4].astype(I32),
                                            info[:, 5].astype(I32), n // tm, tm)
    y_sorted = _experts(_scatter_rows(hx, pos), tile, grp, flags, wgu, wd, l)
    return _combine(y_sorted, pos, x1, gt, gf, n_prompt, tp, final)


def _rope_tables(pos):
    half = HEAD_DIM // 2
    inv = ROPE_BASE ** (-jnp.arange(half, dtype=F32) / half)
    ang = pos.astype(F32)[:, None] * inv[None, :]
    cos, sin = jnp.cos(ang), jnp.sin(ang)
    cos_t = jnp.tile(jnp.concatenate([cos, cos], axis=-1), (1, RET_HEADS))
    sin_t = jnp.tile(jnp.concatenate([-sin, sin], axis=-1), (1, RET_HEADS))
    return cos_t, sin_t


def _mixer_consts(w_in, conv_w, conv_b, dt_bias, a_log, d_skip, g_ssd_norm, g_ret_norm,
                  b_mlstm_i, b_mlstm_f, g_mlstm_norm):
    nl = w_in.shape[0]
    w_p = jnp.concatenate([w_in[:, :, 0:1280], w_in[:, :, 1288:3336], w_in[:, :, 1280:1288],
                           w_in[:, :, 3336:3344], jnp.zeros((nl, D_MODEL, LANES - 16), F32)], axis=2).astype(BF16)
    pad = lambda v: jnp.pad(v, ((0, 0), (0, LANES - v.shape[1])))
    log_gamma = jnp.log(1.0 - 2.0 ** (-5.0 - jnp.arange(RET_HEADS, dtype=F32)))
    lg = jnp.broadcast_to(jnp.concatenate([jnp.zeros((8,), F32), log_gamma])[None, :], (nl, 12))
    ptab = jnp.stack([pad(jnp.concatenate([dt_bias, b_mlstm_i, b_mlstm_f], axis=1)), pad(a_log), pad(lg)]
                     + [jnp.zeros((nl, LANES), F32)] * 5, axis=1)
    consts = (ptab, conv_w, conv_b[:, None, :], jnp.repeat(d_skip, HEAD_DIM, axis=1)[:, None, :],
              g_ssd_norm[:, None, :], g_ret_norm[:, None, :], g_mlstm_norm[:, None, :])
    return w_p, consts


def kernel(x_prompt, x_sample, state_ssd, state_ssd_conv, state_ret, state_mlstm_c, state_mlstm_n,
           state_mlstm_m, c_prompt, c_sample, w_ada, b_ada, g_norm1, g_norm2, w_in, conv_w, conv_b,
           dt_bias, a_log, d_skip, g_ssd_norm, g_ret_norm, b_mlstm_i, b_mlstm_f, g_mlstm_norm, w_out,
           w_router_group, b_router_group, w_router_expert, b_router_expert, w_gate_up, w_down, g_final):
    bp, tp, _ = x_prompt.shape
    bs, ts, _ = x_sample.shape
    assert ts == S_LAST - S_FIRST + 1
    n_srows = bs * S_ROWS
    n_prompt = bp * tp

    mod = _ada(jnp.concatenate([c_prompt, c_sample], axis=0), w_ada, b_ada)

    def mods(l):
        m = mod[l].reshape(bp + bs, 6, D_MODEL)
        return [(m[:bp, k][:, None, :], jnp.repeat(m[bp:, k], S_ROWS, axis=0)) for k in range(6)]

    cos_p, sin_p = _rope_tables(jnp.arange(tp, dtype=I32))
    t8 = jnp.clip(jnp.arange(S_ROWS, dtype=I32) - S_FIRST, 0, ts - 1)
    cos_s, sin_s = _rope_tables(PAST_LEN + t8)
    cos_s, sin_s = jnp.tile(cos_s, (S_BLOCK, 1)), jnp.tile(sin_s, (S_BLOCK, 1))

    w_p, consts = _mixer_consts(w_in, conv_w, conv_b, dt_bias, a_log, d_skip, g_ssd_norm, g_ret_norm,
                                b_mlstm_i, b_mlstm_f, g_mlstm_norm)
    w_o = w_out.astype(BF16)
    zpad = LANES - N_EXPERTS - EXPERT_GROUPS
    wr = jnp.concatenate([w_router_expert, w_router_group, jnp.zeros((DEPTH, D_MODEL, zpad), F32)],
                         axis=2).astype(BF16)
    br = jnp.concatenate([b_router_expert, b_router_group, jnp.zeros((DEPTH, zpad), F32)], axis=1)[:, None, :]
    wgu = w_gate_up.astype(BF16).reshape(DEPTH, EXPERT_GROUPS, EXPERTS_PER_GROUP, D_MODEL, 2 * EXPERT_FF)
    wd = w_down.astype(BF16).reshape(DEPTH, EXPERT_GROUPS, EXPERTS_PER_GROUP, EXPERT_FF, D_MODEL)
    g1, g2, gf = g_norm1[:, None, :], g_norm2[:, None, :], g_final[None, :]
    n_all = state_mlstm_n.reshape(DEPTH, bs, 2, 1, LANES)

    x = (x_prompt.reshape(n_prompt, D_MODEL),
         jnp.pad(x_sample, ((0, 0), (S_FIRST, S_ROWS - 1 - S_LAST), (0, 0))).reshape(n_srows, D_MODEL))
    p_states, s_states = [], []
    for l in range(DEPTH):
        final = l == DEPTH - 1
        sh1, sc1, gt1, sh2, sc2, gt2 = mods(l)
        proj = _inproj(x, sc1, sh1, g1, w_p, l, tp)

        ycat_p, st = _mixer_prompt(proj, bp, tp, cos_p, sin_p, consts, l)
        p_states.append(st)

        tail = jnp.pad(state_ssd_conv[l], ((0, 0), (0, S_ROWS - SSD_CONV + 1), (0, 0))).reshape(n_srows, SSD_XBC)
        mprev = jnp.pad(jnp.repeat(state_mlstm_m[l], S_ROWS, axis=0), ((0, 0), (C_ML, LANES - C_ML - ML_HEADS)))
        ycat_s, u_full, m_rows, ssd_n, ret_n, c_n, n_n = _mixer_sample(
            proj, n_prompt, tail, mprev, cos_s, sin_s, state_ssd, state_ret, state_mlstm_c, n_all, consts, l)
        s_states.append((ssd_n, u_full.reshape(bs, S_ROWS, SSD_XBC)[:, S_LAST - 2:S_LAST + 1], ret_n, c_n,
                         n_n.reshape(bs, ML_HEADS, HEAD_DIM),
                         m_rows.reshape(bs, S_ROWS, LANES)[:, S_LAST, C_ML:C_ML + ML_HEADS]))

        x1, hx, info, counts = _outproj((ycat_p, ycat_s), x, gt1, sc2, sh2, g2, w_o, wr, br, l, tp)
        x = _moe(hx, info, counts, x1, gt2, gf, wgu, wd, l, n_prompt, tp, final)

    y_prompt = x[0].reshape(bp, tp, D_MODEL)
    y_sample = x[1].reshape(bs, S_ROWS, D_MODEL)[:, S_FIRST:S_LAST + 1]
    p_st = [jnp.stack([s[i] for s in p_states], axis=0) for i in range(6)]
    s_st = [jnp.stack([s[i] for s in s_states], axis=0) for i in range(6)]
    return (y_prompt, y_sample, *p_st, *s_st)
```

```python
import functools
import math

import jax
import jax.numpy as jnp
from jax import lax
from jax.experimental import pallas as pl
from jax.experimental.pallas import tpu as pltpu

F32 = jnp.float32
BF16 = jnp.bfloat16
I32 = jnp.int32

D_MODEL = 1024
DEPTH = 2
PAST_LEN = 16384
SSD_HEADS = 8
SSD_WIDTH = 512
SSD_GROUPS = 2
SSD_STATE = 64
SSD_CONV = 4
SSD_XBC = 768
RET_HEADS = 4
RET_WIDTH = 256
ML_HEADS = 4
ML_WIDTH = 256
HEAD_DIM = 64
ROPE_BASE = 10000.0
EPS = 1e-6
EXPERT_GROUPS = 4
EXPERTS_PER_GROUP = 8
N_EXPERTS = 32
EXPERT_FF = 256

LANES = 128
CHUNK = 128
SEQ_PER_STEP = 2
P_Z, P_XBC, P_RET, P_ML, P_SM, P_W = 0, 512, 1280, 2304, 3328, 3456
C_SSD, C_RET, C_ML = 0, 8, 12
N_PAIRS = 8
NEG = -1e30
TOKEN_TILE = 512
MOE_TILE = 256
VMEM_LIMIT = 56 * 1024 * 1024
H_EXT = D_MODEL + LANES
N_CLASS = 256


def _cparams(sem):
    return pltpu.CompilerParams(dimension_semantics=sem, vmem_limit_bytes=VMEM_LIMIT)


def _split3(x):
    x1 = x.astype(BF16)
    r = x - x1.astype(F32)
    x2 = r.astype(BF16)
    r = r - x2.astype(F32)
    return x1, x2, r.astype(BF16)


def _dot01(m01, x):
    return sum(jnp.dot(m01, p, preferred_element_type=F32) for p in _split3(x))


def _dot(a, b):
    return jnp.dot(a, b, preferred_element_type=F32)


def _dot_nt(a, b):
    return lax.dot_general(a, b, (((1,), (1,)), ((), ())), preferred_element_type=F32)


def _dot_tn(a, b):
    return lax.dot_general(a, b, (((0,), (0,)), ((), ())), preferred_element_type=F32)


def _softplus(x):
    return jnp.maximum(x, 0.0) + jnp.log1p(jnp.exp(-jnp.abs(x)))


def _silu(x):
    return x * jax.nn.sigmoid(x)


def _rms(x):
    return x * lax.rsqrt(jnp.mean(x * x, axis=-1, keepdims=True) + EPS)


def _cummax_rows(x, seg):
    t = lax.broadcasted_iota(I32, x.shape, 0) & (seg - 1)
    s = 1
    while s < seg:
        x = jnp.maximum(x, jnp.where(t >= s, pltpu.roll(x, s, 0), NEG))
        s *= 2
    return x


def _rope(x, cos, sin_signed, lane):
    swapped = jnp.where((lane & 63) < 32, pltpu.roll(x, 96, 1), pltpu.roll(x, 32, 1))
    return x * cos + swapped * sin_signed


def _row_spec(tm, width):
    return pl.BlockSpec((tm, width), lambda i, *_: (i, 0))


def _const_spec(a):
    nd = a.ndim
    return pl.BlockSpec(a.shape, lambda *_: (0,) * nd)


def _layer_spec(a, l):
    nd = a.ndim
    return pl.BlockSpec((1,) + a.shape[1:], lambda *_: (l,) + (0,) * (nd - 1))


def _dual_specs(tm, np_tiles, width):
    return [pl.BlockSpec((tm, width), lambda i, *_: (jnp.minimum(i, np_tiles - 1), 0)),
            pl.BlockSpec((tm, width), lambda i, *_: (jnp.maximum(i - np_tiles, 0), 0))]


def _mod_specs(tm, n_prompt, tp):
    np_tiles, last_b = n_prompt // tm, n_prompt // tp - 1
    return [pl.BlockSpec((1, 1, D_MODEL), lambda i, *_: (jnp.minimum(i * tm // tp, last_b), 0, 0)),
            pl.BlockSpec((tm, D_MODEL), lambda i, *_: (jnp.maximum(i - np_tiles, 0), 0))]


def _mod_val(p_ref, s_ref, is_prompt):
    return jnp.where(is_prompt, p_ref[0], s_ref[...])


def _dual_val(p_ref, s_ref, is_prompt):
    return jnp.where(is_prompt, p_ref[...], s_ref[...])


def _ada_kernel(c_ref, w_ref, b_ref, o_ref):
    c = c_ref[...]
    o_ref[0] = _dot(_silu(c).astype(BF16), w_ref[0].astype(BF16)) + b_ref[0]


def _ada(c_all, w_ada, b_ada):
    nb = c_all.shape[0]
    tn = 1536
    return pl.pallas_call(
        _ada_kernel,
        grid=(DEPTH, 6 * D_MODEL // tn),
        in_specs=[pl.BlockSpec((nb, D_MODEL), lambda l, j: (0, 0)),
                  pl.BlockSpec((1, D_MODEL, tn), lambda l, j: (l, 0, j)),
                  pl.BlockSpec((1, 1, tn), lambda l, j: (l, 0, j))],
        out_specs=pl.BlockSpec((1, nb, tn), lambda l, j: (l, 0, j)),
        out_shape=jax.ShapeDtypeStruct((DEPTH, nb, 6 * D_MODEL), F32),
        compiler_params=_cparams(("arbitrary", "arbitrary")),
        name="ada_mod",
    )(c_all, w_ada, b_ada.reshape(DEPTH, 1, 6 * D_MODEL))


def _inproj_kernel(xp, xs, scp, scs, shp, shs, g_ref, w_ref, o_ref, *, np_tiles):
    is_p = pl.program_id(0) < np_tiles
    h = _rms(_dual_val(xp, xs, is_p)) * g_ref[0] * (1.0 + _mod_val(scp, scs, is_p)) + _mod_val(shp, shs, is_p)
    o_ref[...] = _dot(h.astype(BF16), w_ref[0])


def _inproj(x, sc, sh, g, w, l, tp):
    n_prompt, ns = x[0].shape[0], x[1].shape[0]
    tm = TOKEN_TILE
    ms = _mod_specs(tm, n_prompt, tp)
    return pl.pallas_call(
        functools.partial(_inproj_kernel, np_tiles=n_prompt // tm),
        grid=((n_prompt + ns) // tm,),
        in_specs=_dual_specs(tm, n_prompt // tm, D_MODEL) + ms + ms + [_layer_spec(g, l), _layer_spec(w, l)],
        out_specs=_row_spec(tm, P_W),
        out_shape=jax.ShapeDtypeStruct((n_prompt + ns, P_W), F32),
        compiler_params=_cparams(("arbitrary",)),
        name="norm_inproj",
    )(*x, *sc, *sh, g, w)


def _lane_bcast(a, c):
    return jnp.broadcast_to(a[:, c:c + 1], a.shape)


def _pair_lanes(a, c0, c1):
    lm0 = lax.broadcasted_iota(I32, (a.shape[0], LANES), 1) < HEAD_DIM
    return jnp.where(lm0, a[:, c0:c0 + 1], a[:, c1:c1 + 1])


def _ret_factors(ptab, tril, mask, last_fn):
    rows = tril.shape[0]
    lane = lax.broadcasted_iota(I32, (rows, LANES), 1)
    lm0 = lane < HEAD_DIM
    cum = _dot01(tril, jnp.where((lane >= C_RET) & (lane < C_ML), ptab[2:3], 0.0))
    xt = cum.T
    cum_last = last_fn(cum)
    out = []
    for p in range(RET_HEADS // 2):
        c0, c1 = C_RET + 2 * p, C_RET + 2 * p + 1
        b0, b1 = _lane_bcast(cum, c0), _lane_bcast(cum, c1)
        bp = jnp.where(lm0, b0, b1)
        wp = jnp.exp(_pair_lanes(cum_last, c0, c1) - bp)
        out.append((jnp.exp(jnp.where(mask, b0 - xt[c0:c0 + 1, :], -jnp.inf)),
                    jnp.exp(jnp.where(mask, b1 - xt[c1:c1 + 1, :], -jnp.inf)), jnp.exp(bp), wp))
    return out


def _mixer_core(z, us, retb, mlb, small, cos, sin, ptab, cw, cb, dsk, gs, gr, gm,
                mask, tril, seg, mprev, last_fn, st, y_ref, ret_factors):
    rows = small.shape[0]
    lane = lax.broadcasted_iota(I32, (rows, LANES), 1)
    lm0 = lane < HEAD_DIM

    pre = small + ptab[0:1]
    a_neg = -jnp.exp(ptab[1:2])
    dt = _softplus(pre)
    logf = -_softplus(-pre)
    la = jnp.where(lane < C_RET, dt * a_neg,
                   jnp.where(lane < C_ML, ptab[2:3], jnp.where(lane < C_ML + 4, logf, 0.0)))
    cum = _dot01(tril, la)
    ic = pltpu.roll(pre, 4, 1)
    mlm = (lane >= C_ML) & (lane < C_ML + 4)
    d = jnp.where(mlm, ic - cum, NEG)
    m_t = cum + jnp.maximum(mprev, _cummax_rows(d, seg))
    xt = jnp.where(mlm, d, cum).T
    colv = cum - m_t
    cum_last, m_last = last_fn(cum), last_fn(m_t)
    decq = jnp.where(mlm, jnp.exp(cum_last + mprev - m_last), jnp.exp(cum_last))
    yield

    def factors(c0, c1, kind):
        if kind == "ret":
            return ret_factors((c0 - C_RET) // 2)
        if kind == "ssd":
            b0, b1 = _lane_bcast(cum, c0), _lane_bcast(cum, c1)
            bp = jnp.where(lm0, b0, b1)
            wp = jnp.exp(_pair_lanes(cum_last, c0, c1) - bp)
            return (jnp.exp(jnp.where(mask, b0 - xt[c0:c0 + 1, :], -jnp.inf)),
                    jnp.exp(jnp.where(mask, b1 - xt[c1:c1 + 1, :], -jnp.inf)), jnp.exp(bp), wp)
        a0, a1 = _lane_bcast(colv, c0), _lane_bcast(colv, c1)
        eqp = jnp.exp(jnp.where(lm0, a0, a1) + _pair_lanes(mprev, c0, c1))
        wp = jnp.exp(jnp.where(lm0, _lane_bcast(d, c0), _lane_bcast(d, c1))
                     + _pair_lanes(cum_last, c0, c1) - _pair_lanes(m_last, c0, c1))
        return (jnp.exp(jnp.where(mask, a0 + xt[c0:c0 + 1, :], -jnp.inf)),
                jnp.exp(jnp.where(mask, a1 + xt[c1:c1 + 1, :], -jnp.inf)), eqp, wp)

    def pair(idx, qp, kp, vp, c0, c1, kind):
        d0, d1, eqp, wp = factors(c0, c1, kind)
        q0 = jnp.where(lm0, qp, 0.0)
        q1 = jnp.where(lm0, 0.0, qp)
        sc = _dot_nt(jnp.concatenate([q0, q1], axis=0).astype(BF16), kp.astype(BF16))
        s0, s1 = sc[0:rows] * d0, sc[rows:2 * rows] * d1
        v01 = jnp.concatenate([jnp.where(lm0, vp, 0.0), jnp.where(lm0, 0.0, vp)], axis=0).astype(BF16)
        intra = _dot(jnp.concatenate([s0, s1], axis=1).astype(BF16), v01)
        kw = kp * wp
        carried, qn = st.step(idx, qp, eqp, kw, vp, decq, c0, c1, kind == "ml")
        return intra + carried, s0, s1, qn

    def head_norm(o):
        o2 = o * o
        ms0 = jnp.sum(jnp.where(lm0, o2, 0.0), axis=-1, keepdims=True) * (1.0 / HEAD_DIM)
        ms1 = jnp.sum(jnp.where(lm0, 0.0, o2), axis=-1, keepdims=True) * (1.0 / HEAD_DIM)
        return o * jnp.where(lm0, lax.rsqrt(ms0 + EPS), lax.rsqrt(ms1 + EPS))

    conv = cb + us[0] * cw[0:1] + us[1] * cw[1:2] + us[2] * cw[2:3] + us[3] * cw[3:4]
    xc = _silu(conv)
    bb = xc[:, SSD_WIDTH:SSD_WIDTH + LANES]
    cc = xc[:, SSD_WIDTH + LANES:SSD_WIDTH + 2 * LANES]
    br = pltpu.roll(bb, HEAD_DIM, 1)
    cr = pltpu.roll(cc, HEAD_DIM, 1)
    ys = []
    for p in range(4):
        if p < 2:
            kp, qp = jnp.where(lm0, bb, br), jnp.where(lm0, cc, cr)
        else:
            kp, qp = jnp.where(lm0, br, bb), jnp.where(lm0, cr, cc)
        c0, c1 = C_SSD + 2 * p, C_SSD + 2 * p + 1
        sl = slice(LANES * p, LANES * (p + 1))
        xsp = xc[:, sl]
        dtp = jnp.where(lm0, dt[:, c0:c0 + 1], dt[:, c1:c1 + 1])
        o, _, _, _ = pair(p, qp, kp, xsp * dtp, c0, c1, "ssd")
        ys.append((o + dsk[:, sl] * xsp) * _silu(z[:, sl]))
        yield
    for g in range(SSD_GROUPS):
        ya, yb = ys[2 * g], ys[2 * g + 1]
        ms = (jnp.sum(ya * ya, axis=-1, keepdims=True)
              + jnp.sum(yb * yb, axis=-1, keepdims=True)) * (1.0 / (2 * LANES))
        r = lax.rsqrt(ms + EPS)
        for j, yv in ((2 * g, ya), (2 * g + 1, yb)):
            sl = slice(LANES * j, LANES * (j + 1))
            y_ref[:, sl] = (yv * r * gs[:, sl]).astype(y_ref.dtype)

    for p in range(2):
        sl = slice(LANES * p, LANES * (p + 1))
        qp = _rope(retb[:, LANES * p:LANES * (p + 1)], cos[:, sl], sin[:, sl], lane)
        kp = _rope(retb[:, RET_WIDTH + LANES * p:RET_WIDTH + LANES * (p + 1)], cos[:, sl], sin[:, sl], lane)
        kp = kp * (HEAD_DIM ** -0.5)
        vp = retb[:, 2 * RET_WIDTH + LANES * p:2 * RET_WIDTH + LANES * (p + 1)]
        gp = retb[:, 3 * RET_WIDTH + LANES * p:3 * RET_WIDTH + LANES * (p + 1)]
        c0, c1 = C_RET + 2 * p, C_RET + 2 * p + 1
        o, _, _, _ = pair(4 + p, qp, kp, vp, c0, c1, "ret")
        y = head_norm(o) * gr[:, sl] * _silu(gp)
        y_ref[:, SSD_WIDTH + LANES * p:SSD_WIDTH + LANES * (p + 1)] = y.astype(y_ref.dtype)
        yield

    for p in range(2):
        sl = slice(LANES * p, LANES * (p + 1))
        qp = mlb[:, LANES * p:LANES * (p + 1)]
        kp = mlb[:, ML_WIDTH + LANES * p:ML_WIDTH + LANES * (p + 1)] * (HEAD_DIM ** -0.5)
        vp = mlb[:, 2 * ML_WIDTH + LANES * p:2 * ML_WIDTH + LANES * (p + 1)]
        op = mlb[:, 3 * ML_WIDTH + LANES * p:3 * ML_WIDTH + LANES * (p + 1)]
        c0, c1 = C_ML + 2 * p, C_ML + 2 * p + 1
        num, s0, s1, (qn0, qn1) = pair(6 + p, qp, kp, vp, c0, c1, "ml")
        inter0 = jnp.exp(colv[:, c0:c0 + 1] + mprev[:, c0:c0 + 1])
        inter1 = jnp.exp(colv[:, c1:c1 + 1] + mprev[:, c1:c1 + 1])
        den0 = jnp.sum(s0, axis=-1, keepdims=True) + qn0 * inter0
        den1 = jnp.sum(s1, axis=-1, keepdims=True) + qn1 * inter1
        dn0 = jnp.maximum(jnp.abs(den0), jnp.exp(-m_t[:, c0:c0 + 1]))
        dn1 = jnp.maximum(jnp.abs(den1), jnp.exp(-m_t[:, c1:c1 + 1]))
        hh = num / jnp.where(lm0, dn0, dn1)
        y = head_norm(hh) * gm[:, sl] * jax.nn.sigmoid(op)
        off = SSD_WIDTH + RET_WIDTH + LANES * p
        y_ref[:, off:off + LANES] = y.astype(y_ref.dtype)
        yield
    return m_t


def _run_interleaved(gens):
    out = [None] * len(gens)
    live = list(range(len(gens)))
    while live:
        for k in list(live):
            try:
                next(gens[k])
            except StopIteration as stop:
                out[k] = stop.value
                live.remove(k)
    return out


def _half_rows():
    return lax.broadcasted_iota(I32, (LANES, HEAD_DIM), 0) < HEAD_DIM


class _CarriedState:
    def __init__(self, sv, nrow):
        self.sv, self.nrow = sv, nrow

    def step(self, idx, qp, eqp, kw, vp, decq, c0, c1, ml):
        lm0 = lax.broadcasted_iota(I32, (1, LANES), 1) < HEAD_DIM
        drow = _pair_lanes(decq[0:1, :], c0, c1)
        s_old = self.sv[idx]
        carried = _dot(qp.astype(BF16), s_old.astype(BF16)) * eqp
        u = _dot_tn(kw.astype(BF16), vp.astype(BF16))
        same_head = ((lax.broadcasted_iota(I32, (LANES, LANES), 0) < HEAD_DIM)
                     == (lax.broadcasted_iota(I32, (LANES, LANES), 1) < HEAD_DIM))
        self.sv[idx] = s_old * drow + jnp.where(same_head, u, 0.0)
        qn = None
        if ml:
            p = idx - 6
            n_old = self.nrow[p:p + 1, :]
            qn_l = qp * n_old
            qn = (jnp.sum(jnp.where(lm0, qn_l, 0.0), axis=-1, keepdims=True),
                  jnp.sum(jnp.where(lm0, 0.0, qn_l), axis=-1, keepdims=True))
            self.nrow[p:p + 1, :] = n_old * drow + jnp.sum(kw, axis=0, keepdims=True)
        return carried, qn


def _mixer_prompt_kernel(*refs):
    sq = SEQ_PER_STEP
    proj_refs = refs[:sq]
    (cos_ref, sin_ref, ptab_ref, cw_ref, cb_ref, dsk_ref, gs_ref, gr_ref, gm_ref,
     y_ref, sv_o, conv_o, n_o, m_o) = refs[sq:sq + 14]
    scr = refs[sq + 14:]
    sv, nrow, mrow, cbuf = scr[0:sq], scr[sq:2 * sq], scr[2 * sq:3 * sq], scr[3 * sq:4 * sq]
    ret_cache = scr[4 * sq]
    ci = pl.program_id(1)
    rows = proj_refs[0].shape[1]
    ri = lax.broadcasted_iota(I32, (rows, rows), 0)
    cj = lax.broadcasted_iota(I32, (rows, rows), 1)
    mask = cj <= ri
    tril = jnp.where(mask, 1.0, 0.0).astype(BF16)
    last_row = lambda a: a[rows - 1:rows, :]

    @pl.when(ci == 0)
    def _():
        for s in range(sq):
            sv[s][...] = jnp.zeros_like(sv[s])
            nrow[s][...] = jnp.zeros_like(nrow[s])
            mrow[s][...] = jnp.zeros_like(mrow[s])
            cbuf[s][0:8, :] = jnp.zeros((8, SSD_XBC), F32)
        for p, fs in enumerate(_ret_factors(ptab_ref[0], tril, mask, last_row)):
            for k, f in enumerate(fs):
                ret_cache[4 * p + k] = f

    ret_factors = lambda p: tuple(ret_cache[4 * p + k] for k in range(4))
    gens = []
    for s in range(sq):
        proj_ref = proj_refs[s]
        cbuf[s][8:8 + rows, :] = proj_ref[0, :, P_XBC:P_RET]
        us = [cbuf[s][pl.ds(5 + k, rows), :] for k in range(SSD_CONV)]
        gens.append(_mixer_core(
            proj_ref[0, :, P_Z:P_XBC], us, proj_ref[0, :, P_RET:P_ML], proj_ref[0, :, P_ML:P_SM],
            proj_ref[0, :, P_SM:P_W], cos_ref[...], sin_ref[...], ptab_ref[0], cw_ref[0], cb_ref[0],
            dsk_ref[0], gs_ref[0], gr_ref[0], gm_ref[0],
            mask, tril, rows, mrow[s][0:1, :], last_row,
            _CarriedState(sv[s], nrow[s]), y_ref.at[s], ret_factors))
    for s, m_t in enumerate(_run_interleaved(gens)):
        mrow[s][0:1, :] = m_t[rows - 1:rows, :]
        cbuf[s][0:8, :] = cbuf[s][rows:rows + 8, :]

    @pl.when(ci == pl.num_programs(1) - 1)
    def _():
        top = _half_rows()
        for s in range(sq):
            for idx in range(N_PAIRS):
                bd = sv[s][idx]
                sv_o[s, idx] = jnp.where(top, bd[:, :HEAD_DIM], bd[:, HEAD_DIM:])
            conv_o[s] = cbuf[s][0:8, :]
            n_o[s] = nrow[s][...]
            m_o[s] = mrow[s][...]


def _mixer_prompt(proj, nb, t, cos, sin, consts, l):
    rows = math.gcd(t, CHUNK)
    nc = t // rows
    sq = SEQ_PER_STEP
    assert nb % sq == 0

    def proj_spec(s):
        return pl.BlockSpec((1, rows, P_W), lambda b, c: (0, (sq * b + s) * nc + c, 0))

    def seq_spec(*tail):
        return pl.BlockSpec((sq,) + tail, lambda b, c: (b,) + (0,) * len(tail))

    outs = pl.pallas_call(
        _mixer_prompt_kernel,
        grid=(nb // sq, nc),
        in_specs=[proj_spec(s) for s in range(sq)]
                 + [pl.BlockSpec((rows, RET_WIDTH), lambda b, c: (c, 0)),
                    pl.BlockSpec((rows, RET_WIDTH), lambda b, c: (c, 0))] + [_layer_spec(a, l) for a in consts],
        out_specs=[pl.BlockSpec((sq, rows, D_MODEL), lambda b, c: (b, c, 0)),
                   seq_spec(N_PAIRS, LANES, HEAD_DIM), seq_spec(8, SSD_XBC), seq_spec(8, LANES), seq_spec(8, LANES)],
        out_shape=[jax.ShapeDtypeStruct((nb, t, D_MODEL), BF16),
                   jax.ShapeDtypeStruct((nb, N_PAIRS, LANES, HEAD_DIM), F32),
                   jax.ShapeDtypeStruct((nb, 8, SSD_XBC), F32),
                   jax.ShapeDtypeStruct((nb, 8, LANES), F32),
                   jax.ShapeDtypeStruct((nb, 8, LANES), F32)],
        scratch_shapes=[pltpu.VMEM((N_PAIRS, LANES, LANES), F32) for _ in range(sq)]
                       + [pltpu.VMEM((8, LANES), F32) for _ in range(2 * sq)]
                       + [pltpu.VMEM((rows + 8, SSD_XBC), F32) for _ in range(sq)]
                       + [pltpu.VMEM((2 * RET_HEADS, rows, LANES), F32)],
        compiler_params=_cparams(("arbitrary", "arbitrary")),
        name="mixer_prompt",
    )(*([proj[None]] * sq), cos, sin, *consts)
    y, sv, conv, n, m = outs
    sv = sv.reshape(nb, 2 * N_PAIRS, HEAD_DIM, HEAD_DIM)
    states = (sv[:, :8], conv[:, 5:8], sv[:, 8:12], sv[:, 12:16],
              n[:, 0:2].reshape(nb, ML_HEADS, HEAD_DIM), m[:, 0, C_ML:C_ML + 4])
    return y.reshape(nb * t, D_MODEL), states


def _lane_recurrence(s_in, s_out, q_s, k_s, v_s, decay):
    n_tok = len(decay)
    nvb = HEAD_DIM // 8
    nb = q_s.shape[-1]
    dec8 = [jnp.broadcast_to(d, (8, nb)) for d in decay]

    def body(k, acc):
        acc = [list(a) for a in acc]
        qk = [q_s[t, pl.ds(k, 1), :] for t in range(n_tok)]
        kk = [k_s[t, pl.ds(k, 1), :] for t in range(n_tok)]
        for vb in range(nvb):
            rows = pl.ds(8 * vb, 8)
            s = s_in[k, rows, :]
            for t in range(n_tok):
                s = s * dec8[t] + kk[t] * v_s[t, rows, :]
                acc[t][vb] = acc[t][vb] + qk[t] * s
            s_out[k, rows, :] = s
        return tuple(tuple(a) for a in acc)

    init = tuple(tuple(jnp.zeros((8, nb), F32) for _ in range(nvb)) for _ in range(n_tok))
    acc = lax.fori_loop(0, HEAD_DIM, body, init, unroll=2)
    return [jnp.concatenate(list(a), axis=0) for a in acc]


def _conv_T(u_ref, tail_ref, w_ref, b_ref, state_ref):
    n_tok = u_ref.shape[0]
    full = [tail_ref[0, j] for j in range(SSD_CONV - 1)] + [u_ref[t] for t in range(n_tok)]
    outs = []
    for t in range(n_tok):
        acc = b_ref[0]
        for tap in range(SSD_CONV):
            acc = acc + full[t + tap] * w_ref[0, tap]
        outs.append(_silu(acc))
    for j in range(SSD_CONV - 1):
        state_ref[j] = full[n_tok + j]
    return outs


def _ssd_T_kernel(z_ref, xs_ref, b_ref, c_ref, sm_ref, tx_ref, tb_ref, tc_ref, wx_ref, wb_ref, wc_ref,
                  bx_ref, bb_ref, bc_ref, dtb_ref, alog_ref, dsk_ref, g_ref, s_ref,
                  y_ref, so_ref, cx_ref, cb_ref, cc_ref, q_s, k_s, v_s):
    h = pl.program_id(0)
    n_tok = xs_ref.shape[0]
    xs = _conv_T(xs_ref, tx_ref, wx_ref, bx_ref, cx_ref)
    bm = _conv_T(b_ref, tb_ref, wb_ref, bb_ref, cb_ref)
    cm = _conv_T(c_ref, tc_ref, wc_ref, bc_ref, cc_ref)
    a_neg = -jnp.exp(alog_ref[0, 0])
    decay = []
    for t in range(n_tok):
        dt = _softplus(sm_ref[t, pl.ds(C_SSD + h, 1), :] + dtb_ref[0, 0])
        decay.append(jnp.exp(dt * a_neg))
        q_s[t] = cm[t]
        k_s[t] = bm[t]
        v_s[t] = xs[t] * dt
    o = _lane_recurrence(s_ref.at[0, 0], so_ref.at[0], q_s, k_s, v_s, decay)
    hh = h % (SSD_HEADS // SSD_GROUPS)
    row0 = pl.multiple_of(hh * HEAD_DIM, HEAD_DIM)
    for t in range(n_tok):
        y_ref[t, pl.ds(row0, HEAD_DIM), :] = (o[t] + dsk_ref[0, 0] * xs[t]) * _silu(z_ref[t])

    @pl.when(hh == SSD_HEADS // SSD_GROUPS - 1)
    def _():
        for t in range(n_tok):
            blk = y_ref[t]
            ms = jnp.mean(blk * blk, axis=0, keepdims=True)
            y_ref[t] = blk * lax.rsqrt(ms + EPS) * g_ref[0]


def _ret_T_kernel(q_ref, k_ref, v_ref, g_ref, cos_ref, sin_ref, gam_ref, gn_ref, s_ref,
                  y_ref, so_ref, q_s, k_s, v_s):
    n_tok = q_ref.shape[0]
    half = HEAD_DIM // 2

    def rope(x, t):
        x1, x2 = x[:half], x[half:]
        return jnp.concatenate([x1 * cos_ref[t] - x2 * sin_ref[t], x1 * sin_ref[t] + x2 * cos_ref[t]], axis=0)

    for t in range(n_tok):
        q_s[t] = rope(q_ref[t], t)
        k_s[t] = rope(k_ref[t], t) * (HEAD_DIM ** -0.5)
        v_s[t] = v_ref[t]
    o = _lane_recurrence(s_ref.at[0, 0], so_ref.at[0], q_s, k_s, v_s, [gam_ref[0]] * n_tok)
    for t in range(n_tok):
        ms = jnp.mean(o[t] * o[t], axis=0, keepdims=True)
        y_ref[t] = o[t] * lax.rsqrt(ms + EPS) * gn_ref[0] * _silu(g_ref[t])


def _ml_T_kernel(q_ref, k_ref, v_ref, o_ref, sm_ref, bi_ref, bf_ref, gn_ref, c_ref, n_ref, m_ref,
                 y_ref, co_ref, no_ref, mo_ref, q_s, k_s, v_s):
    h = pl.program_id(0)
    n_tok = q_ref.shape[0]
    m = m_ref[0, 0]
    n = n_ref[0, 0]
    decay, qn, m_all = [], [], []
    for t in range(n_tok):
        i_t = sm_ref[t, pl.ds(C_RET + h, 1), :] + bi_ref[0, 0]
        logf = -_softplus(-(sm_ref[t, pl.ds(C_ML + h, 1), :] + bf_ref[0, 0]))
        m_new = jnp.maximum(logf + m, i_t)
        f_t = jnp.exp(logf + m - m_new)
        kw = k_ref[t] * (HEAD_DIM ** -0.5) * jnp.exp(i_t - m_new)
        n = n * f_t + kw
        q_s[t] = q_ref[t]
        k_s[t] = kw
        v_s[t] = v_ref[t]
        decay.append(f_t)
        qn.append(jnp.sum(q_ref[t] * n, axis=0, keepdims=True))
        m_all.append(m_new)
        m = m_new
    num = _lane_recurrence(c_ref.at[0, 0], co_ref.at[0], q_s, k_s, v_s, decay)
    no_ref[0] = n
    mo_ref[0] = m
    for t in range(n_tok):
        hh = num[t] / jnp.maximum(jnp.abs(qn[t]), jnp.exp(-m_all[t]))
        ms = jnp.mean(hh * hh, axis=0, keepdims=True)
        y_ref[t] = hh * lax.rsqrt(ms + EPS) * gn_ref[0] * jax.nn.sigmoid(o_ref[t])


def _mixer_sample_T(projT, convT, cwb, cbb, ssdT, retT, cT, nT, mT, tabs, cos, sin, l):
    n_tok, _, nb = projT.shape
    dtb, alog, dsk, gs, gam, gr, bi, bf, gm = tabs
    hb = lambda col: col // HEAD_DIM
    rep = SSD_HEADS // SSD_GROUPS
    proj_blk = lambda first, div=1: pl.BlockSpec((n_tok, HEAD_DIM, nb), lambda h: (0, first + h // div, 0))
    small = pl.BlockSpec((n_tok, LANES, nb), lambda h: (0, P_SM // LANES, 0))
    lay4 = lambda a, first=0, div=1: pl.BlockSpec((1,) + a.shape[1:2] + (HEAD_DIM, nb),
                                                  lambda h: (l, 0, first + h // div, 0))
    head4 = lambda a: pl.BlockSpec((1, 1) + a.shape[2:], lambda h: (l, h) + (0,) * (a.ndim - 2))
    vm = lambda: pltpu.VMEM((n_tok, HEAD_DIM, nb), F32)
    state_spec = pl.BlockSpec((1, HEAD_DIM, HEAD_DIM, nb), lambda h: (h, 0, 0, 0))
    b0, c0 = hb(SSD_WIDTH), hb(SSD_WIDTH + SSD_GROUPS * SSD_STATE)

    y_ssd, ssd_n, cx, cb, cc = pl.pallas_call(
        _ssd_T_kernel,
        grid=(SSD_HEADS,),
        in_specs=[proj_blk(hb(P_Z)), proj_blk(hb(P_XBC)), proj_blk(hb(P_XBC) + b0, rep),
                  proj_blk(hb(P_XBC) + c0, rep), small,
                  lay4(convT), lay4(convT, b0, rep), lay4(convT, c0, rep),
                  lay4(cwb), lay4(cwb, b0, rep), lay4(cwb, c0, rep),
                  pl.BlockSpec((1, HEAD_DIM, nb), lambda h: (l, h, 0)),
                  pl.BlockSpec((1, HEAD_DIM, nb), lambda h: (l, b0 + h // rep, 0)),
                  pl.BlockSpec((1, HEAD_DIM, nb), lambda h: (l, c0 + h // rep, 0)),
                  head4(dtb), head4(alog), head4(dsk),
                  pl.BlockSpec((1, rep * HEAD_DIM, nb), lambda h: (l, h // rep, 0)), head4(ssdT)],
        out_specs=[pl.BlockSpec((n_tok, rep * HEAD_DIM, nb), lambda h: (0, h // rep, 0)), state_spec,
                   pl.BlockSpec((SSD_CONV - 1, HEAD_DIM, nb), lambda h: (0, h, 0)),
                   pl.BlockSpec((SSD_CONV - 1, HEAD_DIM, nb), lambda h: (0, h // rep, 0)),
                   pl.BlockSpec((SSD_CONV - 1, HEAD_DIM, nb), lambda h: (0, h // rep, 0))],
        out_shape=[jax.ShapeDtypeStruct((n_tok, SSD_WIDTH, nb), F32),
                   jax.ShapeDtypeStruct((SSD_HEADS, HEAD_DIM, HEAD_DIM, nb), F32),
                   jax.ShapeDtypeStruct((SSD_CONV - 1, SSD_WIDTH, nb), F32),
                   jax.ShapeDtypeStruct((SSD_CONV - 1, SSD_GROUPS * SSD_STATE, nb), F32),
                   jax.ShapeDtypeStruct((SSD_CONV - 1, SSD_GROUPS * SSD_STATE, nb), F32)],
        scratch_shapes=[vm(), vm(), vm()],
        compiler_params=_cparams(("arbitrary",)),
        name="sample_ssd",
    )(projT, projT, projT, projT, projT, convT, convT, convT, cwb, cwb, cwb, cbb, cbb, cbb,
      dtb, alog, dsk, gs, ssdT)

    y_ret, ret_n = pl.pallas_call(
        _ret_T_kernel,
        grid=(RET_HEADS,),
        in_specs=[proj_blk(hb(P_RET)), proj_blk(hb(P_RET + RET_WIDTH)), proj_blk(hb(P_RET + 2 * RET_WIDTH)),
                  proj_blk(hb(P_RET + 3 * RET_WIDTH)), _const_spec(cos), _const_spec(sin),
                  pl.BlockSpec((1, 1, nb), lambda h: (h, 0, 0)),
                  pl.BlockSpec((1, HEAD_DIM, nb), lambda h: (l, h, 0)), head4(retT)],
        out_specs=[pl.BlockSpec((n_tok, HEAD_DIM, nb), lambda h: (0, h, 0)), state_spec],
        out_shape=[jax.ShapeDtypeStruct((n_tok, RET_WIDTH, nb), F32),
                   jax.ShapeDtypeStruct((RET_HEADS, HEAD_DIM, HEAD_DIM, nb), F32)],
        scratch_shapes=[vm(), vm(), vm()],
        compiler_params=_cparams(("arbitrary",)),
        name="sample_ret",
    )(projT, projT, projT, projT, cos, sin, gam, gr, retT)

    y_ml, c_n, n_n, m_n = pl.pallas_call(
        _ml_T_kernel,
        grid=(ML_HEADS,),
        in_specs=[proj_blk(hb(P_ML)), proj_blk(hb(P_ML + ML_WIDTH)), proj_blk(hb(P_ML + 2 * ML_WIDTH)),
                  proj_blk(hb(P_ML + 3 * ML_WIDTH)), small, head4(bi), head4(bf),
                  pl.BlockSpec((1, HEAD_DIM, nb), lambda h: (l, h, 0)), head4(cT), head4(nT), head4(mT)],
        out_specs=[pl.BlockSpec((n_tok, HEAD_DIM, nb), lambda h: (0, h, 0)), state_spec,
                   pl.BlockSpec((1, HEAD_DIM, nb), lambda h: (h, 0, 0)),
                   pl.BlockSpec((1, 1, nb), lambda h: (h, 0, 0))],
        out_shape=[jax.ShapeDtypeStruct((n_tok, ML_WIDTH, nb), F32),
                   jax.ShapeDtypeStruct((ML_HEADS, HEAD_DIM, HEAD_DIM, nb), F32),
                   jax.ShapeDtypeStruct((ML_HEADS, HEAD_DIM, nb), F32),
                   jax.ShapeDtypeStruct((ML_HEADS, 1, nb), F32)],
        scratch_shapes=[vm(), vm(), vm()],
        compiler_params=_cparams(("arbitrary",)),
        name="sample_mlstm",
    )(projT, projT, projT, projT, projT, bi, bf, gm, cT, nT, mT)

    y_t = jnp.concatenate([y_ssd, y_ret, y_ml], axis=1)
    conv_n = jnp.concatenate([cx, cb, cc], axis=1)
    return y_t, (ssd_n, conv_n, ret_n, c_n, n_n, m_n)


def _route(logits):
    lane = lax.broadcasted_iota(I32, logits.shape, 1)
    gmask = (lane >= N_EXPERTS) & (lane < N_EXPERTS + EXPERT_GROUPS)
    gl = jnp.where(gmask, logits, -jnp.inf)
    ge = jnp.exp(gl - jnp.max(gl, axis=-1, keepdims=True))
    gprob = ge / jnp.sum(ge, axis=-1, keepdims=True)
    g_w = jnp.max(gprob, axis=-1, keepdims=True)
    g_idx = jnp.min(jnp.where(gmask & (gprob == g_w), lane - N_EXPERTS, LANES), axis=-1, keepdims=True)
    emask = (lane < N_EXPERTS) & ((lane >> 3) == g_idx)
    el = jnp.where(emask, logits, -jnp.inf)
    ee = jnp.exp(el - jnp.max(el, axis=-1, keepdims=True))
    eprob = ee / jnp.sum(ee, axis=-1, keepdims=True)
    p1 = jnp.max(jnp.where(emask, eprob, -1.0), axis=-1, keepdims=True)
    i1 = jnp.min(jnp.where(emask & (eprob == p1), lane, LANES), axis=-1, keepdims=True)
    rest = emask & (lane != i1)
    p2 = jnp.max(jnp.where(rest, eprob, -1.0), axis=-1, keepdims=True)
    i2 = jnp.min(jnp.where(rest & (eprob == p2), lane, LANES), axis=-1, keepdims=True)
    tot = p1 + p2
    key = g_idx * 64 + (jnp.minimum(i1, i2) & 7) * 8 + (jnp.maximum(i1, i2) & 7)
    return g_w * (p1 / tot), g_w * (p2 / tot), i1, i2, key


def _outproj_kernel(yp, ys, xp, xs, gtp, gts, scp, scs, shp, shs, g_ref, w_ref, wr_ref, br_ref,
                    x1_ref, hx_ref, info_ref, cnt_ref, carry, *, np_tiles):
    i = pl.program_id(0)
    is_p = i < np_tiles

    @pl.when(i == 0)
    def _():
        carry[...] = jnp.zeros_like(carry)

    x1 = _dual_val(xp, xs, is_p) + _mod_val(gtp, gts, is_p) * _dot(_dual_val(yp, ys, is_p), w_ref[0])
    x1_ref[...] = x1
    h2 = _rms(x1) * g_ref[0] * (1.0 + _mod_val(scp, scs, is_p)) + _mod_val(shp, shs, is_p)
    h2b = h2.astype(BF16)
    w1, w2, i1, i2, key = _route(_dot(h2b, wr_ref[0]) + br_ref[0])

    tm = x1.shape[0]
    onehot = lax.broadcasted_iota(I32, (tm, N_CLASS), 1) == key
    tril = (lax.broadcasted_iota(I32, (tm, tm), 1) <= lax.broadcasted_iota(I32, (tm, tm), 0))
    upto = _dot(jnp.where(tril, 1.0, 0.0).astype(BF16), jnp.where(onehot, 1.0, 0.0).astype(BF16))
    seen = carry[0:1, :]
    rank = jnp.sum(jnp.where(onehot, upto - 1.0 + seen, 0.0), axis=-1, keepdims=True)
    seen = seen + upto[tm - 1:tm, :]
    carry[0:1, :] = seen
    cnt_ref[...] = jnp.broadcast_to(seen, cnt_ref.shape)

    lane = lax.broadcasted_iota(I32, (tm, LANES), 1)
    info = jnp.zeros((tm, LANES), F32)
    for c, v in enumerate((w1, w2, i1.astype(F32), i2.astype(F32), key.astype(F32), rank)):
        info = jnp.where(lane == c, v, info)
    hx_ref[:, 0:D_MODEL] = h2b.astype(F32)
    hx_ref[:, D_MODEL:H_EXT] = info
    info_ref[...] = info[:, 0:8]


def _outproj(y, x, gt, sc, sh, g, w, wr, br, l, tp):
    n_prompt, ns = x[0].shape[0], x[1].shape[0]
    n = n_prompt + ns
    tm = TOKEN_TILE
    npt = n_prompt // tm
    ms = _mod_specs(tm, n_prompt, tp)
    return pl.pallas_call(
        functools.partial(_outproj_kernel, np_tiles=npt),
        grid=(n // tm,),
        in_specs=_dual_specs(tm, npt, D_MODEL) + _dual_specs(tm, npt, D_MODEL) + ms + ms + ms
                 + [_layer_spec(a, l) for a in (g, w, wr, br)],
        out_specs=[_row_spec(tm, D_MODEL), _row_spec(tm, H_EXT), _row_spec(tm, 8),
                   pl.BlockSpec((8, N_CLASS), lambda i: (0, 0))],
        out_shape=[jax.ShapeDtypeStruct((n, D_MODEL), F32),
                   jax.ShapeDtypeStruct((n, H_EXT), F32),
                   jax.ShapeDtypeStruct((n, 8), F32),
                   jax.ShapeDtypeStruct((8, N_CLASS), F32)],
        scratch_shapes=[pltpu.VMEM((8, N_CLASS), F32)],
        compiler_params=_cparams(("arbitrary",)),
        name="outproj_router",
    )(*y, *x, *gt, *sc, *sh, g, w, wr, br)


def _scatter_kernel(pos_ref, x_ref, o_hbm, buf, sem):
    i, n = pl.program_id(0), pl.num_programs(0)
    tm = x_ref.shape[0]
    slot = i % 2

    def wait_slot(s):
        pltpu.make_async_copy(buf.at[s], o_hbm.at[pl.ds(0, tm), :], sem.at[s]).wait()

    @pl.when(i >= 2)
    def _():
        wait_slot(slot)

    buf[slot] = x_ref[...]

    def body(r, c):
        dst = pos_ref[i * tm + r]
        pltpu.make_async_copy(buf.at[slot, pl.ds(r, 1), :], o_hbm.at[pl.ds(dst, 1), :], sem.at[slot]).start()
        return c
    lax.fori_loop(0, tm, body, 0, unroll=8)

    @pl.when(i == n - 1)
    def _():
        wait_slot(slot)

        @pl.when(n >= 2)
        def _():
            wait_slot(1 - slot)


def _scatter_rows(x, pos):
    n, width = x.shape
    tm = MOE_TILE
    return pl.pallas_call(
        _scatter_kernel,
        grid_spec=pltpu.PrefetchScalarGridSpec(
            num_scalar_prefetch=1, grid=(n // tm,),
            in_specs=[_row_spec(tm, width)],
            out_specs=pl.BlockSpec(memory_space=pl.ANY),
            scratch_shapes=[pltpu.VMEM((2, tm, width), x.dtype), pltpu.SemaphoreType.DMA((2,))]),
        out_shape=jax.ShapeDtypeStruct((n, width), x.dtype),
        compiler_params=_cparams(("arbitrary",)),
        name="scatter_rows",
    )(pos, x)


def _gather_tiles(idx_ref, src_hbm, buf, sem, tm):
    i, n = pl.program_id(0), pl.num_programs(0)

    def issue(tile, slot):
        def body(r, c):
            row = idx_ref[tile * tm + r]
            pltpu.make_async_copy(src_hbm.at[pl.ds(row, 1), :], buf.at[slot, pl.ds(r, 1), :],
                                  sem.at[slot]).start()
            return c
        lax.fori_loop(0, tm, body, 0, unroll=8)

    @pl.when(i == 0)
    def _():
        issue(0, 0)

    @pl.when(i + 1 < n)
    def _():
        issue(i + 1, (i + 1) % 2)

    slot = i % 2
    pltpu.make_async_copy(src_hbm.at[pl.ds(0, tm), :], buf.at[slot], sem.at[slot]).wait()
    return slot


def _experts_kernel(tile_ref, grp_ref, flag_ref, hx_ref, wgu_ref, wd_ref, o_ref):
    j = pl.program_id(0)
    flags = flag_ref[j]
    valid = (flags & 1) != 0

    @pl.when(valid & ((flags & 2) != 0))
    def _():
        o_ref[...] = jnp.zeros_like(o_ref)

    @pl.when(valid)
    def _():
        h = hx_ref[:, 0:D_MODEL].astype(BF16)
        r = hx_ref[:, D_MODEL:H_EXT]
        w1, w2, i1, i2 = r[:, 0:1], r[:, 1:2], r[:, 2:3], r[:, 3:4]
        base = grp_ref[j] * EXPERTS_PER_GROUP
        for e in range(EXPERTS_PER_GROUP):
            @pl.when(((flags >> (8 + e)) & 1) != 0)
            def _():
                eid = (base + e).astype(F32)
                ge = jnp.where(i1 == eid, w1, 0.0) + jnp.where(i2 == eid, w2, 0.0)
                au = _dot(h, wgu_ref[0, 0, e])
                act = _silu(au[:, :EXPERT_FF]) * au[:, EXPERT_FF:]
                o_ref[...] += ge * _dot(act.astype(BF16), wd_ref[0, 0, e])


def _experts(hx_sorted, tile, grp, flags, wgu, wd, l):
    n = hx_sorted.shape[0]
    tm = MOE_TILE
    return pl.pallas_call(
        _experts_kernel,
        grid_spec=pltpu.PrefetchScalarGridSpec(
            num_scalar_prefetch=3, grid=(tile.shape[0],),
            in_specs=[pl.BlockSpec((tm, H_EXT), lambda j, t, g, f: (t[j], 0)),
                      pl.BlockSpec((1, 1, EXPERTS_PER_GROUP, D_MODEL, 2 * EXPERT_FF),
                                   lambda j, t, g, f: (l, g[j], 0, 0, 0)),
                      pl.BlockSpec((1, 1, EXPERTS_PER_GROUP, EXPERT_FF, D_MODEL),
                                   lambda j, t, g, f: (l, g[j], 0, 0, 0))],
            out_specs=pl.BlockSpec((tm, D_MODEL), lambda j, t, g, f: (t[j], 0))),
        out_shape=jax.ShapeDtypeStruct((n, D_MODEL), F32),
        compiler_params=_cparams(("arbitrary",)),
        name="moe_experts",
    )(tile, grp, flags, hx_sorted, wgu, wd)


def _combine_kernel(pos_ref, y_hbm, x1_ref, gtp, gts, gf_ref, op_ref, os_ref, buf, sem, *, np_tiles, final):
    slot = _gather_tiles(pos_ref, y_hbm, buf, sem, x1_ref.shape[0])
    is_p = pl.program_id(0) < np_tiles
    x2 = x1_ref[...] + _mod_val(gtp, gts, is_p) * buf[slot]
    out = _rms(x2) * gf_ref[...] if final else x2

    @pl.when(is_p)
    def _():
        op_ref[...] = out

    @pl.when(jnp.logical_not(is_p))
    def _():
        os_ref[...] = out


def _combine(y_sorted, pos, x1, gt, gf, n_prompt, tp, final):
    n = x1.shape[0]
    tm = MOE_TILE
    npt = n_prompt // tm
    return pl.pallas_call(
        functools.partial(_combine_kernel, np_tiles=npt, final=final),
        grid_spec=pltpu.PrefetchScalarGridSpec(
            num_scalar_prefetch=1, grid=(n // tm,),
            in_specs=[pl.BlockSpec(memory_space=pl.ANY), _row_spec(tm, D_MODEL)]
                     + _mod_specs(tm, n_prompt, tp) + [_const_spec(gf)],
            out_specs=_dual_specs(tm, npt, D_MODEL),
            scratch_shapes=[pltpu.VMEM((2, tm, D_MODEL), F32), pltpu.SemaphoreType.DMA((2,))]),
        out_shape=[jax.ShapeDtypeStruct((n_prompt, D_MODEL), F32),
                   jax.ShapeDtypeStruct((n - n_prompt, D_MODEL), F32)],
        compiler_params=_cparams(("arbitrary",)),
        name="moe_combine",
    )(pos, y_sorted, x1, *gt, gf)


def _routing_tables(cnt, key, rank, n_tiles, tm):
    n_items = n_tiles + EXPERT_GROUPS - 1
    c = jnp.arange(N_CLASS, dtype=I32)
    start_c = jnp.sum(jnp.where(c[:, None] < c[None, :], cnt[:, None], 0), axis=0)
    pos = jnp.sum(jnp.where(key[:, None] == c[None, :], start_c[None, :], 0), axis=1) + rank

    g = jnp.arange(EXPERT_GROUPS, dtype=I32)
    gend = jnp.sum(jnp.where(c[None, :] < 64 * (g[:, None] + 1), cnt[None, :], 0), axis=1)
    t0 = jnp.arange(n_tiles, dtype=I32) * tm
    t1 = t0 + (tm - 1)
    gfirst = jnp.sum((gend[None, :] <= t0[:, None]).astype(I32), axis=1)
    glast = jnp.sum((gend[None, :] <= t1[:, None]).astype(I32), axis=1)
    per_tile = glast - gfirst + 1
    tt = jnp.arange(n_tiles, dtype=I32)
    start_t = jnp.sum(jnp.where(tt[:, None] < tt[None, :], per_tile[:, None], 0), axis=0)
    total = jnp.sum(per_tile)

    j = jnp.arange(n_items, dtype=I32)
    valid = j < total
    tile = jnp.sum((start_t[None, :] <= j[:, None]).astype(I32), axis=1) - 1
    tile = jnp.where(valid, tile, n_tiles - 1)
    sel = tile[:, None] == tt[None, :]
    pick = lambda v: jnp.sum(jnp.where(sel, v[None, :], 0), axis=1)
    grp = jnp.where(valid, pick(gfirst) + (j - pick(start_t)), glast[n_tiles - 1])
    first = valid & (j == pick(start_t))

    in_tile = (cnt[None, :] > 0) & (start_c[None, :] <= t1[:, None]) & ((start_c + cnt)[None, :] > t0[:, None])
    e = jnp.arange(N_EXPERTS, dtype=I32)
    member = (((c >> 6) * 8 + ((c >> 3) & 7))[:, None] == e[None, :]) | (((c >> 6) * 8 + (c & 7))[:, None] == e[None, :])
    present_t = jnp.any(in_tile[:, :, None] & member[None, :, :], axis=1)
    present_j = jnp.any(sel[:, :, None] & present_t[None, :, :], axis=1)
    eg = e[None, :] - grp[:, None] * EXPERTS_PER_GROUP
    bits = jnp.sum(jnp.where(present_j & (eg >= 0) & (eg < EXPERTS_PER_GROUP),
                             1 << (8 + jnp.clip(eg, 0, EXPERTS_PER_GROUP - 1)), 0), axis=1)
    flags = valid.astype(I32) | (first.astype(I32) << 1) | bits
    return pos.astype(I32), tile.astype(I32), grp.astype(I32), flags.astype(I32)


def _moe(hx, info, counts, x1, gt, gf, wgu, wd, l, n_prompt, tp, final):
    n = hx.shape[0]
    tm = MOE_TILE
    pos, tile, grp, flags = _routing_tables(counts[0].astype(I32), info[:, 4].astype(I32),
                                            info[:, 5].astype(I32), n // tm, tm)
    y_sorted = _experts(_scatter_rows(hx, pos), tile, grp, flags, wgu, wd, l)
    return _combine(y_sorted, pos, x1, gt, gf, n_prompt, tp, final)


def _rope_tables(pos):
    half = HEAD_DIM // 2
    inv = ROPE_BASE ** (-jnp.arange(half, dtype=F32) / half)
    ang = pos.astype(F32)[:, None] * inv[None, :]
    cos, sin = jnp.cos(ang), jnp.sin(ang)
    cos_t = jnp.tile(jnp.concatenate([cos, cos], axis=-1), (1, RET_HEADS))
    sin_t = jnp.tile(jnp.concatenate([-sin, sin], axis=-1), (1, RET_HEADS))
    return cos_t, sin_t


def _mixer_consts(w_in, conv_w, conv_b, dt_bias, a_log, d_skip, g_ssd_norm, g_ret_norm,
                  b_mlstm_i, b_mlstm_f, g_mlstm_norm):
    nl = w_in.shape[0]
    w_p = jnp.concatenate([w_in[:, :, 0:1280], w_in[:, :, 1288:3336], w_in[:, :, 1280:1288],
                           w_in[:, :, 3336:3344], jnp.zeros((nl, D_MODEL, LANES - 16), F32)], axis=2).astype(BF16)
    pad = lambda v: jnp.pad(v, ((0, 0), (0, LANES - v.shape[1])))
    log_gamma = jnp.log(1.0 - 2.0 ** (-5.0 - jnp.arange(RET_HEADS, dtype=F32)))
    lg = jnp.broadcast_to(jnp.concatenate([jnp.zeros((8,), F32), log_gamma])[None, :], (nl, 12))
    ptab = jnp.stack([pad(jnp.concatenate([dt_bias, b_mlstm_i, b_mlstm_f], axis=1)), pad(a_log), pad(lg)]
                     + [jnp.zeros((nl, LANES), F32)] * 5, axis=1)
    consts = (ptab, conv_w, conv_b[:, None, :], jnp.repeat(d_skip, HEAD_DIM, axis=1)[:, None, :],
              g_ssd_norm[:, None, :], g_ret_norm[:, None, :], g_mlstm_norm[:, None, :])
    return w_p, consts


def kernel(x_prompt, x_sample, state_ssd, state_ssd_conv, state_ret, state_mlstm_c, state_mlstm_n,
           state_mlstm_m, c_prompt, c_sample, w_ada, b_ada, g_norm1, g_norm2, w_in, conv_w, conv_b,
           dt_bias, a_log, d_skip, g_ssd_norm, g_ret_norm, b_mlstm_i, b_mlstm_f, g_mlstm_norm, w_out,
           w_router_group, b_router_group, w_router_expert, b_router_expert, w_gate_up, w_down, g_final):
    bp, tp, _ = x_prompt.shape
    bs, ts, _ = x_sample.shape
    n_srows = bs * ts
    n_prompt = bp * tp

    mod = _ada(jnp.concatenate([c_prompt, c_sample], axis=0), w_ada, b_ada)

    def mods(l):
        m = mod[l].reshape(bp + bs, 6, D_MODEL)
        return [(m[:bp, k][:, None, :], jnp.tile(m[bp:, k], (ts, 1))) for k in range(6)]

    cos_p, sin_p = _rope_tables(jnp.arange(tp, dtype=I32))
    lanes = lambda v: jnp.broadcast_to(v[..., None], v.shape + (bs,))
    half = HEAD_DIM // 2
    ang = (PAST_LEN + jnp.arange(ts, dtype=I32)).astype(F32)[:, None] * (
        ROPE_BASE ** (-jnp.arange(half, dtype=F32) / half))[None, :]
    cos_s, sin_s = lanes(jnp.cos(ang)), lanes(jnp.sin(ang))
    gamma = 1.0 - 2.0 ** (-5.0 - jnp.arange(RET_HEADS, dtype=F32))
    tabs = (lanes(dt_bias[:, :, None]), lanes(a_log[:, :, None]), lanes(d_skip[:, :, None]), lanes(g_ssd_norm),
            lanes(gamma[:, None]), lanes(g_ret_norm), lanes(b_mlstm_i[:, :, None]), lanes(b_mlstm_f[:, :, None]),
            lanes(g_mlstm_norm))
    cwb, cbb = lanes(conv_w), lanes(conv_b)
    ssd_t = jnp.transpose(state_ssd, (0, 2, 3, 4, 1))
    ret_t = jnp.transpose(state_ret, (0, 2, 3, 4, 1))
    c_t = jnp.transpose(state_mlstm_c, (0, 2, 3, 4, 1))
    n_t = jnp.transpose(state_mlstm_n, (0, 2, 3, 1))
    m_t = jnp.transpose(state_mlstm_m, (0, 2, 1))[:, :, None, :]
    conv_t = jnp.transpose(state_ssd_conv, (0, 2, 3, 1))

    w_p, consts = _mixer_consts(w_in, conv_w, conv_b, dt_bias, a_log, d_skip, g_ssd_norm, g_ret_norm,
                                b_mlstm_i, b_mlstm_f, g_mlstm_norm)
    w_o = w_out.astype(BF16)
    zpad = LANES - N_EXPERTS - EXPERT_GROUPS
    wr = jnp.concatenate([w_router_expert, w_router_group, jnp.zeros((DEPTH, D_MODEL, zpad), F32)],
                         axis=2).astype(BF16)
    br = jnp.concatenate([b_router_expert, b_router_group, jnp.zeros((DEPTH, zpad), F32)], axis=1)[:, None, :]
    wgu = w_gate_up.astype(BF16).reshape(DEPTH, EXPERT_GROUPS, EXPERTS_PER_GROUP, D_MODEL, 2 * EXPERT_FF)
    wd = w_down.astype(BF16).reshape(DEPTH, EXPERT_GROUPS, EXPERTS_PER_GROUP, EXPERT_FF, D_MODEL)
    g1, g2, gf = g_norm1[:, None, :], g_norm2[:, None, :], g_final[None, :]

    x = (x_prompt.reshape(n_prompt, D_MODEL), jnp.transpose(x_sample, (1, 0, 2)).reshape(n_srows, D_MODEL))
    p_states, s_states = [], []
    for l in range(DEPTH):
        final = l == DEPTH - 1
        sh1, sc1, gt1, sh2, sc2, gt2 = mods(l)
        proj = _inproj(x, sc1, sh1, g1, w_p, l, tp)

        ycat_p, st = _mixer_prompt(proj, bp, tp, cos_p, sin_p, consts, l)
        p_states.append(st)

        proj_t = jnp.transpose(proj[n_prompt:].reshape(ts, bs, P_W), (0, 2, 1))
        y_t, st_t = _mixer_sample_T(proj_t, conv_t, cwb, cbb, ssd_t, ret_t, c_t, n_t, m_t, tabs, cos_s, sin_s, l)
        s_states.append(st_t)
        ycat_s = jnp.transpose(y_t, (0, 2, 1)).reshape(n_srows, D_MODEL).astype(BF16)

        x1, hx, info, counts = _outproj((ycat_p, ycat_s), x, gt1, sc2, sh2, g2, w_o, wr, br, l, tp)
        x = _moe(hx, info, counts, x1, gt2, gf, wgu, wd, l, n_prompt, tp, final)

    y_prompt = x[0].reshape(bp, tp, D_MODEL)
    y_sample = jnp.transpose(x[1].reshape(ts, bs, D_MODEL), (1, 0, 2))
    p_st = [jnp.stack([s[i] for s in p_states], axis=0) for i in range(6)]
    s_t = [jnp.stack([s[i] for s in s_states], axis=0) for i in range(6)]
    s_st = [jnp.transpose(s_t[0], (0, 4, 1, 2, 3)), jnp.transpose(s_t[1], (0, 3, 1, 2)),
            jnp.transpose(s_t[2], (0, 4, 1, 2, 3)), jnp.transpose(s_t[3], (0, 4, 1, 2, 3)),
            jnp.transpose(s_t[4], (0, 3, 1, 2)), jnp.transpose(s_t[5][:, :, 0, :], (0, 2, 1))]
    return (y_prompt, y_sample, *p_st, *s_st)
```

```python
import functools
import math

import jax
import jax.numpy as jnp
from jax import lax
from jax.experimental import pallas as pl
from jax.experimental.pallas import tpu as pltpu

F32 = jnp.float32
BF16 = jnp.bfloat16
I32 = jnp.int32

D_MODEL = 1024
DEPTH = 2
PAST_LEN = 16384
SSD_HEADS = 8
SSD_WIDTH = 512
SSD_GROUPS = 2
SSD_STATE = 64
SSD_CONV = 4
SSD_XBC = 768
RET_HEADS = 4
RET_WIDTH = 256
ML_HEADS = 4
ML_WIDTH = 256
HEAD_DIM = 64
ROPE_BASE = 10000.0
EPS = 1e-6
EXPERT_GROUPS = 4
EXPERTS_PER_GROUP = 8
N_EXPERTS = 32
EXPERT_FF = 256

LANES = 128
CHUNK = 128
SEQ_PER_STEP = 4
P_Z, P_XBC, P_RET, P_ML, P_SM, P_W = 0, 512, 1280, 2304, 3328, 3456
C_SSD, C_RET, C_ML = 0, 8, 12
N_PAIRS = 8
NEG = -1e30
TOKEN_TILE = 512
MOE_TILE = 256
VMEM_LIMIT = 56 * 1024 * 1024
H_EXT = D_MODEL + LANES
N_CLASS = 256


def _cparams(sem):
    return pltpu.CompilerParams(dimension_semantics=sem, vmem_limit_bytes=VMEM_LIMIT)


def _split3(x):
    x1 = x.astype(BF16)
    r = x - x1.astype(F32)
    x2 = r.astype(BF16)
    r = r - x2.astype(F32)
    return x1, x2, r.astype(BF16)


def _dot01(m01, x):
    return sum(jnp.dot(m01, p, preferred_element_type=F32) for p in _split3(x))


def _dot(a, b):
    return jnp.dot(a, b, preferred_element_type=F32)


def _dot_nt(a, b):
    return lax.dot_general(a, b, (((1,), (1,)), ((), ())), preferred_element_type=F32)


def _dot_tn(a, b):
    return lax.dot_general(a, b, (((0,), (0,)), ((), ())), preferred_element_type=F32)


def _softplus(x):
    return jnp.maximum(x, 0.0) + jnp.log1p(jnp.exp(-jnp.abs(x)))


def _silu(x):
    return x * jax.nn.sigmoid(x)


def _rms(x):
    return x * lax.rsqrt(jnp.mean(x * x, axis=-1, keepdims=True) + EPS)


def _cummax_rows(x, seg):
    t = lax.broadcasted_iota(I32, x.shape, 0) & (seg - 1)
    s = 1
    while s < seg:
        x = jnp.maximum(x, jnp.where(t >= s, pltpu.roll(x, s, 0), NEG))
        s *= 2
    return x


def _rope(x, cos, sin_signed, lane):
    swapped = jnp.where((lane & 63) < 32, pltpu.roll(x, 96, 1), pltpu.roll(x, 32, 1))
    return x * cos + swapped * sin_signed


def _row_spec(tm, width):
    return pl.BlockSpec((tm, width), lambda i, *_: (i, 0))


def _const_spec(a):
    nd = a.ndim
    return pl.BlockSpec(a.shape, lambda *_: (0,) * nd)


def _layer_spec(a, l):
    nd = a.ndim
    return pl.BlockSpec((1,) + a.shape[1:], lambda *_: (l,) + (0,) * (nd - 1))


def _dual_specs(tm, np_tiles, width):
    return [pl.BlockSpec((tm, width), lambda i, *_: (jnp.minimum(i, np_tiles - 1), 0)),
            pl.BlockSpec((tm, width), lambda i, *_: (jnp.maximum(i - np_tiles, 0), 0))]


def _mod_specs(tm, n_prompt, tp):
    np_tiles, last_b = n_prompt // tm, n_prompt // tp - 1
    return [pl.BlockSpec((1, 1, D_MODEL), lambda i, *_: (jnp.minimum(i * tm // tp, last_b), 0, 0)),
            pl.BlockSpec((tm, D_MODEL), lambda i, *_: (jnp.maximum(i - np_tiles, 0), 0))]


def _mod_val(p_ref, s_ref, is_prompt):
    return jnp.where(is_prompt, p_ref[0], s_ref[...])


def _dual_val(p_ref, s_ref, is_prompt):
    return jnp.where(is_prompt, p_ref[...], s_ref[...])


def _ada_kernel(c_ref, w_ref, b_ref, o_ref):
    c = c_ref[...]
    o_ref[0] = _dot(_silu(c).astype(BF16), w_ref[0].astype(BF16)) + b_ref[0]


def _ada(c_all, w_ada, b_ada):
    nb = c_all.shape[0]
    tn = 1536
    return pl.pallas_call(
        _ada_kernel,
        grid=(DEPTH, 6 * D_MODEL // tn),
        in_specs=[pl.BlockSpec((nb, D_MODEL), lambda l, j: (0, 0)),
                  pl.BlockSpec((1, D_MODEL, tn), lambda l, j: (l, 0, j)),
                  pl.BlockSpec((1, 1, tn), lambda l, j: (l, 0, j))],
        out_specs=pl.BlockSpec((1, nb, tn), lambda l, j: (l, 0, j)),
        out_shape=jax.ShapeDtypeStruct((DEPTH, nb, 6 * D_MODEL), F32),
        compiler_params=_cparams(("arbitrary", "arbitrary")),
        name="ada_mod",
    )(c_all, w_ada, b_ada.reshape(DEPTH, 1, 6 * D_MODEL))


def _inproj_kernel(xp, xs, scp, scs, shp, shs, g_ref, w_ref, o_ref, *, np_tiles):
    is_p = pl.program_id(0) < np_tiles
    h = _rms(_dual_val(xp, xs, is_p)) * g_ref[0] * (1.0 + _mod_val(scp, scs, is_p)) + _mod_val(shp, shs, is_p)
    o_ref[...] = _dot(h.astype(BF16), w_ref[0])


def _inproj(x, sc, sh, g, w, l, tp):
    n_prompt, ns = x[0].shape[0], x[1].shape[0]
    tm = TOKEN_TILE
    ms = _mod_specs(tm, n_prompt, tp)
    return pl.pallas_call(
        functools.partial(_inproj_kernel, np_tiles=n_prompt // tm),
        grid=((n_prompt + ns) // tm,),
        in_specs=_dual_specs(tm, n_prompt // tm, D_MODEL) + ms + ms + [_layer_spec(g, l), _layer_spec(w, l)],
        out_specs=_row_spec(tm, P_W),
        out_shape=jax.ShapeDtypeStruct((n_prompt + ns, P_W), F32),
        compiler_params=_cparams(("arbitrary",)),
        name="norm_inproj",
    )(*x, *sc, *sh, g, w)


def _lane_bcast(a, c):
    return jnp.broadcast_to(a[:, c:c + 1], a.shape)


def _pair_lanes(a, c0, c1):
    lm0 = lax.broadcasted_iota(I32, (a.shape[0], LANES), 1) < HEAD_DIM
    return jnp.where(lm0, a[:, c0:c0 + 1], a[:, c1:c1 + 1])


def _ret_factors(ptab, tril, mask, last_fn):
    rows = tril.shape[0]
    lane = lax.broadcasted_iota(I32, (rows, LANES), 1)
    lm0 = lane < HEAD_DIM
    cum = _dot01(tril, jnp.where((lane >= C_RET) & (lane < C_ML), ptab[2:3], 0.0))
    xt = cum.T
    cum_last = last_fn(cum)
    out = []
    for p in range(RET_HEADS // 2):
        c0, c1 = C_RET + 2 * p, C_RET + 2 * p + 1
        b0, b1 = _lane_bcast(cum, c0), _lane_bcast(cum, c1)
        bp = jnp.where(lm0, b0, b1)
        wp = jnp.exp(_pair_lanes(cum_last, c0, c1) - bp)
        out.append((jnp.exp(jnp.where(mask, b0 - xt[c0:c0 + 1, :], -jnp.inf)),
                    jnp.exp(jnp.where(mask, b1 - xt[c1:c1 + 1, :], -jnp.inf)), jnp.exp(bp), wp))
    return out


def _mixer_core(z, us, retb, mlb, small, cos, sin, ptab, cw, cb, dsk, gs, gr, gm,
                mask, tril, seg, mprev, last_fn, st, y_ref, ret_factors):
    rows = small.shape[0]
    lane = lax.broadcasted_iota(I32, (rows, LANES), 1)
    lm0 = lane < HEAD_DIM

    pre = small + ptab[0:1]
    a_neg = -jnp.exp(ptab[1:2])
    dt = _softplus(pre)
    logf = -_softplus(-pre)
    la = jnp.where(lane < C_RET, dt * a_neg,
                   jnp.where(lane < C_ML, ptab[2:3], jnp.where(lane < C_ML + 4, logf, 0.0)))
    cum = _dot01(tril, la)
    ic = pltpu.roll(pre, 4, 1)
    mlm = (lane >= C_ML) & (lane < C_ML + 4)
    d = jnp.where(mlm, ic - cum, NEG)
    m_t = cum + jnp.maximum(mprev, _cummax_rows(d, seg))
    xt = jnp.where(mlm, d, cum).T
    colv = cum - m_t
    cum_last, m_last = last_fn(cum), last_fn(m_t)
    decq = jnp.where(mlm, jnp.exp(cum_last + mprev - m_last), jnp.exp(cum_last))
    yield

    def factors(c0, c1, kind):
        if kind == "ret":
            return ret_factors((c0 - C_RET) // 2)
        if kind == "ssd":
            b0, b1 = _lane_bcast(cum, c0), _lane_bcast(cum, c1)
            bp = jnp.where(lm0, b0, b1)
            wp = jnp.exp(_pair_lanes(cum_last, c0, c1) - bp)
            return (jnp.exp(jnp.where(mask, b0 - xt[c0:c0 + 1, :], -jnp.inf)),
                    jnp.exp(jnp.where(mask, b1 - xt[c1:c1 + 1, :], -jnp.inf)), jnp.exp(bp), wp)
        a0, a1 = _lane_bcast(colv, c0), _lane_bcast(colv, c1)
        eqp = jnp.exp(jnp.where(lm0, a0, a1) + _pair_lanes(mprev, c0, c1))
        wp = jnp.exp(jnp.where(lm0, _lane_bcast(d, c0), _lane_bcast(d, c1))
                     + _pair_lanes(cum_last, c0, c1) - _pair_lanes(m_last, c0, c1))
        return (jnp.exp(jnp.where(mask, a0 + xt[c0:c0 + 1, :], -jnp.inf)),
                jnp.exp(jnp.where(mask, a1 + xt[c1:c1 + 1, :], -jnp.inf)), eqp, wp)

    def pair(idx, inputs, c0, c1, kind, finish):
        qp, kp, vp = inputs()
        d0, d1, eqp, wp = factors(c0, c1, kind)
        q0 = jnp.where(lm0, qp, 0.0)
        q1 = jnp.where(lm0, 0.0, qp)
        sc = _dot_nt(jnp.concatenate([q0, q1], axis=0).astype(BF16), kp.astype(BF16))
        v01 = jnp.concatenate([jnp.where(lm0, vp, 0.0), jnp.where(lm0, 0.0, vp)], axis=0).astype(BF16)
        kw = kp * wp
        yield
        carried, qn = st.step(idx, qp, eqp, kw, vp, decq, c0, c1, kind == "ml")
        yield
        s0, s1 = sc[0:rows] * d0, sc[rows:2 * rows] * d1
        intra = _dot(jnp.concatenate([s0, s1], axis=1).astype(BF16), v01)
        yield
        finish(intra + carried, s0, s1, qn)

    def head_norm(o):
        o2 = o * o
        ms0 = jnp.sum(jnp.where(lm0, o2, 0.0), axis=-1, keepdims=True) * (1.0 / HEAD_DIM)
        ms1 = jnp.sum(jnp.where(lm0, 0.0, o2), axis=-1, keepdims=True) * (1.0 / HEAD_DIM)
        return o * jnp.where(lm0, lax.rsqrt(ms0 + EPS), lax.rsqrt(ms1 + EPS))

    conv = cb + us[0] * cw[0:1] + us[1] * cw[1:2] + us[2] * cw[2:3] + us[3] * cw[3:4]
    xc = _silu(conv)
    bb = xc[:, SSD_WIDTH:SSD_WIDTH + LANES]
    cc = xc[:, SSD_WIDTH + LANES:SSD_WIDTH + 2 * LANES]
    br = pltpu.roll(bb, HEAD_DIM, 1)
    cr = pltpu.roll(cc, HEAD_DIM, 1)
    ys = [None] * 4
    pairs = []
    for p in range(4):
        c0, c1 = C_SSD + 2 * p, C_SSD + 2 * p + 1
        sl = slice(LANES * p, LANES * (p + 1))

        def ssd_inputs(p=p, c0=c0, c1=c1, sl=sl):
            if p < 2:
                kp, qp = jnp.where(lm0, bb, br), jnp.where(lm0, cc, cr)
            else:
                kp, qp = jnp.where(lm0, br, bb), jnp.where(lm0, cr, cc)
            dtp = jnp.where(lm0, dt[:, c0:c0 + 1], dt[:, c1:c1 + 1])
            return qp, kp, xc[:, sl] * dtp

        def ssd_finish(o, s0, s1, qn, p=p, sl=sl):
            ys[p] = (o + dsk[:, sl] * xc[:, sl]) * _silu(z[:, sl])

        pairs.append(pair(p, ssd_inputs, c0, c1, "ssd", ssd_finish))

    for p in range(2):
        c0, c1 = C_RET + 2 * p, C_RET + 2 * p + 1
        sl = slice(LANES * p, LANES * (p + 1))

        def ret_inputs(p=p, sl=sl):
            qp = _rope(retb[:, LANES * p:LANES * (p + 1)], cos[:, sl], sin[:, sl], lane)
            kp = _rope(retb[:, RET_WIDTH + LANES * p:RET_WIDTH + LANES * (p + 1)], cos[:, sl], sin[:, sl], lane)
            return qp, kp * (HEAD_DIM ** -0.5), retb[:, 2 * RET_WIDTH + LANES * p:2 * RET_WIDTH + LANES * (p + 1)]

        def ret_finish(o, s0, s1, qn, p=p, sl=sl):
            gp = retb[:, 3 * RET_WIDTH + LANES * p:3 * RET_WIDTH + LANES * (p + 1)]
            y = head_norm(o) * gr[:, sl] * _silu(gp)
            y_ref[:, SSD_WIDTH + LANES * p:SSD_WIDTH + LANES * (p + 1)] = y.astype(y_ref.dtype)

        pairs.append(pair(4 + p, ret_inputs, c0, c1, "ret", ret_finish))

    for p in range(2):
        c0, c1 = C_ML + 2 * p, C_ML + 2 * p + 1
        sl = slice(LANES * p, LANES * (p + 1))

        def ml_inputs(p=p):
            return (mlb[:, LANES * p:LANES * (p + 1)],
                    mlb[:, ML_WIDTH + LANES * p:ML_WIDTH + LANES * (p + 1)] * (HEAD_DIM ** -0.5),
                    mlb[:, 2 * ML_WIDTH + LANES * p:2 * ML_WIDTH + LANES * (p + 1)])

        def ml_finish(num, s0, s1, qn, p=p, c0=c0, c1=c1, sl=sl):
            op = mlb[:, 3 * ML_WIDTH + LANES * p:3 * ML_WIDTH + LANES * (p + 1)]
            inter0 = jnp.exp(colv[:, c0:c0 + 1] + mprev[:, c0:c0 + 1])
            inter1 = jnp.exp(colv[:, c1:c1 + 1] + mprev[:, c1:c1 + 1])
            den0 = jnp.sum(s0, axis=-1, keepdims=True) + qn[0] * inter0
            den1 = jnp.sum(s1, axis=-1, keepdims=True) + qn[1] * inter1
            dn0 = jnp.maximum(jnp.abs(den0), jnp.exp(-m_t[:, c0:c0 + 1]))
            dn1 = jnp.maximum(jnp.abs(den1), jnp.exp(-m_t[:, c1:c1 + 1]))
            hh = num / jnp.where(lm0, dn0, dn1)
            y = head_norm(hh) * gm[:, sl] * jax.nn.sigmoid(op)
            off = SSD_WIDTH + RET_WIDTH + LANES * p
            y_ref[:, off:off + LANES] = y.astype(y_ref.dtype)

        pairs.append(pair(6 + p, ml_inputs, c0, c1, "ml", ml_finish))

    live = list(pairs)
    while live:
        for g in list(live):
            try:
                next(g)
            except StopIteration:
                live.remove(g)
        yield

    for g in range(SSD_GROUPS):
        ya, yb = ys[2 * g], ys[2 * g + 1]
        ms = (jnp.sum(ya * ya, axis=-1, keepdims=True)
              + jnp.sum(yb * yb, axis=-1, keepdims=True)) * (1.0 / (2 * LANES))
        r = lax.rsqrt(ms + EPS)
        for j, yv in ((2 * g, ya), (2 * g + 1, yb)):
            sl = slice(LANES * j, LANES * (j + 1))
            y_ref[:, sl] = (yv * r * gs[:, sl]).astype(y_ref.dtype)
    return m_t


def _run_interleaved(gens):
    out = [None] * len(gens)
    live = list(range(len(gens)))
    while live:
        for k in list(live):
            try:
                next(gens[k])
            except StopIteration as stop:
                out[k] = stop.value
                live.remove(k)
    return out


def _half_rows():
    return lax.broadcasted_iota(I32, (LANES, HEAD_DIM), 0) < HEAD_DIM


class _CarriedState:
    def __init__(self, sv, nrow):
        self.sv, self.nrow = sv, nrow

    def step(self, idx, qp, eqp, kw, vp, decq, c0, c1, ml):
        lm0 = lax.broadcasted_iota(I32, (1, LANES), 1) < HEAD_DIM
        drow = _pair_lanes(decq[0:1, :], c0, c1)
        s_old = self.sv[idx]
        carried = _dot(qp.astype(BF16), s_old.astype(BF16)) * eqp
        u = _dot_tn(kw.astype(BF16), vp.astype(BF16))
        same_head = ((lax.broadcasted_iota(I32, (LANES, LANES), 0) < HEAD_DIM)
                     == (lax.broadcasted_iota(I32, (LANES, LANES), 1) < HEAD_DIM))
        self.sv[idx] = s_old * drow + jnp.where(same_head, u, 0.0)
        qn = None
        if ml:
            p = idx - 6
            n_old = self.nrow[p:p + 1, :]
            qn_l = qp * n_old
            qn = (jnp.sum(jnp.where(lm0, qn_l, 0.0), axis=-1, keepdims=True),
                  jnp.sum(jnp.where(lm0, 0.0, qn_l), axis=-1, keepdims=True))
            self.nrow[p:p + 1, :] = n_old * drow + jnp.sum(kw, axis=0, keepdims=True)
        return carried, qn


def _mixer_prompt_kernel(*refs):
    sq = SEQ_PER_STEP
    proj_refs = refs[:sq]
    (cos_ref, sin_ref, ptab_ref, cw_ref, cb_ref, dsk_ref, gs_ref, gr_ref, gm_ref,
     y_ref, sv_o, conv_o, n_o, m_o) = refs[sq:sq + 14]
    scr = refs[sq + 14:]
    sv, nrow, mrow, cbuf = scr[0:sq], scr[sq:2 * sq], scr[2 * sq:3 * sq], scr[3 * sq:4 * sq]
    ret_cache = scr[4 * sq]
    ci = pl.program_id(1)
    rows = proj_refs[0].shape[1]
    ri = lax.broadcasted_iota(I32, (rows, rows), 0)
    cj = lax.broadcasted_iota(I32, (rows, rows), 1)
    mask = cj <= ri
    tril = jnp.where(mask, 1.0, 0.0).astype(BF16)
    last_row = lambda a: a[rows - 1:rows, :]

    @pl.when(ci == 0)
    def _():
        for s in range(sq):
            sv[s][...] = jnp.zeros_like(sv[s])
            nrow[s][...] = jnp.zeros_like(nrow[s])
            mrow[s][...] = jnp.zeros_like(mrow[s])
            cbuf[s][0:8, :] = jnp.zeros((8, SSD_XBC), F32)
        for p, fs in enumerate(_ret_factors(ptab_ref[0], tril, mask, last_row)):
            for k, f in enumerate(fs):
                ret_cache[4 * p + k] = f

    ret_factors = lambda p: tuple(ret_cache[4 * p + k] for k in range(4))
    gens = []
    for s in range(sq):
        proj_ref = proj_refs[s]
        cbuf[s][8:8 + rows, :] = proj_ref[0, :, P_XBC:P_RET]
        us = [cbuf[s][pl.ds(5 + k, rows), :] for k in range(SSD_CONV)]
        gens.append(_mixer_core(
            proj_ref[0, :, P_Z:P_XBC], us, proj_ref[0, :, P_RET:P_ML], proj_ref[0, :, P_ML:P_SM],
            proj_ref[0, :, P_SM:P_W], cos_ref[...], sin_ref[...], ptab_ref[0], cw_ref[0], cb_ref[0],
            dsk_ref[0], gs_ref[0], gr_ref[0], gm_ref[0],
            mask, tril, rows, mrow[s][0:1, :], last_row,
            _CarriedState(sv[s], nrow[s]), y_ref.at[s], ret_factors))
    for s, m_t in enumerate(_run_interleaved(gens)):
        mrow[s][0:1, :] = m_t[rows - 1:rows, :]
        cbuf[s][0:8, :] = cbuf[s][rows:rows + 8, :]

    @pl.when(ci == pl.num_programs(1) - 1)
    def _():
        top = _half_rows()
        for s in range(sq):
            for idx in range(N_PAIRS):
                bd = sv[s][idx]
                sv_o[s, idx] = jnp.where(top, bd[:, :HEAD_DIM], bd[:, HEAD_DIM:])
            conv_o[s] = cbuf[s][0:8, :]
            n_o[s] = nrow[s][...]
            m_o[s] = mrow[s][...]


def _mixer_prompt(proj, nb, t, cos, sin, consts, l):
    rows = math.gcd(t, CHUNK)
    nc = t // rows
    sq = SEQ_PER_STEP
    assert nb % sq == 0

    def proj_spec(s):
        return pl.BlockSpec((1, rows, P_W), lambda b, c: (0, (sq * b + s) * nc + c, 0))

    def seq_spec(*tail):
        return pl.BlockSpec((sq,) + tail, lambda b, c: (b,) + (0,) * len(tail))

    outs = pl.pallas_call(
        _mixer_prompt_kernel,
        grid=(nb // sq, nc),
        in_specs=[proj_spec(s) for s in range(sq)]
                 + [pl.BlockSpec((rows, RET_WIDTH), lambda b, c: (c, 0)),
                    pl.BlockSpec((rows, RET_WIDTH), lambda b, c: (c, 0))] + [_layer_spec(a, l) for a in consts],
        out_specs=[pl.BlockSpec((sq, rows, D_MODEL), lambda b, c: (b, c, 0)),
                   seq_spec(N_PAIRS, LANES, HEAD_DIM), seq_spec(8, SSD_XBC), seq_spec(8, LANES), seq_spec(8, LANES)],
        out_shape=[jax.ShapeDtypeStruct((nb, t, D_MODEL), BF16),
                   jax.ShapeDtypeStruct((nb, N_PAIRS, LANES, HEAD_DIM), F32),
                   jax.ShapeDtypeStruct((nb, 8, SSD_XBC), F32),
                   jax.ShapeDtypeStruct((nb, 8, LANES), F32),
                   jax.ShapeDtypeStruct((nb, 8, LANES), F32)],
        scratch_shapes=[pltpu.VMEM((N_PAIRS, LANES, LANES), F32) for _ in range(sq)]
                       + [pltpu.VMEM((8, LANES), F32) for _ in range(2 * sq)]
                       + [pltpu.VMEM((rows + 8, SSD_XBC), F32) for _ in range(sq)]
                       + [pltpu.VMEM((2 * RET_HEADS, rows, LANES), F32)],
        compiler_params=_cparams(("arbitrary", "arbitrary")),
        name="mixer_prompt",
    )(*([proj[None]] * sq), cos, sin, *consts)
    y, sv, conv, n, m = outs
    sv = sv.reshape(nb, 2 * N_PAIRS, HEAD_DIM, HEAD_DIM)
    states = (sv[:, :8], conv[:, 5:8], sv[:, 8:12], sv[:, 12:16],
              n[:, 0:2].reshape(nb, ML_HEADS, HEAD_DIM), m[:, 0, C_ML:C_ML + 4])
    return y.reshape(nb * t, D_MODEL), states


def _lane_recurrence(s_in, s_out, q_s, k_s, v_s, decay):
    n_tok = len(decay)
    nvb = HEAD_DIM // 8
    nb = q_s.shape[-1]
    dec8 = [jnp.broadcast_to(d, (8, nb)) for d in decay]

    def body(k, acc):
        acc = [list(a) for a in acc]
        qk = [q_s[t, pl.ds(k, 1), :] for t in range(n_tok)]
        kk = [k_s[t, pl.ds(k, 1), :] for t in range(n_tok)]
        for vb in range(nvb):
            rows = pl.ds(8 * vb, 8)
            s = s_in[k, rows, :]
            for t in range(n_tok):
                s = s * dec8[t] + kk[t] * v_s[t, rows, :]
                acc[t][vb] = acc[t][vb] + qk[t] * s
            s_out[k, rows, :] = s
        return tuple(tuple(a) for a in acc)

    init = tuple(tuple(jnp.zeros((8, nb), F32) for _ in range(nvb)) for _ in range(n_tok))
    acc = lax.fori_loop(0, HEAD_DIM, body, init, unroll=2)
    return [jnp.concatenate(list(a), axis=0) for a in acc]


def _conv_T(u_ref, tail_ref, w_ref, b_ref, state_ref):
    n_tok = u_ref.shape[0]
    full = [tail_ref[0, j] for j in range(SSD_CONV - 1)] + [u_ref[t] for t in range(n_tok)]
    outs = []
    for t in range(n_tok):
        acc = b_ref[0]
        for tap in range(SSD_CONV):
            acc = acc + full[t + tap] * w_ref[0, tap]
        outs.append(_silu(acc))
    for j in range(SSD_CONV - 1):
        state_ref[j] = full[n_tok + j]
    return outs


def _ssd_T_kernel(z_ref, xs_ref, b_ref, c_ref, sm_ref, tx_ref, tb_ref, tc_ref, wx_ref, wb_ref, wc_ref,
                  bx_ref, bb_ref, bc_ref, dtb_ref, alog_ref, dsk_ref, g_ref, s_ref,
                  y_ref, so_ref, cx_ref, cb_ref, cc_ref, q_s, k_s, v_s):
    h = pl.program_id(0)
    n_tok = xs_ref.shape[0]
    xs = _conv_T(xs_ref, tx_ref, wx_ref, bx_ref, cx_ref)
    bm = _conv_T(b_ref, tb_ref, wb_ref, bb_ref, cb_ref)
    cm = _conv_T(c_ref, tc_ref, wc_ref, bc_ref, cc_ref)
    a_neg = -jnp.exp(alog_ref[0, 0])
    decay = []
    for t in range(n_tok):
        dt = _softplus(sm_ref[t, pl.ds(C_SSD + h, 1), :] + dtb_ref[0, 0])
        decay.append(jnp.exp(dt * a_neg))
        q_s[t] = cm[t]
        k_s[t] = bm[t]
        v_s[t] = xs[t] * dt
    o = _lane_recurrence(s_ref.at[0, 0], so_ref.at[0], q_s, k_s, v_s, decay)
    hh = h % (SSD_HEADS // SSD_GROUPS)
    row0 = pl.multiple_of(hh * HEAD_DIM, HEAD_DIM)
    for t in range(n_tok):
        y_ref[t, pl.ds(row0, HEAD_DIM), :] = (o[t] + dsk_ref[0, 0] * xs[t]) * _silu(z_ref[t])

    @pl.when(hh == SSD_HEADS // SSD_GROUPS - 1)
    def _():
        for t in range(n_tok):
            blk = y_ref[t]
            ms = jnp.mean(blk * blk, axis=0, keepdims=True)
            y_ref[t] = blk * lax.rsqrt(ms + EPS) * g_ref[0]


def _ret_T_kernel(q_ref, k_ref, v_ref, g_ref, cos_ref, sin_ref, gam_ref, gn_ref, s_ref,
                  y_ref, so_ref, q_s, k_s, v_s):
    n_tok = q_ref.shape[0]
    half = HEAD_DIM // 2

    def rope(x, t):
        x1, x2 = x[:half], x[half:]
        return jnp.concatenate([x1 * cos_ref[t] - x2 * sin_ref[t], x1 * sin_ref[t] + x2 * cos_ref[t]], axis=0)

    for t in range(n_tok):
        q_s[t] = rope(q_ref[t], t)
        k_s[t] = rope(k_ref[t], t) * (HEAD_DIM ** -0.5)
        v_s[t] = v_ref[t]
    o = _lane_recurrence(s_ref.at[0, 0], so_ref.at[0], q_s, k_s, v_s, [gam_ref[0]] * n_tok)
    for t in range(n_tok):
        ms = jnp.mean(o[t] * o[t], axis=0, keepdims=True)
        y_ref[t] = o[t] * lax.rsqrt(ms + EPS) * gn_ref[0] * _silu(g_ref[t])


def _ml_T_kernel(q_ref, k_ref, v_ref, o_ref, sm_ref, bi_ref, bf_ref, gn_ref, c_ref, n_ref, m_ref,
                 y_ref, co_ref, no_ref, mo_ref, q_s, k_s, v_s):
    h = pl.program_id(0)
    n_tok = q_ref.shape[0]
    m = m_ref[0, 0]
    n = n_ref[0, 0]
    decay, qn, m_all = [], [], []
    for t in range(n_tok):
        i_t = sm_ref[t, pl.ds(C_RET + h, 1), :] + bi_ref[0, 0]
        logf = -_softplus(-(sm_ref[t, pl.ds(C_ML + h, 1), :] + bf_ref[0, 0]))
        m_new = jnp.maximum(logf + m, i_t)
        f_t = jnp.exp(logf + m - m_new)
        kw = k_ref[t] * (HEAD_DIM ** -0.5) * jnp.exp(i_t - m_new)
        n = n * f_t + kw
        q_s[t] = q_ref[t]
        k_s[t] = kw
        v_s[t] = v_ref[t]
        decay.append(f_t)
        qn.append(jnp.sum(q_ref[t] * n, axis=0, keepdims=True))
        m_all.append(m_new)
        m = m_new
    num = _lane_recurrence(c_ref.at[0, 0], co_ref.at[0], q_s, k_s, v_s, decay)
    no_ref[0] = n
    mo_ref[0] = m
    for t in range(n_tok):
        hh = num[t] / jnp.maximum(jnp.abs(qn[t]), jnp.exp(-m_all[t]))
        ms = jnp.mean(hh * hh, axis=0, keepdims=True)
        y_ref[t] = hh * lax.rsqrt(ms + EPS) * gn_ref[0] * jax.nn.sigmoid(o_ref[t])


def _mixer_sample_T(projT, convT, cwb, cbb, ssdT, retT, cT, nT, mT, tabs, cos, sin, l):
    n_tok, _, nb = projT.shape
    dtb, alog, dsk, gs, gam, gr, bi, bf, gm = tabs
    hb = lambda col: col // HEAD_DIM
    rep = SSD_HEADS // SSD_GROUPS
    proj_blk = lambda first, div=1: pl.BlockSpec((n_tok, HEAD_DIM, nb), lambda h: (0, first + h // div, 0))
    small = pl.BlockSpec((n_tok, LANES, nb), lambda h: (0, P_SM // LANES, 0))
    lay4 = lambda a, first=0, div=1: pl.BlockSpec((1,) + a.shape[1:2] + (HEAD_DIM, nb),
                                                  lambda h: (l, 0, first + h // div, 0))
    head4 = lambda a: pl.BlockSpec((1, 1) + a.shape[2:], lambda h: (l, h) + (0,) * (a.ndim - 2))
    vm = lambda: pltpu.VMEM((n_tok, HEAD_DIM, nb), F32)
    state_spec = pl.BlockSpec((1, HEAD_DIM, HEAD_DIM, nb), lambda h: (h, 0, 0, 0))
    b0, c0 = hb(SSD_WIDTH), hb(SSD_WIDTH + SSD_GROUPS * SSD_STATE)

    y_ssd, ssd_n, cx, cb, cc = pl.pallas_call(
        _ssd_T_kernel,
        grid=(SSD_HEADS,),
        in_specs=[proj_blk(hb(P_Z)), proj_blk(hb(P_XBC)), proj_blk(hb(P_XBC) + b0, rep),
                  proj_blk(hb(P_XBC) + c0, rep), small,
                  lay4(convT), lay4(convT, b0, rep), lay4(convT, c0, rep),
                  lay4(cwb), lay4(cwb, b0, rep), lay4(cwb, c0, rep),
                  pl.BlockSpec((1, HEAD_DIM, nb), lambda h: (l, h, 0)),
                  pl.BlockSpec((1, HEAD_DIM, nb), lambda h: (l, b0 + h // rep, 0)),
                  pl.BlockSpec((1, HEAD_DIM, nb), lambda h: (l, c0 + h // rep, 0)),
                  head4(dtb), head4(alog), head4(dsk),
                  pl.BlockSpec((1, rep * HEAD_DIM, nb), lambda h: (l, h // rep, 0)), head4(ssdT)],
        out_specs=[pl.BlockSpec((n_tok, rep * HEAD_DIM, nb), lambda h: (0, h // rep, 0)), state_spec,
                   pl.BlockSpec((SSD_CONV - 1, HEAD_DIM, nb), lambda h: (0, h, 0)),
                   pl.BlockSpec((SSD_CONV - 1, HEAD_DIM, nb), lambda h: (0, h // rep, 0)),
                   pl.BlockSpec((SSD_CONV - 1, HEAD_DIM, nb), lambda h: (0, h // rep, 0))],
        out_shape=[jax.ShapeDtypeStruct((n_tok, SSD_WIDTH, nb), F32),
                   jax.ShapeDtypeStruct((SSD_HEADS, HEAD_DIM, HEAD_DIM, nb), F32),
                   jax.ShapeDtypeStruct((SSD_CONV - 1, SSD_WIDTH, nb), F32),
                   jax.ShapeDtypeStruct((SSD_CONV - 1, SSD_GROUPS * SSD_STATE, nb), F32),
                   jax.ShapeDtypeStruct((SSD_CONV - 1, SSD_GROUPS * SSD_STATE, nb), F32)],
        scratch_shapes=[vm(), vm(), vm()],
        compiler_params=_cparams(("arbitrary",)),
        name="sample_ssd",
    )(projT, projT, projT, projT, projT, convT, convT, convT, cwb, cwb, cwb, cbb, cbb, cbb,
      dtb, alog, dsk, gs, ssdT)

    y_ret, ret_n = pl.pallas_call(
        _ret_T_kernel,
        grid=(RET_HEADS,),
        in_specs=[proj_blk(hb(P_RET)), proj_blk(hb(P_RET + RET_WIDTH)), proj_blk(hb(P_RET + 2 * RET_WIDTH)),
                  proj_blk(hb(P_RET + 3 * RET_WIDTH)), _const_spec(cos), _const_spec(sin),
                  pl.BlockSpec((1, 1, nb), lambda h: (h, 0, 0)),
                  pl.BlockSpec((1, HEAD_DIM, nb), lambda h: (l, h, 0)), head4(retT)],
        out_specs=[pl.BlockSpec((n_tok, HEAD_DIM, nb), lambda h: (0, h, 0)), state_spec],
        out_shape=[jax.ShapeDtypeStruct((n_tok, RET_WIDTH, nb), F32),
                   jax.ShapeDtypeStruct((RET_HEADS, HEAD_DIM, HEAD_DIM, nb), F32)],
        scratch_shapes=[vm(), vm(), vm()],
        compiler_params=_cparams(("arbitrary",)),
        name="sample_ret",
    )(projT, projT, projT, projT, cos, sin, gam, gr, retT)

    y_ml, c_n, n_n, m_n = pl.pallas_call(
        _ml_T_kernel,
        grid=(ML_HEADS,),
        in_specs=[proj_blk(hb(P_ML)), proj_blk(hb(P_ML + ML_WIDTH)), proj_blk(hb(P_ML + 2 * ML_WIDTH)),
                  proj_blk(hb(P_ML + 3 * ML_WIDTH)), small, head4(bi), head4(bf),
                  pl.BlockSpec((1, HEAD_DIM, nb), lambda h: (l, h, 0)), head4(cT), head4(nT), head4(mT)],
        out_specs=[pl.BlockSpec((n_tok, HEAD_DIM, nb), lambda h: (0, h, 0)), state_spec,
                   pl.BlockSpec((1, HEAD_DIM, nb), lambda h: (h, 0, 0)),
                   pl.BlockSpec((1, 1, nb), lambda h: (h, 0, 0))],
        out_shape=[jax.ShapeDtypeStruct((n_tok, ML_WIDTH, nb), F32),
                   jax.ShapeDtypeStruct((ML_HEADS, HEAD_DIM, HEAD_DIM, nb), F32),
                   jax.ShapeDtypeStruct((ML_HEADS, HEAD_DIM, nb), F32),
                   jax.ShapeDtypeStruct((ML_HEADS, 1, nb), F32)],
        scratch_shapes=[vm(), vm(), vm()],
        compiler_params=_cparams(("arbitrary",)),
        name="sample_mlstm",
    )(projT, projT, projT, projT, projT, bi, bf, gm, cT, nT, mT)

    y_t = jnp.concatenate([y_ssd, y_ret, y_ml], axis=1)
    conv_n = jnp.concatenate([cx, cb, cc], axis=1)
    return y_t, (ssd_n, conv_n, ret_n, c_n, n_n, m_n)


def _route(logits):
    lane = lax.broadcasted_iota(I32, logits.shape, 1)
    gmask = (lane >= N_EXPERTS) & (lane < N_EXPERTS + EXPERT_GROUPS)
    gl = jnp.where(gmask, logits, -jnp.inf)
    ge = jnp.exp(gl - jnp.max(gl, axis=-1, keepdims=True))
    gprob = ge / jnp.sum(ge, axis=-1, keepdims=True)
    g_w = jnp.max(gprob, axis=-1, keepdims=True)
    g_idx = jnp.min(jnp.where(gmask & (gprob == g_w), lane - N_EXPERTS, LANES), axis=-1, keepdims=True)
    emask = (lane < N_EXPERTS) & ((lane >> 3) == g_idx)
    el = jnp.where(emask, logits, -jnp.inf)
    ee = jnp.exp(el - jnp.max(el, axis=-1, keepdims=True))
    eprob = ee / jnp.sum(ee, axis=-1, keepdims=True)
    p1 = jnp.max(jnp.where(emask, eprob, -1.0), axis=-1, keepdims=True)
    i1 = jnp.min(jnp.where(emask & (eprob == p1), lane, LANES), axis=-1, keepdims=True)
    rest = emask & (lane != i1)
    p2 = jnp.max(jnp.where(rest, eprob, -1.0), axis=-1, keepdims=True)
    i2 = jnp.min(jnp.where(rest & (eprob == p2), lane, LANES), axis=-1, keepdims=True)
    tot = p1 + p2
    key = g_idx * 64 + (jnp.minimum(i1, i2) & 7) * 8 + (jnp.maximum(i1, i2) & 7)
    return g_w * (p1 / tot), g_w * (p2 / tot), i1, i2, key


def _outproj_kernel(yp, ys, xp, xs, gtp, gts, scp, scs, shp, shs, g_ref, w_ref, wr_ref, br_ref,
                    x1_ref, hx_ref, info_ref, cnt_ref, carry, *, np_tiles):
    i = pl.program_id(0)
    is_p = i < np_tiles

    @pl.when(i == 0)
    def _():
        carry[...] = jnp.zeros_like(carry)

    x1 = _dual_val(xp, xs, is_p) + _mod_val(gtp, gts, is_p) * _dot(_dual_val(yp, ys, is_p), w_ref[0])
    x1_ref[...] = x1
    h2 = _rms(x1) * g_ref[0] * (1.0 + _mod_val(scp, scs, is_p)) + _mod_val(shp, shs, is_p)
    h2b = h2.astype(BF16)
    w1, w2, i1, i2, key = _route(_dot(h2b, wr_ref[0]) + br_ref[0])

    tm = x1.shape[0]
    onehot = lax.broadcasted_iota(I32, (tm, N_CLASS), 1) == key
    tril = (lax.broadcasted_iota(I32, (tm, tm), 1) <= lax.broadcasted_iota(I32, (tm, tm), 0))
    upto = _dot(jnp.where(tril, 1.0, 0.0).astype(BF16), jnp.where(onehot, 1.0, 0.0).astype(BF16))
    seen = carry[0:1, :]
    rank = jnp.sum(jnp.where(onehot, upto - 1.0 + seen, 0.0), axis=-1, keepdims=True)
    seen = seen + upto[tm - 1:tm, :]
    carry[0:1, :] = seen
    cnt_ref[...] = jnp.broadcast_to(seen, cnt_ref.shape)

    lane = lax.broadcasted_iota(I32, (tm, LANES), 1)
    info = jnp.zeros((tm, LANES), F32)
    for c, v in enumerate((w1, w2, i1.astype(F32), i2.astype(F32), key.astype(F32), rank)):
        info = jnp.where(lane == c, v, info)
    hx_ref[:, 0:D_MODEL] = h2b.astype(F32)
    hx_ref[:, D_MODEL:H_EXT] = info
    info_ref[...] = info[:, 0:8]


def _outproj(y, x, gt, sc, sh, g, w, wr, br, l, tp):
    n_prompt, ns = x[0].shape[0], x[1].shape[0]
    n = n_prompt + ns
    tm = TOKEN_TILE
    npt = n_prompt // tm
    ms = _mod_specs(tm, n_prompt, tp)
    return pl.pallas_call(
        functools.partial(_outproj_kernel, np_tiles=npt),
        grid=(n // tm,),
        in_specs=_dual_specs(tm, npt, D_MODEL) + _dual_specs(tm, npt, D_MODEL) + ms + ms + ms
                 + [_layer_spec(a, l) for a in (g, w, wr, br)],
        out_specs=[_row_spec(tm, D_MODEL), _row_spec(tm, H_EXT), _row_spec(tm, 8),
                   pl.BlockSpec((8, N_CLASS), lambda i: (0, 0))],
        out_shape=[jax.ShapeDtypeStruct((n, D_MODEL), F32),
                   jax.ShapeDtypeStruct((n, H_EXT), F32),
                   jax.ShapeDtypeStruct((n, 8), F32),
                   jax.ShapeDtypeStruct((8, N_CLASS), F32)],
        scratch_shapes=[pltpu.VMEM((8, N_CLASS), F32)],
        compiler_params=_cparams(("arbitrary",)),
        name="outproj_router",
    )(*y, *x, *gt, *sc, *sh, g, w, wr, br)


def _scatter_kernel(pos_ref, x_ref, o_hbm, buf, sem):
    i, n = pl.program_id(0), pl.num_programs(0)
    tm = x_ref.shape[0]
    slot = i % 2

    def wait_slot(s):
        pltpu.make_async_copy(buf.at[s], o_hbm.at[pl.ds(0, tm), :], sem.at[s]).wait()

    @pl.when(i >= 2)
    def _():
        wait_slot(slot)

    buf[slot] = x_ref[...]

    def body(r, c):
        dst = pos_ref[i * tm + r]
        pltpu.make_async_copy(buf.at[slot, pl.ds(r, 1), :], o_hbm.at[pl.ds(dst, 1), :], sem.at[slot]).start()
        return c
    lax.fori_loop(0, tm, body, 0, unroll=8)

    @pl.when(i == n - 1)
    def _():
        wait_slot(slot)

        @pl.when(n >= 2)
        def _():
            wait_slot(1 - slot)


def _scatter_rows(x, pos):
    n, width = x.shape
    tm = MOE_TILE
    return pl.pallas_call(
        _scatter_kernel,
        grid_spec=pltpu.PrefetchScalarGridSpec(
            num_scalar_prefetch=1, grid=(n // tm,),
            in_specs=[_row_spec(tm, width)],
            out_specs=pl.BlockSpec(memory_space=pl.ANY),
            scratch_shapes=[pltpu.VMEM((2, tm, width), x.dtype), pltpu.SemaphoreType.DMA((2,))]),
        out_shape=jax.ShapeDtypeStruct((n, width), x.dtype),
        compiler_params=_cparams(("arbitrary",)),
        name="scatter_rows",
    )(pos, x)


def _gather_tiles(idx_ref, src_hbm, buf, sem, tm):
    i, n = pl.program_id(0), pl.num_programs(0)

    def issue(tile, slot):
        def body(r, c):
            row = idx_ref[tile * tm + r]
            pltpu.make_async_copy(src_hbm.at[pl.ds(row, 1), :], buf.at[slot, pl.ds(r, 1), :],
                                  sem.at[slot]).start()
            return c
        lax.fori_loop(0, tm, body, 0, unroll=8)

    @pl.when(i == 0)
    def _():
        issue(0, 0)

    @pl.when(i + 1 < n)
    def _():
        issue(i + 1, (i + 1) % 2)

    slot = i % 2
    pltpu.make_async_copy(src_hbm.at[pl.ds(0, tm), :], buf.at[slot], sem.at[slot]).wait()
    return slot


def _experts_kernel(tile_ref, grp_ref, flag_ref, hx_ref, wgu_ref, wd_ref, o_ref):
    j = pl.program_id(0)
    flags = flag_ref[j]
    valid = (flags & 1) != 0

    @pl.when(valid & ((flags & 2) != 0))
    def _():
        o_ref[...] = jnp.zeros_like(o_ref)

    @pl.when(valid)
    def _():
        h = hx_ref[:, 0:D_MODEL].astype(BF16)
        r = hx_ref[:, D_MODEL:H_EXT]
        w1, w2, i1, i2 = r[:, 0:1], r[:, 1:2], r[:, 2:3], r[:, 3:4]
        base = grp_ref[j] * EXPERTS_PER_GROUP
        for e in range(EXPERTS_PER_GROUP):
            @pl.when(((flags >> (8 + e)) & 1) != 0)
            def _():
                eid = (base + e).astype(F32)
                ge = jnp.where(i1 == eid, w1, 0.0) + jnp.where(i2 == eid, w2, 0.0)
                au = _dot(h, wgu_ref[0, 0, e])
                act = _silu(au[:, :EXPERT_FF]) * au[:, EXPERT_FF:]
                o_ref[...] += ge * _dot(act.astype(BF16), wd_ref[0, 0, e])


def _experts(hx_sorted, tile, grp, flags, wgu, wd, l):
    n = hx_sorted.shape[0]
    tm = MOE_TILE
    return pl.pallas_call(
        _experts_kernel,
        grid_spec=pltpu.PrefetchScalarGridSpec(
            num_scalar_prefetch=3, grid=(tile.shape[0],),
            in_specs=[pl.BlockSpec((tm, H_EXT), lambda j, t, g, f: (t[j], 0)),
                      pl.BlockSpec((1, 1, EXPERTS_PER_GROUP, D_MODEL, 2 * EXPERT_FF),
                                   lambda j, t, g, f: (l, g[j], 0, 0, 0)),
                      pl.BlockSpec((1, 1, EXPERTS_PER_GROUP, EXPERT_FF, D_MODEL),
                                   lambda j, t, g, f: (l, g[j], 0, 0, 0))],
            out_specs=pl.BlockSpec((tm, D_MODEL), lambda j, t, g, f: (t[j], 0))),
        out_shape=jax.ShapeDtypeStruct((n, D_MODEL), F32),
        compiler_params=_cparams(("arbitrary",)),
        name="moe_experts",
    )(tile, grp, flags, hx_sorted, wgu, wd)


def _combine_kernel(pos_ref, y_hbm, x1_ref, gtp, gts, gf_ref, op_ref, os_ref, buf, sem, *, np_tiles, final):
    slot = _gather_tiles(pos_ref, y_hbm, buf, sem, x1_ref.shape[0])
    is_p = pl.program_id(0) < np_tiles
    x2 = x1_ref[...] + _mod_val(gtp, gts, is_p) * buf[slot]
    out = _rms(x2) * gf_ref[...] if final else x2

    @pl.when(is_p)
    def _():
        op_ref[...] = out

    @pl.when(jnp.logical_not(is_p))
    def _():
        os_ref[...] = out


def _combine(y_sorted, pos, x1, gt, gf, n_prompt, tp, final):
    n = x1.shape[0]
    tm = MOE_TILE
    npt = n_prompt // tm
    return pl.pallas_call(
        functools.partial(_combine_kernel, np_tiles=npt, final=final),
        grid_spec=pltpu.PrefetchScalarGridSpec(
            num_scalar_prefetch=1, grid=(n // tm,),
            in_specs=[pl.BlockSpec(memory_space=pl.ANY), _row_spec(tm, D_MODEL)]
                     + _mod_specs(tm, n_prompt, tp) + [_const_spec(gf)],
            out_specs=_dual_specs(tm, npt, D_MODEL),
            scratch_shapes=[pltpu.VMEM((2, tm, D_MODEL), F32), pltpu.SemaphoreType.DMA((2,))]),
        out_shape=[jax.ShapeDtypeStruct((n_prompt, D_MODEL), F32),
                   jax.ShapeDtypeStruct((n - n_prompt, D_MODEL), F32)],
        compiler_params=_cparams(("arbitrary",)),
        name="moe_combine",
    )(pos, y_sorted, x1, *gt, gf)


def _routing_tables(cnt, key, rank, n_tiles, tm):
    n_items = n_tiles + EXPERT_GROUPS - 1
    c = jnp.arange(N_CLASS, dtype=I32)
    start_c = jnp.sum(jnp.where(c[:, None] < c[None, :], cnt[:, None], 0), axis=0)
    pos = jnp.sum(jnp.where(key[:, None] == c[None, :], start_c[None, :], 0), axis=1) + rank

    g = jnp.arange(EXPERT_GROUPS, dtype=I32)
    gend = jnp.sum(jnp.where(c[None, :] < 64 * (g[:, None] + 1), cnt[None, :], 0), axis=1)
    t0 = jnp.arange(n_tiles, dtype=I32) * tm
    t1 = t0 + (tm - 1)
    gfirst = jnp.sum((gend[None, :] <= t0[:, None]).astype(I32), axis=1)
    glast = jnp.sum((gend[None, :] <= t1[:, None]).astype(I32), axis=1)
    per_tile = glast - gfirst + 1
    tt = jnp.arange(n_tiles, dtype=I32)
    start_t = jnp.sum(jnp.where(tt[:, None] < tt[None, :], per_tile[:, None], 0), axis=0)
    total = jnp.sum(per_tile)

    j = jnp.arange(n_items, dtype=I32)
    valid = j < total
    tile = jnp.sum((start_t[None, :] <= j[:, None]).astype(I32), axis=1) - 1
    tile = jnp.where(valid, tile, n_tiles - 1)
    sel = tile[:, None] == tt[None, :]
    pick = lambda v: jnp.sum(jnp.where(sel, v[None, :], 0), axis=1)
    grp = jnp.where(valid, pick(gfirst) + (j - pick(start_t)), glast[n_tiles - 1])
    first = valid & (j == pick(start_t))

    in_tile = (cnt[None, :] > 0) & (start_c[None, :] <= t1[:, None]) & ((start_c + cnt)[None, :] > t0[:, None])
    e = jnp.arange(N_EXPERTS, dtype=I32)
    member = (((c >> 6) * 8 + ((c >> 3) & 7))[:, None] == e[None, :]) | (((c >> 6) * 8 + (c & 7))[:, None] == e[None, :])
    present_t = jnp.any(in_tile[:, :, None] & member[None, :, :], axis=1)
    present_j = jnp.any(sel[:, :, None] & present_t[None, :, :], axis=1)
    eg = e[None, :] - grp[:, None] * EXPERTS_PER_GROUP
    bits = jnp.sum(jnp.where(present_j & (eg >= 0) & (eg < EXPERTS_PER_GROUP),
                             1 << (8 + jnp.clip(eg, 0, EXPERTS_PER_GROUP - 1)), 0), axis=1)
    flags = valid.astype(I32) | (first.astype(I32) << 1) | bits
    return pos.astype(I32), tile.astype(I32), grp.astype(I32), flags.astype(I32)


def _moe(hx, info, counts, x1, gt, gf, wgu, wd, l, n_prompt, tp, final):
    n = hx.shape[0]
    tm = MOE_TILE
    pos, tile, grp, flags = _routing_tables(counts[0].astype(I32), info[:, 4].astype(I32),
                                            info[:, 5].astype(I32), n // tm, tm)
    y_sorted = _experts(_scatter_rows(hx, pos), tile, grp, flags, wgu, wd, l)
    return _combine(y_sorted, pos, x1, gt, gf, n_prompt, tp, final)


def _rope_tables(pos):
    half = HEAD_DIM // 2
    inv = ROPE_BASE ** (-jnp.arange(half, dtype=F32) / half)
    ang = pos.astype(F32)[:, None] * inv[None, :]
    cos, sin = jnp.cos(ang), jnp.sin(ang)
    cos_t = jnp.tile(jnp.concatenate([cos, cos], axis=-1), (1, RET_HEADS))
    sin_t = jnp.tile(jnp.concatenate([-sin, sin], axis=-1), (1, RET_HEADS))
    return cos_t, sin_t


def _mixer_consts(w_in, conv_w, conv_b, dt_bias, a_log, d_skip, g_ssd_norm, g_ret_norm,
                  b_mlstm_i, b_mlstm_f, g_mlstm_norm):
    nl = w_in.shape[0]
    w_p = jnp.concatenate([w_in[:, :, 0:1280], w_in[:, :, 1288:3336], w_in[:, :, 1280:1288],
                           w_in[:, :, 3336:3344], jnp.zeros((nl, D_MODEL, LANES - 16), F32)], axis=2).astype(BF16)
    pad = lambda v: jnp.pad(v, ((0, 0), (0, LANES - v.shape[1])))
    log_gamma = jnp.log(1.0 - 2.0 ** (-5.0 - jnp.arange(RET_HEADS, dtype=F32)))
    lg = jnp.broadcast_to(jnp.concatenate([jnp.zeros((8,), F32), log_gamma])[None, :], (nl, 12))
    ptab = jnp.stack([pad(jnp.concatenate([dt_bias, b_mlstm_i, b_mlstm_f], axis=1)), pad(a_log), pad(lg)]
                     + [jnp.zeros((nl, LANES), F32)] * 5, axis=1)
    consts = (ptab, conv_w, conv_b[:, None, :], jnp.repeat(d_skip, HEAD_DIM, axis=1)[:, None, :],
              g_ssd_norm[:, None, :], g_ret_norm[:, None, :], g_mlstm_norm[:, None, :])
    return w_p, consts


def kernel(x_prompt, x_sample, state_ssd, state_ssd_conv, state_ret, state_mlstm_c, state_mlstm_n,
           state_mlstm_m, c_prompt, c_sample, w_ada, b_ada, g_norm1, g_norm2, w_in, conv_w, conv_b,
           dt_bias, a_log, d_skip, g_ssd_norm, g_ret_norm, b_mlstm_i, b_mlstm_f, g_mlstm_norm, w_out,
           w_router_group, b_router_group, w_router_expert, b_router_expert, w_gate_up, w_down, g_final):
    bp, tp, _ = x_prompt.shape
    bs, ts, _ = x_sample.shape
    n_srows = bs * ts
    n_prompt = bp * tp

    mod = _ada(jnp.concatenate([c_prompt, c_sample], axis=0), w_ada, b_ada)

    def mods(l):
        m = mod[l].reshape(bp + bs, 6, D_MODEL)
        return [(m[:bp, k][:, None, :], jnp.tile(m[bp:, k], (ts, 1))) for k in range(6)]

    cos_p, sin_p = _rope_tables(jnp.arange(tp, dtype=I32))
    lanes = lambda v: jnp.broadcast_to(v[..., None], v.shape + (bs,))
    half = HEAD_DIM // 2
    ang = (PAST_LEN + jnp.arange(ts, dtype=I32)).astype(F32)[:, None] * (
        ROPE_BASE ** (-jnp.arange(half, dtype=F32) / half))[None, :]
    cos_s, sin_s = lanes(jnp.cos(ang)), lanes(jnp.sin(ang))
    gamma = 1.0 - 2.0 ** (-5.0 - jnp.arange(RET_HEADS, dtype=F32))
    tabs = (lanes(dt_bias[:, :, None]), lanes(a_log[:, :, None]), lanes(d_skip[:, :, None]), lanes(g_ssd_norm),
            lanes(gamma[:, None]), lanes(g_ret_norm), lanes(b_mlstm_i[:, :, None]), lanes(b_mlstm_f[:, :, None]),
            lanes(g_mlstm_norm))
    cwb, cbb = lanes(conv_w), lanes(conv_b)
    ssd_t = jnp.transpose(state_ssd, (0, 2, 3, 4, 1))
    ret_t = jnp.transpose(state_ret, (0, 2, 3, 4, 1))
    c_t = jnp.transpose(state_mlstm_c, (0, 2, 3, 4, 1))
    n_t = jnp.transpose(state_mlstm_n, (0, 2, 3, 1))
    m_t = jnp.transpose(state_mlstm_m, (0, 2, 1))[:, :, None, :]
    conv_t = jnp.transpose(state_ssd_conv, (0, 2, 3, 1))

    w_p, consts = _mixer_consts(w_in, conv_w, conv_b, dt_bias, a_log, d_skip, g_ssd_norm, g_ret_norm,
                                b_mlstm_i, b_mlstm_f, g_mlstm_norm)
    w_o = w_out.astype(BF16)
    zpad = LANES - N_EXPERTS - EXPERT_GROUPS
    wr = jnp.concatenate([w_router_expert, w_router_group, jnp.zeros((DEPTH, D_MODEL, zpad), F32)],
                         axis=2).astype(BF16)
    br = jnp.concatenate([b_router_expert, b_router_group, jnp.zeros((DEPTH, zpad), F32)], axis=1)[:, None, :]
    wgu = w_gate_up.astype(BF16).reshape(DEPTH, EXPERT_GROUPS, EXPERTS_PER_GROUP, D_MODEL, 2 * EXPERT_FF)
    wd = w_down.astype(BF16).reshape(DEPTH, EXPERT_GROUPS, EXPERTS_PER_GROUP, EXPERT_FF, D_MODEL)
    g1, g2, gf = g_norm1[:, None, :], g_norm2[:, None, :], g_final[None, :]

    x = (x_prompt.reshape(n_prompt, D_MODEL), jnp.transpose(x_sample, (1, 0, 2)).reshape(n_srows, D_MODEL))
    p_states, s_states = [], []
    for l in range(DEPTH):
        final = l == DEPTH - 1
        sh1, sc1, gt1, sh2, sc2, gt2 = mods(l)
        proj = _inproj(x, sc1, sh1, g1, w_p, l, tp)

        ycat_p, st = _mixer_prompt(proj, bp, tp, cos_p, sin_p, consts, l)
        p_states.append(st)

        proj_t = jnp.transpose(proj[n_prompt:].reshape(ts, bs, P_W), (0, 2, 1))
        y_t, st_t = _mixer_sample_T(proj_t, conv_t, cwb, cbb, ssd_t, ret_t, c_t, n_t, m_t, tabs, cos_s, sin_s, l)
        s_states.append(st_t)
        ycat_s = jnp.transpose(y_t, (0, 2, 1)).reshape(n_srows, D_MODEL).astype(BF16)

        x1, hx, info, counts = _outproj((ycat_p, ycat_s), x, gt1, sc2, sh2, g2, w_o, wr, br, l, tp)
        x = _moe(hx, info, counts, x1, gt2, gf, wgu, wd, l, n_prompt, tp, final)

    y_prompt = x[0].reshape(bp, tp, D_MODEL)
    y_sample = jnp.transpose(x[1].reshape(ts, bs, D_MODEL), (1, 0, 2))
    p_st = [jnp.stack([s[i] for s in p_states], axis=0) for i in range(6)]
    s_t = [jnp.stack([s[i] for s in s_states], axis=0) for i in range(6)]
    s_st = [jnp.transpose(s_t[0], (0, 4, 1, 2, 3)), jnp.transpose(s_t[1], (0, 3, 1, 2)),
            jnp.transpose(s_t[2], (0, 4, 1, 2, 3)), jnp.transpose(s_t[3], (0, 4, 1, 2, 3)),
            jnp.transpose(s_t[4], (0, 3, 1, 2)), jnp.transpose(s_t[5][:, :, 0, :], (0, 2, 1))]
    return (y_prompt, y_sample, *p_st, *s_st)
```

```python
import functools
import math

import jax
import jax.numpy as jnp
from jax import lax
from jax.experimental import pallas as pl
from jax.experimental.pallas import tpu as pltpu

F32 = jnp.float32
BF16 = jnp.bfloat16
I32 = jnp.int32

D_MODEL = 1024
DEPTH = 2
PAST_LEN = 16384
SSD_HEADS = 8
SSD_WIDTH = 512
SSD_GROUPS = 2
SSD_STATE = 64
SSD_CONV = 4
SSD_XBC = 768
RET_HEADS = 4
RET_WIDTH = 256
ML_HEADS = 4
ML_WIDTH = 256
HEAD_DIM = 64
ROPE_BASE = 10000.0
EPS = 1e-6
EXPERT_GROUPS = 4
EXPERTS_PER_GROUP = 8
N_EXPERTS = 32
EXPERT_FF = 256

LANES = 128
CHUNK = 128
SEQ_PER_STEP = 4
P_Z, P_XBC, P_RET, P_ML, P_SM, P_W = 0, 512, 1280, 2304, 3328, 3456
C_SSD, C_RET, C_ML = 0, 8, 12
N_PAIRS = 8
NEG = -1e30
TOKEN_TILE = 512
MOE_TILE = 256
OUTPROJ_ROW_BLOCKS = 2
EXPERT_ROW_BLOCKS = 2
VMEM_LIMIT = 56 * 1024 * 1024
H_EXT = D_MODEL + LANES
N_CLASS = 256


def _cparams(sem):
    return pltpu.CompilerParams(dimension_semantics=sem, vmem_limit_bytes=VMEM_LIMIT)


def _split3(x):
    x1 = x.astype(BF16)
    r = x - x1.astype(F32)
    x2 = r.astype(BF16)
    r = r - x2.astype(F32)
    return x1, x2, r.astype(BF16)


def _dot01(m01, x):
    return sum(jnp.dot(m01, p, preferred_element_type=F32) for p in _split3(x))


def _dot(a, b):
    return jnp.dot(a, b, preferred_element_type=F32)


def _dot_nt(a, b):
    return lax.dot_general(a, b, (((1,), (1,)), ((), ())), preferred_element_type=F32)


def _dot_tn(a, b):
    return lax.dot_general(a, b, (((0,), (0,)), ((), ())), preferred_element_type=F32)


def _softplus(x):
    return jnp.maximum(x, 0.0) + jnp.log1p(jnp.exp(-jnp.abs(x)))


def _silu(x):
    return x * jax.nn.sigmoid(x)


def _rms(x):
    return x * lax.rsqrt(jnp.mean(x * x, axis=-1, keepdims=True) + EPS)


def _cummax_rows(x, seg):
    t = lax.broadcasted_iota(I32, x.shape, 0) & (seg - 1)
    s = 1
    while s < seg:
        x = jnp.maximum(x, jnp.where(t >= s, pltpu.roll(x, s, 0), NEG))
        s *= 2
    return x


def _rope(x, cos, sin_signed, lane):
    swapped = jnp.where((lane & 63) < 32, pltpu.roll(x, 96, 1), pltpu.roll(x, 32, 1))
    return x * cos + swapped * sin_signed


def _row_spec(tm, width):
    return pl.BlockSpec((tm, width), lambda i, *_: (i, 0))


def _const_spec(a):
    nd = a.ndim
    return pl.BlockSpec(a.shape, lambda *_: (0,) * nd)


def _layer_spec(a, l):
    nd = a.ndim
    return pl.BlockSpec((1,) + a.shape[1:], lambda *_: (l,) + (0,) * (nd - 1))


def _dual_specs(tm, np_tiles, width):
    return [pl.BlockSpec((tm, width), lambda i, *_: (jnp.minimum(i, np_tiles - 1), 0)),
            pl.BlockSpec((tm, width), lambda i, *_: (jnp.maximum(i - np_tiles, 0), 0))]


def _mod_specs(tm, n_prompt, tp):
    np_tiles, last_b = n_prompt // tm, n_prompt // tp - 1
    return [pl.BlockSpec((1, 1, D_MODEL), lambda i, *_: (jnp.minimum(i * tm // tp, last_b), 0, 0)),
            pl.BlockSpec((tm, D_MODEL), lambda i, *_: (jnp.maximum(i - np_tiles, 0), 0))]


def _mod_val(p_ref, s_ref, is_prompt):
    return jnp.where(is_prompt, p_ref[0], s_ref[...])


def _dual_val(p_ref, s_ref, is_prompt):
    return jnp.where(is_prompt, p_ref[...], s_ref[...])


def _ada_kernel(c_ref, w_ref, b_ref, o_ref):
    c = c_ref[...]
    o_ref[0] = _dot(_silu(c).astype(BF16), w_ref[0].astype(BF16)) + b_ref[0]


def _ada(c_all, w_ada, b_ada):
    nb = c_all.shape[0]
    tn = 1536
    return pl.pallas_call(
        _ada_kernel,
        grid=(DEPTH, 6 * D_MODEL // tn),
        in_specs=[pl.BlockSpec((nb, D_MODEL), lambda l, j: (0, 0)),
                  pl.BlockSpec((1, D_MODEL, tn), lambda l, j: (l, 0, j)),
                  pl.BlockSpec((1, 1, tn), lambda l, j: (l, 0, j))],
        out_specs=pl.BlockSpec((1, nb, tn), lambda l, j: (l, 0, j)),
        out_shape=jax.ShapeDtypeStruct((DEPTH, nb, 6 * D_MODEL), F32),
        compiler_params=_cparams(("arbitrary", "arbitrary")),
        name="ada_mod",
    )(c_all, w_ada, b_ada.reshape(DEPTH, 1, 6 * D_MODEL))


def _inproj_kernel(xp, xs, scp, scs, shp, shs, g_ref, w_ref, o_ref, *, np_tiles):
    is_p = pl.program_id(0) < np_tiles
    h = _rms(_dual_val(xp, xs, is_p)) * g_ref[0] * (1.0 + _mod_val(scp, scs, is_p)) + _mod_val(shp, shs, is_p)
    o_ref[...] = _dot(h.astype(BF16), w_ref[0])


def _inproj(x, sc, sh, g, w, l, tp):
    n_prompt, ns = x[0].shape[0], x[1].shape[0]
    tm = TOKEN_TILE
    ms = _mod_specs(tm, n_prompt, tp)
    return pl.pallas_call(
        functools.partial(_inproj_kernel, np_tiles=n_prompt // tm),
        grid=((n_prompt + ns) // tm,),
        in_specs=_dual_specs(tm, n_prompt // tm, D_MODEL) + ms + ms + [_layer_spec(g, l), _layer_spec(w, l)],
        out_specs=_row_spec(tm, P_W),
        out_shape=jax.ShapeDtypeStruct((n_prompt + ns, P_W), F32),
        compiler_params=_cparams(("arbitrary",)),
        name="norm_inproj",
    )(*x, *sc, *sh, g, w)


def _lane_bcast(a, c):
    return jnp.broadcast_to(a[:, c:c + 1], a.shape)


def _pair_lanes(a, c0, c1):
    lm0 = lax.broadcasted_iota(I32, (a.shape[0], LANES), 1) < HEAD_DIM
    return jnp.where(lm0, a[:, c0:c0 + 1], a[:, c1:c1 + 1])


def _ret_factors(ptab, tril, mask, last_fn):
    rows = tril.shape[0]
    lane = lax.broadcasted_iota(I32, (rows, LANES), 1)
    lm0 = lane < HEAD_DIM
    cum = _dot01(tril, jnp.where((lane >= C_RET) & (lane < C_ML), ptab[2:3], 0.0))
    xt = cum.T
    cum_last = last_fn(cum)
    out = []
    for p in range(RET_HEADS // 2):
        c0, c1 = C_RET + 2 * p, C_RET + 2 * p + 1
        b0, b1 = _lane_bcast(cum, c0), _lane_bcast(cum, c1)
        bp = jnp.where(lm0, b0, b1)
        wp = jnp.exp(_pair_lanes(cum_last, c0, c1) - bp)
        out.append((jnp.exp(jnp.where(mask, b0 - xt[c0:c0 + 1, :], -jnp.inf)),
                    jnp.exp(jnp.where(mask, b1 - xt[c1:c1 + 1, :], -jnp.inf)), jnp.exp(bp), wp))
    return out


def _mixer_core(z, us, retb, mlb, small, cos, sin, ptab, cw, cb, dsk, gs, gr, gm,
                mask, tril, seg, mprev, last_fn, st, y_ref, ret_factors):
    rows = small.shape[0]
    lane = lax.broadcasted_iota(I32, (rows, LANES), 1)
    lm0 = lane < HEAD_DIM

    pre = small + ptab[0:1]
    a_neg = -jnp.exp(ptab[1:2])
    dt = _softplus(pre)
    logf = -_softplus(-pre)
    la = jnp.where(lane < C_RET, dt * a_neg,
                   jnp.where(lane < C_ML, ptab[2:3], jnp.where(lane < C_ML + 4, logf, 0.0)))
    cum = _dot01(tril, la)
    ic = pltpu.roll(pre, 4, 1)
    mlm = (lane >= C_ML) & (lane < C_ML + 4)
    d = jnp.where(mlm, ic - cum, NEG)
    m_t = cum + jnp.maximum(mprev, _cummax_rows(d, seg))
    xt = jnp.where(mlm, d, cum).T
    colv = cum - m_t
    cum_last, m_last = last_fn(cum), last_fn(m_t)
    decq = jnp.where(mlm, jnp.exp(cum_last + mprev - m_last), jnp.exp(cum_last))
    yield

    def factors(c0, c1, kind):
        if kind == "ret":
            return ret_factors((c0 - C_RET) // 2)
        if kind == "ssd":
            b0, b1 = _lane_bcast(cum, c0), _lane_bcast(cum, c1)
            bp = jnp.where(lm0, b0, b1)
            wp = jnp.exp(_pair_lanes(cum_last, c0, c1) - bp)
            return (jnp.exp(jnp.where(mask, b0 - xt[c0:c0 + 1, :], -jnp.inf)),
                    jnp.exp(jnp.where(mask, b1 - xt[c1:c1 + 1, :], -jnp.inf)), jnp.exp(bp), wp)
        a0, a1 = _lane_bcast(colv, c0), _lane_bcast(colv, c1)
        eqp = jnp.exp(jnp.where(lm0, a0, a1) + _pair_lanes(mprev, c0, c1))
        wp = jnp.exp(jnp.where(lm0, _lane_bcast(d, c0), _lane_bcast(d, c1))
                     + _pair_lanes(cum_last, c0, c1) - _pair_lanes(m_last, c0, c1))
        return (jnp.exp(jnp.where(mask, a0 + xt[c0:c0 + 1, :], -jnp.inf)),
                jnp.exp(jnp.where(mask, a1 + xt[c1:c1 + 1, :], -jnp.inf)), eqp, wp)

    def pair(idx, inputs, c0, c1, kind, finish):
        qp, kp, vp = inputs()
        q0 = jnp.where(lm0, qp, 0.0)
        q1 = jnp.where(lm0, 0.0, qp)
        sc = _dot_nt(jnp.concatenate([q0, q1], axis=0).astype(BF16), kp.astype(BF16))
        yield
        d0, d1, eqp, wp = factors(c0, c1, kind)
        v01 = jnp.concatenate([jnp.where(lm0, vp, 0.0), jnp.where(lm0, 0.0, vp)], axis=0).astype(BF16)
        kw = kp * wp
        yield
        carried, qn = st.step(idx, qp, eqp, kw, vp, decq, c0, c1, kind == "ml")
        yield
        s0, s1 = sc[0:rows] * d0, sc[rows:2 * rows] * d1
        intra = _dot(jnp.concatenate([s0, s1], axis=1).astype(BF16), v01)
        yield
        finish(intra + carried, s0, s1, qn)

    def head_norm(o):
        o2 = o * o
        ms0 = jnp.sum(jnp.where(lm0, o2, 0.0), axis=-1, keepdims=True) * (1.0 / HEAD_DIM)
        ms1 = jnp.sum(jnp.where(lm0, 0.0, o2), axis=-1, keepdims=True) * (1.0 / HEAD_DIM)
        return o * jnp.where(lm0, lax.rsqrt(ms0 + EPS), lax.rsqrt(ms1 + EPS))

    conv = cb + us[0] * cw[0:1] + us[1] * cw[1:2] + us[2] * cw[2:3] + us[3] * cw[3:4]
    xc = _silu(conv)
    bb = xc[:, SSD_WIDTH:SSD_WIDTH + LANES]
    cc = xc[:, SSD_WIDTH + LANES:SSD_WIDTH + 2 * LANES]
    br = pltpu.roll(bb, HEAD_DIM, 1)
    cr = pltpu.roll(cc, HEAD_DIM, 1)
    ys = [None] * 4
    pairs = []
    for p in range(4):
        c0, c1 = C_SSD + 2 * p, C_SSD + 2 * p + 1
        sl = slice(LANES * p, LANES * (p + 1))

        def ssd_inputs(p=p, c0=c0, c1=c1, sl=sl):
            if p < 2:
                kp, qp = jnp.where(lm0, bb, br), jnp.where(lm0, cc, cr)
            else:
                kp, qp = jnp.where(lm0, br, bb), jnp.where(lm0, cr, cc)
            dtp = jnp.where(lm0, dt[:, c0:c0 + 1], dt[:, c1:c1 + 1])
            return qp, kp, xc[:, sl] * dtp

        def ssd_finish(o, s0, s1, qn, p=p, sl=sl):
            ys[p] = (o + dsk[:, sl] * xc[:, sl]) * _silu(z[:, sl])

        pairs.append(pair(p, ssd_inputs, c0, c1, "ssd", ssd_finish))

    for p in range(2):
        c0, c1 = C_RET + 2 * p, C_RET + 2 * p + 1
        sl = slice(LANES * p, LANES * (p + 1))

        def ret_inputs(p=p, sl=sl):
            qp = _rope(retb[:, LANES * p:LANES * (p + 1)], cos[:, sl], sin[:, sl], lane)
            kp = _rope(retb[:, RET_WIDTH + LANES * p:RET_WIDTH + LANES * (p + 1)], cos[:, sl], sin[:, sl], lane)
            return qp, kp * (HEAD_DIM ** -0.5), retb[:, 2 * RET_WIDTH + LANES * p:2 * RET_WIDTH + LANES * (p + 1)]

        def ret_finish(o, s0, s1, qn, p=p, sl=sl):
            gp = retb[:, 3 * RET_WIDTH + LANES * p:3 * RET_WIDTH + LANES * (p + 1)]
            y = head_norm(o) * gr[:, sl] * _silu(gp)
            y_ref[:, SSD_WIDTH + LANES * p:SSD_WIDTH + LANES * (p + 1)] = y.astype(y_ref.dtype)

        pairs.append(pair(4 + p, ret_inputs, c0, c1, "ret", ret_finish))

    for p in range(2):
        c0, c1 = C_ML + 2 * p, C_ML + 2 * p + 1
        sl = slice(LANES * p, LANES * (p + 1))

        def ml_inputs(p=p):
            return (mlb[:, LANES * p:LANES * (p + 1)],
                    mlb[:, ML_WIDTH + LANES * p:ML_WIDTH + LANES * (p + 1)] * (HEAD_DIM ** -0.5),
                    mlb[:, 2 * ML_WIDTH + LANES * p:2 * ML_WIDTH + LANES * (p + 1)])

        def ml_finish(num, s0, s1, qn, p=p, c0=c0, c1=c1, sl=sl):
            op = mlb[:, 3 * ML_WIDTH + LANES * p:3 * ML_WIDTH + LANES * (p + 1)]
            inter0 = jnp.exp(colv[:, c0:c0 + 1] + mprev[:, c0:c0 + 1])
            inter1 = jnp.exp(colv[:, c1:c1 + 1] + mprev[:, c1:c1 + 1])
            den0 = jnp.sum(s0, axis=-1, keepdims=True) + qn[0] * inter0
            den1 = jnp.sum(s1, axis=-1, keepdims=True) + qn[1] * inter1
            dn0 = jnp.maximum(jnp.abs(den0), jnp.exp(-m_t[:, c0:c0 + 1]))
            dn1 = jnp.maximum(jnp.abs(den1), jnp.exp(-m_t[:, c1:c1 + 1]))
            hh = num / jnp.where(lm0, dn0, dn1)
            y = head_norm(hh) * gm[:, sl] * jax.nn.sigmoid(op)
            off = SSD_WIDTH + RET_WIDTH + LANES * p
            y_ref[:, off:off + LANES] = y.astype(y_ref.dtype)

        pairs.append(pair(6 + p, ml_inputs, c0, c1, "ml", ml_finish))

    live = list(pairs)
    while live:
        for g in list(live):
            try:
                next(g)
            except StopIteration:
                live.remove(g)
        yield

    for g in range(SSD_GROUPS):
        ya, yb = ys[2 * g], ys[2 * g + 1]
        ms = (jnp.sum(ya * ya, axis=-1, keepdims=True)
              + jnp.sum(yb * yb, axis=-1, keepdims=True)) * (1.0 / (2 * LANES))
        r = lax.rsqrt(ms + EPS)
        for j, yv in ((2 * g, ya), (2 * g + 1, yb)):
            sl = slice(LANES * j, LANES * (j + 1))
            y_ref[:, sl] = (yv * r * gs[:, sl]).astype(y_ref.dtype)
    return m_t


def _run_interleaved(gens):
    out = [None] * len(gens)
    live = list(range(len(gens)))
    while live:
        for k in list(live):
            try:
                next(gens[k])
            except StopIteration as stop:
                out[k] = stop.value
                live.remove(k)
    return out


def _half_rows():
    return lax.broadcasted_iota(I32, (LANES, HEAD_DIM), 0) < HEAD_DIM


class _CarriedState:
    def __init__(self, sv, nrow):
        self.sv, self.nrow = sv, nrow

    def step(self, idx, qp, eqp, kw, vp, decq, c0, c1, ml):
        lm0 = lax.broadcasted_iota(I32, (1, LANES), 1) < HEAD_DIM
        drow = _pair_lanes(decq[0:1, :], c0, c1)
        s_old = self.sv[idx]
        carried = _dot(qp.astype(BF16), s_old.astype(BF16)) * eqp
        u = _dot_tn(kw.astype(BF16), vp.astype(BF16))
        same_head = ((lax.broadcasted_iota(I32, (LANES, LANES), 0) < HEAD_DIM)
                     == (lax.broadcasted_iota(I32, (LANES, LANES), 1) < HEAD_DIM))
        self.sv[idx] = s_old * drow + jnp.where(same_head, u, 0.0)
        qn = None
        if ml:
            p = idx - 6
            n_old = self.nrow[p:p + 1, :]
            qn_l = qp * n_old
            qn = (jnp.sum(jnp.where(lm0, qn_l, 0.0), axis=-1, keepdims=True),
                  jnp.sum(jnp.where(lm0, 0.0, qn_l), axis=-1, keepdims=True))
            self.nrow[p:p + 1, :] = n_old * drow + jnp.sum(kw, axis=0, keepdims=True)
        return carried, qn


def _mixer_prompt_kernel(*refs):
    sq = SEQ_PER_STEP
    proj_refs = refs[:sq]
    (cos_ref, sin_ref, ptab_ref, cw_ref, cb_ref, dsk_ref, gs_ref, gr_ref, gm_ref,
     y_ref, sv_o, conv_o, n_o, m_o) = refs[sq:sq + 14]
    scr = refs[sq + 14:]
    sv, nrow, mrow, cbuf = scr[0:sq], scr[sq:2 * sq], scr[2 * sq:3 * sq], scr[3 * sq:4 * sq]
    ret_cache = scr[4 * sq]
    ci = pl.program_id(1)
    rows = proj_refs[0].shape[1]
    ri = lax.broadcasted_iota(I32, (rows, rows), 0)
    cj = lax.broadcasted_iota(I32, (rows, rows), 1)
    mask = cj <= ri
    tril = jnp.where(mask, 1.0, 0.0).astype(BF16)
    last_row = lambda a: a[rows - 1:rows, :]

    @pl.when(ci == 0)
    def _():
        for s in range(sq):
            sv[s][...] = jnp.zeros_like(sv[s])
            nrow[s][...] = jnp.zeros_like(nrow[s])
            mrow[s][...] = jnp.zeros_like(mrow[s])
            cbuf[s][0:8, :] = jnp.zeros((8, SSD_XBC), F32)
        for p, fs in enumerate(_ret_factors(ptab_ref[0], tril, mask, last_row)):
            for k, f in enumerate(fs):
                ret_cache[4 * p + k] = f

    ret_factors = lambda p: tuple(ret_cache[4 * p + k] for k in range(4))
    gens = []
    for s in range(sq):
        proj_ref = proj_refs[s]
        cbuf[s][8:8 + rows, :] = proj_ref[0, :, P_XBC:P_RET]
        us = [cbuf[s][pl.ds(5 + k, rows), :] for k in range(SSD_CONV)]
        gens.append(_mixer_core(
            proj_ref[0, :, P_Z:P_XBC], us, proj_ref[0, :, P_RET:P_ML], proj_ref[0, :, P_ML:P_SM],
            proj_ref[0, :, P_SM:P_W], cos_ref[...], sin_ref[...], ptab_ref[0], cw_ref[0], cb_ref[0],
            dsk_ref[0], gs_ref[0], gr_ref[0], gm_ref[0],
            mask, tril, rows, mrow[s][0:1, :], last_row,
            _CarriedState(sv[s], nrow[s]), y_ref.at[s], ret_factors))
    for s, m_t in enumerate(_run_interleaved(gens)):
        mrow[s][0:1, :] = m_t[rows - 1:rows, :]
        cbuf[s][0:8, :] = cbuf[s][rows:rows + 8, :]

    @pl.when(ci == pl.num_programs(1) - 1)
    def _():
        top = _half_rows()
        for s in range(sq):
            for idx in range(N_PAIRS):
                bd = sv[s][idx]
                sv_o[s, idx] = jnp.where(top, bd[:, :HEAD_DIM], bd[:, HEAD_DIM:])
            conv_o[s] = cbuf[s][0:8, :]
            n_o[s] = nrow[s][...]
            m_o[s] = mrow[s][...]


def _mixer_prompt(proj, nb, t, cos, sin, consts, l):
    rows = math.gcd(t, CHUNK)
    nc = t // rows
    sq = SEQ_PER_STEP
    assert nb % sq == 0

    def proj_spec(s):
        return pl.BlockSpec((1, rows, P_W), lambda b, c: (0, (sq * b + s) * nc + c, 0))

    def seq_spec(*tail):
        return pl.BlockSpec((sq,) + tail, lambda b, c: (b,) + (0,) * len(tail))

    outs = pl.pallas_call(
        _mixer_prompt_kernel,
        grid=(nb // sq, nc),
        in_specs=[proj_spec(s) for s in range(sq)]
                 + [pl.BlockSpec((rows, RET_WIDTH), lambda b, c: (c, 0)),
                    pl.BlockSpec((rows, RET_WIDTH), lambda b, c: (c, 0))] + [_layer_spec(a, l) for a in consts],
        out_specs=[pl.BlockSpec((sq, rows, D_MODEL), lambda b, c: (b, c, 0)),
                   seq_spec(N_PAIRS, LANES, HEAD_DIM), seq_spec(8, SSD_XBC), seq_spec(8, LANES), seq_spec(8, LANES)],
        out_shape=[jax.ShapeDtypeStruct((nb, t, D_MODEL), BF16),
                   jax.ShapeDtypeStruct((nb, N_PAIRS, LANES, HEAD_DIM), F32),
                   jax.ShapeDtypeStruct((nb, 8, SSD_XBC), F32),
                   jax.ShapeDtypeStruct((nb, 8, LANES), F32),
                   jax.ShapeDtypeStruct((nb, 8, LANES), F32)],
        scratch_shapes=[pltpu.VMEM((N_PAIRS, LANES, LANES), F32) for _ in range(sq)]
                       + [pltpu.VMEM((8, LANES), F32) for _ in range(2 * sq)]
                       + [pltpu.VMEM((rows + 8, SSD_XBC), F32) for _ in range(sq)]
                       + [pltpu.VMEM((2 * RET_HEADS, rows, LANES), F32)],
        compiler_params=_cparams(("arbitrary", "arbitrary")),
        name="mixer_prompt",
    )(*([proj[None]] * sq), cos, sin, *consts)
    y, sv, conv, n, m = outs
    sv = sv.reshape(nb, 2 * N_PAIRS, HEAD_DIM, HEAD_DIM)
    states = (sv[:, :8], conv[:, 5:8], sv[:, 8:12], sv[:, 12:16],
              n[:, 0:2].reshape(nb, ML_HEADS, HEAD_DIM), m[:, 0, C_ML:C_ML + 4])
    return y.reshape(nb * t, D_MODEL), states


def _lane_recurrence(s_in, s_out, q_s, k_s, v_s, decay):
    n_tok = len(decay)
    nvb = HEAD_DIM // 8
    nb = q_s.shape[-1]
    dec8 = [jnp.broadcast_to(d, (8, nb)) for d in decay]

    def body(k, acc):
        acc = [list(a) for a in acc]
        qk = [q_s[t, pl.ds(k, 1), :] for t in range(n_tok)]
        kk = [k_s[t, pl.ds(k, 1), :] for t in range(n_tok)]
        for vb in range(nvb):
            rows = pl.ds(8 * vb, 8)
            s = s_in[k, rows, :]
            for t in range(n_tok):
                s = s * dec8[t] + kk[t] * v_s[t, rows, :]
                acc[t][vb] = acc[t][vb] + qk[t] * s
            s_out[k, rows, :] = s
        return tuple(tuple(a) for a in acc)

    init = tuple(tuple(jnp.zeros((8, nb), F32) for _ in range(nvb)) for _ in range(n_tok))
    acc = lax.fori_loop(0, HEAD_DIM, body, init, unroll=2)
    return [jnp.concatenate(list(a), axis=0) for a in acc]


def _conv_T(u_ref, tail_ref, w_ref, b_ref, state_ref):
    n_tok = u_ref.shape[0]
    full = [tail_ref[0, j] for j in range(SSD_CONV - 1)] + [u_ref[t] for t in range(n_tok)]
    outs = []
    for t in range(n_tok):
        acc = b_ref[0]
        for tap in range(SSD_CONV):
            acc = acc + full[t + tap] * w_ref[0, tap]
        outs.append(_silu(acc))
    for j in range(SSD_CONV - 1):
        state_ref[j] = full[n_tok + j]
    return outs


def _ssd_T_kernel(z_ref, xs_ref, b_ref, c_ref, sm_ref, tx_ref, tb_ref, tc_ref, wx_ref, wb_ref, wc_ref,
                  bx_ref, bb_ref, bc_ref, dtb_ref, alog_ref, dsk_ref, g_ref, s_ref,
                  y_ref, so_ref, cx_ref, cb_ref, cc_ref, q_s, k_s, v_s):
    h = pl.program_id(0)
    n_tok = xs_ref.shape[0]
    xs = _conv_T(xs_ref, tx_ref, wx_ref, bx_ref, cx_ref)
    bm = _conv_T(b_ref, tb_ref, wb_ref, bb_ref, cb_ref)
    cm = _conv_T(c_ref, tc_ref, wc_ref, bc_ref, cc_ref)
    a_neg = -jnp.exp(alog_ref[0, 0])
    decay = []
    for t in range(n_tok):
        dt = _softplus(sm_ref[t, pl.ds(C_SSD + h, 1), :] + dtb_ref[0, 0])
        decay.append(jnp.exp(dt * a_neg))
        q_s[t] = cm[t]
        k_s[t] = bm[t]
        v_s[t] = xs[t] * dt
    o = _lane_recurrence(s_ref.at[0, 0], so_ref.at[0], q_s, k_s, v_s, decay)
    hh = h % (SSD_HEADS // SSD_GROUPS)
    row0 = pl.multiple_of(hh * HEAD_DIM, HEAD_DIM)
    for t in range(n_tok):
        y_ref[t, pl.ds(row0, HEAD_DIM), :] = (o[t] + dsk_ref[0, 0] * xs[t]) * _silu(z_ref[t])

    @pl.when(hh == SSD_HEADS // SSD_GROUPS - 1)
    def _():
        for t in range(n_tok):
            blk = y_ref[t]
            ms = jnp.mean(blk * blk, axis=0, keepdims=True)
            y_ref[t] = blk * lax.rsqrt(ms + EPS) * g_ref[0]


def _ret_T_kernel(q_ref, k_ref, v_ref, g_ref, cos_ref, sin_ref, gam_ref, gn_ref, s_ref,
                  y_ref, so_ref, q_s, k_s, v_s):
    n_tok = q_ref.shape[0]
    half = HEAD_DIM // 2

    def rope(x, t):
        x1, x2 = x[:half], x[half:]
        return jnp.concatenate([x1 * cos_ref[t] - x2 * sin_ref[t], x1 * sin_ref[t] + x2 * cos_ref[t]], axis=0)

    for t in range(n_tok):
        q_s[t] = rope(q_ref[t], t)
        k_s[t] = rope(k_ref[t], t) * (HEAD_DIM ** -0.5)
        v_s[t] = v_ref[t]
    o = _lane_recurrence(s_ref.at[0, 0], so_ref.at[0], q_s, k_s, v_s, [gam_ref[0]] * n_tok)
    for t in range(n_tok):
        ms = jnp.mean(o[t] * o[t], axis=0, keepdims=True)
        y_ref[t] = o[t] * lax.rsqrt(ms + EPS) * gn_ref[0] * _silu(g_ref[t])


def _ml_T_kernel(q_ref, k_ref, v_ref, o_ref, sm_ref, bi_ref, bf_ref, gn_ref, c_ref, n_ref, m_ref,
                 y_ref, co_ref, no_ref, mo_ref, q_s, k_s, v_s):
    h = pl.program_id(0)
    n_tok = q_ref.shape[0]
    m = m_ref[0, 0]
    n = n_ref[0, 0]
    decay, qn, m_all = [], [], []
    for t in range(n_tok):
        i_t = sm_ref[t, pl.ds(C_RET + h, 1), :] + bi_ref[0, 0]
        logf = -_softplus(-(sm_ref[t, pl.ds(C_ML + h, 1), :] + bf_ref[0, 0]))
        m_new = jnp.maximum(logf + m, i_t)
        f_t = jnp.exp(logf + m - m_new)
        kw = k_ref[t] * (HEAD_DIM ** -0.5) * jnp.exp(i_t - m_new)
        n = n * f_t + kw
        q_s[t] = q_ref[t]
        k_s[t] = kw
        v_s[t] = v_ref[t]
        decay.append(f_t)
        qn.append(jnp.sum(q_ref[t] * n, axis=0, keepdims=True))
        m_all.append(m_new)
        m = m_new
    num = _lane_recurrence(c_ref.at[0, 0], co_ref.at[0], q_s, k_s, v_s, decay)
    no_ref[0] = n
    mo_ref[0] = m
    for t in range(n_tok):
        hh = num[t] / jnp.maximum(jnp.abs(qn[t]), jnp.exp(-m_all[t]))
        ms = jnp.mean(hh * hh, axis=0, keepdims=True)
        y_ref[t] = hh * lax.rsqrt(ms + EPS) * gn_ref[0] * jax.nn.sigmoid(o_ref[t])


def _mixer_sample_T(projT, convT, cwb, cbb, ssdT, retT, cT, nT, mT, tabs, cos, sin, l):
    n_tok, _, nb = projT.shape
    dtb, alog, dsk, gs, gam, gr, bi, bf, gm = tabs
    hb = lambda col: col // HEAD_DIM
    rep = SSD_HEADS // SSD_GROUPS
    proj_blk = lambda first, div=1: pl.BlockSpec((n_tok, HEAD_DIM, nb), lambda h: (0, first + h // div, 0))
    small = pl.BlockSpec((n_tok, LANES, nb), lambda h: (0, P_SM // LANES, 0))
    lay4 = lambda a, first=0, div=1: pl.BlockSpec((1,) + a.shape[1:2] + (HEAD_DIM, nb),
                                                  lambda h: (l, 0, first + h // div, 0))
    head4 = lambda a: pl.BlockSpec((1, 1) + a.shape[2:], lambda h: (l, h) + (0,) * (a.ndim - 2))
    vm = lambda: pltpu.VMEM((n_tok, HEAD_DIM, nb), F32)
    state_spec = pl.BlockSpec((1, HEAD_DIM, HEAD_DIM, nb), lambda h: (h, 0, 0, 0))
    b0, c0 = hb(SSD_WIDTH), hb(SSD_WIDTH + SSD_GROUPS * SSD_STATE)

    y_ssd, ssd_n, cx, cb, cc = pl.pallas_call(
        _ssd_T_kernel,
        grid=(SSD_HEADS,),
        in_specs=[proj_blk(hb(P_Z)), proj_blk(hb(P_XBC)), proj_blk(hb(P_XBC) + b0, rep),
                  proj_blk(hb(P_XBC) + c0, rep), small,
                  lay4(convT), lay4(convT, b0, rep), lay4(convT, c0, rep),
                  lay4(cwb), lay4(cwb, b0, rep), lay4(cwb, c0, rep),
                  pl.BlockSpec((1, HEAD_DIM, nb), lambda h: (l, h, 0)),
                  pl.BlockSpec((1, HEAD_DIM, nb), lambda h: (l, b0 + h // rep, 0)),
                  pl.BlockSpec((1, HEAD_DIM, nb), lambda h: (l, c0 + h // rep, 0)),
                  head4(dtb), head4(alog), head4(dsk),
                  pl.BlockSpec((1, rep * HEAD_DIM, nb), lambda h: (l, h // rep, 0)), head4(ssdT)],
        out_specs=[pl.BlockSpec((n_tok, rep * HEAD_DIM, nb), lambda h: (0, h // rep, 0)), state_spec,
                   pl.BlockSpec((SSD_CONV - 1, HEAD_DIM, nb), lambda h: (0, h, 0)),
                   pl.BlockSpec((SSD_CONV - 1, HEAD_DIM, nb), lambda h: (0, h // rep, 0)),
                   pl.BlockSpec((SSD_CONV - 1, HEAD_DIM, nb), lambda h: (0, h // rep, 0))],
        out_shape=[jax.ShapeDtypeStruct((n_tok, SSD_WIDTH, nb), F32),
                   jax.ShapeDtypeStruct((SSD_HEADS, HEAD_DIM, HEAD_DIM, nb), F32),
                   jax.ShapeDtypeStruct((SSD_CONV - 1, SSD_WIDTH, nb), F32),
                   jax.ShapeDtypeStruct((SSD_CONV - 1, SSD_GROUPS * SSD_STATE, nb), F32),
                   jax.ShapeDtypeStruct((SSD_CONV - 1, SSD_GROUPS * SSD_STATE, nb), F32)],
        scratch_shapes=[vm(), vm(), vm()],
        compiler_params=_cparams(("arbitrary",)),
        name="sample_ssd",
    )(projT, projT, projT, projT, projT, convT, convT, convT, cwb, cwb, cwb, cbb, cbb, cbb,
      dtb, alog, dsk, gs, ssdT)

    y_ret, ret_n = pl.pallas_call(
        _ret_T_kernel,
        grid=(RET_HEADS,),
        in_specs=[proj_blk(hb(P_RET)), proj_blk(hb(P_RET + RET_WIDTH)), proj_blk(hb(P_RET + 2 * RET_WIDTH)),
                  proj_blk(hb(P_RET + 3 * RET_WIDTH)), _const_spec(cos), _const_spec(sin),
                  pl.BlockSpec((1, 1, nb), lambda h: (h, 0, 0)),
                  pl.BlockSpec((1, HEAD_DIM, nb), lambda h: (l, h, 0)), head4(retT)],
        out_specs=[pl.BlockSpec((n_tok, HEAD_DIM, nb), lambda h: (0, h, 0)), state_spec],
        out_shape=[jax.ShapeDtypeStruct((n_tok, RET_WIDTH, nb), F32),
                   jax.ShapeDtypeStruct((RET_HEADS, HEAD_DIM, HEAD_DIM, nb), F32)],
        scratch_shapes=[vm(), vm(), vm()],
        compiler_params=_cparams(("arbitrary",)),
        name="sample_ret",
    )(projT, projT, projT, projT, cos, sin, gam, gr, retT)

    y_ml, c_n, n_n, m_n = pl.pallas_call(
        _ml_T_kernel,
        grid=(ML_HEADS,),
        in_specs=[proj_blk(hb(P_ML)), proj_blk(hb(P_ML + ML_WIDTH)), proj_blk(hb(P_ML + 2 * ML_WIDTH)),
                  proj_blk(hb(P_ML + 3 * ML_WIDTH)), small, head4(bi), head4(bf),
                  pl.BlockSpec((1, HEAD_DIM, nb), lambda h: (l, h, 0)), head4(cT), head4(nT), head4(mT)],
        out_specs=[pl.BlockSpec((n_tok, HEAD_DIM, nb), lambda h: (0, h, 0)), state_spec,
                   pl.BlockSpec((1, HEAD_DIM, nb), lambda h: (h, 0, 0)),
                   pl.BlockSpec((1, 1, nb), lambda h: (h, 0, 0))],
        out_shape=[jax.ShapeDtypeStruct((n_tok, ML_WIDTH, nb), F32),
                   jax.ShapeDtypeStruct((ML_HEADS, HEAD_DIM, HEAD_DIM, nb), F32),
                   jax.ShapeDtypeStruct((ML_HEADS, HEAD_DIM, nb), F32),
                   jax.ShapeDtypeStruct((ML_HEADS, 1, nb), F32)],
        scratch_shapes=[vm(), vm(), vm()],
        compiler_params=_cparams(("arbitrary",)),
        name="sample_mlstm",
    )(projT, projT, projT, projT, projT, bi, bf, gm, cT, nT, mT)

    y_t = jnp.concatenate([y_ssd, y_ret, y_ml], axis=1)
    conv_n = jnp.concatenate([cx, cb, cc], axis=1)
    return y_t, (ssd_n, conv_n, ret_n, c_n, n_n, m_n)


def _route(logits):
    lane = lax.broadcasted_iota(I32, logits.shape, 1)
    gmask = (lane >= N_EXPERTS) & (lane < N_EXPERTS + EXPERT_GROUPS)
    gl = jnp.where(gmask, logits, -jnp.inf)
    ge = jnp.exp(gl - jnp.max(gl, axis=-1, keepdims=True))
    gprob = ge / jnp.sum(ge, axis=-1, keepdims=True)
    g_w = jnp.max(gprob, axis=-1, keepdims=True)
    g_idx = jnp.min(jnp.where(gmask & (gprob == g_w), lane - N_EXPERTS, LANES), axis=-1, keepdims=True)
    emask = (lane < N_EXPERTS) & ((lane >> 3) == g_idx)
    el = jnp.where(emask, logits, -jnp.inf)
    ee = jnp.exp(el - jnp.max(el, axis=-1, keepdims=True))
    eprob = ee / jnp.sum(ee, axis=-1, keepdims=True)
    p1 = jnp.max(jnp.where(emask, eprob, -1.0), axis=-1, keepdims=True)
    i1 = jnp.min(jnp.where(emask & (eprob == p1), lane, LANES), axis=-1, keepdims=True)
    rest = emask & (lane != i1)
    p2 = jnp.max(jnp.where(rest, eprob, -1.0), axis=-1, keepdims=True)
    i2 = jnp.min(jnp.where(rest & (eprob == p2), lane, LANES), axis=-1, keepdims=True)
    tot = p1 + p2
    key = g_idx * 64 + (jnp.minimum(i1, i2) & 7) * 8 + (jnp.maximum(i1, i2) & 7)
    return g_w * (p1 / tot), g_w * (p2 / tot), i1, i2, key


def _outproj_kernel(yp, ys, xp, xs, gtp, gts, scp, scs, shp, shs, g_ref, w_ref, wr_ref, br_ref,
                    x1_ref, hx_ref, info_ref, cnt_ref, carry, *, np_tiles):
    i = pl.program_id(0)
    is_p = i < np_tiles

    @pl.when(i == 0)
    def _():
        carry[...] = jnp.zeros_like(carry)

    tm = x1_ref.shape[0]
    nblk = OUTPROJ_ROW_BLOCKS
    rb = tm // nblk
    blocks = [slice(k * rb, (k + 1) * rb) for k in range(nblk)]

    def body(x_ref, y_ref, gt, sc, sh):
        mod = lambda m, b: m if m.shape[0] == 1 else m[b]
        x1s = [x_ref[b, :] + mod(gt, b) * _dot(y_ref[b, :], w_ref[0]) for b in blocks]
        for b, x1 in zip(blocks, x1s):
            x1_ref[b, :] = x1
        h2bs = [(_rms(x1) * g_ref[0] * (1.0 + mod(sc, b)) + mod(sh, b)).astype(BF16)
                for b, x1 in zip(blocks, x1s)]
        routes = [_route(_dot(h2b, wr_ref[0]) + br_ref[0]) for h2b in h2bs]

        tril = (lax.broadcasted_iota(I32, (rb, rb), 1) <= lax.broadcasted_iota(I32, (rb, rb), 0))
        tril = jnp.where(tril, 1.0, 0.0).astype(BF16)
        onehots = [lax.broadcasted_iota(I32, (rb, N_CLASS), 1) == r[4] for r in routes]
        uptos = [_dot(tril, jnp.where(oh, 1.0, 0.0).astype(BF16)) for oh in onehots]
        seen = carry[0:1, :]
        ranks = []
        for oh, upto in zip(onehots, uptos):
            ranks.append(jnp.sum(jnp.where(oh, upto - 1.0 + seen, 0.0), axis=-1, keepdims=True))
            seen = seen + upto[rb - 1:rb, :]
        carry[0:1, :] = seen
        cnt_ref[...] = jnp.broadcast_to(seen, cnt_ref.shape)

        lane = lax.broadcasted_iota(I32, (rb, LANES), 1)
        for b, h2b, (w1, w2, i1, i2, key), rank in zip(blocks, h2bs, routes, ranks):
            info = jnp.zeros((rb, LANES), F32)
            for c, v in enumerate((w1, w2, i1.astype(F32), i2.astype(F32), key.astype(F32), rank)):
                info = jnp.where(lane == c, v, info)
            hx_ref[b, 0:D_MODEL] = h2b.astype(F32)
            hx_ref[b, D_MODEL:H_EXT] = info
            info_ref[b, :] = info[:, 0:8]

    @pl.when(is_p)
    def _():
        body(xp, yp, gtp[0], scp[0], shp[0])

    @pl.when(jnp.logical_not(is_p))
    def _():
        body(xs, ys, gts[...], scs[...], shs[...])


def _outproj(y, x, gt, sc, sh, g, w, wr, br, l, tp):
    n_prompt, ns = x[0].shape[0], x[1].shape[0]
    n = n_prompt + ns
    tm = TOKEN_TILE
    npt = n_prompt // tm
    ms = _mod_specs(tm, n_prompt, tp)
    return pl.pallas_call(
        functools.partial(_outproj_kernel, np_tiles=npt),
        grid=(n // tm,),
        in_specs=_dual_specs(tm, npt, D_MODEL) + _dual_specs(tm, npt, D_MODEL) + ms + ms + ms
                 + [_layer_spec(a, l) for a in (g, w, wr, br)],
        out_specs=[_row_spec(tm, D_MODEL), _row_spec(tm, H_EXT), _row_spec(tm, 8),
                   pl.BlockSpec((8, N_CLASS), lambda i: (0, 0))],
        out_shape=[jax.ShapeDtypeStruct((n, D_MODEL), F32),
                   jax.ShapeDtypeStruct((n, H_EXT), F32),
                   jax.ShapeDtypeStruct((n, 8), F32),
                   jax.ShapeDtypeStruct((8, N_CLASS), F32)],
        scratch_shapes=[pltpu.VMEM((8, N_CLASS), F32)],
        compiler_params=_cparams(("arbitrary",)),
        name="outproj_router",
    )(*y, *x, *gt, *sc, *sh, g, w, wr, br)


def _scatter_kernel(pos_ref, x_ref, o_hbm, buf, sem):
    i, n = pl.program_id(0), pl.num_programs(0)
    tm = x_ref.shape[0]
    slot = i % 2

    def wait_slot(s):
        pltpu.make_async_copy(buf.at[s], o_hbm.at[pl.ds(0, tm), :], sem.at[s]).wait()

    @pl.when(i >= 2)
    def _():
        wait_slot(slot)

    buf[slot] = x_ref[...]

    def body(r, c):
        dst = pos_ref[i * tm + r]
        pltpu.make_async_copy(buf.at[slot, pl.ds(r, 1), :], o_hbm.at[pl.ds(dst, 1), :], sem.at[slot]).start()
        return c
    lax.fori_loop(0, tm, body, 0, unroll=8)

    @pl.when(i == n - 1)
    def _():
        wait_slot(slot)

        @pl.when(n >= 2)
        def _():
            wait_slot(1 - slot)


def _scatter_rows(x, pos):
    n, width = x.shape
    tm = MOE_TILE
    return pl.pallas_call(
        _scatter_kernel,
        grid_spec=pltpu.PrefetchScalarGridSpec(
            num_scalar_prefetch=1, grid=(n // tm,),
            in_specs=[_row_spec(tm, width)],
            out_specs=pl.BlockSpec(memory_space=pl.ANY),
            scratch_shapes=[pltpu.VMEM((2, tm, width), x.dtype), pltpu.SemaphoreType.DMA((2,))]),
        out_shape=jax.ShapeDtypeStruct((n, width), x.dtype),
        compiler_params=_cparams(("arbitrary",)),
        name="scatter_rows",
    )(pos, x)


def _gather_tiles(idx_ref, src_hbm, buf, sem, tm):
    i, n = pl.program_id(0), pl.num_programs(0)

    def issue(tile, slot):
        def body(r, c):
            row = idx_ref[tile * tm + r]
            pltpu.make_async_copy(src_hbm.at[pl.ds(row, 1), :], buf.at[slot, pl.ds(r, 1), :],
                                  sem.at[slot]).start()
            return c
        lax.fori_loop(0, tm, body, 0, unroll=8)

    @pl.when(i == 0)
    def _():
        issue(0, 0)

    @pl.when(i + 1 < n)
    def _():
        issue(i + 1, (i + 1) % 2)

    slot = i % 2
    pltpu.make_async_copy(src_hbm.at[pl.ds(0, tm), :], buf.at[slot], sem.at[slot]).wait()
    return slot


def _experts_kernel(tile_ref, grp_ref, flag_ref, hx_ref, wgu_ref, wd_ref, o_ref):
    j = pl.program_id(0)
    flags = flag_ref[j]
    valid = (flags & 1) != 0

    @pl.when(valid & ((flags & 2) != 0))
    def _():
        o_ref[...] = jnp.zeros_like(o_ref)

    @pl.when(valid)
    def _():
        h = hx_ref[:, 0:D_MODEL].astype(BF16)
        r = hx_ref[:, D_MODEL:H_EXT]
        w1, w2, i1, i2 = r[:, 0:1], r[:, 1:2], r[:, 2:3], r[:, 3:4]
        base = grp_ref[j] * EXPERTS_PER_GROUP
        for e in range(EXPERTS_PER_GROUP):
            @pl.when(((flags >> (8 + e)) & 1) != 0)
            def _():
                eid = (base + e).astype(F32)
                ge = jnp.where(i1 == eid, w1, 0.0) + jnp.where(i2 == eid, w2, 0.0)
                nblk = EXPERT_ROW_BLOCKS
                rb = h.shape[0] // nblk
                aus = [_dot(h[k * rb:(k + 1) * rb], wgu_ref[0, 0, e]) for k in range(nblk)]
                acts = [(_silu(au[:, :EXPERT_FF]) * au[:, EXPERT_FF:]).astype(BF16) for au in aus]
                yes = [_dot(act, wd_ref[0, 0, e]) for act in acts]
                for k in range(nblk):
                    o_ref[k * rb:(k + 1) * rb, :] += ge[k * rb:(k + 1) * rb] * yes[k]


def _experts(hx_sorted, tile, grp, flags, wgu, wd, l):
    n = hx_sorted.shape[0]
    tm = MOE_TILE
    return pl.pallas_call(
        _experts_kernel,
        grid_spec=pltpu.PrefetchScalarGridSpec(
            num_scalar_prefetch=3, grid=(tile.shape[0],),
            in_specs=[pl.BlockSpec((tm, H_EXT), lambda j, t, g, f: (t[j], 0)),
                      pl.BlockSpec((1, 1, EXPERTS_PER_GROUP, D_MODEL, 2 * EXPERT_FF),
                                   lambda j, t, g, f: (l, g[j], 0, 0, 0)),
                      pl.BlockSpec((1, 1, EXPERTS_PER_GROUP, EXPERT_FF, D_MODEL),
                                   lambda j, t, g, f: (l, g[j], 0, 0, 0))],
            out_specs=pl.BlockSpec((tm, D_MODEL), lambda j, t, g, f: (t[j], 0))),
        out_shape=jax.ShapeDtypeStruct((n, D_MODEL), F32),
        compiler_params=_cparams(("arbitrary",)),
        name="moe_experts",
    )(tile, grp, flags, hx_sorted, wgu, wd)


def _combine_kernel(pos_ref, y_hbm, x1_ref, gtp, gts, gf_ref, op_ref, os_ref, buf, sem, *, np_tiles, final):
    slot = _gather_tiles(pos_ref, y_hbm, buf, sem, x1_ref.shape[0])
    is_p = pl.program_id(0) < np_tiles
    x2 = x1_ref[...] + _mod_val(gtp, gts, is_p) * buf[slot]
    out = _rms(x2) * gf_ref[...] if final else x2

    @pl.when(is_p)
    def _():
        op_ref[...] = out

    @pl.when(jnp.logical_not(is_p))
    def _():
        os_ref[...] = out


def _combine(y_sorted, pos, x1, gt, gf, n_prompt, tp, final):
    n = x1.shape[0]
    tm = MOE_TILE
    npt = n_prompt // tm
    return pl.pallas_call(
        functools.partial(_combine_kernel, np_tiles=npt, final=final),
        grid_spec=pltpu.PrefetchScalarGridSpec(
            num_scalar_prefetch=1, grid=(n // tm,),
            in_specs=[pl.BlockSpec(memory_space=pl.ANY), _row_spec(tm, D_MODEL)]
                     + _mod_specs(tm, n_prompt, tp) + [_const_spec(gf)],
            out_specs=_dual_specs(tm, npt, D_MODEL),
            scratch_shapes=[pltpu.VMEM((2, tm, D_MODEL), F32), pltpu.SemaphoreType.DMA((2,))]),
        out_shape=[jax.ShapeDtypeStruct((n_prompt, D_MODEL), F32),
                   jax.ShapeDtypeStruct((n - n_prompt, D_MODEL), F32)],
        compiler_params=_cparams(("arbitrary",)),
        name="moe_combine",
    )(pos, y_sorted, x1, *gt, gf)


def _routing_tables(cnt, key, rank, n_tiles, tm):
    n_items = n_tiles + EXPERT_GROUPS - 1
    c = jnp.arange(N_CLASS, dtype=I32)
    start_c = jnp.sum(jnp.where(c[:, None] < c[None, :], cnt[:, None], 0), axis=0)
    pos = jnp.sum(jnp.where(key[:, None] == c[None, :], start_c[None, :], 0), axis=1) + rank

    g = jnp.arange(EXPERT_GROUPS, dtype=I32)
    gend = jnp.sum(jnp.where(c[None, :] < 64 * (g[:, None] + 1), cnt[None, :], 0), axis=1)
    t0 = jnp.arange(n_tiles, dtype=I32) * tm
    t1 = t0 + (tm - 1)
    gfirst = jnp.sum((gend[None, :] <= t0[:, None]).astype(I32), axis=1)
    glast = jnp.sum((gend[None, :] <= t1[:, None]).astype(I32), axis=1)
    per_tile = glast - gfirst + 1
    tt = jnp.arange(n_tiles, dtype=I32)
    start_t = jnp.sum(jnp.where(tt[:, None] < tt[None, :], per_tile[:, None], 0), axis=0)
    total = jnp.sum(per_tile)

    j = jnp.arange(n_items, dtype=I32)
    valid = j < total
    tile = jnp.sum((start_t[None, :] <= j[:, None]).astype(I32), axis=1) - 1
    tile = jnp.where(valid, tile, n_tiles - 1)
    sel = tile[:, None] == tt[None, :]
    pick = lambda v: jnp.sum(jnp.where(sel, v[None, :], 0), axis=1)
    grp = jnp.where(valid, pick(gfirst) + (j - pick(start_t)), glast[n_tiles - 1])
    first = valid & (j == pick(start_t))

    in_tile = (cnt[None, :] > 0) & (start_c[None, :] <= t1[:, None]) & ((start_c + cnt)[None, :] > t0[:, None])
    e = jnp.arange(N_EXPERTS, dtype=I32)
    member = (((c >> 6) * 8 + ((c >> 3) & 7))[:, None] == e[None, :]) | (((c >> 6) * 8 + (c & 7))[:, None] == e[None, :])
    present_t = jnp.any(in_tile[:, :, None] & member[None, :, :], axis=1)
    present_j = jnp.any(sel[:, :, None] & present_t[None, :, :], axis=1)
    eg = e[None, :] - grp[:, None] * EXPERTS_PER_GROUP
    bits = jnp.sum(jnp.where(present_j & (eg >= 0) & (eg < EXPERTS_PER_GROUP),
                             1 << (8 + jnp.clip(eg, 0, EXPERTS_PER_GROUP - 1)), 0), axis=1)
    flags = valid.astype(I32) | (first.astype(I32) << 1) | bits
    return pos.astype(I32), tile.astype(I32), grp.astype(I32), flags.astype(I32)


def _moe(hx, info, counts, x1, gt, gf, wgu, wd, l, n_prompt, tp, final):
    n = hx.shape[0]
    tm = MOE_TILE
    pos, tile, grp, flags = _routing_tables(counts[0].astype(I32), info[:, 4].astype(I32),
                                            info[:, 5].astype(I32), n // tm, tm)
    y_sorted = _experts(_scatter_rows(hx, pos), tile, grp, flags, wgu, wd, l)
    return _combine(y_sorted, pos, x1, gt, gf, n_prompt, tp, final)


def _rope_tables(pos):
    half = HEAD_DIM // 2
    inv = ROPE_BASE ** (-jnp.arange(half, dtype=F32) / half)
    ang = pos.astype(F32)[:, None] * inv[None, :]
    cos, sin = jnp.cos(ang), jnp.sin(ang)
    cos_t = jnp.tile(jnp.concatenate([cos, cos], axis=-1), (1, RET_HEADS))
    sin_t = jnp.tile(jnp.concatenate([-sin, sin], axis=-1), (1, RET_HEADS))
    return cos_t, sin_t


def _mixer_consts(w_in, conv_w, conv_b, dt_bias, a_log, d_skip, g_ssd_norm, g_ret_norm,
                  b_mlstm_i, b_mlstm_f, g_mlstm_norm):
    nl = w_in.shape[0]
    w_p = jnp.concatenate([w_in[:, :, 0:1280], w_in[:, :, 1288:3336], w_in[:, :, 1280:1288],
                           w_in[:, :, 3336:3344], jnp.zeros((nl, D_MODEL, LANES - 16), F32)], axis=2).astype(BF16)
    pad = lambda v: jnp.pad(v, ((0, 0), (0, LANES - v.shape[1])))
    log_gamma = jnp.log(1.0 - 2.0 ** (-5.0 - jnp.arange(RET_HEADS, dtype=F32)))
    lg = jnp.broadcast_to(jnp.concatenate([jnp.zeros((8,), F32), log_gamma])[None, :], (nl, 12))
    ptab = jnp.stack([pad(jnp.concatenate([dt_bias, b_mlstm_i, b_mlstm_f], axis=1)), pad(a_log), pad(lg)]
                     + [jnp.zeros((nl, LANES), F32)] * 5, axis=1)
    consts = (ptab, conv_w, conv_b[:, None, :], jnp.repeat(d_skip, HEAD_DIM, axis=1)[:, None, :],
              g_ssd_norm[:, None, :], g_ret_norm[:, None, :], g_mlstm_norm[:, None, :])
    return w_p, consts


def kernel(x_prompt, x_sample, state_ssd, state_ssd_conv, state_ret, state_mlstm_c, state_mlstm_n,
           state_mlstm_m, c_prompt, c_sample, w_ada, b_ada, g_norm1, g_norm2, w_in, conv_w, conv_b,
           dt_bias, a_log, d_skip, g_ssd_norm, g_ret_norm, b_mlstm_i, b_mlstm_f, g_mlstm_norm, w_out,
           w_router_group, b_router_group, w_router_expert, b_router_expert, w_gate_up, w_down, g_final):
    bp, tp, _ = x_prompt.shape
    bs, ts, _ = x_sample.shape
    n_srows = bs * ts
    n_prompt = bp * tp

    mod = _ada(jnp.concatenate([c_prompt, c_sample], axis=0), w_ada, b_ada)

    def mods(l):
        m = mod[l].reshape(bp + bs, 6, D_MODEL)
        return [(m[:bp, k][:, None, :], jnp.tile(m[bp:, k], (ts, 1))) for k in range(6)]

    cos_p, sin_p = _rope_tables(jnp.arange(tp, dtype=I32))
    lanes = lambda v: jnp.broadcast_to(v[..., None], v.shape + (bs,))
    half = HEAD_DIM // 2
    ang = (PAST_LEN + jnp.arange(ts, dtype=I32)).astype(F32)[:, None] * (
        ROPE_BASE ** (-jnp.arange(half, dtype=F32) / half))[None, :]
    cos_s, sin_s = lanes(jnp.cos(ang)), lanes(jnp.sin(ang))
    gamma = 1.0 - 2.0 ** (-5.0 - jnp.arange(RET_HEADS, dtype=F32))
    tabs = (lanes(dt_bias[:, :, None]), lanes(a_log[:, :, None]), lanes(d_skip[:, :, None]), lanes(g_ssd_norm),
            lanes(gamma[:, None]), lanes(g_ret_norm), lanes(b_mlstm_i[:, :, None]), lanes(b_mlstm_f[:, :, None]),
            lanes(g_mlstm_norm))
    cwb, cbb = lanes(conv_w), lanes(conv_b)
    ssd_t = jnp.transpose(state_ssd, (0, 2, 3, 4, 1))
    ret_t = jnp.transpose(state_ret, (0, 2, 3, 4, 1))
    c_t = jnp.transpose(state_mlstm_c, (0, 2, 3, 4, 1))
    n_t = jnp.transpose(state_mlstm_n, (0, 2, 3, 1))
    m_t = jnp.transpose(state_mlstm_m, (0, 2, 1))[:, :, None, :]
    conv_t = jnp.transpose(state_ssd_conv, (0, 2, 3, 1))

    w_p, consts = _mixer_consts(w_in, conv_w, conv_b, dt_bias, a_log, d_skip, g_ssd_norm, g_ret_norm,
                                b_mlstm_i, b_mlstm_f, g_mlstm_norm)
    w_o = w_out.astype(BF16)
    zpad = LANES - N_EXPERTS - EXPERT_GROUPS
    wr = jnp.concatenate([w_router_expert, w_router_group, jnp.zeros((DEPTH, D_MODEL, zpad), F32)],
                         axis=2).astype(BF16)
    br = jnp.concatenate([b_router_expert, b_router_group, jnp.zeros((DEPTH, zpad), F32)], axis=1)[:, None, :]
    wgu = w_gate_up.astype(BF16).reshape(DEPTH, EXPERT_GROUPS, EXPERTS_PER_GROUP, D_MODEL, 2 * EXPERT_FF)
    wd = w_down.astype(BF16).reshape(DEPTH, EXPERT_GROUPS, EXPERTS_PER_GROUP, EXPERT_FF, D_MODEL)
    g1, g2, gf = g_norm1[:, None, :], g_norm2[:, None, :], g_final[None, :]

    x = (x_prompt.reshape(n_prompt, D_MODEL), jnp.transpose(x_sample, (1, 0, 2)).reshape(n_srows, D_MODEL))
    p_states, s_states = [], []
    for l in range(DEPTH):
        final = l == DEPTH - 1
        sh1, sc1, gt1, sh2, sc2, gt2 = mods(l)
        proj = _inproj(x, sc1, sh1, g1, w_p, l, tp)

        ycat_p, st = _mixer_prompt(proj, bp, tp, cos_p, sin_p, consts, l)
        p_states.append(st)

        proj_t = jnp.transpose(proj[n_prompt:].reshape(ts, bs, P_W), (0, 2, 1))
        y_t, st_t = _mixer_sample_T(proj_t, conv_t, cwb, cbb, ssd_t, ret_t, c_t, n_t, m_t, tabs, cos_s, sin_s, l)
        s_states.append(st_t)
        ycat_s = jnp.transpose(y_t, (0, 2, 1)).reshape(n_srows, D_MODEL).astype(BF16)

        x1, hx, info, counts = _outproj((ycat_p, ycat_s), x, gt1, sc2, sh2, g2, w_o, wr, br, l, tp)
        x = _moe(hx, info, counts, x1, gt2, gf, wgu, wd, l, n_prompt, tp, final)

    y_prompt = x[0].reshape(bp, tp, D_MODEL)
    y_sample = jnp.transpose(x[1].reshape(ts, bs, D_MODEL), (1, 0, 2))
    p_st = [jnp.stack([s[i] for s in p_states], axis=0) for i in range(6)]
    s_t = [jnp.stack([s[i] for s in s_states], axis=0) for i in range(6)]
    s_st = [jnp.transpose(s_t[0], (0, 4, 1, 2, 3)), jnp.transpose(s_t[1], (0, 3, 1, 2)),
            jnp.transpose(s_t[2], (0, 4, 1, 2, 3)), jnp.transpose(s_t[3], (0, 4, 1, 2, 3)),
            jnp.transpose(s_t[4], (0, 3, 1, 2)), jnp.transpose(s_t[5][:, :, 0, :], (0, 2, 1))]
    return (y_prompt, y_sample, *p_st, *s_st)
```

```python
import functools
import math

import jax
import jax.numpy as jnp
from jax import lax
from jax.experimental import pallas as pl
from jax.experimental.pallas import tpu as pltpu

F32 = jnp.float32
BF16 = jnp.bfloat16
I32 = jnp.int32

D_MODEL = 1024
DEPTH = 2
PAST_LEN = 16384
SSD_HEADS = 8
SSD_WIDTH = 512
SSD_GROUPS = 2
SSD_STATE = 64
SSD_CONV = 4
SSD_XBC = 768
RET_HEADS = 4
RET_WIDTH = 256
ML_HEADS = 4
ML_WIDTH = 256
HEAD_DIM = 64
ROPE_BASE = 10000.0
EPS = 1e-6
EXPERT_GROUPS = 4
EXPERTS_PER_GROUP = 8
N_EXPERTS = 32
EXPERT_FF = 256

LANES = 128
CHUNK = 128
SEQ_PER_STEP = 4
P_Z, P_XBC, P_RET, P_ML, P_SM, P_W = 0, 512, 1280, 2304, 3328, 3456
C_SSD, C_RET, C_ML = 0, 8, 12
N_PAIRS = 8
NEG = -1e30
TOKEN_TILE = 512
MOE_TILE = 256
OUTPROJ_ROW_BLOCKS = 2
EXPERT_ROW_BLOCKS = 2
VMEM_LIMIT = 56 * 1024 * 1024
H_EXT = D_MODEL + LANES
N_CLASS = 256


def _cparams(sem):
    return pltpu.CompilerParams(dimension_semantics=sem, vmem_limit_bytes=VMEM_LIMIT)


def _split3(x):
    x1 = x.astype(BF16)
    r = x - x1.astype(F32)
    x2 = r.astype(BF16)
    r = r - x2.astype(F32)
    return x1, x2, r.astype(BF16)


def _dot01(m01, x):
    return sum(jnp.dot(m01, p, preferred_element_type=F32) for p in _split3(x))


def _dot(a, b):
    return jnp.dot(a, b, preferred_element_type=F32)


def _dot_nt(a, b):
    return lax.dot_general(a, b, (((1,), (1,)), ((), ())), preferred_element_type=F32)


def _dot_tn(a, b):
    return lax.dot_general(a, b, (((0,), (0,)), ((), ())), preferred_element_type=F32)


def _softplus(x):
    return jnp.maximum(x, 0.0) + jnp.log1p(jnp.exp(-jnp.abs(x)))


def _silu(x):
    return x * jax.nn.sigmoid(x)


def _rms(x):
    return x * lax.rsqrt(jnp.mean(x * x, axis=-1, keepdims=True) + EPS)


def _cummax_rows(x, seg):
    t = lax.broadcasted_iota(I32, x.shape, 0) & (seg - 1)
    s = 1
    while s < seg:
        x = jnp.maximum(x, jnp.where(t >= s, pltpu.roll(x, s, 0), NEG))
        s *= 2
    return x


def _rope(x, cos, sin_signed, lane):
    swapped = jnp.where((lane & 63) < 32, pltpu.roll(x, 96, 1), pltpu.roll(x, 32, 1))
    return x * cos + swapped * sin_signed


def _row_spec(tm, width):
    return pl.BlockSpec((tm, width), lambda i, *_: (i, 0))


def _const_spec(a):
    nd = a.ndim
    return pl.BlockSpec(a.shape, lambda *_: (0,) * nd)


def _layer_spec(a, l):
    nd = a.ndim
    return pl.BlockSpec((1,) + a.shape[1:], lambda *_: (l,) + (0,) * (nd - 1))


def _dual_specs(tm, np_tiles, width):
    return [pl.BlockSpec((tm, width), lambda i, *_: (jnp.minimum(i, np_tiles - 1), 0)),
            pl.BlockSpec((tm, width), lambda i, *_: (jnp.maximum(i - np_tiles, 0), 0))]


def _mod_specs(mod, k, l, tm, n_prompt, tp):
    _, full, bs = mod
    nb = full.shape[2]
    last_b = n_prompt // tp - 1
    base = (l * 6 + k) * nb + bs
    return [pl.BlockSpec((1, 1, D_MODEL), lambda i, *_: (base + jnp.minimum(i * tm // tp, last_b), 0, 0)),
            pl.BlockSpec((1, 1, bs, D_MODEL), lambda i, *_: (l, k, 0, 0))]


def _sample_mod(s_ref, tm):
    v = s_ref[0, 0]
    return jnp.tile(v, (tm // v.shape[0], 1))


def _mod_val(p_ref, s_ref, is_prompt, tm):
    return jnp.where(is_prompt, p_ref[0], _sample_mod(s_ref, tm))


def _dual_val(p_ref, s_ref, is_prompt):
    return jnp.where(is_prompt, p_ref[...], s_ref[...])


def _ada_kernel(c_ref, w_ref, b_ref, o_ref):
    c = c_ref[...]
    o_ref[0, 0] = _dot(_silu(c).astype(BF16), w_ref[0].astype(BF16)) + b_ref[0]


def _ada(c_all, w_ada, b_ada):
    nb = c_all.shape[0]
    return pl.pallas_call(
        _ada_kernel,
        grid=(DEPTH, 6),
        in_specs=[pl.BlockSpec((nb, D_MODEL), lambda l, k: (0, 0)),
                  pl.BlockSpec((1, D_MODEL, D_MODEL), lambda l, k: (l, 0, k)),
                  pl.BlockSpec((1, 1, D_MODEL), lambda l, k: (l, 0, k))],
        out_specs=pl.BlockSpec((1, 1, nb, D_MODEL), lambda l, k: (l, k, 0, 0)),
        out_shape=jax.ShapeDtypeStruct((DEPTH, 6, nb, D_MODEL), F32),
        compiler_params=_cparams(("arbitrary", "arbitrary")),
        name="ada_mod",
    )(c_all, w_ada, b_ada.reshape(DEPTH, 1, 6 * D_MODEL))


def _inproj_kernel(xp, xs, scp, scs, shp, shs, g_ref, w_ref, o_ref, *, np_tiles):
    is_p = pl.program_id(0) < np_tiles
    tm = o_ref.shape[0]
    h = (_rms(_dual_val(xp, xs, is_p)) * g_ref[0] * (1.0 + _mod_val(scp, scs, is_p, tm))
         + _mod_val(shp, shs, is_p, tm))
    o_ref[...] = _dot_nt(h.astype(BF16), w_ref[0])


def _inproj(x, mod, k_sc, k_sh, g, w, l, tp):
    n_prompt, ns = x[0].shape[0], x[1].shape[0]
    tm = TOKEN_TILE
    ms = _mod_specs(mod, k_sc, l, tm, n_prompt, tp) + _mod_specs(mod, k_sh, l, tm, n_prompt, tp)
    return pl.pallas_call(
        functools.partial(_inproj_kernel, np_tiles=n_prompt // tm),
        grid=((n_prompt + ns) // tm,),
        in_specs=_dual_specs(tm, n_prompt // tm, D_MODEL) + ms + [_layer_spec(g, l), _layer_spec(w, l)],
        out_specs=_row_spec(tm, P_W),
        out_shape=jax.ShapeDtypeStruct((n_prompt + ns, P_W), F32),
        compiler_params=_cparams(("arbitrary",)),
        name="norm_inproj",
    )(*x, mod[0], mod[1], mod[0], mod[1], g, w)


def _lane_bcast(a, c):
    return jnp.broadcast_to(a[:, c:c + 1], a.shape)


def _pair_lanes(a, c0, c1):
    lm0 = lax.broadcasted_iota(I32, (a.shape[0], LANES), 1) < HEAD_DIM
    return jnp.where(lm0, a[:, c0:c0 + 1], a[:, c1:c1 + 1])


def _ret_factors(ptab, tril, mask, last_fn):
    rows = tril.shape[0]
    lane = lax.broadcasted_iota(I32, (rows, LANES), 1)
    lm0 = lane < HEAD_DIM
    cum = _dot01(tril, jnp.where((lane >= C_RET) & (lane < C_ML), ptab[2:3], 0.0))
    xt = cum.T
    cum_last = last_fn(cum)
    out = []
    for p in range(RET_HEADS // 2):
        c0, c1 = C_RET + 2 * p, C_RET + 2 * p + 1
        b0, b1 = _lane_bcast(cum, c0), _lane_bcast(cum, c1)
        bp = jnp.where(lm0, b0, b1)
        wp = jnp.exp(_pair_lanes(cum_last, c0, c1) - bp)
        out.append((jnp.exp(jnp.where(mask, b0 - xt[c0:c0 + 1, :], -jnp.inf)),
                    jnp.exp(jnp.where(mask, b1 - xt[c1:c1 + 1, :], -jnp.inf)), jnp.exp(bp), wp))
    return out


def _mixer_core(z, us, retb, mlb, small, cos, sin, ptab, cw, cb, dsk, gs, gr, gm,
                mask, tril, seg, mprev, last_fn, st, y_ref, ret_factors):
    rows = small.shape[0]
    lane = lax.broadcasted_iota(I32, (rows, LANES), 1)
    lm0 = lane < HEAD_DIM

    pre = small + ptab[0:1]
    a_neg = -jnp.exp(ptab[1:2])
    dt = _softplus(pre)
    logf = -_softplus(-pre)
    la = jnp.where(lane < C_RET, dt * a_neg,
                   jnp.where(lane < C_ML, ptab[2:3], jnp.where(lane < C_ML + 4, logf, 0.0)))
    cum = _dot01(tril, la)
    ic = pltpu.roll(pre, 4, 1)
    mlm = (lane >= C_ML) & (lane < C_ML + 4)
    d = jnp.where(mlm, ic - cum, NEG)
    m_t = cum + jnp.maximum(mprev, _cummax_rows(d, seg))
    xt = jnp.where(mlm, d, cum).T
    colv = cum - m_t
    cum_last, m_last = last_fn(cum), last_fn(m_t)
    decq = jnp.where(mlm, jnp.exp(cum_last + mprev - m_last), jnp.exp(cum_last))
    yield

    def factors(c0, c1, kind):
        if kind == "ret":
            return ret_factors((c0 - C_RET) // 2)
        if kind == "ssd":
            b0, b1 = _lane_bcast(cum, c0), _lane_bcast(cum, c1)
            bp = jnp.where(lm0, b0, b1)
            wp = jnp.exp(_pair_lanes(cum_last, c0, c1) - bp)
            return (jnp.exp(jnp.where(mask, b0 - xt[c0:c0 + 1, :], -jnp.inf)),
                    jnp.exp(jnp.where(mask, b1 - xt[c1:c1 + 1, :], -jnp.inf)), jnp.exp(bp), wp)
        a0, a1 = _lane_bcast(colv, c0), _lane_bcast(colv, c1)
        eqp = jnp.exp(jnp.where(lm0, a0, a1) + _pair_lanes(mprev, c0, c1))
        wp = jnp.exp(jnp.where(lm0, _lane_bcast(d, c0), _lane_bcast(d, c1))
                     + _pair_lanes(cum_last, c0, c1) - _pair_lanes(m_last, c0, c1))
        return (jnp.exp(jnp.where(mask, a0 + xt[c0:c0 + 1, :], -jnp.inf)),
                jnp.exp(jnp.where(mask, a1 + xt[c1:c1 + 1, :], -jnp.inf)), eqp, wp)

    def pair(idx, inputs, c0, c1, kind, finish):
        qp, kp, vp = inputs()
        q0 = jnp.where(lm0, qp, 0.0)
        q1 = jnp.where(lm0, 0.0, qp)
        sc = _dot_nt(jnp.concatenate([q0, q1], axis=0).astype(BF16), kp.astype(BF16))
        yield
        d0, d1, eqp, wp = factors(c0, c1, kind)
        v01 = jnp.concatenate([jnp.where(lm0, vp, 0.0), jnp.where(lm0, 0.0, vp)], axis=0).astype(BF16)
        kw = kp * wp
        yield
        carried, qn = st.step(idx, qp, eqp, kw, vp, decq, c0, c1, kind == "ml")
        yield
        s0, s1 = sc[0:rows] * d0, sc[rows:2 * rows] * d1
        intra = _dot(jnp.concatenate([s0, s1], axis=1).astype(BF16), v01)
        yield
        finish(intra + carried, s0, s1, qn)

    def head_norm(o):
        o2 = o * o
        ms0 = jnp.sum(jnp.where(lm0, o2, 0.0), axis=-1, keepdims=True) * (1.0 / HEAD_DIM)
        ms1 = jnp.sum(jnp.where(lm0, 0.0, o2), axis=-1, keepdims=True) * (1.0 / HEAD_DIM)
        return o * jnp.where(lm0, lax.rsqrt(ms0 + EPS), lax.rsqrt(ms1 + EPS))

    conv = cb + us[0] * cw[0:1] + us[1] * cw[1:2] + us[2] * cw[2:3] + us[3] * cw[3:4]
    xc = _silu(conv)
    bb = xc[:, SSD_WIDTH:SSD_WIDTH + LANES]
    cc = xc[:, SSD_WIDTH + LANES:SSD_WIDTH + 2 * LANES]
    br = pltpu.roll(bb, HEAD_DIM, 1)
    cr = pltpu.roll(cc, HEAD_DIM, 1)
    ys = [None] * 4
    pairs = []
    for p in range(4):
        c0, c1 = C_SSD + 2 * p, C_SSD + 2 * p + 1
        sl = slice(LANES * p, LANES * (p + 1))

        def ssd_inputs(p=p, c0=c0, c1=c1, sl=sl):
            if p < 2:
                kp, qp = jnp.where(lm0, bb, br), jnp.where(lm0, cc, cr)
            else:
                kp, qp = jnp.where(lm0, br, bb), jnp.where(lm0, cr, cc)
            dtp = jnp.where(lm0, dt[:, c0:c0 + 1], dt[:, c1:c1 + 1])
            return qp, kp, xc[:, sl] * dtp

        def ssd_finish(o, s0, s1, qn, p=p, sl=sl):
            ys[p] = (o + dsk[:, sl] * xc[:, sl]) * _silu(z[:, sl])

        pairs.append(pair(p, ssd_inputs, c0, c1, "ssd", ssd_finish))

    for p in range(2):
        c0, c1 = C_RET + 2 * p, C_RET + 2 * p + 1
        sl = slice(LANES * p, LANES * (p + 1))

        def ret_inputs(p=p, sl=sl):
            qp = _rope(retb[:, LANES * p:LANES * (p + 1)], cos[:, sl], sin[:, sl], lane)
            kp = _rope(retb[:, RET_WIDTH + LANES * p:RET_WIDTH + LANES * (p + 1)], cos[:, sl], sin[:, sl], lane)
            return qp, kp * (HEAD_DIM ** -0.5), retb[:, 2 * RET_WIDTH + LANES * p:2 * RET_WIDTH + LANES * (p + 1)]

        def ret_finish(o, s0, s1, qn, p=p, sl=sl):
            gp = retb[:, 3 * RET_WIDTH + LANES * p:3 * RET_WIDTH + LANES * (p + 1)]
            y = head_norm(o) * gr[:, sl] * _silu(gp)
            y_ref[:, SSD_WIDTH + LANES * p:SSD_WIDTH + LANES * (p + 1)] = y.astype(y_ref.dtype)

        pairs.append(pair(4 + p, ret_inputs, c0, c1, "ret", ret_finish))

    for p in range(2):
        c0, c1 = C_ML + 2 * p, C_ML + 2 * p + 1
        sl = slice(LANES * p, LANES * (p + 1))

        def ml_inputs(p=p):
            return (mlb[:, LANES * p:LANES * (p + 1)],
                    mlb[:, ML_WIDTH + LANES * p:ML_WIDTH + LANES * (p + 1)] * (HEAD_DIM ** -0.5),
                    mlb[:, 2 * ML_WIDTH + LANES * p:2 * ML_WIDTH + LANES * (p + 1)])

        def ml_finish(num, s0, s1, qn, p=p, c0=c0, c1=c1, sl=sl):
            op = mlb[:, 3 * ML_WIDTH + LANES * p:3 * ML_WIDTH + LANES * (p + 1)]
            inter0 = jnp.exp(colv[:, c0:c0 + 1] + mprev[:, c0:c0 + 1])
            inter1 = jnp.exp(colv[:, c1:c1 + 1] + mprev[:, c1:c1 + 1])
            den0 = jnp.sum(s0, axis=-1, keepdims=True) + qn[0] * inter0
            den1 = jnp.sum(s1, axis=-1, keepdims=True) + qn[1] * inter1
            dn0 = jnp.maximum(jnp.abs(den0), jnp.exp(-m_t[:, c0:c0 + 1]))
            dn1 = jnp.maximum(jnp.abs(den1), jnp.exp(-m_t[:, c1:c1 + 1]))
            hh = num / jnp.where(lm0, dn0, dn1)
            y = head_norm(hh) * gm[:, sl] * jax.nn.sigmoid(op)
            off = SSD_WIDTH + RET_WIDTH + LANES * p
            y_ref[:, off:off + LANES] = y.astype(y_ref.dtype)

        pairs.append(pair(6 + p, ml_inputs, c0, c1, "ml", ml_finish))

    live = list(pairs)
    while live:
        for g in list(live):
            try:
                next(g)
            except StopIteration:
                live.remove(g)
        yield

    for g in range(SSD_GROUPS):
        ya, yb = ys[2 * g], ys[2 * g + 1]
        ms = (jnp.sum(ya * ya, axis=-1, keepdims=True)
              + jnp.sum(yb * yb, axis=-1, keepdims=True)) * (1.0 / (2 * LANES))
        r = lax.rsqrt(ms + EPS)
        for j, yv in ((2 * g, ya), (2 * g + 1, yb)):
            sl = slice(LANES * j, LANES * (j + 1))
            y_ref[:, sl] = (yv * r * gs[:, sl]).astype(y_ref.dtype)
    return m_t


def _run_interleaved(gens):
    out = [None] * len(gens)
    live = list(range(len(gens)))
    while live:
        for k in list(live):
            try:
                next(gens[k])
            except StopIteration as stop:
                out[k] = stop.value
                live.remove(k)
    return out


def _half_rows():
    return lax.broadcasted_iota(I32, (LANES, HEAD_DIM), 0) < HEAD_DIM


class _CarriedState:
    def __init__(self, sv, nrow):
        self.sv, self.nrow = sv, nrow

    def step(self, idx, qp, eqp, kw, vp, decq, c0, c1, ml):
        lm0 = lax.broadcasted_iota(I32, (1, LANES), 1) < HEAD_DIM
        drow = _pair_lanes(decq[0:1, :], c0, c1)
        s_old = self.sv[idx]
        carried = _dot(qp.astype(BF16), s_old.astype(BF16)) * eqp
        u = _dot_tn(kw.astype(BF16), vp.astype(BF16))
        same_head = ((lax.broadcasted_iota(I32, (LANES, LANES), 0) < HEAD_DIM)
                     == (lax.broadcasted_iota(I32, (LANES, LANES), 1) < HEAD_DIM))
        self.sv[idx] = s_old * drow + jnp.where(same_head, u, 0.0)
        qn = None
        if ml:
            p = idx - 6
            n_old = self.nrow[p:p + 1, :]
            qn_l = qp * n_old
            qn = (jnp.sum(jnp.where(lm0, qn_l, 0.0), axis=-1, keepdims=True),
                  jnp.sum(jnp.where(lm0, 0.0, qn_l), axis=-1, keepdims=True))
            self.nrow[p:p + 1, :] = n_old * drow + jnp.sum(kw, axis=0, keepdims=True)
        return carried, qn


def _mixer_prompt_kernel(*refs):
    sq = SEQ_PER_STEP
    proj_refs = refs[:sq]
    (cos_ref, sin_ref, ptab_ref, cw_ref, cb_ref, dsk_ref, gs_ref, gr_ref, gm_ref,
     y_ref, sv_o, conv_o, n_o, m_o) = refs[sq:sq + 14]
    scr = refs[sq + 14:]
    sv, nrow, mrow, cbuf = scr[0:sq], scr[sq:2 * sq], scr[2 * sq:3 * sq], scr[3 * sq:4 * sq]
    ret_cache = scr[4 * sq]
    ci = pl.program_id(1)
    rows = proj_refs[0].shape[1]
    ri = lax.broadcasted_iota(I32, (rows, rows), 0)
    cj = lax.broadcasted_iota(I32, (rows, rows), 1)
    mask = cj <= ri
    tril = jnp.where(mask, 1.0, 0.0).astype(BF16)
    last_row = lambda a: a[rows - 1:rows, :]

    @pl.when(ci == 0)
    def _():
        for s in range(sq):
            sv[s][...] = jnp.zeros_like(sv[s])
            nrow[s][...] = jnp.zeros_like(nrow[s])
            mrow[s][...] = jnp.zeros_like(mrow[s])
            cbuf[s][0:8, :] = jnp.zeros((8, SSD_XBC), F32)
        for p, fs in enumerate(_ret_factors(ptab_ref[0], tril, mask, last_row)):
            for k, f in enumerate(fs):
                ret_cache[4 * p + k] = f

    ret_factors = lambda p: tuple(ret_cache[4 * p + k] for k in range(4))
    gens = []
    for s in range(sq):
        proj_ref = proj_refs[s]
        cbuf[s][8:8 + rows, :] = proj_ref[0, :, P_XBC:P_RET]
        us = [cbuf[s][pl.ds(5 + k, rows), :] for k in range(SSD_CONV)]
        gens.append(_mixer_core(
            proj_ref[0, :, P_Z:P_XBC], us, proj_ref[0, :, P_RET:P_ML], proj_ref[0, :, P_ML:P_SM],
            proj_ref[0, :, P_SM:P_W], cos_ref[...], sin_ref[...], ptab_ref[0], cw_ref[0], cb_ref[0],
            dsk_ref[0], gs_ref[0], gr_ref[0], gm_ref[0],
            mask, tril, rows, mrow[s][0:1, :], last_row,
            _CarriedState(sv[s], nrow[s]), y_ref.at[s], ret_factors))
    for s, m_t in enumerate(_run_interleaved(gens)):
        mrow[s][0:1, :] = m_t[rows - 1:rows, :]
        cbuf[s][0:8, :] = cbuf[s][rows:rows + 8, :]

    @pl.when(ci == pl.num_programs(1) - 1)
    def _():
        top = _half_rows()
        for s in range(sq):
            for idx in range(N_PAIRS):
                bd = sv[s][idx]
                sv_o[s, idx] = jnp.where(top, bd[:, :HEAD_DIM], bd[:, HEAD_DIM:])
            conv_o[s] = cbuf[s][0:8, :]
            n_o[s] = nrow[s][...]
            m_o[s] = mrow[s][...]


def _mixer_prompt(proj, nb, t, cos, sin, consts, l):
    rows = math.gcd(t, CHUNK)
    nc = t // rows
    sq = SEQ_PER_STEP
    assert nb % sq == 0

    def proj_spec(s):
        return pl.BlockSpec((1, rows, P_W), lambda b, c: (0, (sq * b + s) * nc + c, 0))

    def seq_spec(*tail):
        return pl.BlockSpec((sq,) + tail, lambda b, c: (b,) + (0,) * len(tail))

    outs = pl.pallas_call(
        _mixer_prompt_kernel,
        grid=(nb // sq, nc),
        in_specs=[proj_spec(s) for s in range(sq)]
                 + [pl.BlockSpec((rows, RET_WIDTH), lambda b, c: (c, 0)),
                    pl.BlockSpec((rows, RET_WIDTH), lambda b, c: (c, 0))] + [_layer_spec(a, l) for a in consts],
        out_specs=[pl.BlockSpec((sq, rows, D_MODEL), lambda b, c: (b, c, 0)),
                   seq_spec(N_PAIRS, LANES, HEAD_DIM), seq_spec(8, SSD_XBC), seq_spec(8, LANES), seq_spec(8, LANES)],
        out_shape=[jax.ShapeDtypeStruct((nb, t, D_MODEL), BF16),
                   jax.ShapeDtypeStruct((nb, N_PAIRS, LANES, HEAD_DIM), F32),
                   jax.ShapeDtypeStruct((nb, 8, SSD_XBC), F32),
                   jax.ShapeDtypeStruct((nb, 8, LANES), F32),
                   jax.ShapeDtypeStruct((nb, 8, LANES), F32)],
        scratch_shapes=[pltpu.VMEM((N_PAIRS, LANES, LANES), F32) for _ in range(sq)]
                       + [pltpu.VMEM((8, LANES), F32) for _ in range(2 * sq)]
                       + [pltpu.VMEM((rows + 8, SSD_XBC), F32) for _ in range(sq)]
                       + [pltpu.VMEM((2 * RET_HEADS, rows, LANES), F32)],
        compiler_params=_cparams(("arbitrary", "arbitrary")),
        name="mixer_prompt",
    )(*([proj[None]] * sq), cos, sin, *consts)
    y, sv, conv, n, m = outs
    sv = sv.reshape(nb, 2 * N_PAIRS, HEAD_DIM, HEAD_DIM)
    states = (sv[:, :8], conv[:, 5:8], sv[:, 8:12], sv[:, 12:16],
              n[:, 0:2].reshape(nb, ML_HEADS, HEAD_DIM), m[:, 0, C_ML:C_ML + 4])
    return y.reshape(nb * t, D_MODEL), states


def _lane_recurrence(s_in, s_out, q_s, k_s, v_s, decay):
    n_tok = len(decay)
    nvb = HEAD_DIM // 8
    nb = q_s.shape[-1]
    dec8 = [jnp.broadcast_to(d, (8, nb)) for d in decay]

    def body(k, acc):
        acc = [list(a) for a in acc]
        qk = [q_s[t, pl.ds(k, 1), :] for t in range(n_tok)]
        kk = [k_s[t, pl.ds(k, 1), :] for t in range(n_tok)]
        for vb in range(nvb):
            rows = pl.ds(8 * vb, 8)
            s = s_in[k, rows, :]
            for t in range(n_tok):
                s = s * dec8[t] + kk[t] * v_s[t, rows, :]
                acc[t][vb] = acc[t][vb] + qk[t] * s
            s_out[k, rows, :] = s
        return tuple(tuple(a) for a in acc)

    init = tuple(tuple(jnp.zeros((8, nb), F32) for _ in range(nvb)) for _ in range(n_tok))
    acc = lax.fori_loop(0, HEAD_DIM, body, init, unroll=2)
    return [jnp.concatenate(list(a), axis=0) for a in acc]


def _conv_T(u_ref, tail_ref, w_ref, b_ref, state_ref):
    n_tok = u_ref.shape[0]
    full = [tail_ref[0, j] for j in range(SSD_CONV - 1)] + [u_ref[t] for t in range(n_tok)]
    outs = []
    for t in range(n_tok):
        acc = b_ref[0]
        for tap in range(SSD_CONV):
            acc = acc + full[t + tap] * w_ref[0, tap]
        outs.append(_silu(acc))
    for j in range(SSD_CONV - 1):
        state_ref[j] = full[n_tok + j]
    return outs


def _ssd_T_kernel(z_ref, xs_ref, b_ref, c_ref, sm_ref, tx_ref, tb_ref, tc_ref, wx_ref, wb_ref, wc_ref,
                  bx_ref, bb_ref, bc_ref, dtb_ref, alog_ref, dsk_ref, g_ref, s_ref,
                  y_ref, so_ref, cx_ref, cb_ref, cc_ref, q_s, k_s, v_s):
    h = pl.program_id(0)
    n_tok = xs_ref.shape[0]
    xs = _conv_T(xs_ref, tx_ref, wx_ref, bx_ref, cx_ref)
    bm = _conv_T(b_ref, tb_ref, wb_ref, bb_ref, cb_ref)
    cm = _conv_T(c_ref, tc_ref, wc_ref, bc_ref, cc_ref)
    a_neg = -jnp.exp(alog_ref[0, 0])
    decay = []
    for t in range(n_tok):
        dt = _softplus(sm_ref[t, pl.ds(C_SSD + h, 1), :] + dtb_ref[0, 0])
        decay.append(jnp.exp(dt * a_neg))
        q_s[t] = cm[t]
        k_s[t] = bm[t]
        v_s[t] = xs[t] * dt
    o = _lane_recurrence(s_ref.at[0, 0], so_ref.at[0], q_s, k_s, v_s, decay)
    hh = h % (SSD_HEADS // SSD_GROUPS)
    row0 = pl.multiple_of(hh * HEAD_DIM, HEAD_DIM)
    for t in range(n_tok):
        y_ref[t, pl.ds(row0, HEAD_DIM), :] = (o[t] + dsk_ref[0, 0] * xs[t]) * _silu(z_ref[t])

    @pl.when(hh == SSD_HEADS // SSD_GROUPS - 1)
    def _():
        for t in range(n_tok):
            blk = y_ref[t]
            ms = jnp.mean(blk * blk, axis=0, keepdims=True)
            y_ref[t] = blk * lax.rsqrt(ms + EPS) * g_ref[0]


def _ret_T_kernel(q_ref, k_ref, v_ref, g_ref, cos_ref, sin_ref, gam_ref, gn_ref, s_ref,
                  y_ref, so_ref, q_s, k_s, v_s):
    n_tok = q_ref.shape[0]
    half = HEAD_DIM // 2

    def rope(x, t):
        x1, x2 = x[:half], x[half:]
        return jnp.concatenate([x1 * cos_ref[t] - x2 * sin_ref[t], x1 * sin_ref[t] + x2 * cos_ref[t]], axis=0)

    for t in range(n_tok):
        q_s[t] = rope(q_ref[t], t)
        k_s[t] = rope(k_ref[t], t) * (HEAD_DIM ** -0.5)
        v_s[t] = v_ref[t]
    o = _lane_recurrence(s_ref.at[0, 0], so_ref.at[0], q_s, k_s, v_s, [gam_ref[0]] * n_tok)
    for t in range(n_tok):
        ms = jnp.mean(o[t] * o[t], axis=0, keepdims=True)
        y_ref[t] = o[t] * lax.rsqrt(ms + EPS) * gn_ref[0] * _silu(g_ref[t])


def _ml_T_kernel(q_ref, k_ref, v_ref, o_ref, sm_ref, bi_ref, bf_ref, gn_ref, c_ref, n_ref, m_ref,
                 y_ref, co_ref, no_ref, mo_ref, q_s, k_s, v_s):
    h = pl.program_id(0)
    n_tok = q_ref.shape[0]
    m = m_ref[0, 0]
    n = n_ref[0, 0]
    decay, qn, m_all = [], [], []
    for t in range(n_tok):
        i_t = sm_ref[t, pl.ds(C_RET + h, 1), :] + bi_ref[0, 0]
        logf = -_softplus(-(sm_ref[t, pl.ds(C_ML + h, 1), :] + bf_ref[0, 0]))
        m_new = jnp.maximum(logf + m, i_t)
        f_t = jnp.exp(logf + m - m_new)
        kw = k_ref[t] * (HEAD_DIM ** -0.5) * jnp.exp(i_t - m_new)
        n = n * f_t + kw
        q_s[t] = q_ref[t]
        k_s[t] = kw
        v_s[t] = v_ref[t]
        decay.append(f_t)
        qn.append(jnp.sum(q_ref[t] * n, axis=0, keepdims=True))
        m_all.append(m_new)
        m = m_new
    num = _lane_recurrence(c_ref.at[0, 0], co_ref.at[0], q_s, k_s, v_s, decay)
    no_ref[0] = n
    mo_ref[0] = m
    for t in range(n_tok):
        hh = num[t] / jnp.maximum(jnp.abs(qn[t]), jnp.exp(-m_all[t]))
        ms = jnp.mean(hh * hh, axis=0, keepdims=True)
        y_ref[t] = hh * lax.rsqrt(ms + EPS) * gn_ref[0] * jax.nn.sigmoid(o_ref[t])


def _mixer_sample_T(projT, convT, cwb, cbb, ssdT, retT, cT, nT, mT, tabs, cos, sin, l):
    n_tok, _, nb = projT.shape
    dtb, alog, dsk, gs, gam, gr, bi, bf, gm = tabs
    hb = lambda col: col // HEAD_DIM
    rep = SSD_HEADS // SSD_GROUPS
    proj_blk = lambda first, div=1: pl.BlockSpec((n_tok, HEAD_DIM, nb), lambda h: (0, first + h // div, 0))
    small = pl.BlockSpec((n_tok, LANES, nb), lambda h: (0, P_SM // LANES, 0))
    lay4 = lambda a, first=0, div=1: pl.BlockSpec((1,) + a.shape[1:2] + (HEAD_DIM, a.shape[-1]),
                                                  lambda h: (l, 0, first + h // div, 0))
    chan = lambda a, first=0, div=1, n=1: pl.BlockSpec((1, n * HEAD_DIM, 1), lambda h: (l, first + h // div, 0))
    head4 = lambda a: pl.BlockSpec((1, 1) + a.shape[2:], lambda h: (l, h) + (0,) * (a.ndim - 2))
    vm = lambda: pltpu.VMEM((n_tok, HEAD_DIM, nb), F32)
    state_spec = pl.BlockSpec((1, HEAD_DIM, HEAD_DIM, nb), lambda h: (h, 0, 0, 0))
    b0, c0 = hb(SSD_WIDTH), hb(SSD_WIDTH + SSD_GROUPS * SSD_STATE)

    y_ssd, ssd_n, cx, cb, cc = pl.pallas_call(
        _ssd_T_kernel,
        grid=(SSD_HEADS,),
        in_specs=[proj_blk(hb(P_Z)), proj_blk(hb(P_XBC)), proj_blk(hb(P_XBC) + b0, rep),
                  proj_blk(hb(P_XBC) + c0, rep), small,
                  lay4(convT), lay4(convT, b0, rep), lay4(convT, c0, rep),
                  lay4(cwb), lay4(cwb, b0, rep), lay4(cwb, c0, rep),
                  chan(cbb), chan(cbb, b0, rep), chan(cbb, c0, rep),
                  head4(dtb), head4(alog), head4(dsk), chan(gs, 0, rep, rep), head4(ssdT)],
        out_specs=[pl.BlockSpec((n_tok, rep * HEAD_DIM, nb), lambda h: (0, h // rep, 0)), state_spec,
                   pl.BlockSpec((SSD_CONV - 1, HEAD_DIM, nb), lambda h: (0, h, 0)),
                   pl.BlockSpec((SSD_CONV - 1, HEAD_DIM, nb), lambda h: (0, h // rep, 0)),
                   pl.BlockSpec((SSD_CONV - 1, HEAD_DIM, nb), lambda h: (0, h // rep, 0))],
        out_shape=[jax.ShapeDtypeStruct((n_tok, SSD_WIDTH, nb), F32),
                   jax.ShapeDtypeStruct((SSD_HEADS, HEAD_DIM, HEAD_DIM, nb), F32),
                   jax.ShapeDtypeStruct((SSD_CONV - 1, SSD_WIDTH, nb), F32),
                   jax.ShapeDtypeStruct((SSD_CONV - 1, SSD_GROUPS * SSD_STATE, nb), F32),
                   jax.ShapeDtypeStruct((SSD_CONV - 1, SSD_GROUPS * SSD_STATE, nb), F32)],
        scratch_shapes=[vm(), vm(), vm()],
        compiler_params=_cparams(("arbitrary",)),
        name="sample_ssd",
    )(projT, projT, projT, projT, projT, convT, convT, convT, cwb, cwb, cwb, cbb, cbb, cbb,
      dtb, alog, dsk, gs, ssdT)

    y_ret, ret_n = pl.pallas_call(
        _ret_T_kernel,
        grid=(RET_HEADS,),
        in_specs=[proj_blk(hb(P_RET)), proj_blk(hb(P_RET + RET_WIDTH)), proj_blk(hb(P_RET + 2 * RET_WIDTH)),
                  proj_blk(hb(P_RET + 3 * RET_WIDTH)), _const_spec(cos), _const_spec(sin),
                  pl.BlockSpec((1, 1, 1), lambda h: (h, 0, 0)), chan(gr), head4(retT)],
        out_specs=[pl.BlockSpec((n_tok, HEAD_DIM, nb), lambda h: (0, h, 0)), state_spec],
        out_shape=[jax.ShapeDtypeStruct((n_tok, RET_WIDTH, nb), F32),
                   jax.ShapeDtypeStruct((RET_HEADS, HEAD_DIM, HEAD_DIM, nb), F32)],
        scratch_shapes=[vm(), vm(), vm()],
        compiler_params=_cparams(("arbitrary",)),
        name="sample_ret",
    )(projT, projT, projT, projT, cos, sin, gam, gr, retT)

    y_ml, c_n, n_n, m_n = pl.pallas_call(
        _ml_T_kernel,
        grid=(ML_HEADS,),
        in_specs=[proj_blk(hb(P_ML)), proj_blk(hb(P_ML + ML_WIDTH)), proj_blk(hb(P_ML + 2 * ML_WIDTH)),
                  proj_blk(hb(P_ML + 3 * ML_WIDTH)), small, head4(bi), head4(bf),
                  chan(gm), head4(cT), head4(nT), head4(mT)],
        out_specs=[pl.BlockSpec((n_tok, HEAD_DIM, nb), lambda h: (0, h, 0)), state_spec,
                   pl.BlockSpec((1, HEAD_DIM, nb), lambda h: (h, 0, 0)),
                   pl.BlockSpec((1, 1, nb), lambda h: (h, 0, 0))],
        out_shape=[jax.ShapeDtypeStruct((n_tok, ML_WIDTH, nb), F32),
                   jax.ShapeDtypeStruct((ML_HEADS, HEAD_DIM, HEAD_DIM, nb), F32),
                   jax.ShapeDtypeStruct((ML_HEADS, HEAD_DIM, nb), F32),
                   jax.ShapeDtypeStruct((ML_HEADS, 1, nb), F32)],
        scratch_shapes=[vm(), vm(), vm()],
        compiler_params=_cparams(("arbitrary",)),
        name="sample_mlstm",
    )(projT, projT, projT, projT, projT, bi, bf, gm, cT, nT, mT)

    y_t = jnp.concatenate([y_ssd, y_ret, y_ml], axis=1)
    conv_n = jnp.concatenate([cx, cb, cc], axis=1)
    return y_t, (ssd_n, conv_n, ret_n, c_n, n_n, m_n)


def _route(logits):
    lane = lax.broadcasted_iota(I32, logits.shape, 1)
    gmask = (lane >= N_EXPERTS) & (lane < N_EXPERTS + EXPERT_GROUPS)
    gl = jnp.where(gmask, logits, -jnp.inf)
    ge = jnp.exp(gl - jnp.max(gl, axis=-1, keepdims=True))
    gprob = ge / jnp.sum(ge, axis=-1, keepdims=True)
    g_w = jnp.max(gprob, axis=-1, keepdims=True)
    g_idx = jnp.min(jnp.where(gmask & (gprob == g_w), lane - N_EXPERTS, LANES), axis=-1, keepdims=True)
    emask = (lane < N_EXPERTS) & ((lane >> 3) == g_idx)
    el = jnp.where(emask, logits, -jnp.inf)
    ee = jnp.exp(el - jnp.max(el, axis=-1, keepdims=True))
    eprob = ee / jnp.sum(ee, axis=-1, keepdims=True)
    p1 = jnp.max(jnp.where(emask, eprob, -1.0), axis=-1, keepdims=True)
    i1 = jnp.min(jnp.where(emask & (eprob == p1), lane, LANES), axis=-1, keepdims=True)
    rest = emask & (lane != i1)
    p2 = jnp.max(jnp.where(rest, eprob, -1.0), axis=-1, keepdims=True)
    i2 = jnp.min(jnp.where(rest & (eprob == p2), lane, LANES), axis=-1, keepdims=True)
    tot = p1 + p2
    key = g_idx * 64 + (jnp.minimum(i1, i2) & 7) * 8 + (jnp.maximum(i1, i2) & 7)
    return g_w * (p1 / tot), g_w * (p2 / tot), i1, i2, key


def _outproj_kernel(yp, ys, xp, xs, gtp, gts, scp, scs, shp, shs, g_ref, w_ref, wr_ref, br_ref,
                    x1_ref, hx_ref, info_ref, cnt_ref, carry, *, np_tiles):
    i = pl.program_id(0)
    is_p = i < np_tiles

    @pl.when(i == 0)
    def _():
        carry[...] = jnp.zeros_like(carry)

    tm = x1_ref.shape[0]
    nblk = OUTPROJ_ROW_BLOCKS
    rb = tm // nblk
    blocks = [slice(k * rb, (k + 1) * rb) for k in range(nblk)]

    def body(x_ref, y_ref, gt, sc, sh):
        mod = lambda m, b: m if m.shape[0] == 1 else m[b]
        x1s = [x_ref[b, :] + mod(gt, b) * _dot(y_ref[b, :], w_ref[0]) for b in blocks]
        for b, x1 in zip(blocks, x1s):
            x1_ref[b, :] = x1
        h2bs = [(_rms(x1) * g_ref[0] * (1.0 + mod(sc, b)) + mod(sh, b)).astype(BF16)
                for b, x1 in zip(blocks, x1s)]
        routes = [_route(_dot(h2b, wr_ref[0]) + br_ref[0]) for h2b in h2bs]

        tril = (lax.broadcasted_iota(I32, (rb, rb), 1) <= lax.broadcasted_iota(I32, (rb, rb), 0))
        tril = jnp.where(tril, 1.0, 0.0).astype(BF16)
        onehots = [lax.broadcasted_iota(I32, (rb, N_CLASS), 1) == r[4] for r in routes]
        uptos = [_dot(tril, jnp.where(oh, 1.0, 0.0).astype(BF16)) for oh in onehots]
        seen = carry[0:1, :]
        ranks = []
        for oh, upto in zip(onehots, uptos):
            ranks.append(jnp.sum(jnp.where(oh, upto - 1.0 + seen, 0.0), axis=-1, keepdims=True))
            seen = seen + upto[rb - 1:rb, :]
        carry[0:1, :] = seen
        cnt_ref[...] = jnp.broadcast_to(seen, cnt_ref.shape)

        lane = lax.broadcasted_iota(I32, (rb, LANES), 1)
        for b, h2b, (w1, w2, i1, i2, key), rank in zip(blocks, h2bs, routes, ranks):
            info = jnp.zeros((rb, LANES), F32)
            for c, v in enumerate((w1, w2, i1.astype(F32), i2.astype(F32), key.astype(F32), rank)):
                info = jnp.where(lane == c, v, info)
            hx_ref[b, 0:D_MODEL] = h2b.astype(F32)
            hx_ref[b, D_MODEL:H_EXT] = info
            info_ref[b, :] = info[:, 0:8]

    @pl.when(is_p)
    def _():
        body(xp, yp, gtp[0], scp[0], shp[0])

    @pl.when(jnp.logical_not(is_p))
    def _():
        body(xs, ys, _sample_mod(gts, tm), _sample_mod(scs, tm), _sample_mod(shs, tm))


def _outproj(y, x, mod, k_gt, k_sc, k_sh, g, w, wr, br, l, tp):
    n_prompt, ns = x[0].shape[0], x[1].shape[0]
    n = n_prompt + ns
    tm = TOKEN_TILE
    npt = n_prompt // tm
    ms = [s for k in (k_gt, k_sc, k_sh) for s in _mod_specs(mod, k, l, tm, n_prompt, tp)]
    return pl.pallas_call(
        functools.partial(_outproj_kernel, np_tiles=npt),
        grid=(n // tm,),
        in_specs=_dual_specs(tm, npt, D_MODEL) + _dual_specs(tm, npt, D_MODEL) + ms
                 + [_layer_spec(a, l) for a in (g, w, wr, br)],
        out_specs=[_row_spec(tm, D_MODEL), _row_spec(tm, H_EXT), _row_spec(tm, 8),
                   pl.BlockSpec((8, N_CLASS), lambda i: (0, 0))],
        out_shape=[jax.ShapeDtypeStruct((n, D_MODEL), F32),
                   jax.ShapeDtypeStruct((n, H_EXT), F32),
                   jax.ShapeDtypeStruct((n, 8), F32),
                   jax.ShapeDtypeStruct((8, N_CLASS), F32)],
        scratch_shapes=[pltpu.VMEM((8, N_CLASS), F32)],
        compiler_params=_cparams(("arbitrary",)),
        name="outproj_router",
    )(*y, *x, *([mod[0], mod[1]] * 3), g, w, wr, br)


def _scatter_kernel(pos_ref, x_ref, o_hbm, buf, sem):
    i, n = pl.program_id(0), pl.num_programs(0)
    tm = x_ref.shape[0]
    slot = i % 2

    def wait_slot(s):
        pltpu.make_async_copy(buf.at[s], o_hbm.at[pl.ds(0, tm), :], sem.at[s]).wait()

    @pl.when(i >= 2)
    def _():
        wait_slot(slot)

    buf[slot] = x_ref[...]

    def body(r, c):
        dst = pos_ref[i * tm + r]
        pltpu.make_async_copy(buf.at[slot, pl.ds(r, 1), :], o_hbm.at[pl.ds(dst, 1), :], sem.at[slot]).start()
        return c
    lax.fori_loop(0, tm, body, 0, unroll=8)

    @pl.when(i == n - 1)
    def _():
        wait_slot(slot)

        @pl.when(n >= 2)
        def _():
            wait_slot(1 - slot)


def _scatter_rows(x, pos):
    n, width = x.shape
    tm = MOE_TILE
    return pl.pallas_call(
        _scatter_kernel,
        grid_spec=pltpu.PrefetchScalarGridSpec(
            num_scalar_prefetch=1, grid=(n // tm,),
            in_specs=[_row_spec(tm, width)],
            out_specs=pl.BlockSpec(memory_space=pl.ANY),
            scratch_shapes=[pltpu.VMEM((2, tm, width), x.dtype), pltpu.SemaphoreType.DMA((2,))]),
        out_shape=jax.ShapeDtypeStruct((n, width), x.dtype),
        compiler_params=_cparams(("arbitrary",)),
        name="scatter_rows",
    )(pos, x)


def _gather_tiles(idx_ref, src_hbm, buf, sem, tm):
    i, n = pl.program_id(0), pl.num_programs(0)

    def issue(tile, slot):
        def body(r, c):
            row = idx_ref[tile * tm + r]
            pltpu.make_async_copy(src_hbm.at[pl.ds(row, 1), :], buf.at[slot, pl.ds(r, 1), :],
                                  sem.at[slot]).start()
            return c
        lax.fori_loop(0, tm, body, 0, unroll=8)

    @pl.when(i == 0)
    def _():
        issue(0, 0)

    @pl.when(i + 1 < n)
    def _():
        issue(i + 1, (i + 1) % 2)

    slot = i % 2
    pltpu.make_async_copy(src_hbm.at[pl.ds(0, tm), :], buf.at[slot], sem.at[slot]).wait()
    return slot


def _experts_kernel(tile_ref, grp_ref, flag_ref, hx_ref, wgu_ref, wd_ref, o_ref):
    j = pl.program_id(0)
    flags = flag_ref[j]
    valid = (flags & 1) != 0

    @pl.when(valid & ((flags & 2) != 0))
    def _():
        o_ref[...] = jnp.zeros_like(o_ref)

    @pl.when(valid)
    def _():
        h = hx_ref[:, 0:D_MODEL].astype(BF16)
        r = hx_ref[:, D_MODEL:H_EXT]
        w1, w2, i1, i2 = r[:, 0:1], r[:, 1:2], r[:, 2:3], r[:, 3:4]
        base = grp_ref[j] * EXPERTS_PER_GROUP
        for e in range(EXPERTS_PER_GROUP):
            @pl.when(((flags >> (8 + e)) & 1) != 0)
            def _():
                eid = (base + e).astype(F32)
                ge = jnp.where(i1 == eid, w1, 0.0) + jnp.where(i2 == eid, w2, 0.0)
                nblk = EXPERT_ROW_BLOCKS
                rb = h.shape[0] // nblk
                aus = [_dot(h[k * rb:(k + 1) * rb], wgu_ref[0, 0, e]) for k in range(nblk)]
                acts = [(_silu(au[:, :EXPERT_FF]) * au[:, EXPERT_FF:]).astype(BF16) for au in aus]
                yes = [_dot(act, wd_ref[0, 0, e]) for act in acts]
                for k in range(nblk):
                    o_ref[k * rb:(k + 1) * rb, :] += ge[k * rb:(k + 1) * rb] * yes[k]


def _experts(hx_sorted, tile, grp, flags, wgu, wd, l):
    n = hx_sorted.shape[0]
    tm = MOE_TILE
    return pl.pallas_call(
        _experts_kernel,
        grid_spec=pltpu.PrefetchScalarGridSpec(
            num_scalar_prefetch=3, grid=(tile.shape[0],),
            in_specs=[pl.BlockSpec((tm, H_EXT), lambda j, t, g, f: (t[j], 0)),
                      pl.BlockSpec((1, 1, EXPERTS_PER_GROUP, D_MODEL, 2 * EXPERT_FF),
                                   lambda j, t, g, f: (l, g[j], 0, 0, 0)),
                      pl.BlockSpec((1, 1, EXPERTS_PER_GROUP, EXPERT_FF, D_MODEL),
                                   lambda j, t, g, f: (l, g[j], 0, 0, 0))],
            out_specs=pl.BlockSpec((tm, D_MODEL), lambda j, t, g, f: (t[j], 0))),
        out_shape=jax.ShapeDtypeStruct((n, D_MODEL), F32),
        compiler_params=_cparams(("arbitrary",)),
        name="moe_experts",
    )(tile, grp, flags, hx_sorted, wgu, wd)


def _combine_kernel(pos_ref, y_hbm, x1_ref, gtp, gts, gf_ref, op_ref, os_ref, buf, sem, *, np_tiles, final):
    slot = _gather_tiles(pos_ref, y_hbm, buf, sem, x1_ref.shape[0])
    is_p = pl.program_id(0) < np_tiles
    x2 = x1_ref[...] + _mod_val(gtp, gts, is_p, x1_ref.shape[0]) * buf[slot]
    out = _rms(x2) * gf_ref[...] if final else x2

    @pl.when(is_p)
    def _():
        op_ref[...] = out

    @pl.when(jnp.logical_not(is_p))
    def _():
        os_ref[...] = out


def _combine(y_sorted, pos, x1, mod, k_gt, gf, l, n_prompt, tp, final):
    n = x1.shape[0]
    tm = MOE_TILE
    npt = n_prompt // tm
    return pl.pallas_call(
        functools.partial(_combine_kernel, np_tiles=npt, final=final),
        grid_spec=pltpu.PrefetchScalarGridSpec(
            num_scalar_prefetch=1, grid=(n // tm,),
            in_specs=[pl.BlockSpec(memory_space=pl.ANY), _row_spec(tm, D_MODEL)]
                     + _mod_specs(mod, k_gt, l, tm, n_prompt, tp) + [_const_spec(gf)],
            out_specs=_dual_specs(tm, npt, D_MODEL),
            scratch_shapes=[pltpu.VMEM((2, tm, D_MODEL), F32), pltpu.SemaphoreType.DMA((2,))]),
        out_shape=[jax.ShapeDtypeStruct((n_prompt, D_MODEL), F32),
                   jax.ShapeDtypeStruct((n - n_prompt, D_MODEL), F32)],
        compiler_params=_cparams(("arbitrary",)),
        name="moe_combine",
    )(pos, y_sorted, x1, mod[0], mod[1], gf)


def _routing_tables(cnt, key, rank, n_tiles, tm):
    n_items = n_tiles + EXPERT_GROUPS - 1
    c = jnp.arange(N_CLASS, dtype=I32)
    start_c = jnp.sum(jnp.where(c[:, None] < c[None, :], cnt[:, None], 0), axis=0)
    pos = jnp.sum(jnp.where(key[:, None] == c[None, :], start_c[None, :], 0), axis=1) + rank

    g = jnp.arange(EXPERT_GROUPS, dtype=I32)
    gend = jnp.sum(jnp.where(c[None, :] < 64 * (g[:, None] + 1), cnt[None, :], 0), axis=1)
    t0 = jnp.arange(n_tiles, dtype=I32) * tm
    t1 = t0 + (tm - 1)
    gfirst = jnp.sum((gend[None, :] <= t0[:, None]).astype(I32), axis=1)
    glast = jnp.sum((gend[None, :] <= t1[:, None]).astype(I32), axis=1)
    per_tile = glast - gfirst + 1
    tt = jnp.arange(n_tiles, dtype=I32)
    start_t = jnp.sum(jnp.where(tt[:, None] < tt[None, :], per_tile[:, None], 0), axis=0)
    total = jnp.sum(per_tile)

    j = jnp.arange(n_items, dtype=I32)
    valid = j < total
    tile = jnp.sum((start_t[None, :] <= j[:, None]).astype(I32), axis=1) - 1
    tile = jnp.where(valid, tile, n_tiles - 1)
    sel = tile[:, None] == tt[None, :]
    pick = lambda v: jnp.sum(jnp.where(sel, v[None, :], 0), axis=1)
    grp = jnp.where(valid, pick(gfirst) + (j - pick(start_t)), glast[n_tiles - 1])
    first = valid & (j == pick(start_t))

    in_tile = (cnt[None, :] > 0) & (start_c[None, :] <= t1[:, None]) & ((start_c + cnt)[None, :] > t0[:, None])
    e = jnp.arange(N_EXPERTS, dtype=I32)
    member = (((c >> 6) * 8 + ((c >> 3) & 7))[:, None] == e[None, :]) | (((c >> 6) * 8 + (c & 7))[:, None] == e[None, :])
    present_t = jnp.any(in_tile[:, :, None] & member[None, :, :], axis=1)
    present_j = jnp.any(sel[:, :, None] & present_t[None, :, :], axis=1)
    eg = e[None, :] - grp[:, None] * EXPERTS_PER_GROUP
    bits = jnp.sum(jnp.where(present_j & (eg >= 0) & (eg < EXPERTS_PER_GROUP),
                             1 << (8 + jnp.clip(eg, 0, EXPERTS_PER_GROUP - 1)), 0), axis=1)
    flags = valid.astype(I32) | (first.astype(I32) << 1) | bits
    return pos.astype(I32), tile.astype(I32), grp.astype(I32), flags.astype(I32)


def _moe(hx, info, counts, x1, mod, k_gt, gf, wgu, wd, l, n_prompt, tp, final):
    n = hx.shape[0]
    tm = MOE_TILE
    pos, tile, grp, flags = _routing_tables(counts[0].astype(I32), info[:, 4].astype(I32),
                                            info[:, 5].astype(I32), n // tm, tm)
    y_sorted = _experts(_scatter_rows(hx, pos), tile, grp, flags, wgu, wd, l)
    return _combine(y_sorted, pos, x1, mod, k_gt, gf, l, n_prompt, tp, final)


def _rope_tables(pos):
    half = HEAD_DIM // 2
    inv = ROPE_BASE ** (-jnp.arange(half, dtype=F32) / half)
    ang = pos.astype(F32)[:, None] * inv[None, :]
    cos, sin = jnp.cos(ang), jnp.sin(ang)
    cos_t = jnp.tile(jnp.concatenate([cos, cos], axis=-1), (1, RET_HEADS))
    sin_t = jnp.tile(jnp.concatenate([-sin, sin], axis=-1), (1, RET_HEADS))
    return cos_t, sin_t


def _mixer_consts(w_in, conv_w, conv_b, dt_bias, a_log, d_skip, g_ssd_norm, g_ret_norm,
                  b_mlstm_i, b_mlstm_f, g_mlstm_norm):
    nl = w_in.shape[0]
    w_t = jnp.transpose(w_in, (0, 2, 1))
    w_p = jnp.concatenate([w_t[:, 0:1280], w_t[:, 1288:3336], w_t[:, 1280:1288], w_t[:, 3336:3344],
                           jnp.zeros((nl, LANES - 16, D_MODEL), F32)], axis=1).astype(BF16)
    pad = lambda v: jnp.pad(v, ((0, 0), (0, LANES - v.shape[1])))
    log_gamma = jnp.log(1.0 - 2.0 ** (-5.0 - jnp.arange(RET_HEADS, dtype=F32)))
    lg = jnp.broadcast_to(jnp.concatenate([jnp.zeros((8,), F32), log_gamma])[None, :], (nl, 12))
    ptab = jnp.stack([pad(jnp.concatenate([dt_bias, b_mlstm_i, b_mlstm_f], axis=1)), pad(a_log), pad(lg)]
                     + [jnp.zeros((nl, LANES), F32)] * 5, axis=1)
    consts = (ptab, conv_w, conv_b[:, None, :], jnp.repeat(d_skip, HEAD_DIM, axis=1)[:, None, :],
              g_ssd_norm[:, None, :], g_ret_norm[:, None, :], g_mlstm_norm[:, None, :])
    return w_p, consts


def kernel(x_prompt, x_sample, state_ssd, state_ssd_conv, state_ret, state_mlstm_c, state_mlstm_n,
           state_mlstm_m, c_prompt, c_sample, w_ada, b_ada, g_norm1, g_norm2, w_in, conv_w, conv_b,
           dt_bias, a_log, d_skip, g_ssd_norm, g_ret_norm, b_mlstm_i, b_mlstm_f, g_mlstm_norm, w_out,
           w_router_group, b_router_group, w_router_expert, b_router_expert, w_gate_up, w_down, g_final):
    bp, tp, _ = x_prompt.shape
    bs, ts, _ = x_sample.shape
    n_srows = bs * ts
    n_prompt = bp * tp

    mod4 = _ada(jnp.concatenate([c_sample, c_prompt], axis=0), w_ada, b_ada)
    mod = (mod4.reshape(DEPTH * 6 * (bs + bp), 1, D_MODEL), mod4, bs)
    sh1, sc1, gt1, sh2, sc2, gt2 = range(6)

    cos_p, sin_p = _rope_tables(jnp.arange(tp, dtype=I32))
    lanes = lambda v: v[..., None]
    half = HEAD_DIM // 2
    ang = (PAST_LEN + jnp.arange(ts, dtype=I32)).astype(F32)[:, None] * (
        ROPE_BASE ** (-jnp.arange(half, dtype=F32) / half))[None, :]
    cos_s, sin_s = lanes(jnp.cos(ang)), lanes(jnp.sin(ang))
    gamma = 1.0 - 2.0 ** (-5.0 - jnp.arange(RET_HEADS, dtype=F32))
    tabs = (lanes(dt_bias[:, :, None]), lanes(a_log[:, :, None]), lanes(d_skip[:, :, None]), lanes(g_ssd_norm),
            lanes(gamma[:, None]), lanes(g_ret_norm), lanes(b_mlstm_i[:, :, None]), lanes(b_mlstm_f[:, :, None]),
            lanes(g_mlstm_norm))
    cwb, cbb = lanes(conv_w), lanes(conv_b)
    ssd_t = jnp.transpose(state_ssd, (0, 2, 3, 4, 1))
    ret_t = jnp.transpose(state_ret, (0, 2, 3, 4, 1))
    c_t = jnp.transpose(state_mlstm_c, (0, 2, 3, 4, 1))
    n_t = jnp.transpose(state_mlstm_n, (0, 2, 3, 1))
    m_t = jnp.transpose(state_mlstm_m, (0, 2, 1))[:, :, None, :]
    conv_t = jnp.transpose(state_ssd_conv, (0, 2, 3, 1))

    w_p, consts = _mixer_consts(w_in, conv_w, conv_b, dt_bias, a_log, d_skip, g_ssd_norm, g_ret_norm,
                                b_mlstm_i, b_mlstm_f, g_mlstm_norm)
    w_o = w_out.astype(BF16)
    zpad = LANES - N_EXPERTS - EXPERT_GROUPS
    wr = jnp.concatenate([w_router_expert, w_router_group, jnp.zeros((DEPTH, D_MODEL, zpad), F32)],
                         axis=2).astype(BF16)
    br = jnp.concatenate([b_router_expert, b_router_group, jnp.zeros((DEPTH, zpad), F32)], axis=1)[:, None, :]
    wgu = w_gate_up.astype(BF16).reshape(DEPTH, EXPERT_GROUPS, EXPERTS_PER_GROUP, D_MODEL, 2 * EXPERT_FF)
    wd = w_down.astype(BF16).reshape(DEPTH, EXPERT_GROUPS, EXPERTS_PER_GROUP, EXPERT_FF, D_MODEL)
    g1, g2, gf = g_norm1[:, None, :], g_norm2[:, None, :], g_final[None, :]

    x = (x_prompt.reshape(n_prompt, D_MODEL), jnp.transpose(x_sample, (1, 0, 2)).reshape(n_srows, D_MODEL))
    p_states, s_states = [], []
    for l in range(DEPTH):
        final = l == DEPTH - 1
        proj = _inproj(x, mod, sc1, sh1, g1, w_p, l, tp)

        ycat_p, st = _mixer_prompt(proj, bp, tp, cos_p, sin_p, consts, l)
        p_states.append(st)

        proj_t = jnp.transpose(proj[n_prompt:].reshape(ts, bs, P_W), (0, 2, 1))
        y_t, st_t = _mixer_sample_T(proj_t, conv_t, cwb, cbb, ssd_t, ret_t, c_t, n_t, m_t, tabs, cos_s, sin_s, l)
        s_states.append(st_t)
        ycat_s = jnp.transpose(y_t, (0, 2, 1)).reshape(n_srows, D_MODEL).astype(BF16)

        x1, hx, info, counts = _outproj((ycat_p, ycat_s), x, mod, gt1, sc2, sh2, g2, w_o, wr, br, l, tp)
        x = _moe(hx, info, counts, x1, mod, gt2, gf, wgu, wd, l, n_prompt, tp, final)

    y_prompt = x[0].reshape(bp, tp, D_MODEL)
    y_sample = jnp.transpose(x[1].reshape(ts, bs, D_MODEL), (1, 0, 2))
    p_st = [jnp.stack([s[i] for s in p_states], axis=0) for i in range(6)]
    s_t = [jnp.stack([s[i] for s in s_states], axis=0) for i in range(6)]
    s_st = [jnp.transpose(s_t[0], (0, 4, 1, 2, 3)), jnp.transpose(s_t[1], (0, 3, 1, 2)),
            jnp.transpose(s_t[2], (0, 4, 1, 2, 3)), jnp.transpose(s_t[3], (0, 4, 1, 2, 3)),
            jnp.transpose(s_t[4], (0, 3, 1, 2)), jnp.transpose(s_t[5][:, :, 0, :], (0, 2, 1))]
    return (y_prompt, y_sample, *p_st, *s_st)
```

```python
import functools
import math

import jax
import jax.numpy as jnp
import numpy as np
from jax import lax
from jax.experimental import pallas as pl
from jax.experimental.pallas import tpu as pltpu

F32 = jnp.float32
BF16 = jnp.bfloat16
I32 = jnp.int32

D_MODEL = 1024
DEPTH = 2
PAST_LEN = 16384
SSD_HEADS = 8
SSD_WIDTH = 512
SSD_GROUPS = 2
SSD_STATE = 64
SSD_CONV = 4
SSD_XBC = 768
RET_HEADS = 4
RET_WIDTH = 256
ML_HEADS = 4
ML_WIDTH = 256
HEAD_DIM = 64
ROPE_BASE = 10000.0
EPS = 1e-6
EXPERT_GROUPS = 4
EXPERTS_PER_GROUP = 8
N_EXPERTS = 32
EXPERT_FF = 256

LANES = 128
CHUNK = 128
SEQ_PER_STEP = 4
P_Z, P_XBC, P_RET, P_ML, P_SM, P_W = 0, 512, 1280, 2304, 3328, 3456
C_SSD, C_RET, C_ML = 0, 8, 12
N_PAIRS = 8
NEG = -1e30
TOKEN_TILE = 512
MOE_TILE = 256
OUTPROJ_ROW_BLOCKS = 2
EXPERT_ROW_BLOCKS = 2
VMEM_LIMIT = 56 * 1024 * 1024
H_EXT = D_MODEL + LANES
PAIRS_PER_GROUP = 32
N_CLASS = EXPERT_GROUPS * PAIRS_PER_GROUP


def _cparams(sem):
    return pltpu.CompilerParams(dimension_semantics=sem, vmem_limit_bytes=VMEM_LIMIT)


def _split3(x):
    x1 = x.astype(BF16)
    r = x - x1.astype(F32)
    x2 = r.astype(BF16)
    r = r - x2.astype(F32)
    return x1, x2, r.astype(BF16)


def _dot01(m01, x):
    return sum(jnp.dot(m01, p, preferred_element_type=F32) for p in _split3(x))


def _dot(a, b):
    return jnp.dot(a, b, preferred_element_type=F32)


def _dot_nt(a, b):
    return lax.dot_general(a, b, (((1,), (1,)), ((), ())), preferred_element_type=F32)


def _dot_tn(a, b):
    return lax.dot_general(a, b, (((0,), (0,)), ((), ())), preferred_element_type=F32)


def _softplus(x):
    return jnp.maximum(x, 0.0) + jnp.log1p(jnp.exp(-jnp.abs(x)))


def _silu(x):
    return x * jax.nn.sigmoid(x)


def _rms(x):
    return x * lax.rsqrt(jnp.mean(x * x, axis=-1, keepdims=True) + EPS)


def _cummax_rows(x, seg):
    t = lax.broadcasted_iota(I32, x.shape, 0) & (seg - 1)
    s = 1
    while s < seg:
        x = jnp.maximum(x, jnp.where(t >= s, pltpu.roll(x, s, 0), NEG))
        s *= 2
    return x


def _rope(x, cos, sin_signed, lane):
    swapped = jnp.where((lane & 63) < 32, pltpu.roll(x, 96, 1), pltpu.roll(x, 32, 1))
    return x * cos + swapped * sin_signed


def _row_spec(tm, width):
    return pl.BlockSpec((tm, width), lambda i, *_: (i, 0))


def _const_spec(a):
    nd = a.ndim
    return pl.BlockSpec(a.shape, lambda *_: (0,) * nd)


def _layer_spec(a, l):
    nd = a.ndim
    return pl.BlockSpec((1,) + a.shape[1:], lambda *_: (l,) + (0,) * (nd - 1))


def _dual_specs(tm, np_tiles, width):
    return [pl.BlockSpec((tm, width), lambda i, *_: (jnp.minimum(i, np_tiles - 1), 0)),
            pl.BlockSpec((tm, width), lambda i, *_: (jnp.maximum(i - np_tiles, 0), 0))]


def _mod_specs(mod, k, l, tm, n_prompt, tp):
    full, bs = mod
    bp = full.shape[2] - bs
    assert bs % bp == 0 and bp % 8 == 0
    return [pl.BlockSpec((1, 1, bp, D_MODEL), lambda i, *_: (l, k, bs // bp, 0)),
            pl.BlockSpec((1, 1, bs, D_MODEL), lambda i, *_: (l, k, 0, 0))]


def _prompt_mod(p_ref, tm, tp):
    b = jnp.minimum(pl.program_id(0) * tm // tp, p_ref.shape[2] - 1)
    return p_ref[0, 0, pl.ds(b, 1), :]


def _sample_mod(s_ref, tm):
    v = s_ref[0, 0]
    return jnp.tile(v, (tm // v.shape[0], 1))


def _mod_val(p_ref, s_ref, is_prompt, tm, tp):
    return jnp.where(is_prompt, _prompt_mod(p_ref, tm, tp), _sample_mod(s_ref, tm))


def _dual_val(p_ref, s_ref, is_prompt):
    return jnp.where(is_prompt, p_ref[...], s_ref[...])


def _ada_kernel(c_ref, w_ref, b_ref, o_ref):
    c = c_ref[...]
    o_ref[0, 0] = _dot(_silu(c).astype(BF16), w_ref[0].astype(BF16)) + b_ref[0]


def _ada(c_all, w_ada, b_ada):
    nb = c_all.shape[0]
    return pl.pallas_call(
        _ada_kernel,
        grid=(DEPTH, 6),
        in_specs=[pl.BlockSpec((nb, D_MODEL), lambda l, k: (0, 0)),
                  pl.BlockSpec((1, D_MODEL, D_MODEL), lambda l, k: (l, 0, k)),
                  pl.BlockSpec((1, 1, D_MODEL), lambda l, k: (l, 0, k))],
        out_specs=pl.BlockSpec((1, 1, nb, D_MODEL), lambda l, k: (l, k, 0, 0)),
        out_shape=jax.ShapeDtypeStruct((DEPTH, 6, nb, D_MODEL), F32),
        compiler_params=_cparams(("arbitrary", "arbitrary")),
        name="ada_mod",
    )(c_all, w_ada, b_ada.reshape(DEPTH, 1, 6 * D_MODEL))


def _inproj_kernel(xp, xs, scp, scs, shp, shs, g_ref, w_ref, o_ref, *, np_tiles, tp):
    is_p = pl.program_id(0) < np_tiles
    tm = o_ref.shape[0]
    h = (_rms(_dual_val(xp, xs, is_p)) * g_ref[0] * (1.0 + _mod_val(scp, scs, is_p, tm, tp))
         + _mod_val(shp, shs, is_p, tm, tp))
    o_ref[...] = _dot_nt(h.astype(BF16), w_ref[0])


def _inproj(x, mod, k_sc, k_sh, g, w, l, tp):
    n_prompt, ns = x[0].shape[0], x[1].shape[0]
    tm = TOKEN_TILE
    ms = _mod_specs(mod, k_sc, l, tm, n_prompt, tp) + _mod_specs(mod, k_sh, l, tm, n_prompt, tp)
    return pl.pallas_call(
        functools.partial(_inproj_kernel, np_tiles=n_prompt // tm, tp=tp),
        grid=((n_prompt + ns) // tm,),
        in_specs=_dual_specs(tm, n_prompt // tm, D_MODEL) + ms + [_layer_spec(g, l), _layer_spec(w, l)],
        out_specs=_row_spec(tm, P_W),
        out_shape=jax.ShapeDtypeStruct((n_prompt + ns, P_W), F32),
        compiler_params=_cparams(("arbitrary",)),
        name="norm_inproj",
    )(*x, *([mod[0]] * 4), g, w)


def _lane_bcast(a, c):
    return jnp.broadcast_to(a[:, c:c + 1], a.shape)


def _pair_lanes(a, c0, c1):
    lm0 = lax.broadcasted_iota(I32, (a.shape[0], LANES), 1) < HEAD_DIM
    return jnp.where(lm0, a[:, c0:c0 + 1], a[:, c1:c1 + 1])


def _ret_factors(ptab, tril, mask, last_fn):
    rows = tril.shape[0]
    lane = lax.broadcasted_iota(I32, (rows, LANES), 1)
    lm0 = lane < HEAD_DIM
    cum = _dot01(tril, jnp.where((lane >= C_RET) & (lane < C_ML), ptab[2:3], 0.0))
    xt = cum.T
    cum_last = last_fn(cum)
    out = []
    for p in range(RET_HEADS // 2):
        c0, c1 = C_RET + 2 * p, C_RET + 2 * p + 1
        b0, b1 = _lane_bcast(cum, c0), _lane_bcast(cum, c1)
        bp = jnp.where(lm0, b0, b1)
        wp = jnp.exp(_pair_lanes(cum_last, c0, c1) - bp)
        out.append((jnp.exp(jnp.where(mask, b0 - xt[c0:c0 + 1, :], -jnp.inf)),
                    jnp.exp(jnp.where(mask, b1 - xt[c1:c1 + 1, :], -jnp.inf)), jnp.exp(bp), wp))
    return out


def _mixer_core(z, us, retb, mlb, small, cos, sin, ptab, cw, cb, dsk, gs, gr, gm,
                mask, tril, seg, mprev, last_fn, st, y_ref, ret_factors):
    rows = small.shape[0]
    lane = lax.broadcasted_iota(I32, (rows, LANES), 1)
    lm0 = lane < HEAD_DIM

    pre = small + ptab[0:1]
    a_neg = -jnp.exp(ptab[1:2])
    dt = _softplus(pre)
    logf = -_softplus(-pre)
    la = jnp.where(lane < C_RET, dt * a_neg,
                   jnp.where(lane < C_ML, ptab[2:3], jnp.where(lane < C_ML + 4, logf, 0.0)))
    cum = _dot01(tril, la)
    ic = pltpu.roll(pre, 4, 1)
    mlm = (lane >= C_ML) & (lane < C_ML + 4)
    d = jnp.where(mlm, ic - cum, NEG)
    m_t = cum + jnp.maximum(mprev, _cummax_rows(d, seg))
    xt = jnp.where(mlm, d, cum).T
    colv = cum - m_t
    cum_last, m_last = last_fn(cum), last_fn(m_t)
    decq = jnp.where(mlm, jnp.exp(cum_last + mprev - m_last), jnp.exp(cum_last))
    yield

    def factors(c0, c1, kind):
        if kind == "ret":
            return ret_factors((c0 - C_RET) // 2)
        if kind == "ssd":
            b0, b1 = _lane_bcast(cum, c0), _lane_bcast(cum, c1)
            bp = jnp.where(lm0, b0, b1)
            wp = jnp.exp(_pair_lanes(cum_last, c0, c1) - bp)
            return (jnp.exp(jnp.where(mask, b0 - xt[c0:c0 + 1, :], -jnp.inf)),
                    jnp.exp(jnp.where(mask, b1 - xt[c1:c1 + 1, :], -jnp.inf)), jnp.exp(bp), wp)
        a0, a1 = _lane_bcast(colv, c0), _lane_bcast(colv, c1)
        eqp = jnp.exp(jnp.where(lm0, a0, a1) + _pair_lanes(mprev, c0, c1))
        wp = jnp.exp(jnp.where(lm0, _lane_bcast(d, c0), _lane_bcast(d, c1))
                     + _pair_lanes(cum_last, c0, c1) - _pair_lanes(m_last, c0, c1))
        return (jnp.exp(jnp.where(mask, a0 + xt[c0:c0 + 1, :], -jnp.inf)),
                jnp.exp(jnp.where(mask, a1 + xt[c1:c1 + 1, :], -jnp.inf)), eqp, wp)

    def pair(idx, inputs, c0, c1, kind, finish):
        qp, kp, vp = inputs()
        q0 = jnp.where(lm0, qp, 0.0)
        q1 = jnp.where(lm0, 0.0, qp)
        sc = _dot_nt(jnp.concatenate([q0, q1], axis=0).astype(BF16), kp.astype(BF16))
        yield
        d0, d1, eqp, wp = factors(c0, c1, kind)
        v01 = jnp.concatenate([jnp.where(lm0, vp, 0.0), jnp.where(lm0, 0.0, vp)], axis=0).astype(BF16)
        kw = kp * wp
        yield
        carried, qn = st.step(idx, qp, eqp, kw, vp, decq, c0, c1, kind == "ml")
        yield
        s0, s1 = sc[0:rows] * d0, sc[rows:2 * rows] * d1
        intra = _dot(jnp.concatenate([s0, s1], axis=1).astype(BF16), v01)
        yield
        finish(intra + carried, s0, s1, qn)

    def head_norm(o):
        o2 = o * o
        ms0 = jnp.sum(jnp.where(lm0, o2, 0.0), axis=-1, keepdims=True) * (1.0 / HEAD_DIM)
        ms1 = jnp.sum(jnp.where(lm0, 0.0, o2), axis=-1, keepdims=True) * (1.0 / HEAD_DIM)
        return o * jnp.where(lm0, lax.rsqrt(ms0 + EPS), lax.rsqrt(ms1 + EPS))

    conv = cb + us[0] * cw[0:1] + us[1] * cw[1:2] + us[2] * cw[2:3] + us[3] * cw[3:4]
    xc = _silu(conv)
    bb = xc[:, SSD_WIDTH:SSD_WIDTH + LANES]
    cc = xc[:, SSD_WIDTH + LANES:SSD_WIDTH + 2 * LANES]
    br = pltpu.roll(bb, HEAD_DIM, 1)
    cr = pltpu.roll(cc, HEAD_DIM, 1)
    ys = [None] * 4
    pairs = []
    for p in range(4):
        c0, c1 = C_SSD + 2 * p, C_SSD + 2 * p + 1
        sl = slice(LANES * p, LANES * (p + 1))

        def ssd_inputs(p=p, c0=c0, c1=c1, sl=sl):
            if p < 2:
                kp, qp = jnp.where(lm0, bb, br), jnp.where(lm0, cc, cr)
            else:
                kp, qp = jnp.where(lm0, br, bb), jnp.where(lm0, cr, cc)
            dtp = jnp.where(lm0, dt[:, c0:c0 + 1], dt[:, c1:c1 + 1])
            return qp, kp, xc[:, sl] * dtp

        def ssd_finish(o, s0, s1, qn, p=p, sl=sl):
            ys[p] = (o + dsk[:, sl] * xc[:, sl]) * _silu(z[:, sl])

        pairs.append(pair(p, ssd_inputs, c0, c1, "ssd", ssd_finish))

    for p in range(2):
        c0, c1 = C_RET + 2 * p, C_RET + 2 * p + 1
        sl = slice(LANES * p, LANES * (p + 1))

        def ret_inputs(p=p, sl=sl):
            qp = _rope(retb[:, LANES * p:LANES * (p + 1)], cos[:, sl], sin[:, sl], lane)
            kp = _rope(retb[:, RET_WIDTH + LANES * p:RET_WIDTH + LANES * (p + 1)], cos[:, sl], sin[:, sl], lane)
            return qp, kp * (HEAD_DIM ** -0.5), retb[:, 2 * RET_WIDTH + LANES * p:2 * RET_WIDTH + LANES * (p + 1)]

        def ret_finish(o, s0, s1, qn, p=p, sl=sl):
            gp = retb[:, 3 * RET_WIDTH + LANES * p:3 * RET_WIDTH + LANES * (p + 1)]
            y = head_norm(o) * gr[:, sl] * _silu(gp)
            y_ref[:, SSD_WIDTH + LANES * p:SSD_WIDTH + LANES * (p + 1)] = y.astype(y_ref.dtype)

        pairs.append(pair(4 + p, ret_inputs, c0, c1, "ret", ret_finish))

    for p in range(2):
        c0, c1 = C_ML + 2 * p, C_ML + 2 * p + 1
        sl = slice(LANES * p, LANES * (p + 1))

        def ml_inputs(p=p):
            return (mlb[:, LANES * p:LANES * (p + 1)],
                    mlb[:, ML_WIDTH + LANES * p:ML_WIDTH + LANES * (p + 1)] * (HEAD_DIM ** -0.5),
                    mlb[:, 2 * ML_WIDTH + LANES * p:2 * ML_WIDTH + LANES * (p + 1)])

        def ml_finish(num, s0, s1, qn, p=p, c0=c0, c1=c1, sl=sl):
            op = mlb[:, 3 * ML_WIDTH + LANES * p:3 * ML_WIDTH + LANES * (p + 1)]
            inter0 = jnp.exp(colv[:, c0:c0 + 1] + mprev[:, c0:c0 + 1])
            inter1 = jnp.exp(colv[:, c1:c1 + 1] + mprev[:, c1:c1 + 1])
            den0 = jnp.sum(s0, axis=-1, keepdims=True) + qn[0] * inter0
            den1 = jnp.sum(s1, axis=-1, keepdims=True) + qn[1] * inter1
            dn0 = jnp.maximum(jnp.abs(den0), jnp.exp(-m_t[:, c0:c0 + 1]))
            dn1 = jnp.maximum(jnp.abs(den1), jnp.exp(-m_t[:, c1:c1 + 1]))
            hh = num / jnp.where(lm0, dn0, dn1)
            y = head_norm(hh) * gm[:, sl] * jax.nn.sigmoid(op)
            off = SSD_WIDTH + RET_WIDTH + LANES * p
            y_ref[:, off:off + LANES] = y.astype(y_ref.dtype)

        pairs.append(pair(6 + p, ml_inputs, c0, c1, "ml", ml_finish))

    live = list(pairs)
    while live:
        for g in list(live):
            try:
                next(g)
            except StopIteration:
                live.remove(g)
        yield

    for g in range(SSD_GROUPS):
        ya, yb = ys[2 * g], ys[2 * g + 1]
        ms = (jnp.sum(ya * ya, axis=-1, keepdims=True)
              + jnp.sum(yb * yb, axis=-1, keepdims=True)) * (1.0 / (2 * LANES))
        r = lax.rsqrt(ms + EPS)
        for j, yv in ((2 * g, ya), (2 * g + 1, yb)):
            sl = slice(LANES * j, LANES * (j + 1))
            y_ref[:, sl] = (yv * r * gs[:, sl]).astype(y_ref.dtype)
    return m_t


def _run_interleaved(gens):
    out = [None] * len(gens)
    live = list(range(len(gens)))
    while live:
        for k in list(live):
            try:
                next(gens[k])
            except StopIteration as stop:
                out[k] = stop.value
                live.remove(k)
    return out


def _half_rows():
    return lax.broadcasted_iota(I32, (LANES, HEAD_DIM), 0) < HEAD_DIM


class _CarriedState:
    def __init__(self, sv, nrow):
        self.sv, self.nrow = sv, nrow

    def step(self, idx, qp, eqp, kw, vp, decq, c0, c1, ml):
        lm0 = lax.broadcasted_iota(I32, (1, LANES), 1) < HEAD_DIM
        drow = _pair_lanes(decq[0:1, :], c0, c1)
        s_old = self.sv[idx]
        carried = _dot(qp.astype(BF16), s_old.astype(BF16)) * eqp
        u = _dot_tn(kw.astype(BF16), vp.astype(BF16))
        same_head = ((lax.broadcasted_iota(I32, (LANES, LANES), 0) < HEAD_DIM)
                     == (lax.broadcasted_iota(I32, (LANES, LANES), 1) < HEAD_DIM))
        self.sv[idx] = s_old * drow + jnp.where(same_head, u, 0.0)
        qn = None
        if ml:
            p = idx - 6
            n_old = self.nrow[p:p + 1, :]
            qn_l = qp * n_old
            qn = (jnp.sum(jnp.where(lm0, qn_l, 0.0), axis=-1, keepdims=True),
                  jnp.sum(jnp.where(lm0, 0.0, qn_l), axis=-1, keepdims=True))
            self.nrow[p:p + 1, :] = n_old * drow + jnp.sum(kw, axis=0, keepdims=True)
        return carried, qn


def _mixer_prompt_kernel(*refs):
    sq = SEQ_PER_STEP
    proj_refs = refs[:sq]
    (cos_ref, sin_ref, ptab_ref, cw_ref, cb_ref, dsk_ref, gs_ref, gr_ref, gm_ref,
     y_ref, sv_o, conv_o, n_o, m_o) = refs[sq:sq + 14]
    scr = refs[sq + 14:]
    sv, nrow, mrow, cbuf = scr[0:sq], scr[sq:2 * sq], scr[2 * sq:3 * sq], scr[3 * sq:4 * sq]
    ret_cache = scr[4 * sq]
    ci = pl.program_id(1)
    rows = proj_refs[0].shape[1]
    ri = lax.broadcasted_iota(I32, (rows, rows), 0)
    cj = lax.broadcasted_iota(I32, (rows, rows), 1)
    mask = cj <= ri
    tril = jnp.where(mask, 1.0, 0.0).astype(BF16)
    last_row = lambda a: a[rows - 1:rows, :]

    @pl.when(ci == 0)
    def _():
        for s in range(sq):
            sv[s][...] = jnp.zeros_like(sv[s])
            nrow[s][...] = jnp.zeros_like(nrow[s])
            mrow[s][...] = jnp.zeros_like(mrow[s])
            cbuf[s][0:8, :] = jnp.zeros((8, SSD_XBC), F32)
        for p, fs in enumerate(_ret_factors(ptab_ref[0], tril, mask, last_row)):
            for k, f in enumerate(fs):
                ret_cache[4 * p + k] = f

    ret_factors = lambda p: tuple(ret_cache[4 * p + k] for k in range(4))
    gens = []
    for s in range(sq):
        proj_ref = proj_refs[s]
        cbuf[s][8:8 + rows, :] = proj_ref[0, :, P_XBC:P_RET]
        us = [cbuf[s][pl.ds(5 + k, rows), :] for k in range(SSD_CONV)]
        gens.append(_mixer_core(
            proj_ref[0, :, P_Z:P_XBC], us, proj_ref[0, :, P_RET:P_ML], proj_ref[0, :, P_ML:P_SM],
            proj_ref[0, :, P_SM:P_W], cos_ref[...], sin_ref[...], ptab_ref[0], cw_ref[0], cb_ref[0],
            dsk_ref[0], gs_ref[0], gr_ref[0], gm_ref[0],
            mask, tril, rows, mrow[s][0:1, :], last_row,
            _CarriedState(sv[s], nrow[s]), y_ref.at[s], ret_factors))
    for s, m_t in enumerate(_run_interleaved(gens)):
        mrow[s][0:1, :] = m_t[rows - 1:rows, :]
        cbuf[s][0:8, :] = cbuf[s][rows:rows + 8, :]

    @pl.when(ci == pl.num_programs(1) - 1)
    def _():
        top = _half_rows()
        for s in range(sq):
            for idx in range(N_PAIRS):
                bd = sv[s][idx]
                sv_o[s, idx] = jnp.where(top, bd[:, :HEAD_DIM], bd[:, HEAD_DIM:])
            conv_o[s] = cbuf[s][0:8, :]
            n_o[s] = nrow[s][...]
            m_o[s] = mrow[s][...]


def _mixer_prompt(proj, nb, t, cos, sin, consts, l):
    rows = math.gcd(t, CHUNK)
    nc = t // rows
    sq = SEQ_PER_STEP
    assert nb % sq == 0

    def proj_spec(s):
        return pl.BlockSpec((1, rows, P_W), lambda b, c: (0, (sq * b + s) * nc + c, 0))

    def seq_spec(*tail):
        return pl.BlockSpec((sq,) + tail, lambda b, c: (b,) + (0,) * len(tail))

    outs = pl.pallas_call(
        _mixer_prompt_kernel,
        grid=(nb // sq, nc),
        in_specs=[proj_spec(s) for s in range(sq)]
                 + [pl.BlockSpec((rows, RET_WIDTH), lambda b, c: (c, 0)),
                    pl.BlockSpec((rows, RET_WIDTH), lambda b, c: (c, 0))] + [_layer_spec(a, l) for a in consts],
        out_specs=[pl.BlockSpec((sq, rows, D_MODEL), lambda b, c: (b, c, 0)),
                   seq_spec(N_PAIRS, LANES, HEAD_DIM), seq_spec(8, SSD_XBC), seq_spec(8, LANES), seq_spec(8, LANES)],
        out_shape=[jax.ShapeDtypeStruct((nb, t, D_MODEL), BF16),
                   jax.ShapeDtypeStruct((nb, N_PAIRS, LANES, HEAD_DIM), F32),
                   jax.ShapeDtypeStruct((nb, 8, SSD_XBC), F32),
                   jax.ShapeDtypeStruct((nb, 8, LANES), F32),
                   jax.ShapeDtypeStruct((nb, 8, LANES), F32)],
        scratch_shapes=[pltpu.VMEM((N_PAIRS, LANES, LANES), F32) for _ in range(sq)]
                       + [pltpu.VMEM((8, LANES), F32) for _ in range(2 * sq)]
                       + [pltpu.VMEM((rows + 8, SSD_XBC), F32) for _ in range(sq)]
                       + [pltpu.VMEM((2 * RET_HEADS, rows, LANES), F32)],
        compiler_params=_cparams(("arbitrary", "arbitrary")),
        name="mixer_prompt",
    )(*([proj[None]] * sq), cos, sin, *consts)
    y, sv, conv, n, m = outs
    sv = sv.reshape(nb, 2 * N_PAIRS, HEAD_DIM, HEAD_DIM)
    states = (sv[:, :8], conv[:, 5:8], sv[:, 8:12], sv[:, 12:16],
              n[:, 0:2].reshape(nb, ML_HEADS, HEAD_DIM), m[:, 0, C_ML:C_ML + 4])
    return y.reshape(nb * t, D_MODEL), states


def _lane_recurrence(s_in, s_out, q_s, k_s, v_s, decay):
    n_tok = len(decay)
    nvb = HEAD_DIM // 8
    nb = q_s.shape[-1]
    dec8 = [jnp.broadcast_to(d, (8, nb)) for d in decay]

    def body(k, acc):
        acc = [list(a) for a in acc]
        qk = [q_s[t, pl.ds(k, 1), :] for t in range(n_tok)]
        kk = [k_s[t, pl.ds(k, 1), :] for t in range(n_tok)]
        for vb in range(nvb):
            rows = pl.ds(8 * vb, 8)
            s = s_in[k, rows, :]
            for t in range(n_tok):
                s = s * dec8[t] + kk[t] * v_s[t, rows, :]
                acc[t][vb] = acc[t][vb] + qk[t] * s
            s_out[k, rows, :] = s
        return tuple(tuple(a) for a in acc)

    init = tuple(tuple(jnp.zeros((8, nb), F32) for _ in range(nvb)) for _ in range(n_tok))
    acc = lax.fori_loop(0, HEAD_DIM, body, init, unroll=2)
    return [jnp.concatenate(list(a), axis=0) for a in acc]


def _conv_T(u_ref, tail_ref, w_ref, b_ref, state_ref):
    n_tok = u_ref.shape[0]
    full = [tail_ref[0, j] for j in range(SSD_CONV - 1)] + [u_ref[t] for t in range(n_tok)]
    outs = []
    for t in range(n_tok):
        acc = b_ref[0]
        for tap in range(SSD_CONV):
            acc = acc + full[t + tap] * w_ref[0, tap]
        outs.append(_silu(acc))
    for j in range(SSD_CONV - 1):
        state_ref[j] = full[n_tok + j]
    return outs


def _ssd_T_kernel(z_ref, xs_ref, b_ref, c_ref, sm_ref, tx_ref, tb_ref, tc_ref, wx_ref, wb_ref, wc_ref,
                  bx_ref, bb_ref, bc_ref, dtb_ref, alog_ref, dsk_ref, g_ref, s_ref,
                  y_ref, so_ref, cx_ref, cb_ref, cc_ref, q_s, k_s, v_s):
    h = pl.program_id(0)
    n_tok = xs_ref.shape[0]
    xs = _conv_T(xs_ref, tx_ref, wx_ref, bx_ref, cx_ref)
    bm = _conv_T(b_ref, tb_ref, wb_ref, bb_ref, cb_ref)
    cm = _conv_T(c_ref, tc_ref, wc_ref, bc_ref, cc_ref)
    a_neg = -jnp.exp(alog_ref[0, 0])
    decay = []
    for t in range(n_tok):
        dt = _softplus(sm_ref[t, pl.ds(C_SSD + h, 1), :] + dtb_ref[0, 0])
        decay.append(jnp.exp(dt * a_neg))
        q_s[t] = cm[t]
        k_s[t] = bm[t]
        v_s[t] = xs[t] * dt
    o = _lane_recurrence(s_ref.at[0, 0], so_ref.at[0, 0], q_s, k_s, v_s, decay)
    hh = h % (SSD_HEADS // SSD_GROUPS)
    row0 = pl.multiple_of(hh * HEAD_DIM, HEAD_DIM)
    for t in range(n_tok):
        y_ref[t, pl.ds(row0, HEAD_DIM), :] = (o[t] + dsk_ref[0, 0] * xs[t]) * _silu(z_ref[t])

    @pl.when(hh == SSD_HEADS // SSD_GROUPS - 1)
    def _():
        for t in range(n_tok):
            blk = y_ref[t]
            ms = jnp.mean(blk * blk, axis=0, keepdims=True)
            y_ref[t] = blk * lax.rsqrt(ms + EPS) * g_ref[0]


def _ret_T_kernel(q_ref, k_ref, v_ref, g_ref, cos_ref, sin_ref, gam_ref, gn_ref, s_ref,
                  y_ref, so_ref, q_s, k_s, v_s):
    n_tok = q_ref.shape[0]
    half = HEAD_DIM // 2

    def rope(x, t):
        x1, x2 = x[:half], x[half:]
        return jnp.concatenate([x1 * cos_ref[t] - x2 * sin_ref[t], x1 * sin_ref[t] + x2 * cos_ref[t]], axis=0)

    for t in range(n_tok):
        q_s[t] = rope(q_ref[t], t)
        k_s[t] = rope(k_ref[t], t) * (HEAD_DIM ** -0.5)
        v_s[t] = v_ref[t]
    o = _lane_recurrence(s_ref.at[0, 0], so_ref.at[0, 0], q_s, k_s, v_s, [gam_ref[0]] * n_tok)
    for t in range(n_tok):
        ms = jnp.mean(o[t] * o[t], axis=0, keepdims=True)
        y_ref[t] = o[t] * lax.rsqrt(ms + EPS) * gn_ref[0] * _silu(g_ref[t])


def _ml_T_kernel(q_ref, k_ref, v_ref, o_ref, sm_ref, bi_ref, bf_ref, gn_ref, c_ref, n_ref, m_ref,
                 y_ref, co_ref, no_ref, mo_ref, q_s, k_s, v_s):
    h = pl.program_id(0)
    n_tok = q_ref.shape[0]
    m = m_ref[0, 0]
    n = n_ref[0, 0]
    decay, qn, m_all = [], [], []
    for t in range(n_tok):
        i_t = sm_ref[t, pl.ds(C_RET + h, 1), :] + bi_ref[0, 0]
        logf = -_softplus(-(sm_ref[t, pl.ds(C_ML + h, 1), :] + bf_ref[0, 0]))
        m_new = jnp.maximum(logf + m, i_t)
        f_t = jnp.exp(logf + m - m_new)
        kw = k_ref[t] * (HEAD_DIM ** -0.5) * jnp.exp(i_t - m_new)
        n = n * f_t + kw
        q_s[t] = q_ref[t]
        k_s[t] = kw
        v_s[t] = v_ref[t]
        decay.append(f_t)
        qn.append(jnp.sum(q_ref[t] * n, axis=0, keepdims=True))
        m_all.append(m_new)
        m = m_new
    num = _lane_recurrence(c_ref.at[0, 0], co_ref.at[0, 0], q_s, k_s, v_s, decay)
    no_ref[0] = n
    mo_ref[0] = m
    for t in range(n_tok):
        hh = num[t] / jnp.maximum(jnp.abs(qn[t]), jnp.exp(-m_all[t]))
        ms = jnp.mean(hh * hh, axis=0, keepdims=True)
        y_ref[t] = hh * lax.rsqrt(ms + EPS) * gn_ref[0] * jax.nn.sigmoid(o_ref[t])


def _mixer_sample_T(projT, convT, cwb, cbb, ssdT, retT, cT, nT, mT, tabs, cos, sin, l):
    n_tok, _, nb = projT.shape
    dtb, alog, dsk, gs, gam, gr, bi, bf, gm = tabs
    hb = lambda col: col // HEAD_DIM
    rep = SSD_HEADS // SSD_GROUPS
    proj_blk = lambda first, div=1: pl.BlockSpec((n_tok, HEAD_DIM, nb), lambda h: (0, first + h // div, 0))
    small = pl.BlockSpec((n_tok, LANES, nb), lambda h: (0, P_SM // LANES, 0))
    lay4 = lambda a, first=0, div=1: pl.BlockSpec((1,) + a.shape[1:2] + (HEAD_DIM, a.shape[-1]),
                                                  lambda h: (l, 0, first + h // div, 0))
    chan = lambda a, first=0, div=1, n=1: pl.BlockSpec((1, n * HEAD_DIM, 1), lambda h: (l, first + h // div, 0))
    head4 = lambda a: pl.BlockSpec((1, 1) + a.shape[2:], lambda h: (l, h) + (0,) * (a.ndim - 2))
    vm = lambda: pltpu.VMEM((n_tok, HEAD_DIM, nb), F32)
    state_spec = pl.BlockSpec((1, 1, HEAD_DIM, HEAD_DIM, nb), lambda h: (0, h, 0, 0, 0))
    stacked = lambda heads: jax.ShapeDtypeStruct((1, heads, HEAD_DIM, HEAD_DIM, nb), F32)
    b0, c0 = hb(SSD_WIDTH), hb(SSD_WIDTH + SSD_GROUPS * SSD_STATE)

    y_ssd, ssd_n, cx, cb, cc = pl.pallas_call(
        _ssd_T_kernel,
        grid=(SSD_HEADS,),
        in_specs=[proj_blk(hb(P_Z)), proj_blk(hb(P_XBC)), proj_blk(hb(P_XBC) + b0, rep),
                  proj_blk(hb(P_XBC) + c0, rep), small,
                  lay4(convT), lay4(convT, b0, rep), lay4(convT, c0, rep),
                  lay4(cwb), lay4(cwb, b0, rep), lay4(cwb, c0, rep),
                  chan(cbb), chan(cbb, b0, rep), chan(cbb, c0, rep),
                  head4(dtb), head4(alog), head4(dsk), chan(gs, 0, rep, rep), head4(ssdT)],
        out_specs=[pl.BlockSpec((n_tok, rep * HEAD_DIM, nb), lambda h: (0, h // rep, 0)), state_spec,
                   pl.BlockSpec((SSD_CONV - 1, HEAD_DIM, nb), lambda h: (0, h, 0)),
                   pl.BlockSpec((SSD_CONV - 1, HEAD_DIM, nb), lambda h: (0, h // rep, 0)),
                   pl.BlockSpec((SSD_CONV - 1, HEAD_DIM, nb), lambda h: (0, h // rep, 0))],
        out_shape=[jax.ShapeDtypeStruct((n_tok, SSD_WIDTH, nb), F32),
                   stacked(SSD_HEADS),
                   jax.ShapeDtypeStruct((SSD_CONV - 1, SSD_WIDTH, nb), F32),
                   jax.ShapeDtypeStruct((SSD_CONV - 1, SSD_GROUPS * SSD_STATE, nb), F32),
                   jax.ShapeDtypeStruct((SSD_CONV - 1, SSD_GROUPS * SSD_STATE, nb), F32)],
        scratch_shapes=[vm(), vm(), vm()],
        compiler_params=_cparams(("arbitrary",)),
        name="sample_ssd",
    )(projT, projT, projT, projT, projT, convT, convT, convT, cwb, cwb, cwb, cbb, cbb, cbb,
      dtb, alog, dsk, gs, ssdT)

    y_ret, ret_n = pl.pallas_call(
        _ret_T_kernel,
        grid=(RET_HEADS,),
        in_specs=[proj_blk(hb(P_RET)), proj_blk(hb(P_RET + RET_WIDTH)), proj_blk(hb(P_RET + 2 * RET_WIDTH)),
                  proj_blk(hb(P_RET + 3 * RET_WIDTH)), _const_spec(cos), _const_spec(sin),
                  pl.BlockSpec((1, 1, 1), lambda h: (h, 0, 0)), chan(gr), head4(retT)],
        out_specs=[pl.BlockSpec((n_tok, HEAD_DIM, nb), lambda h: (0, h, 0)), state_spec],
        out_shape=[jax.ShapeDtypeStruct((n_tok, RET_WIDTH, nb), F32),
                   stacked(RET_HEADS)],
        scratch_shapes=[vm(), vm(), vm()],
        compiler_params=_cparams(("arbitrary",)),
        name="sample_ret",
    )(projT, projT, projT, projT, cos, sin, gam, gr, retT)

    y_ml, c_n, n_n, m_n = pl.pallas_call(
        _ml_T_kernel,
        grid=(ML_HEADS,),
        in_specs=[proj_blk(hb(P_ML)), proj_blk(hb(P_ML + ML_WIDTH)), proj_blk(hb(P_ML + 2 * ML_WIDTH)),
                  proj_blk(hb(P_ML + 3 * ML_WIDTH)), small, head4(bi), head4(bf),
                  chan(gm), head4(cT), head4(nT), head4(mT)],
        out_specs=[pl.BlockSpec((n_tok, HEAD_DIM, nb), lambda h: (0, h, 0)), state_spec,
                   pl.BlockSpec((1, HEAD_DIM, nb), lambda h: (h, 0, 0)),
                   pl.BlockSpec((1, 1, nb), lambda h: (h, 0, 0))],
        out_shape=[jax.ShapeDtypeStruct((n_tok, ML_WIDTH, nb), F32),
                   stacked(ML_HEADS),
                   jax.ShapeDtypeStruct((ML_HEADS, HEAD_DIM, nb), F32),
                   jax.ShapeDtypeStruct((ML_HEADS, 1, nb), F32)],
        scratch_shapes=[vm(), vm(), vm()],
        compiler_params=_cparams(("arbitrary",)),
        name="sample_mlstm",
    )(projT, projT, projT, projT, projT, bi, bf, gm, cT, nT, mT)

    y_t = jnp.concatenate([y_ssd, y_ret, y_ml], axis=1)
    conv_n = jnp.concatenate([cx, cb, cc], axis=1)
    return y_t, (ssd_n, conv_n, ret_n, c_n, n_n, m_n)


def _route(logits):
    lane = lax.broadcasted_iota(I32, logits.shape, 1)
    gmask = (lane >= N_EXPERTS) & (lane < N_EXPERTS + EXPERT_GROUPS)
    gl = jnp.where(gmask, logits, -jnp.inf)
    ge = jnp.exp(gl - jnp.max(gl, axis=-1, keepdims=True))
    gprob = ge / jnp.sum(ge, axis=-1, keepdims=True)
    g_w = jnp.max(gprob, axis=-1, keepdims=True)
    g_idx = jnp.min(jnp.where(gmask & (gprob == g_w), lane - N_EXPERTS, LANES), axis=-1, keepdims=True)
    emask = (lane < N_EXPERTS) & ((lane >> 3) == g_idx)
    el = jnp.where(emask, logits, -jnp.inf)
    ee = jnp.exp(el - jnp.max(el, axis=-1, keepdims=True))
    eprob = ee / jnp.sum(ee, axis=-1, keepdims=True)
    p1 = jnp.max(jnp.where(emask, eprob, -1.0), axis=-1, keepdims=True)
    i1 = jnp.min(jnp.where(emask & (eprob == p1), lane, LANES), axis=-1, keepdims=True)
    rest = emask & (lane != i1)
    p2 = jnp.max(jnp.where(rest, eprob, -1.0), axis=-1, keepdims=True)
    i2 = jnp.min(jnp.where(rest & (eprob == p2), lane, LANES), axis=-1, keepdims=True)
    tot = p1 + p2
    lo, hi = jnp.minimum(i1, i2) & 7, jnp.maximum(i1, i2) & 7
    key = g_idx * PAIRS_PER_GROUP + ((lo * (13 - lo)) >> 1) + hi - 1
    return g_w * (p1 / tot), g_w * (p2 / tot), i1, i2, key


def _outproj_kernel(yp, ys, xp, xs, gtp, gts, scp, scs, shp, shs, g_ref, w_ref, wr_ref, br_ref,
                    x1_ref, hx_ref, info_ref, cnt_ref, carry, *, np_tiles, tp):
    i = pl.program_id(0)
    is_p = i < np_tiles

    @pl.when(i == 0)
    def _():
        carry[...] = jnp.zeros_like(carry)

    tm = x1_ref.shape[0]
    nblk = OUTPROJ_ROW_BLOCKS
    rb = tm // nblk
    blocks = [slice(k * rb, (k + 1) * rb) for k in range(nblk)]

    def body(x_ref, y_ref, gt, sc, sh):
        mod = lambda m, b: m if m.shape[0] == 1 else m[b]
        x1s = [x_ref[b, :] + mod(gt, b) * _dot(y_ref[b, :], w_ref[0]) for b in blocks]
        for b, x1 in zip(blocks, x1s):
            x1_ref[b, :] = x1
        h2bs = [(_rms(x1) * g_ref[0] * (1.0 + mod(sc, b)) + mod(sh, b)).astype(BF16)
                for b, x1 in zip(blocks, x1s)]
        routes = [_route(_dot(h2b, wr_ref[0]) + br_ref[0]) for h2b in h2bs]

        tril = (lax.broadcasted_iota(I32, (rb, rb), 1) <= lax.broadcasted_iota(I32, (rb, rb), 0))
        tril = jnp.where(tril, 1.0, 0.0).astype(BF16)
        onehots = [lax.broadcasted_iota(I32, (rb, N_CLASS), 1) == r[4] for r in routes]
        uptos = [_dot(tril, jnp.where(oh, 1.0, 0.0).astype(BF16)) for oh in onehots]
        seen = carry[0:1, :]
        ranks = []
        for oh, upto in zip(onehots, uptos):
            ranks.append(jnp.sum(jnp.where(oh, upto - 1.0 + seen, 0.0), axis=-1, keepdims=True))
            seen = seen + upto[rb - 1:rb, :]
        carry[0:1, :] = seen
        cnt_ref[...] = jnp.broadcast_to(seen, cnt_ref.shape)

        lane = lax.broadcasted_iota(I32, (rb, LANES), 1)
        for b, h2b, (w1, w2, i1, i2, key), rank in zip(blocks, h2bs, routes, ranks):
            info = jnp.zeros((rb, LANES), F32)
            for c, v in enumerate((w1, w2, i1.astype(F32), i2.astype(F32), key.astype(F32), rank)):
                info = jnp.where(lane == c, v, info)
            hx_ref[b, 0:D_MODEL] = h2b.astype(F32)
            hx_ref[b, D_MODEL:H_EXT] = info
            info_ref[b, :] = info[:, 0:8]

    @pl.when(is_p)
    def _():
        body(xp, yp, _prompt_mod(gtp, tm, tp), _prompt_mod(scp, tm, tp), _prompt_mod(shp, tm, tp))

    @pl.when(jnp.logical_not(is_p))
    def _():
        body(xs, ys, _sample_mod(gts, tm), _sample_mod(scs, tm), _sample_mod(shs, tm))


def _outproj(y, x, mod, k_gt, k_sc, k_sh, g, w, wr, br, l, tp):
    n_prompt, ns = x[0].shape[0], x[1].shape[0]
    n = n_prompt + ns
    tm = TOKEN_TILE
    npt = n_prompt // tm
    ms = [s for k in (k_gt, k_sc, k_sh) for s in _mod_specs(mod, k, l, tm, n_prompt, tp)]
    return pl.pallas_call(
        functools.partial(_outproj_kernel, np_tiles=npt, tp=tp),
        grid=(n // tm,),
        in_specs=_dual_specs(tm, npt, D_MODEL) + _dual_specs(tm, npt, D_MODEL) + ms
                 + [_layer_spec(a, l) for a in (g, w, wr, br)],
        out_specs=[_row_spec(tm, D_MODEL), _row_spec(tm, H_EXT), _row_spec(tm, 8),
                   pl.BlockSpec((8, N_CLASS), lambda i: (0, 0))],
        out_shape=[jax.ShapeDtypeStruct((n, D_MODEL), F32),
                   jax.ShapeDtypeStruct((n, H_EXT), F32),
                   jax.ShapeDtypeStruct((n, 8), F32),
                   jax.ShapeDtypeStruct((8, N_CLASS), F32)],
        scratch_shapes=[pltpu.VMEM((8, N_CLASS), F32)],
        compiler_params=_cparams(("arbitrary",)),
        name="outproj_router",
    )(*y, *x, *([mod[0]] * 6), g, w, wr, br)


def _scatter_kernel(pos_ref, x_ref, o_hbm, buf, sem):
    i, n = pl.program_id(0), pl.num_programs(0)
    tm = x_ref.shape[0]
    slot = i % 2

    def wait_slot(s):
        pltpu.make_async_copy(buf.at[s], o_hbm.at[pl.ds(0, tm), :], sem.at[s]).wait()

    @pl.when(i >= 2)
    def _():
        wait_slot(slot)

    buf[slot] = x_ref[...]

    def body(r, c):
        dst = pos_ref[i * tm + r]
        pltpu.make_async_copy(buf.at[slot, pl.ds(r, 1), :], o_hbm.at[pl.ds(dst, 1), :], sem.at[slot]).start()
        return c
    lax.fori_loop(0, tm, body, 0, unroll=8)

    @pl.when(i == n - 1)
    def _():
        wait_slot(slot)

        @pl.when(n >= 2)
        def _():
            wait_slot(1 - slot)


def _scatter_rows(x, pos):
    n, width = x.shape
    tm = MOE_TILE
    return pl.pallas_call(
        _scatter_kernel,
        grid_spec=pltpu.PrefetchScalarGridSpec(
            num_scalar_prefetch=1, grid=(n // tm,),
            in_specs=[_row_spec(tm, width)],
            out_specs=pl.BlockSpec(memory_space=pl.ANY),
            scratch_shapes=[pltpu.VMEM((2, tm, width), x.dtype), pltpu.SemaphoreType.DMA((2,))]),
        out_shape=jax.ShapeDtypeStruct((n, width), x.dtype),
        compiler_params=_cparams(("arbitrary",)),
        name="scatter_rows",
    )(pos, x)


def _gather_tiles(idx_ref, src_hbm, buf, sem, tm):
    i, n = pl.program_id(0), pl.num_programs(0)

    def issue(tile, slot):
        def body(r, c):
            row = idx_ref[tile * tm + r]
            pltpu.make_async_copy(src_hbm.at[pl.ds(row, 1), :], buf.at[slot, pl.ds(r, 1), :],
                                  sem.at[slot]).start()
            return c
        lax.fori_loop(0, tm, body, 0, unroll=8)

    @pl.when(i == 0)
    def _():
        issue(0, 0)

    @pl.when(i + 1 < n)
    def _():
        issue(i + 1, (i + 1) % 2)

    slot = i % 2
    pltpu.make_async_copy(src_hbm.at[pl.ds(0, tm), :], buf.at[slot], sem.at[slot]).wait()
    return slot


def _experts_kernel(tile_ref, grp_ref, flag_ref, hx_ref, wgu_ref, wd_ref, o_ref):
    j = pl.program_id(0)
    flags = flag_ref[j]
    valid = (flags & 1) != 0

    @pl.when(valid & ((flags & 2) != 0))
    def _():
        o_ref[...] = jnp.zeros_like(o_ref)

    @pl.when(valid)
    def _():
        h = hx_ref[:, 0:D_MODEL].astype(BF16)
        r = hx_ref[:, D_MODEL:H_EXT]
        w1, w2, i1, i2 = r[:, 0:1], r[:, 1:2], r[:, 2:3], r[:, 3:4]
        base = grp_ref[j] * EXPERTS_PER_GROUP
        for e in range(EXPERTS_PER_GROUP):
            @pl.when(((flags >> (8 + e)) & 1) != 0)
            def _():
                eid = (base + e).astype(F32)
                ge = jnp.where(i1 == eid, w1, 0.0) + jnp.where(i2 == eid, w2, 0.0)
                nblk = EXPERT_ROW_BLOCKS
                rb = h.shape[0] // nblk
                aus = [_dot(h[k * rb:(k + 1) * rb], wgu_ref[0, 0, e]) for k in range(nblk)]
                acts = [(_silu(au[:, :EXPERT_FF]) * au[:, EXPERT_FF:]).astype(BF16) for au in aus]
                yes = [_dot(act, wd_ref[0, 0, e]) for act in acts]
                for k in range(nblk):
                    o_ref[k * rb:(k + 1) * rb, :] += ge[k * rb:(k + 1) * rb] * yes[k]


def _experts(hx_sorted, tile, grp, flags, wgu, wd, l):
    n = hx_sorted.shape[0]
    tm = MOE_TILE
    return pl.pallas_call(
        _experts_kernel,
        grid_spec=pltpu.PrefetchScalarGridSpec(
            num_scalar_prefetch=3, grid=(tile.shape[0],),
            in_specs=[pl.BlockSpec((tm, H_EXT), lambda j, t, g, f: (t[j], 0)),
                      pl.BlockSpec((1, 1, EXPERTS_PER_GROUP, D_MODEL, 2 * EXPERT_FF),
                                   lambda j, t, g, f: (l, g[j], 0, 0, 0)),
                      pl.BlockSpec((1, 1, EXPERTS_PER_GROUP, EXPERT_FF, D_MODEL),
                                   lambda j, t, g, f: (l, g[j], 0, 0, 0))],
            out_specs=pl.BlockSpec((tm, D_MODEL), lambda j, t, g, f: (t[j], 0))),
        out_shape=jax.ShapeDtypeStruct((n, D_MODEL), F32),
        compiler_params=_cparams(("arbitrary",)),
        name="moe_experts",
    )(tile, grp, flags, hx_sorted, wgu, wd)


def _combine_kernel(pos_ref, y_hbm, x1_ref, gtp, gts, gf_ref, op_ref, os_ref, buf, sem, *, np_tiles, tp, final):
    slot = _gather_tiles(pos_ref, y_hbm, buf, sem, x1_ref.shape[0])
    is_p = pl.program_id(0) < np_tiles
    x2 = x1_ref[...] + _mod_val(gtp, gts, is_p, x1_ref.shape[0], tp) * buf[slot]
    out = _rms(x2) * gf_ref[...] if final else x2

    @pl.when(is_p)
    def _():
        op_ref[...] = out

    @pl.when(jnp.logical_not(is_p))
    def _():
        os_ref[...] = out


def _combine(y_sorted, pos, x1, mod, k_gt, gf, l, n_prompt, tp, final):
    n = x1.shape[0]
    tm = MOE_TILE
    npt = n_prompt // tm
    return pl.pallas_call(
        functools.partial(_combine_kernel, np_tiles=npt, tp=tp, final=final),
        grid_spec=pltpu.PrefetchScalarGridSpec(
            num_scalar_prefetch=1, grid=(n // tm,),
            in_specs=[pl.BlockSpec(memory_space=pl.ANY), _row_spec(tm, D_MODEL)]
                     + _mod_specs(mod, k_gt, l, tm, n_prompt, tp) + [_const_spec(gf)],
            out_specs=_dual_specs(tm, npt, D_MODEL),
            scratch_shapes=[pltpu.VMEM((2, tm, D_MODEL), F32), pltpu.SemaphoreType.DMA((2,))]),
        out_shape=[jax.ShapeDtypeStruct((n_prompt, D_MODEL), F32),
                   jax.ShapeDtypeStruct((n - n_prompt, D_MODEL), F32)],
        compiler_params=_cparams(("arbitrary",)),
        name="moe_combine",
    )(pos, y_sorted, x1, mod[0], mod[0], gf)


def _class_members():
    table = np.zeros((N_CLASS, N_EXPERTS), dtype=bool)
    for grp in range(EXPERT_GROUPS):
        for lo in range(EXPERTS_PER_GROUP):
            for hi in range(lo + 1, EXPERTS_PER_GROUP):
                cls = grp * PAIRS_PER_GROUP + ((lo * (13 - lo)) >> 1) + hi - 1
                table[cls, grp * EXPERTS_PER_GROUP + lo] = True
                table[cls, grp * EXPERTS_PER_GROUP + hi] = True
    return table


def _routing_tables(cnt, key, rank, n_tiles, tm):
    n_items = n_tiles + EXPERT_GROUPS - 1
    c = jnp.arange(N_CLASS, dtype=I32)
    start_c = jnp.sum(jnp.where(c[:, None] < c[None, :], cnt[:, None], 0), axis=0)
    pos = jnp.sum(jnp.where(key[:, None] == c[None, :], start_c[None, :], 0), axis=1) + rank

    g = jnp.arange(EXPERT_GROUPS, dtype=I32)
    gend = jnp.sum(jnp.where(c[None, :] < PAIRS_PER_GROUP * (g[:, None] + 1), cnt[None, :], 0), axis=1)
    t0 = jnp.arange(n_tiles, dtype=I32) * tm
    t1 = t0 + (tm - 1)
    gfirst = jnp.sum((gend[None, :] <= t0[:, None]).astype(I32), axis=1)
    glast = jnp.sum((gend[None, :] <= t1[:, None]).astype(I32), axis=1)
    per_tile = glast - gfirst + 1
    tt = jnp.arange(n_tiles, dtype=I32)
    start_t = jnp.sum(jnp.where(tt[:, None] < tt[None, :], per_tile[:, None], 0), axis=0)
    total = jnp.sum(per_tile)

    j = jnp.arange(n_items, dtype=I32)
    valid = j < total
    tile = jnp.sum((start_t[None, :] <= j[:, None]).astype(I32), axis=1) - 1
    tile = jnp.where(valid, tile, n_tiles - 1)
    sel = tile[:, None] == tt[None, :]
    pick = lambda v: jnp.sum(jnp.where(sel, v[None, :], 0), axis=1)
    grp = jnp.where(valid, pick(gfirst) + (j - pick(start_t)), glast[n_tiles - 1])
    first = valid & (j == pick(start_t))

    in_tile = (cnt[None, :] > 0) & (start_c[None, :] <= t1[:, None]) & ((start_c + cnt)[None, :] > t0[:, None])
    e = jnp.arange(N_EXPERTS, dtype=I32)
    member = jnp.asarray(_class_members())
    present_t = jnp.any(in_tile[:, :, None] & member[None, :, :], axis=1)
    present_j = jnp.any(sel[:, :, None] & present_t[None, :, :], axis=1)
    eg = e[None, :] - grp[:, None] * EXPERTS_PER_GROUP
    bits = jnp.sum(jnp.where(present_j & (eg >= 0) & (eg < EXPERTS_PER_GROUP),
                             1 << (8 + jnp.clip(eg, 0, EXPERTS_PER_GROUP - 1)), 0), axis=1)
    flags = valid.astype(I32) | (first.astype(I32) << 1) | bits
    return pos.astype(I32), tile.astype(I32), grp.astype(I32), flags.astype(I32)


def _moe(hx, info, counts, x1, mod, k_gt, gf, wgu, wd, l, n_prompt, tp, final):
    n = hx.shape[0]
    tm = MOE_TILE
    pos, tile, grp, flags = _routing_tables(counts[0].astype(I32), info[:, 4].astype(I32),
                                            info[:, 5].astype(I32), n // tm, tm)
    y_sorted = _experts(_scatter_rows(hx, pos), tile, grp, flags, wgu, wd, l)
    return _combine(y_sorted, pos, x1, mod, k_gt, gf, l, n_prompt, tp, final)


def _rope_tables(pos):
    half = HEAD_DIM // 2
    inv = ROPE_BASE ** (-jnp.arange(half, dtype=F32) / half)
    ang = pos.astype(F32)[:, None] * inv[None, :]
    cos, sin = jnp.cos(ang), jnp.sin(ang)
    cos_t = jnp.tile(jnp.concatenate([cos, cos], axis=-1), (1, RET_HEADS))
    sin_t = jnp.tile(jnp.concatenate([-sin, sin], axis=-1), (1, RET_HEADS))
    return cos_t, sin_t


def _mixer_consts(w_in, conv_w, conv_b, dt_bias, a_log, d_skip, g_ssd_norm, g_ret_norm,
                  b_mlstm_i, b_mlstm_f, g_mlstm_norm):
    nl = w_in.shape[0]
    w_t = jnp.transpose(w_in, (0, 2, 1))
    w_p = jnp.concatenate([w_t[:, 0:1280], w_t[:, 1288:3336], w_t[:, 1280:1288], w_t[:, 3336:3344],
                           jnp.zeros((nl, LANES - 16, D_MODEL), F32)], axis=1).astype(BF16)
    pad = lambda v: jnp.pad(v, ((0, 0), (0, LANES - v.shape[1])))
    log_gamma = jnp.log(1.0 - 2.0 ** (-5.0 - jnp.arange(RET_HEADS, dtype=F32)))
    lg = jnp.broadcast_to(jnp.concatenate([jnp.zeros((8,), F32), log_gamma])[None, :], (nl, 12))
    ptab = jnp.stack([pad(jnp.concatenate([dt_bias, b_mlstm_i, b_mlstm_f], axis=1)), pad(a_log), pad(lg)]
                     + [jnp.zeros((nl, LANES), F32)] * 5, axis=1)
    consts = (ptab, conv_w, conv_b[:, None, :], jnp.repeat(d_skip, HEAD_DIM, axis=1)[:, None, :],
              g_ssd_norm[:, None, :], g_ret_norm[:, None, :], g_mlstm_norm[:, None, :])
    return w_p, consts


def kernel(x_prompt, x_sample, state_ssd, state_ssd_conv, state_ret, state_mlstm_c, state_mlstm_n,
           state_mlstm_m, c_prompt, c_sample, w_ada, b_ada, g_norm1, g_norm2, w_in, conv_w, conv_b,
           dt_bias, a_log, d_skip, g_ssd_norm, g_ret_norm, b_mlstm_i, b_mlstm_f, g_mlstm_norm, w_out,
           w_router_group, b_router_group, w_router_expert, b_router_expert, w_gate_up, w_down, g_final):
    bp, tp, _ = x_prompt.shape
    bs, ts, _ = x_sample.shape
    n_srows = bs * ts
    n_prompt = bp * tp

    mod4 = _ada(jnp.concatenate([c_sample, c_prompt], axis=0), w_ada, b_ada)
    mod = (mod4, bs)
    sh1, sc1, gt1, sh2, sc2, gt2 = range(6)

    cos_p, sin_p = _rope_tables(jnp.arange(tp, dtype=I32))
    lanes = lambda v: v[..., None]
    half = HEAD_DIM // 2
    ang = (PAST_LEN + jnp.arange(ts, dtype=I32)).astype(F32)[:, None] * (
        ROPE_BASE ** (-jnp.arange(half, dtype=F32) / half))[None, :]
    cos_s, sin_s = lanes(jnp.cos(ang)), lanes(jnp.sin(ang))
    gamma = 1.0 - 2.0 ** (-5.0 - jnp.arange(RET_HEADS, dtype=F32))
    tabs = (lanes(dt_bias[:, :, None]), lanes(a_log[:, :, None]), lanes(d_skip[:, :, None]), lanes(g_ssd_norm),
            lanes(gamma[:, None]), lanes(g_ret_norm), lanes(b_mlstm_i[:, :, None]), lanes(b_mlstm_f[:, :, None]),
            lanes(g_mlstm_norm))
    cwb, cbb = lanes(conv_w), lanes(conv_b)
    ssd_t = jnp.transpose(state_ssd, (0, 2, 3, 4, 1))
    ret_t = jnp.transpose(state_ret, (0, 2, 3, 4, 1))
    c_t = jnp.transpose(state_mlstm_c, (0, 2, 3, 4, 1))
    n_t = jnp.transpose(state_mlstm_n, (0, 2, 3, 1))
    m_t = jnp.transpose(state_mlstm_m, (0, 2, 1))[:, :, None, :]
    conv_t = jnp.transpose(state_ssd_conv, (0, 2, 3, 1))

    w_p, consts = _mixer_consts(w_in, conv_w, conv_b, dt_bias, a_log, d_skip, g_ssd_norm, g_ret_norm,
                                b_mlstm_i, b_mlstm_f, g_mlstm_norm)
    w_o = w_out.astype(BF16)
    zpad = LANES - N_EXPERTS - EXPERT_GROUPS
    wr = jnp.concatenate([w_router_expert, w_router_group, jnp.zeros((DEPTH, D_MODEL, zpad), F32)],
                         axis=2).astype(BF16)
    br = jnp.concatenate([b_router_expert, b_router_group, jnp.zeros((DEPTH, zpad), F32)], axis=1)[:, None, :]
    wgu = w_gate_up.astype(BF16).reshape(DEPTH, EXPERT_GROUPS, EXPERTS_PER_GROUP, D_MODEL, 2 * EXPERT_FF)
    wd = w_down.astype(BF16).reshape(DEPTH, EXPERT_GROUPS, EXPERTS_PER_GROUP, EXPERT_FF, D_MODEL)
    g1, g2, gf = g_norm1[:, None, :], g_norm2[:, None, :], g_final[None, :]

    x = (x_prompt.reshape(n_prompt, D_MODEL), jnp.transpose(x_sample, (1, 0, 2)).reshape(n_srows, D_MODEL))
    p_states, s_states = [], []
    for l in range(DEPTH):
        final = l == DEPTH - 1
        proj = _inproj(x, mod, sc1, sh1, g1, w_p, l, tp)

        ycat_p, st = _mixer_prompt(proj, bp, tp, cos_p, sin_p, consts, l)
        p_states.append(st)

        proj_t = jnp.transpose(proj[n_prompt:].reshape(ts, bs, P_W), (0, 2, 1))
        y_t, st_t = _mixer_sample_T(proj_t, conv_t, cwb, cbb, ssd_t, ret_t, c_t, n_t, m_t, tabs, cos_s, sin_s, l)
        s_states.append(st_t)
        ycat_s = jnp.transpose(y_t, (0, 2, 1)).reshape(n_srows, D_MODEL).astype(BF16)

        x1, hx, info, counts = _outproj((ycat_p, ycat_s), x, mod, gt1, sc2, sh2, g2, w_o, wr, br, l, tp)
        x = _moe(hx, info, counts, x1, mod, gt2, gf, wgu, wd, l, n_prompt, tp, final)

    y_prompt = x[0].reshape(bp, tp, D_MODEL)
    y_sample = jnp.transpose(x[1].reshape(ts, bs, D_MODEL), (1, 0, 2))
    p_st = [jnp.stack([s[i] for s in p_states], axis=0) for i in range(6)]
    s_t = [jnp.concatenate([s[i] for s in s_states], axis=0) if i in (0, 2, 3)
           else jnp.stack([s[i] for s in s_states], axis=0) for i in range(6)]
    s_st = [jnp.transpose(s_t[0], (0, 4, 1, 2, 3)), jnp.transpose(s_t[1], (0, 3, 1, 2)),
            jnp.transpose(s_t[2], (0, 4, 1, 2, 3)), jnp.transpose(s_t[3], (0, 4, 1, 2, 3)),
            jnp.transpose(s_t[4], (0, 3, 1, 2)), jnp.transpose(s_t[5][:, :, 0, :], (0, 2, 1))]
    return (y_prompt, y_sample, *p_st, *s_st)
```

```python
import functools
import math

import jax
import jax.numpy as jnp
import numpy as np
from jax import lax
from jax.experimental import pallas as pl
from jax.experimental.pallas import tpu as pltpu

F32 = jnp.float32
BF16 = jnp.bfloat16
I32 = jnp.int32

D_MODEL = 1024
DEPTH = 2
PAST_LEN = 16384
SSD_HEADS = 8
SSD_WIDTH = 512
SSD_GROUPS = 2
SSD_STATE = 64
SSD_CONV = 4
SSD_XBC = 768
RET_HEADS = 4
RET_WIDTH = 256
ML_HEADS = 4
ML_WIDTH = 256
HEAD_DIM = 64
ROPE_BASE = 10000.0
EPS = 1e-6
EXPERT_GROUPS = 4
EXPERTS_PER_GROUP = 8
N_EXPERTS = 32
EXPERT_FF = 256

LANES = 128
CHUNK = 128
SEQ_PER_STEP = 4
P_Z, P_XBC, P_RET, P_ML, P_SM, P_W = 0, 512, 1280, 2304, 3328, 3456
C_SSD, C_RET, C_ML = 0, 8, 12
N_PAIRS = 8
NEG = -1e30
TOKEN_TILE = 512
MOE_TILE = 256
OUTPROJ_ROW_BLOCKS = 2
EXPERT_ROW_BLOCKS = 2
VMEM_LIMIT = 56 * 1024 * 1024
H_EXT = D_MODEL + LANES
PAIRS_PER_GROUP = 32
N_CLASS = EXPERT_GROUPS * PAIRS_PER_GROUP


def _cparams(sem):
    return pltpu.CompilerParams(dimension_semantics=sem, vmem_limit_bytes=VMEM_LIMIT)


def _split3(x):
    x1 = x.astype(BF16)
    r = x - x1.astype(F32)
    x2 = r.astype(BF16)
    r = r - x2.astype(F32)
    return x1, x2, r.astype(BF16)


def _dot01(m01, x):
    return sum(jnp.dot(m01, p, preferred_element_type=F32) for p in _split3(x))


def _dot(a, b):
    return jnp.dot(a, b, preferred_element_type=F32)


def _dot_nt(a, b):
    return lax.dot_general(a, b, (((1,), (1,)), ((), ())), preferred_element_type=F32)


def _dot_tn(a, b):
    return lax.dot_general(a, b, (((0,), (0,)), ((), ())), preferred_element_type=F32)


def _softplus(x):
    return jnp.maximum(x, 0.0) + jnp.log1p(jnp.exp(-jnp.abs(x)))


def _silu(x):
    return x * jax.nn.sigmoid(x)


def _rms(x):
    return x * lax.rsqrt(jnp.mean(x * x, axis=-1, keepdims=True) + EPS)


def _cummax_rows(x, seg):
    t = lax.broadcasted_iota(I32, x.shape, 0) & (seg - 1)
    s = 1
    while s < seg:
        x = jnp.maximum(x, jnp.where(t >= s, pltpu.roll(x, s, 0), NEG))
        s *= 2
    return x


def _rope(x, cos, sin_signed, lane):
    swapped = jnp.where((lane & 63) < 32, pltpu.roll(x, 96, 1), pltpu.roll(x, 32, 1))
    return x * cos + swapped * sin_signed


def _row_spec(tm, width):
    return pl.BlockSpec((tm, width), lambda i, *_: (i, 0))


def _const_spec(a):
    nd = a.ndim
    return pl.BlockSpec(a.shape, lambda *_: (0,) * nd)


def _layer_spec(a, l):
    nd = a.ndim
    return pl.BlockSpec((1,) + a.shape[1:], lambda *_: (l,) + (0,) * (nd - 1))


def _dual_specs(tm, np_tiles, width):
    return [pl.BlockSpec((tm, width), lambda i, *_: (jnp.minimum(i, np_tiles - 1), 0)),
            pl.BlockSpec((tm, width), lambda i, *_: (jnp.maximum(i - np_tiles, 0), 0))]


def _mod_specs(mod, k, l, tm, n_prompt, tp):
    full, bs = mod
    bp = full.shape[2] - bs
    assert bs % bp == 0 and bp % 8 == 0
    return [pl.BlockSpec((1, 1, bp, D_MODEL), lambda i, *_: (l, k, bs // bp, 0)),
            pl.BlockSpec((1, 1, bs, D_MODEL), lambda i, *_: (l, k, 0, 0))]


def _prompt_mod(p_ref, tm, tp):
    b = jnp.minimum(pl.program_id(0) * tm // tp, p_ref.shape[2] - 1)
    return p_ref[0, 0, pl.ds(b, 1), :]


def _sample_mod(s_ref, tm):
    v = s_ref[0, 0]
    return jnp.tile(v, (tm // v.shape[0], 1))


def _mod_val(p_ref, s_ref, is_prompt, tm, tp):
    return jnp.where(is_prompt, _prompt_mod(p_ref, tm, tp), _sample_mod(s_ref, tm))


def _dual_val(p_ref, s_ref, is_prompt):
    return jnp.where(is_prompt, p_ref[...], s_ref[...])


def _ada_kernel(c_ref, w_ref, b_ref, o_ref):
    c = c_ref[...]
    o_ref[0, 0] = _dot(_silu(c).astype(BF16), w_ref[0].astype(BF16)) + b_ref[0]


def _ada(c_all, w_ada, b_ada):
    nb = c_all.shape[0]
    return pl.pallas_call(
        _ada_kernel,
        grid=(DEPTH, 6),
        in_specs=[pl.BlockSpec((nb, D_MODEL), lambda l, k: (0, 0)),
                  pl.BlockSpec((1, D_MODEL, D_MODEL), lambda l, k: (l, 0, k)),
                  pl.BlockSpec((1, 1, D_MODEL), lambda l, k: (l, 0, k))],
        out_specs=pl.BlockSpec((1, 1, nb, D_MODEL), lambda l, k: (l, k, 0, 0)),
        out_shape=jax.ShapeDtypeStruct((DEPTH, 6, nb, D_MODEL), F32),
        compiler_params=_cparams(("arbitrary", "arbitrary")),
        name="ada_mod",
    )(c_all, w_ada, b_ada.reshape(DEPTH, 1, 6 * D_MODEL))


def _inproj_kernel(xp, xs, scp, scs, shp, shs, g_ref, w_ref, o_ref, *, np_tiles, tp):
    is_p = pl.program_id(0) < np_tiles
    tm = o_ref.shape[0]
    h = (_rms(_dual_val(xp, xs, is_p)) * g_ref[0] * (1.0 + _mod_val(scp, scs, is_p, tm, tp))
         + _mod_val(shp, shs, is_p, tm, tp))
    o_ref[...] = _dot_nt(h.astype(BF16), w_ref[0])


def _inproj(x, mod, k_sc, k_sh, g, w, l, tp):
    n_prompt, ns = x[0].shape[0], x[1].shape[0]
    tm = TOKEN_TILE
    ms = _mod_specs(mod, k_sc, l, tm, n_prompt, tp) + _mod_specs(mod, k_sh, l, tm, n_prompt, tp)
    return pl.pallas_call(
        functools.partial(_inproj_kernel, np_tiles=n_prompt // tm, tp=tp),
        grid=((n_prompt + ns) // tm,),
        in_specs=_dual_specs(tm, n_prompt // tm, D_MODEL) + ms + [_layer_spec(g, l), _layer_spec(w, l)],
        out_specs=_row_spec(tm, P_W),
        out_shape=jax.ShapeDtypeStruct((n_prompt + ns, P_W), F32),
        compiler_params=_cparams(("arbitrary",)),
        name="norm_inproj",
    )(*x, *([mod[0]] * 4), g, w)


def _lane_bcast(a, c):
    return jnp.broadcast_to(a[:, c:c + 1], a.shape)


def _pair_lanes(a, c0, c1):
    lm0 = lax.broadcasted_iota(I32, (a.shape[0], LANES), 1) < HEAD_DIM
    return jnp.where(lm0, a[:, c0:c0 + 1], a[:, c1:c1 + 1])


def _ret_factors(ptab, tril, mask, last_fn):
    rows = tril.shape[0]
    lane = lax.broadcasted_iota(I32, (rows, LANES), 1)
    lm0 = lane < HEAD_DIM
    cum = _dot01(tril, jnp.where((lane >= C_RET) & (lane < C_ML), ptab[2:3], 0.0))
    xt = cum.T
    cum_last = last_fn(cum)
    out = []
    for p in range(RET_HEADS // 2):
        c0, c1 = C_RET + 2 * p, C_RET + 2 * p + 1
        b0, b1 = _lane_bcast(cum, c0), _lane_bcast(cum, c1)
        bp = jnp.where(lm0, b0, b1)
        wp = jnp.exp(_pair_lanes(cum_last, c0, c1) - bp)
        out.append((jnp.exp(jnp.where(mask, b0 - xt[c0:c0 + 1, :], -jnp.inf)),
                    jnp.exp(jnp.where(mask, b1 - xt[c1:c1 + 1, :], -jnp.inf)), jnp.exp(bp), wp))
    return out


def _mixer_core(z, us, retb, mlb, small, cos, sin, ptab, cw, cb, dsk, gs, gr, gm,
                mask, tril, seg, mprev, last_fn, st, y_ref, ret_factors):
    rows = small.shape[0]
    lane = lax.broadcasted_iota(I32, (rows, LANES), 1)
    lm0 = lane < HEAD_DIM

    pre = small + ptab[0:1]
    a_neg = -jnp.exp(ptab[1:2])
    dt = _softplus(pre)
    logf = -_softplus(-pre)
    la = jnp.where(lane < C_RET, dt * a_neg,
                   jnp.where(lane < C_ML, ptab[2:3], jnp.where(lane < C_ML + 4, logf, 0.0)))
    cum = _dot01(tril, la)
    ic = pltpu.roll(pre, 4, 1)
    mlm = (lane >= C_ML) & (lane < C_ML + 4)
    d = jnp.where(mlm, ic - cum, NEG)
    m_t = cum + jnp.maximum(mprev, _cummax_rows(d, seg))
    xt = jnp.where(mlm, d, cum).T
    colv = cum - m_t
    cum_last, m_last = last_fn(cum), last_fn(m_t)
    decq = jnp.where(mlm, jnp.exp(cum_last + mprev - m_last), jnp.exp(cum_last))
    yield

    def factors(c0, c1, kind):
        if kind == "ret":
            return ret_factors((c0 - C_RET) // 2)
        if kind == "ssd":
            b0, b1 = _lane_bcast(cum, c0), _lane_bcast(cum, c1)
            bp = jnp.where(lm0, b0, b1)
            wp = jnp.exp(_pair_lanes(cum_last, c0, c1) - bp)
            return (jnp.exp(jnp.where(mask, b0 - xt[c0:c0 + 1, :], -jnp.inf)),
                    jnp.exp(jnp.where(mask, b1 - xt[c1:c1 + 1, :], -jnp.inf)), jnp.exp(bp), wp)
        a0, a1 = _lane_bcast(colv, c0), _lane_bcast(colv, c1)
        eqp = jnp.exp(jnp.where(lm0, a0, a1) + _pair_lanes(mprev, c0, c1))
        wp = jnp.exp(jnp.where(lm0, _lane_bcast(d, c0), _lane_bcast(d, c1))
                     + _pair_lanes(cum_last, c0, c1) - _pair_lanes(m_last, c0, c1))
        return (jnp.exp(jnp.where(mask, a0 + xt[c0:c0 + 1, :], -jnp.inf)),
                jnp.exp(jnp.where(mask, a1 + xt[c1:c1 + 1, :], -jnp.inf)), eqp, wp)

    def pair(idx, inputs, c0, c1, kind, finish):
        qp, kp, vp = inputs()
        q0 = jnp.where(lm0, qp, 0.0)
        q1 = jnp.where(lm0, 0.0, qp)
        sc = _dot_nt(jnp.concatenate([q0, q1], axis=0).astype(BF16), kp.astype(BF16))
        yield
        d0, d1, eqp, wp = factors(c0, c1, kind)
        v01 = jnp.concatenate([jnp.where(lm0, vp, 0.0), jnp.where(lm0, 0.0, vp)], axis=0).astype(BF16)
        kw = kp * wp
        yield
        carried, qn = st.step(idx, qp, eqp, kw, vp, decq, c0, c1, kind == "ml")
        yield
        s0, s1 = sc[0:rows] * d0, sc[rows:2 * rows] * d1
        intra = _dot(jnp.concatenate([s0, s1], axis=1).astype(BF16), v01)
        yield
        finish(intra + carried, s0, s1, qn)

    def head_norm(o):
        o2 = o * o
        ms0 = jnp.sum(jnp.where(lm0, o2, 0.0), axis=-1, keepdims=True) * (1.0 / HEAD_DIM)
        ms1 = jnp.sum(jnp.where(lm0, 0.0, o2), axis=-1, keepdims=True) * (1.0 / HEAD_DIM)
        return o * jnp.where(lm0, lax.rsqrt(ms0 + EPS), lax.rsqrt(ms1 + EPS))

    conv = cb + us[0] * cw[0:1] + us[1] * cw[1:2] + us[2] * cw[2:3] + us[3] * cw[3:4]
    xc = _silu(conv)
    bb = xc[:, SSD_WIDTH:SSD_WIDTH + LANES]
    cc = xc[:, SSD_WIDTH + LANES:SSD_WIDTH + 2 * LANES]
    br = pltpu.roll(bb, HEAD_DIM, 1)
    cr = pltpu.roll(cc, HEAD_DIM, 1)
    ys = [None] * 4
    pairs = []
    for p in range(4):
        c0, c1 = C_SSD + 2 * p, C_SSD + 2 * p + 1
        sl = slice(LANES * p, LANES * (p + 1))

        def ssd_inputs(p=p, c0=c0, c1=c1, sl=sl):
            if p < 2:
                kp, qp = jnp.where(lm0, bb, br), jnp.where(lm0, cc, cr)
            else:
                kp, qp = jnp.where(lm0, br, bb), jnp.where(lm0, cr, cc)
            dtp = jnp.where(lm0, dt[:, c0:c0 + 1], dt[:, c1:c1 + 1])
            return qp, kp, xc[:, sl] * dtp

        def ssd_finish(o, s0, s1, qn, p=p, sl=sl):
            ys[p] = (o + dsk[:, sl] * xc[:, sl]) * _silu(z[:, sl])

        pairs.append(pair(p, ssd_inputs, c0, c1, "ssd", ssd_finish))

    for p in range(2):
        c0, c1 = C_RET + 2 * p, C_RET + 2 * p + 1
        sl = slice(LANES * p, LANES * (p + 1))

        def ret_inputs(p=p, sl=sl):
            qp = _rope(retb[:, LANES * p:LANES * (p + 1)], cos[:, sl], sin[:, sl], lane)
            kp = _rope(retb[:, RET_WIDTH + LANES * p:RET_WIDTH + LANES * (p + 1)], cos[:, sl], sin[:, sl], lane)
            return qp, kp * (HEAD_DIM ** -0.5), retb[:, 2 * RET_WIDTH + LANES * p:2 * RET_WIDTH + LANES * (p + 1)]

        def ret_finish(o, s0, s1, qn, p=p, sl=sl):
            gp = retb[:, 3 * RET_WIDTH + LANES * p:3 * RET_WIDTH + LANES * (p + 1)]
            y = head_norm(o) * gr[:, sl] * _silu(gp)
            y_ref[:, SSD_WIDTH + LANES * p:SSD_WIDTH + LANES * (p + 1)] = y.astype(y_ref.dtype)

        pairs.append(pair(4 + p, ret_inputs, c0, c1, "ret", ret_finish))

    for p in range(2):
        c0, c1 = C_ML + 2 * p, C_ML + 2 * p + 1
        sl = slice(LANES * p, LANES * (p + 1))

        def ml_inputs(p=p):
            return (mlb[:, LANES * p:LANES * (p + 1)],
                    mlb[:, ML_WIDTH + LANES * p:ML_WIDTH + LANES * (p + 1)] * (HEAD_DIM ** -0.5),
                    mlb[:, 2 * ML_WIDTH + LANES * p:2 * ML_WIDTH + LANES * (p + 1)])

        def ml_finish(num, s0, s1, qn, p=p, c0=c0, c1=c1, sl=sl):
            op = mlb[:, 3 * ML_WIDTH + LANES * p:3 * ML_WIDTH + LANES * (p + 1)]
            inter0 = jnp.exp(colv[:, c0:c0 + 1] + mprev[:, c0:c0 + 1])
            inter1 = jnp.exp(colv[:, c1:c1 + 1] + mprev[:, c1:c1 + 1])
            den0 = jnp.sum(s0, axis=-1, keepdims=True) + qn[0] * inter0
            den1 = jnp.sum(s1, axis=-1, keepdims=True) + qn[1] * inter1
            dn0 = jnp.maximum(jnp.abs(den0), jnp.exp(-m_t[:, c0:c0 + 1]))
            dn1 = jnp.maximum(jnp.abs(den1), jnp.exp(-m_t[:, c1:c1 + 1]))
            hh = num / jnp.where(lm0, dn0, dn1)
            y = head_norm(hh) * gm[:, sl] * jax.nn.sigmoid(op)
            off = SSD_WIDTH + RET_WIDTH + LANES * p
            y_ref[:, off:off + LANES] = y.astype(y_ref.dtype)

        pairs.append(pair(6 + p, ml_inputs, c0, c1, "ml", ml_finish))

    live = list(pairs)
    while live:
        for g in list(live):
            try:
                next(g)
            except StopIteration:
                live.remove(g)
        yield

    for g in range(SSD_GROUPS):
        ya, yb = ys[2 * g], ys[2 * g + 1]
        ms = (jnp.sum(ya * ya, axis=-1, keepdims=True)
              + jnp.sum(yb * yb, axis=-1, keepdims=True)) * (1.0 / (2 * LANES))
        r = lax.rsqrt(ms + EPS)
        for j, yv in ((2 * g, ya), (2 * g + 1, yb)):
            sl = slice(LANES * j, LANES * (j + 1))
            y_ref[:, sl] = (yv * r * gs[:, sl]).astype(y_ref.dtype)
    return m_t


def _run_interleaved(gens):
    out = [None] * len(gens)
    live = list(range(len(gens)))
    while live:
        for k in list(live):
            try:
                next(gens[k])
            except StopIteration as stop:
                out[k] = stop.value
                live.remove(k)
    return out


def _half_rows():
    return lax.broadcasted_iota(I32, (LANES, HEAD_DIM), 0) < HEAD_DIM


class _CarriedState:
    def __init__(self, sv, nrow):
        self.sv, self.nrow = sv, nrow

    def step(self, idx, qp, eqp, kw, vp, decq, c0, c1, ml):
        lm0 = lax.broadcasted_iota(I32, (1, LANES), 1) < HEAD_DIM
        drow = _pair_lanes(decq[0:1, :], c0, c1)
        s_old = self.sv[idx]
        carried = _dot(qp.astype(BF16), s_old.astype(BF16)) * eqp
        u = _dot_tn(kw.astype(BF16), vp.astype(BF16))
        same_head = ((lax.broadcasted_iota(I32, (LANES, LANES), 0) < HEAD_DIM)
                     == (lax.broadcasted_iota(I32, (LANES, LANES), 1) < HEAD_DIM))
        self.sv[idx] = s_old * drow + jnp.where(same_head, u, 0.0)
        qn = None
        if ml:
            p = idx - 6
            n_old = self.nrow[p:p + 1, :]
            qn_l = qp * n_old
            qn = (jnp.sum(jnp.where(lm0, qn_l, 0.0), axis=-1, keepdims=True),
                  jnp.sum(jnp.where(lm0, 0.0, qn_l), axis=-1, keepdims=True))
            self.nrow[p:p + 1, :] = n_old * drow + jnp.sum(kw, axis=0, keepdims=True)
        return carried, qn


def _mixer_prompt_kernel(*refs):
    sq = SEQ_PER_STEP
    proj_refs = refs[:sq]
    (cos_ref, sin_ref, ptab_ref, cw_ref, cb_ref, dsk_ref, gs_ref, gr_ref, gm_ref,
     y_ref, sv_o, conv_o, n_o, m_o) = refs[sq:sq + 14]
    scr = refs[sq + 14:]
    sv, nrow, mrow, cbuf = scr[0:sq], scr[sq:2 * sq], scr[2 * sq:3 * sq], scr[3 * sq:4 * sq]
    ret_cache = scr[4 * sq]
    ci = pl.program_id(1)
    rows = proj_refs[0].shape[1]
    ri = lax.broadcasted_iota(I32, (rows, rows), 0)
    cj = lax.broadcasted_iota(I32, (rows, rows), 1)
    mask = cj <= ri
    tril = jnp.where(mask, 1.0, 0.0).astype(BF16)
    last_row = lambda a: a[rows - 1:rows, :]

    @pl.when(ci == 0)
    def _():
        for s in range(sq):
            sv[s][...] = jnp.zeros_like(sv[s])
            nrow[s][...] = jnp.zeros_like(nrow[s])
            mrow[s][...] = jnp.zeros_like(mrow[s])
            cbuf[s][0:8, :] = jnp.zeros((8, SSD_XBC), F32)
        for p, fs in enumerate(_ret_factors(ptab_ref[0], tril, mask, last_row)):
            for k, f in enumerate(fs):
                ret_cache[4 * p + k] = f

    ret_factors = lambda p: tuple(ret_cache[4 * p + k] for k in range(4))
    gens = []
    for s in range(sq):
        proj_ref = proj_refs[s]
        cbuf[s][8:8 + rows, :] = proj_ref[0, :, P_XBC:P_RET]
        us = [cbuf[s][pl.ds(5 + k, rows), :] for k in range(SSD_CONV)]
        gens.append(_mixer_core(
            proj_ref[0, :, P_Z:P_XBC], us, proj_ref[0, :, P_RET:P_ML], proj_ref[0, :, P_ML:P_SM],
            proj_ref[0, :, P_SM:P_W], cos_ref[...], sin_ref[...], ptab_ref[0], cw_ref[0], cb_ref[0],
            dsk_ref[0], gs_ref[0], gr_ref[0], gm_ref[0],
            mask, tril, rows, mrow[s][0:1, :], last_row,
            _CarriedState(sv[s], nrow[s]), y_ref.at[s], ret_factors))
    for s, m_t in enumerate(_run_interleaved(gens)):
        mrow[s][0:1, :] = m_t[rows - 1:rows, :]
        cbuf[s][0:8, :] = cbuf[s][rows:rows + 8, :]

    @pl.when(ci == pl.num_programs(1) - 1)
    def _():
        top = _half_rows()
        for s in range(sq):
            for idx in range(N_PAIRS):
                bd = sv[s][idx]
                sv_o[s, idx] = jnp.where(top, bd[:, :HEAD_DIM], bd[:, HEAD_DIM:])
            conv_o[s] = cbuf[s][0:8, :]
            n_o[s] = nrow[s][...]
            m_o[s] = mrow[s][...]


def _mixer_prompt(proj, nb, t, cos, sin, consts, l):
    rows = math.gcd(t, CHUNK)
    nc = t // rows
    sq = SEQ_PER_STEP
    assert nb % sq == 0

    def proj_spec(s):
        return pl.BlockSpec((1, rows, P_W), lambda b, c: (0, (sq * b + s) * nc + c, 0))

    def seq_spec(*tail):
        return pl.BlockSpec((sq,) + tail, lambda b, c: (b,) + (0,) * len(tail))

    outs = pl.pallas_call(
        _mixer_prompt_kernel,
        grid=(nb // sq, nc),
        in_specs=[proj_spec(s) for s in range(sq)]
                 + [pl.BlockSpec((rows, RET_WIDTH), lambda b, c: (c, 0)),
                    pl.BlockSpec((rows, RET_WIDTH), lambda b, c: (c, 0))] + [_layer_spec(a, l) for a in consts],
        out_specs=[pl.BlockSpec((sq, rows, D_MODEL), lambda b, c: (b, c, 0)),
                   seq_spec(N_PAIRS, LANES, HEAD_DIM), seq_spec(8, SSD_XBC), seq_spec(8, LANES), seq_spec(8, LANES)],
        out_shape=[jax.ShapeDtypeStruct((nb, t, D_MODEL), BF16),
                   jax.ShapeDtypeStruct((nb, N_PAIRS, LANES, HEAD_DIM), F32),
                   jax.ShapeDtypeStruct((nb, 8, SSD_XBC), F32),
                   jax.ShapeDtypeStruct((nb, 8, LANES), F32),
                   jax.ShapeDtypeStruct((nb, 8, LANES), F32)],
        scratch_shapes=[pltpu.VMEM((N_PAIRS, LANES, LANES), F32) for _ in range(sq)]
                       + [pltpu.VMEM((8, LANES), F32) for _ in range(2 * sq)]
                       + [pltpu.VMEM((rows + 8, SSD_XBC), F32) for _ in range(sq)]
                       + [pltpu.VMEM((2 * RET_HEADS, rows, LANES), F32)],
        compiler_params=_cparams(("arbitrary", "arbitrary")),
        name="mixer_prompt",
    )(*([proj[None]] * sq), cos, sin, *consts)
    y, sv, conv, n, m = outs
    sv = sv.reshape(nb, 2 * N_PAIRS, HEAD_DIM, HEAD_DIM)
    states = (sv[:, :8], conv[:, 5:8], sv[:, 8:12], sv[:, 12:16],
              n[:, 0:2].reshape(nb, ML_HEADS, HEAD_DIM), m[:, 0, C_ML:C_ML + 4])
    return y.reshape(nb * t, D_MODEL), states


def _lane_recurrence(s_in, s_out, q_s, k_s, v_s, decay):
    n_tok = len(decay)
    nvb = HEAD_DIM // 8
    nb = q_s.shape[-1]
    dec8 = [jnp.broadcast_to(d, (8, nb)) for d in decay]

    def body(k, acc):
        acc = [list(a) for a in acc]
        qk = [q_s[t, pl.ds(k, 1), :] for t in range(n_tok)]
        kk = [k_s[t, pl.ds(k, 1), :] for t in range(n_tok)]
        for vb in range(nvb):
            rows = pl.ds(8 * vb, 8)
            s = s_in[k, rows, :]
            for t in range(n_tok):
                s = s * dec8[t] + kk[t] * v_s[t, rows, :]
                acc[t][vb] = acc[t][vb] + qk[t] * s
            s_out[k, rows, :] = s
        return tuple(tuple(a) for a in acc)

    init = tuple(tuple(jnp.zeros((8, nb), F32) for _ in range(nvb)) for _ in range(n_tok))
    acc = lax.fori_loop(0, HEAD_DIM, body, init, unroll=2)
    return [jnp.concatenate(list(a), axis=0) for a in acc]


def _conv_T(u_ref, tail_ref, w_ref, b_ref, state_ref):
    n_tok = u_ref.shape[0]
    full = [tail_ref[0, j] for j in range(SSD_CONV - 1)] + [u_ref[t] for t in range(n_tok)]
    outs = []
    for t in range(n_tok):
        acc = b_ref[0]
        for tap in range(SSD_CONV):
            acc = acc + full[t + tap] * w_ref[0, tap]
        outs.append(_silu(acc))
    for j in range(SSD_CONV - 1):
        state_ref[j] = full[n_tok + j]
    return outs


def _ssd_T_kernel(z_ref, xs_ref, b_ref, c_ref, sm_ref, tx_ref, tb_ref, tc_ref, wx_ref, wb_ref, wc_ref,
                  bx_ref, bb_ref, bc_ref, dtb_ref, alog_ref, dsk_ref, g_ref, s_ref,
                  y_ref, so_ref, cx_ref, cb_ref, cc_ref, q_s, k_s, v_s):
    h = pl.program_id(0)
    n_tok = xs_ref.shape[0]
    xs = _conv_T(xs_ref, tx_ref, wx_ref, bx_ref, cx_ref)
    bm = _conv_T(b_ref, tb_ref, wb_ref, bb_ref, cb_ref)
    cm = _conv_T(c_ref, tc_ref, wc_ref, bc_ref, cc_ref)
    a_neg = -jnp.exp(alog_ref[0, 0])
    decay = []
    for t in range(n_tok):
        dt = _softplus(sm_ref[t, pl.ds(C_SSD + h, 1), :] + dtb_ref[0, 0])
        decay.append(jnp.exp(dt * a_neg))
        q_s[t] = cm[t]
        k_s[t] = bm[t]
        v_s[t] = xs[t] * dt
    o = _lane_recurrence(s_ref.at[0, 0], so_ref.at[0, 0], q_s, k_s, v_s, decay)
    hh = h % (SSD_HEADS // SSD_GROUPS)
    row0 = pl.multiple_of(hh * HEAD_DIM, HEAD_DIM)
    for t in range(n_tok):
        y_ref[t, pl.ds(row0, HEAD_DIM), :] = (o[t] + dsk_ref[0, 0] * xs[t]) * _silu(z_ref[t])

    @pl.when(hh == SSD_HEADS // SSD_GROUPS - 1)
    def _():
        for t in range(n_tok):
            blk = y_ref[t]
            ms = jnp.mean(blk * blk, axis=0, keepdims=True)
            y_ref[t] = blk * lax.rsqrt(ms + EPS) * g_ref[0]


def _ret_T_kernel(q_ref, k_ref, v_ref, g_ref, cos_ref, sin_ref, gam_ref, gn_ref, s_ref,
                  y_ref, so_ref, q_s, k_s, v_s):
    n_tok = q_ref.shape[0]
    half = HEAD_DIM // 2

    def rope(x, t):
        x1, x2 = x[:half], x[half:]
        return jnp.concatenate([x1 * cos_ref[t] - x2 * sin_ref[t], x1 * sin_ref[t] + x2 * cos_ref[t]], axis=0)

    for t in range(n_tok):
        q_s[t] = rope(q_ref[t], t)
        k_s[t] = rope(k_ref[t], t) * (HEAD_DIM ** -0.5)
        v_s[t] = v_ref[t]
    o = _lane_recurrence(s_ref.at[0, 0], so_ref.at[0, 0], q_s, k_s, v_s, [gam_ref[0]] * n_tok)
    for t in range(n_tok):
        ms = jnp.mean(o[t] * o[t], axis=0, keepdims=True)
        y_ref[t] = o[t] * lax.rsqrt(ms + EPS) * gn_ref[0] * _silu(g_ref[t])


def _ml_T_kernel(q_ref, k_ref, v_ref, o_ref, sm_ref, bi_ref, bf_ref, gn_ref, c_ref, n_ref, m_ref,
                 y_ref, co_ref, no_ref, mo_ref, q_s, k_s, v_s):
    h = pl.program_id(0)
    n_tok = q_ref.shape[0]
    m = m_ref[0, 0]
    n = n_ref[0, 0]
    decay, qn, m_all = [], [], []
    for t in range(n_tok):
        i_t = sm_ref[t, pl.ds(C_RET + h, 1), :] + bi_ref[0, 0]
        logf = -_softplus(-(sm_ref[t, pl.ds(C_ML + h, 1), :] + bf_ref[0, 0]))
        m_new = jnp.maximum(logf + m, i_t)
        f_t = jnp.exp(logf + m - m_new)
        kw = k_ref[t] * (HEAD_DIM ** -0.5) * jnp.exp(i_t - m_new)
        n = n * f_t + kw
        q_s[t] = q_ref[t]
        k_s[t] = kw
        v_s[t] = v_ref[t]
        decay.append(f_t)
        qn.append(jnp.sum(q_ref[t] * n, axis=0, keepdims=True))
        m_all.append(m_new)
        m = m_new
    num = _lane_recurrence(c_ref.at[0, 0], co_ref.at[0, 0], q_s, k_s, v_s, decay)
    no_ref[0] = n
    mo_ref[0] = m
    for t in range(n_tok):
        hh = num[t] / jnp.maximum(jnp.abs(qn[t]), jnp.exp(-m_all[t]))
        ms = jnp.mean(hh * hh, axis=0, keepdims=True)
        y_ref[t] = hh * lax.rsqrt(ms + EPS) * gn_ref[0] * jax.nn.sigmoid(o_ref[t])


def _mixer_sample_T(projT, convT, cwb, cbb, ssdT, retT, cT, nT, mT, tabs, cos, sin, l):
    n_tok, _, nb = projT.shape
    dtb, alog, dsk, gs, gam, gr, bi, bf, gm = tabs
    hb = lambda col: col // HEAD_DIM
    rep = SSD_HEADS // SSD_GROUPS
    proj_blk = lambda first, div=1: pl.BlockSpec((n_tok, HEAD_DIM, nb), lambda h: (0, first + h // div, 0))
    small = pl.BlockSpec((n_tok, LANES, nb), lambda h: (0, P_SM // LANES, 0))
    lay4 = lambda a, first=0, div=1: pl.BlockSpec((1,) + a.shape[1:2] + (HEAD_DIM, a.shape[-1]),
                                                  lambda h: (l, 0, first + h // div, 0))
    chan = lambda a, first=0, div=1, n=1: pl.BlockSpec((1, n * HEAD_DIM, 1), lambda h: (l, first + h // div, 0))
    head4 = lambda a: pl.BlockSpec((1, 1) + a.shape[2:], lambda h: (l, h) + (0,) * (a.ndim - 2))
    vm = lambda: pltpu.VMEM((n_tok, HEAD_DIM, nb), F32)
    state_spec = pl.BlockSpec((1, 1, HEAD_DIM, HEAD_DIM, nb), lambda h: (0, h, 0, 0, 0))
    stacked = lambda heads: jax.ShapeDtypeStruct((1, heads, HEAD_DIM, HEAD_DIM, nb), F32)
    b0, c0 = hb(SSD_WIDTH), hb(SSD_WIDTH + SSD_GROUPS * SSD_STATE)

    y_ssd, ssd_n, cx, cb, cc = pl.pallas_call(
        _ssd_T_kernel,
        grid=(SSD_HEADS,),
        in_specs=[proj_blk(hb(P_Z)), proj_blk(hb(P_XBC)), proj_blk(hb(P_XBC) + b0, rep),
                  proj_blk(hb(P_XBC) + c0, rep), small,
                  lay4(convT), lay4(convT, b0, rep), lay4(convT, c0, rep),
                  lay4(cwb), lay4(cwb, b0, rep), lay4(cwb, c0, rep),
                  chan(cbb), chan(cbb, b0, rep), chan(cbb, c0, rep),
                  head4(dtb), head4(alog), head4(dsk), chan(gs, 0, rep, rep), head4(ssdT)],
        out_specs=[pl.BlockSpec((n_tok, rep * HEAD_DIM, nb), lambda h: (0, h // rep, 0)), state_spec,
                   pl.BlockSpec((SSD_CONV - 1, HEAD_DIM, nb), lambda h: (0, h, 0)),
                   pl.BlockSpec((SSD_CONV - 1, HEAD_DIM, nb), lambda h: (0, h // rep, 0)),
                   pl.BlockSpec((SSD_CONV - 1, HEAD_DIM, nb), lambda h: (0, h // rep, 0))],
        out_shape=[jax.ShapeDtypeStruct((n_tok, SSD_WIDTH, nb), F32),
                   stacked(SSD_HEADS),
                   jax.ShapeDtypeStruct((SSD_CONV - 1, SSD_WIDTH, nb), F32),
                   jax.ShapeDtypeStruct((SSD_CONV - 1, SSD_GROUPS * SSD_STATE, nb), F32),
                   jax.ShapeDtypeStruct((SSD_CONV - 1, SSD_GROUPS * SSD_STATE, nb), F32)],
        scratch_shapes=[vm(), vm(), vm()],
        compiler_params=_cparams(("arbitrary",)),
        name="sample_ssd",
    )(projT, projT, projT, projT, projT, convT, convT, convT, cwb, cwb, cwb, cbb, cbb, cbb,
      dtb, alog, dsk, gs, ssdT)

    y_ret, ret_n = pl.pallas_call(
        _ret_T_kernel,
        grid=(RET_HEADS,),
        in_specs=[proj_blk(hb(P_RET)), proj_blk(hb(P_RET + RET_WIDTH)), proj_blk(hb(P_RET + 2 * RET_WIDTH)),
                  proj_blk(hb(P_RET + 3 * RET_WIDTH)), _const_spec(cos), _const_spec(sin),
                  pl.BlockSpec((1, 1, 1), lambda h: (h, 0, 0)), chan(gr), head4(retT)],
        out_specs=[pl.BlockSpec((n_tok, HEAD_DIM, nb), lambda h: (0, h, 0)), state_spec],
        out_shape=[jax.ShapeDtypeStruct((n_tok, RET_WIDTH, nb), F32),
                   stacked(RET_HEADS)],
        scratch_shapes=[vm(), vm(), vm()],
        compiler_params=_cparams(("arbitrary",)),
        name="sample_ret",
    )(projT, projT, projT, projT, cos, sin, gam, gr, retT)

    y_ml, c_n, n_n, m_n = pl.pallas_call(
        _ml_T_kernel,
        grid=(ML_HEADS,),
        in_specs=[proj_blk(hb(P_ML)), proj_blk(hb(P_ML + ML_WIDTH)), proj_blk(hb(P_ML + 2 * ML_WIDTH)),
                  proj_blk(hb(P_ML + 3 * ML_WIDTH)), small, head4(bi), head4(bf),
                  chan(gm), head4(cT), head4(nT), head4(mT)],
        out_specs=[pl.BlockSpec((n_tok, HEAD_DIM, nb), lambda h: (0, h, 0)), state_spec,
                   pl.BlockSpec((1, HEAD_DIM, nb), lambda h: (h, 0, 0)),
                   pl.BlockSpec((1, 1, nb), lambda h: (h, 0, 0))],
        out_shape=[jax.ShapeDtypeStruct((n_tok, ML_WIDTH, nb), F32),
                   stacked(ML_HEADS),
                   jax.ShapeDtypeStruct((ML_HEADS, HEAD_DIM, nb), F32),
                   jax.ShapeDtypeStruct((ML_HEADS, 1, nb), F32)],
        scratch_shapes=[vm(), vm(), vm()],
        compiler_params=_cparams(("arbitrary",)),
        name="sample_mlstm",
    )(projT, projT, projT, projT, projT, bi, bf, gm, cT, nT, mT)

    y_t = jnp.concatenate([y_ssd, y_ret, y_ml], axis=1)
    conv_n = jnp.concatenate([cx, cb, cc], axis=1)
    return y_t, (ssd_n, conv_n, ret_n, c_n, n_n, m_n)


def _route(logits):
    lane = lax.broadcasted_iota(I32, logits.shape, 1)
    gmask = (lane >= N_EXPERTS) & (lane < N_EXPERTS + EXPERT_GROUPS)
    gl = jnp.where(gmask, logits, -jnp.inf)
    ge = jnp.exp(gl - jnp.max(gl, axis=-1, keepdims=True))
    gprob = ge / jnp.sum(ge, axis=-1, keepdims=True)
    g_w = jnp.max(gprob, axis=-1, keepdims=True)
    g_idx = jnp.min(jnp.where(gmask & (gprob == g_w), lane - N_EXPERTS, LANES), axis=-1, keepdims=True)
    emask = (lane < N_EXPERTS) & ((lane >> 3) == g_idx)
    el = jnp.where(emask, logits, -jnp.inf)
    ee = jnp.exp(el - jnp.max(el, axis=-1, keepdims=True))
    eprob = ee / jnp.sum(ee, axis=-1, keepdims=True)
    p1 = jnp.max(jnp.where(emask, eprob, -1.0), axis=-1, keepdims=True)
    i1 = jnp.min(jnp.where(emask & (eprob == p1), lane, LANES), axis=-1, keepdims=True)
    rest = emask & (lane != i1)
    p2 = jnp.max(jnp.where(rest, eprob, -1.0), axis=-1, keepdims=True)
    i2 = jnp.min(jnp.where(rest & (eprob == p2), lane, LANES), axis=-1, keepdims=True)
    tot = p1 + p2
    lo, hi = jnp.minimum(i1, i2) & 7, jnp.maximum(i1, i2) & 7
    key = g_idx * PAIRS_PER_GROUP + ((lo * (13 - lo)) >> 1) + hi - 1
    return g_w * (p1 / tot), g_w * (p2 / tot), i1, i2, key


def _outproj_kernel(yp, ys, xp, xs, gtp, gts, scp, scs, shp, shs, g_ref, w_ref, wr_ref, br_ref,
                    x1_ref, hx_ref, info_ref, cnt_ref, carry, *, np_tiles, tp):
    i = pl.program_id(0)
    is_p = i < np_tiles

    @pl.when(i == 0)
    def _():
        carry[...] = jnp.zeros_like(carry)

    tm = x1_ref.shape[0]
    nblk = OUTPROJ_ROW_BLOCKS
    rb = tm // nblk
    blocks = [slice(k * rb, (k + 1) * rb) for k in range(nblk)]

    def body(x_ref, y_ref, gt, sc, sh):
        mod = lambda m, b: m if m.shape[0] == 1 else m[b]
        x1s = [x_ref[b, :] + mod(gt, b) * _dot(y_ref[b, :], w_ref[0]) for b in blocks]
        for b, x1 in zip(blocks, x1s):
            x1_ref[b, :] = x1
        h2bs = [(_rms(x1) * g_ref[0] * (1.0 + mod(sc, b)) + mod(sh, b)).astype(BF16)
                for b, x1 in zip(blocks, x1s)]
        routes = [_route(_dot(h2b, wr_ref[0]) + br_ref[0]) for h2b in h2bs]

        tril = (lax.broadcasted_iota(I32, (rb, rb), 1) <= lax.broadcasted_iota(I32, (rb, rb), 0))
        tril = jnp.where(tril, 1.0, 0.0).astype(BF16)
        onehots = [lax.broadcasted_iota(I32, (rb, N_CLASS), 1) == r[4] for r in routes]
        uptos = [_dot(tril, jnp.where(oh, 1.0, 0.0).astype(BF16)) for oh in onehots]
        seen = carry[0:1, :]
        ranks = []
        for oh, upto in zip(onehots, uptos):
            ranks.append(jnp.sum(jnp.where(oh, upto - 1.0 + seen, 0.0), axis=-1, keepdims=True))
            seen = seen + upto[rb - 1:rb, :]
        carry[0:1, :] = seen
        cnt_ref[...] = jnp.broadcast_to(seen, cnt_ref.shape)

        lane = lax.broadcasted_iota(I32, (rb, LANES), 1)
        for b, h2b, (w1, w2, i1, i2, key), rank in zip(blocks, h2bs, routes, ranks):
            info = jnp.zeros((rb, LANES), F32)
            for c, v in enumerate((w1, w2, i1.astype(F32), i2.astype(F32), key.astype(F32), rank)):
                info = jnp.where(lane == c, v, info)
            hx_ref[b, 0:D_MODEL] = h2b.astype(F32)
            hx_ref[b, D_MODEL:H_EXT] = info
            info_ref[b, :] = info[:, 0:8]

    @pl.when(is_p)
    def _():
        body(xp, yp, _prompt_mod(gtp, tm, tp), _prompt_mod(scp, tm, tp), _prompt_mod(shp, tm, tp))

    @pl.when(jnp.logical_not(is_p))
    def _():
        body(xs, ys, _sample_mod(gts, tm), _sample_mod(scs, tm), _sample_mod(shs, tm))


def _outproj(y, x, mod, k_gt, k_sc, k_sh, g, w, wr, br, l, tp):
    n_prompt, ns = x[0].shape[0], x[1].shape[0]
    n = n_prompt + ns
    tm = TOKEN_TILE
    npt = n_prompt // tm
    ms = [s for k in (k_gt, k_sc, k_sh) for s in _mod_specs(mod, k, l, tm, n_prompt, tp)]
    return pl.pallas_call(
        functools.partial(_outproj_kernel, np_tiles=npt, tp=tp),
        grid=(n // tm,),
        in_specs=_dual_specs(tm, npt, D_MODEL) + _dual_specs(tm, npt, D_MODEL) + ms
                 + [_layer_spec(a, l) for a in (g, w, wr, br)],
        out_specs=[_row_spec(tm, D_MODEL), _row_spec(tm, H_EXT), _row_spec(tm, 8),
                   pl.BlockSpec((8, N_CLASS), lambda i: (0, 0))],
        out_shape=[jax.ShapeDtypeStruct((n, D_MODEL), F32),
                   jax.ShapeDtypeStruct((n, H_EXT), F32),
                   jax.ShapeDtypeStruct((n, 8), F32),
                   jax.ShapeDtypeStruct((8, N_CLASS), F32)],
        scratch_shapes=[pltpu.VMEM((8, N_CLASS), F32)],
        compiler_params=_cparams(("arbitrary",)),
        name="outproj_router",
    )(*y, *x, *([mod[0]] * 6), g, w, wr, br)


def _scatter_kernel(pos_ref, x_ref, o_hbm, buf, sem):
    i, n = pl.program_id(0), pl.num_programs(0)
    tm = x_ref.shape[0]
    slot = i % 2

    def wait_slot(s):
        pltpu.make_async_copy(buf.at[s], o_hbm.at[pl.ds(0, tm), :], sem.at[s]).wait()

    @pl.when(i >= 2)
    def _():
        wait_slot(slot)

    buf[slot] = x_ref[...]

    def body(r, c):
        dst = pos_ref[i * tm + r]
        pltpu.make_async_copy(buf.at[slot, pl.ds(r, 1), :], o_hbm.at[pl.ds(dst, 1), :], sem.at[slot]).start()
        return c
    lax.fori_loop(0, tm, body, 0, unroll=8)

    @pl.when(i == n - 1)
    def _():
        wait_slot(slot)

        @pl.when(n >= 2)
        def _():
            wait_slot(1 - slot)


def _scatter_rows(x, pos):
    n, width = x.shape
    tm = TOKEN_TILE
    return pl.pallas_call(
        _scatter_kernel,
        grid_spec=pltpu.PrefetchScalarGridSpec(
            num_scalar_prefetch=1, grid=(n // tm,),
            in_specs=[_row_spec(tm, width)],
            out_specs=pl.BlockSpec(memory_space=pl.ANY),
            scratch_shapes=[pltpu.VMEM((2, tm, width), x.dtype), pltpu.SemaphoreType.DMA((2,))]),
        out_shape=jax.ShapeDtypeStruct((n, width), x.dtype),
        compiler_params=_cparams(("arbitrary",)),
        name="scatter_rows",
    )(pos, x)


def _gather_tiles(idx_ref, src_hbm, buf, sem, tm):
    i, n = pl.program_id(0), pl.num_programs(0)

    def issue(tile, slot):
        def body(r, c):
            row = idx_ref[tile * tm + r]
            pltpu.make_async_copy(src_hbm.at[pl.ds(row, 1), :], buf.at[slot, pl.ds(r, 1), :],
                                  sem.at[slot]).start()
            return c
        lax.fori_loop(0, tm, body, 0, unroll=8)

    @pl.when(i == 0)
    def _():
        issue(0, 0)

    @pl.when(i + 1 < n)
    def _():
        issue(i + 1, (i + 1) % 2)

    slot = i % 2
    pltpu.make_async_copy(src_hbm.at[pl.ds(0, tm), :], buf.at[slot], sem.at[slot]).wait()
    return slot


def _experts_kernel(tile_ref, grp_ref, flag_ref, hx_ref, wgu_ref, wd_ref, o_ref):
    j = pl.program_id(0)
    flags = flag_ref[j]
    valid = (flags & 1) != 0

    @pl.when(valid & ((flags & 2) != 0))
    def _():
        o_ref[...] = jnp.zeros_like(o_ref)

    @pl.when(valid)
    def _():
        h = hx_ref[:, 0:D_MODEL].astype(BF16)
        r = hx_ref[:, D_MODEL:H_EXT]
        w1, w2, i1, i2 = r[:, 0:1], r[:, 1:2], r[:, 2:3], r[:, 3:4]
        base = grp_ref[j] * EXPERTS_PER_GROUP
        for e in range(EXPERTS_PER_GROUP):
            @pl.when(((flags >> (8 + e)) & 1) != 0)
            def _():
                eid = (base + e).astype(F32)
                ge = jnp.where(i1 == eid, w1, 0.0) + jnp.where(i2 == eid, w2, 0.0)
                nblk = EXPERT_ROW_BLOCKS
                rb = h.shape[0] // nblk
                aus = [_dot(h[k * rb:(k + 1) * rb], wgu_ref[0, 0, e]) for k in range(nblk)]
                acts = [(_silu(au[:, :EXPERT_FF]) * au[:, EXPERT_FF:]).astype(BF16) for au in aus]
                yes = [_dot(act, wd_ref[0, 0, e]) for act in acts]
                for k in range(nblk):
                    o_ref[k * rb:(k + 1) * rb, :] += ge[k * rb:(k + 1) * rb] * yes[k]


def _experts(hx_sorted, tile, grp, flags, wgu, wd, l):
    n = hx_sorted.shape[0]
    tm = MOE_TILE
    return pl.pallas_call(
        _experts_kernel,
        grid_spec=pltpu.PrefetchScalarGridSpec(
            num_scalar_prefetch=3, grid=(tile.shape[0],),
            in_specs=[pl.BlockSpec((tm, H_EXT), lambda j, t, g, f: (t[j], 0)),
                      pl.BlockSpec((1, 1, EXPERTS_PER_GROUP, D_MODEL, 2 * EXPERT_FF),
                                   lambda j, t, g, f: (l, g[j], 0, 0, 0)),
                      pl.BlockSpec((1, 1, EXPERTS_PER_GROUP, EXPERT_FF, D_MODEL),
                                   lambda j, t, g, f: (l, g[j], 0, 0, 0))],
            out_specs=pl.BlockSpec((tm, D_MODEL), lambda j, t, g, f: (t[j], 0))),
        out_shape=jax.ShapeDtypeStruct((n, D_MODEL), F32),
        compiler_params=_cparams(("arbitrary",)),
        name="moe_experts",
    )(tile, grp, flags, hx_sorted, wgu, wd)


def _combine_kernel(pos_ref, y_hbm, x1_ref, gtp, gts, gf_ref, op_ref, os_ref, buf, sem, *, np_tiles, tp, final):
    slot = _gather_tiles(pos_ref, y_hbm, buf, sem, x1_ref.shape[0])
    is_p = pl.program_id(0) < np_tiles
    x2 = x1_ref[...] + _mod_val(gtp, gts, is_p, x1_ref.shape[0], tp) * buf[slot]
    out = _rms(x2) * gf_ref[...] if final else x2

    @pl.when(is_p)
    def _():
        op_ref[...] = out

    @pl.when(jnp.logical_not(is_p))
    def _():
        os_ref[...] = out


def _combine(y_sorted, pos, x1, mod, k_gt, gf, l, n_prompt, tp, final):
    n = x1.shape[0]
    tm = TOKEN_TILE
    npt = n_prompt // tm
    return pl.pallas_call(
        functools.partial(_combine_kernel, np_tiles=npt, tp=tp, final=final),
        grid_spec=pltpu.PrefetchScalarGridSpec(
            num_scalar_prefetch=1, grid=(n // tm,),
            in_specs=[pl.BlockSpec(memory_space=pl.ANY), _row_spec(tm, D_MODEL)]
                     + _mod_specs(mod, k_gt, l, tm, n_prompt, tp) + [_const_spec(gf)],
            out_specs=_dual_specs(tm, npt, D_MODEL),
            scratch_shapes=[pltpu.VMEM((2, tm, D_MODEL), F32), pltpu.SemaphoreType.DMA((2,))]),
        out_shape=[jax.ShapeDtypeStruct((n_prompt, D_MODEL), F32),
                   jax.ShapeDtypeStruct((n - n_prompt, D_MODEL), F32)],
        compiler_params=_cparams(("arbitrary",)),
        name="moe_combine",
    )(pos, y_sorted, x1, mod[0], mod[0], gf)


def _class_members():
    table = np.zeros((N_CLASS, N_EXPERTS), dtype=bool)
    for grp in range(EXPERT_GROUPS):
        for lo in range(EXPERTS_PER_GROUP):
            for hi in range(lo + 1, EXPERTS_PER_GROUP):
                cls = grp * PAIRS_PER_GROUP + ((lo * (13 - lo)) >> 1) + hi - 1
                table[cls, grp * EXPERTS_PER_GROUP + lo] = True
                table[cls, grp * EXPERTS_PER_GROUP + hi] = True
    return table


def _routing_tables(cnt, key, rank, n_tiles, tm):
    n_items = n_tiles + EXPERT_GROUPS - 1
    c = jnp.arange(N_CLASS, dtype=I32)
    start_c = jnp.sum(jnp.where(c[:, None] < c[None, :], cnt[:, None], 0), axis=0)
    pos = jnp.sum(jnp.where(key[:, None] == c[None, :], start_c[None, :], 0), axis=1) + rank

    g = jnp.arange(EXPERT_GROUPS, dtype=I32)
    gend = jnp.sum(jnp.where(c[None, :] < PAIRS_PER_GROUP * (g[:, None] + 1), cnt[None, :], 0), axis=1)
    t0 = jnp.arange(n_tiles, dtype=I32) * tm
    t1 = t0 + (tm - 1)
    gfirst = jnp.sum((gend[None, :] <= t0[:, None]).astype(I32), axis=1)
    glast = jnp.sum((gend[None, :] <= t1[:, None]).astype(I32), axis=1)
    per_tile = glast - gfirst + 1
    tt = jnp.arange(n_tiles, dtype=I32)
    start_t = jnp.sum(jnp.where(tt[:, None] < tt[None, :], per_tile[:, None], 0), axis=0)
    total = jnp.sum(per_tile)

    j = jnp.arange(n_items, dtype=I32)
    valid = j < total
    tile = jnp.sum((start_t[None, :] <= j[:, None]).astype(I32), axis=1) - 1
    tile = jnp.where(valid, tile, n_tiles - 1)
    sel = tile[:, None] == tt[None, :]
    pick = lambda v: jnp.sum(jnp.where(sel, v[None, :], 0), axis=1)
    grp = jnp.where(valid, pick(gfirst) + (j - pick(start_t)), glast[n_tiles - 1])
    first = valid & (j == pick(start_t))

    in_tile = (cnt[None, :] > 0) & (start_c[None, :] <= t1[:, None]) & ((start_c + cnt)[None, :] > t0[:, None])
    e = jnp.arange(N_EXPERTS, dtype=I32)
    member = jnp.asarray(_class_members())
    present_t = jnp.any(in_tile[:, :, None] & member[None, :, :], axis=1)
    present_j = jnp.any(sel[:, :, None] & present_t[None, :, :], axis=1)
    eg = e[None, :] - grp[:, None] * EXPERTS_PER_GROUP
    bits = jnp.sum(jnp.where(present_j & (eg >= 0) & (eg < EXPERTS_PER_GROUP),
                             1 << (8 + jnp.clip(eg, 0, EXPERTS_PER_GROUP - 1)), 0), axis=1)
    flags = valid.astype(I32) | (first.astype(I32) << 1) | bits
    return pos.astype(I32), tile.astype(I32), grp.astype(I32), flags.astype(I32)


def _moe(hx, info, counts, x1, mod, k_gt, gf, wgu, wd, l, n_prompt, tp, final):
    n = hx.shape[0]
    tm = MOE_TILE
    pos, tile, grp, flags = _routing_tables(counts[0].astype(I32), info[:, 4].astype(I32),
                                            info[:, 5].astype(I32), n // tm, tm)
    y_sorted = _experts(_scatter_rows(hx, pos), tile, grp, flags, wgu, wd, l)
    return _combine(y_sorted, pos, x1, mod, k_gt, gf, l, n_prompt, tp, final)


def _rope_tables(pos):
    half = HEAD_DIM // 2
    inv = ROPE_BASE ** (-jnp.arange(half, dtype=F32) / half)
    ang = pos.astype(F32)[:, None] * inv[None, :]
    cos, sin = jnp.cos(ang), jnp.sin(ang)
    cos_t = jnp.tile(jnp.concatenate([cos, cos], axis=-1), (1, RET_HEADS))
    sin_t = jnp.tile(jnp.concatenate([-sin, sin], axis=-1), (1, RET_HEADS))
    return cos_t, sin_t


def _mixer_consts(w_in, conv_w, conv_b, dt_bias, a_log, d_skip, g_ssd_norm, g_ret_norm,
                  b_mlstm_i, b_mlstm_f, g_mlstm_norm):
    nl = w_in.shape[0]
    w_t = jnp.transpose(w_in, (0, 2, 1))
    w_p = jnp.concatenate([w_t[:, 0:1280], w_t[:, 1288:3336], w_t[:, 1280:1288], w_t[:, 3336:3344],
                           jnp.zeros((nl, LANES - 16, D_MODEL), F32)], axis=1).astype(BF16)
    pad = lambda v: jnp.pad(v, ((0, 0), (0, LANES - v.shape[1])))
    log_gamma = jnp.log(1.0 - 2.0 ** (-5.0 - jnp.arange(RET_HEADS, dtype=F32)))
    lg = jnp.broadcast_to(jnp.concatenate([jnp.zeros((8,), F32), log_gamma])[None, :], (nl, 12))
    ptab = jnp.stack([pad(jnp.concatenate([dt_bias, b_mlstm_i, b_mlstm_f], axis=1)), pad(a_log), pad(lg)]
                     + [jnp.zeros((nl, LANES), F32)] * 5, axis=1)
    consts = (ptab, conv_w, conv_b[:, None, :], jnp.repeat(d_skip, HEAD_DIM, axis=1)[:, None, :],
              g_ssd_norm[:, None, :], g_ret_norm[:, None, :], g_mlstm_norm[:, None, :])
    return w_p, consts


def kernel(x_prompt, x_sample, state_ssd, state_ssd_conv, state_ret, state_mlstm_c, state_mlstm_n,
           state_mlstm_m, c_prompt, c_sample, w_ada, b_ada, g_norm1, g_norm2, w_in, conv_w, conv_b,
           dt_bias, a_log, d_skip, g_ssd_norm, g_ret_norm, b_mlstm_i, b_mlstm_f, g_mlstm_norm, w_out,
           w_router_group, b_router_group, w_router_expert, b_router_expert, w_gate_up, w_down, g_final):
    bp, tp, _ = x_prompt.shape
    bs, ts, _ = x_sample.shape
    n_srows = bs * ts
    n_prompt = bp * tp

    mod4 = _ada(jnp.concatenate([c_sample, c_prompt], axis=0), w_ada, b_ada)
    mod = (mod4, bs)
    sh1, sc1, gt1, sh2, sc2, gt2 = range(6)

    cos_p, sin_p = _rope_tables(jnp.arange(tp, dtype=I32))
    lanes = lambda v: v[..., None]
    half = HEAD_DIM // 2
    ang = (PAST_LEN + jnp.arange(ts, dtype=I32)).astype(F32)[:, None] * (
        ROPE_BASE ** (-jnp.arange(half, dtype=F32) / half))[None, :]
    cos_s, sin_s = lanes(jnp.cos(ang)), lanes(jnp.sin(ang))
    gamma = 1.0 - 2.0 ** (-5.0 - jnp.arange(RET_HEADS, dtype=F32))
    tabs = (lanes(dt_bias[:, :, None]), lanes(a_log[:, :, None]), lanes(d_skip[:, :, None]), lanes(g_ssd_norm),
            lanes(gamma[:, None]), lanes(g_ret_norm), lanes(b_mlstm_i[:, :, None]), lanes(b_mlstm_f[:, :, None]),
            lanes(g_mlstm_norm))
    cwb, cbb = lanes(conv_w), lanes(conv_b)
    ssd_t = jnp.transpose(state_ssd, (0, 2, 3, 4, 1))
    ret_t = jnp.transpose(state_ret, (0, 2, 3, 4, 1))
    c_t = jnp.transpose(state_mlstm_c, (0, 2, 3, 4, 1))
    n_t = jnp.transpose(state_mlstm_n, (0, 2, 3, 1))
    m_t = jnp.transpose(state_mlstm_m, (0, 2, 1))[:, :, None, :]
    conv_t = jnp.transpose(state_ssd_conv, (0, 2, 3, 1))

    w_p, consts = _mixer_consts(w_in, conv_w, conv_b, dt_bias, a_log, d_skip, g_ssd_norm, g_ret_norm,
                                b_mlstm_i, b_mlstm_f, g_mlstm_norm)
    w_o = w_out.astype(BF16)
    zpad = LANES - N_EXPERTS - EXPERT_GROUPS
    wr = jnp.concatenate([w_router_expert, w_router_group, jnp.zeros((DEPTH, D_MODEL, zpad), F32)],
                         axis=2).astype(BF16)
    br = jnp.concatenate([b_router_expert, b_router_group, jnp.zeros((DEPTH, zpad), F32)], axis=1)[:, None, :]
    wgu = w_gate_up.astype(BF16).reshape(DEPTH, EXPERT_GROUPS, EXPERTS_PER_GROUP, D_MODEL, 2 * EXPERT_FF)
    wd = w_down.astype(BF16).reshape(DEPTH, EXPERT_GROUPS, EXPERTS_PER_GROUP, EXPERT_FF, D_MODEL)
    g1, g2, gf = g_norm1[:, None, :], g_norm2[:, None, :], g_final[None, :]

    x = (x_prompt.reshape(n_prompt, D_MODEL), jnp.transpose(x_sample, (1, 0, 2)).reshape(n_srows, D_MODEL))
    p_states, s_states = [], []
    for l in range(DEPTH):
        final = l == DEPTH - 1
        proj = _inproj(x, mod, sc1, sh1, g1, w_p, l, tp)

        ycat_p, st = _mixer_prompt(proj, bp, tp, cos_p, sin_p, consts, l)
        p_states.append(st)

        proj_t = jnp.transpose(proj[n_prompt:].reshape(ts, bs, P_W), (0, 2, 1))
        y_t, st_t = _mixer_sample_T(proj_t, conv_t, cwb, cbb, ssd_t, ret_t, c_t, n_t, m_t, tabs, cos_s, sin_s, l)
        s_states.append(st_t)
        ycat_s = jnp.transpose(y_t, (0, 2, 1)).reshape(n_srows, D_MODEL).astype(BF16)

        x1, hx, info, counts = _outproj((ycat_p, ycat_s), x, mod, gt1, sc2, sh2, g2, w_o, wr, br, l, tp)
        x = _moe(hx, info, counts, x1, mod, gt2, gf, wgu, wd, l, n_prompt, tp, final)

    y_prompt = x[0].reshape(bp, tp, D_MODEL)
    y_sample = jnp.transpose(x[1].reshape(ts, bs, D_MODEL), (1, 0, 2))
    p_st = [jnp.stack([s[i] for s in p_states], axis=0) for i in range(6)]
    s_t = [jnp.concatenate([s[i] for s in s_states], axis=0) if i in (0, 2, 3)
           else jnp.stack([s[i] for s in s_states], axis=0) for i in range(6)]
    s_st = [jnp.transpose(s_t[0], (0, 4, 1, 2, 3)), jnp.transpose(s_t[1], (0, 3, 1, 2)),
            jnp.transpose(s_t[2], (0, 4, 1, 2, 3)), jnp.transpose(s_t[3], (0, 4, 1, 2, 3)),
            jnp.transpose(s_t[4], (0, 3, 1, 2)), jnp.transpose(s_t[5][:, :, 0, :], (0, 2, 1))]
    return (y_prompt, y_sample, *p_st, *s_st)
```

```python
import functools
import math

import jax
import jax.numpy as jnp
import numpy as np
from jax import lax
from jax.experimental import pallas as pl
from jax.experimental.pallas import tpu as pltpu

F32 = jnp.float32
BF16 = jnp.bfloat16
I32 = jnp.int32

D_MODEL = 1024
DEPTH = 2
PAST_LEN = 16384
SSD_HEADS = 8
SSD_WIDTH = 512
SSD_GROUPS = 2
SSD_STATE = 64
SSD_CONV = 4
SSD_XBC = 768
RET_HEADS = 4
RET_WIDTH = 256
ML_HEADS = 4
ML_WIDTH = 256
HEAD_DIM = 64
ROPE_BASE = 10000.0
EPS = 1e-6
EXPERT_GROUPS = 4
EXPERTS_PER_GROUP = 8
N_EXPERTS = 32
EXPERT_FF = 256

LANES = 128
CHUNK = 128
SEQ_PER_STEP = 4
P_Z, P_XBC, P_RET, P_ML, P_SM, P_W = 0, 512, 1280, 2304, 3328, 3456
C_SSD, C_RET, C_ML = 0, 8, 12
N_PAIRS = 8
NEG = -1e30
TOKEN_TILE = 512
MOE_TILE = 256
OUTPROJ_ROW_BLOCKS = 2
EXPERT_ROW_BLOCKS = 2
VMEM_LIMIT = 56 * 1024 * 1024
H_EXT = D_MODEL + LANES
PAIRS_PER_GROUP = 32
N_CLASS = EXPERT_GROUPS * PAIRS_PER_GROUP


def _cparams(sem):
    return pltpu.CompilerParams(dimension_semantics=sem, vmem_limit_bytes=VMEM_LIMIT)


def _split3(x):
    x1 = x.astype(BF16)
    r = x - x1.astype(F32)
    x2 = r.astype(BF16)
    r = r - x2.astype(F32)
    return x1, x2, r.astype(BF16)


def _dot01(m01, x):
    return sum(jnp.dot(m01, p, preferred_element_type=F32) for p in _split3(x))


def _dot(a, b):
    return jnp.dot(a, b, preferred_element_type=F32)


def _dot_nt(a, b):
    return lax.dot_general(a, b, (((1,), (1,)), ((), ())), preferred_element_type=F32)


def _dot_tn(a, b):
    return lax.dot_general(a, b, (((0,), (0,)), ((), ())), preferred_element_type=F32)


def _softplus(x):
    return jnp.maximum(x, 0.0) + jnp.log1p(jnp.exp(-jnp.abs(x)))


def _silu(x):
    return x * jax.nn.sigmoid(x)


def _rms(x):
    return x * lax.rsqrt(jnp.mean(x * x, axis=-1, keepdims=True) + EPS)


def _cummax_rows(x, seg):
    t = lax.broadcasted_iota(I32, x.shape, 0) & (seg - 1)
    s = 1
    while s < seg:
        x = jnp.maximum(x, jnp.where(t >= s, pltpu.roll(x, s, 0), NEG))
        s *= 2
    return x


def _rope(x, cos, sin_signed, lane):
    swapped = jnp.where((lane & 63) < 32, pltpu.roll(x, 96, 1), pltpu.roll(x, 32, 1))
    return x * cos + swapped * sin_signed


def _row_spec(tm, width):
    return pl.BlockSpec((tm, width), lambda i, *_: (i, 0))


def _const_spec(a):
    nd = a.ndim
    return pl.BlockSpec(a.shape, lambda *_: (0,) * nd)


def _layer_spec(a, l):
    nd = a.ndim
    return pl.BlockSpec((1,) + a.shape[1:], lambda *_: (l,) + (0,) * (nd - 1))


def _dual_specs(tm, np_tiles, width):
    return [pl.BlockSpec((tm, width), lambda i, *_: (jnp.minimum(i, np_tiles - 1), 0)),
            pl.BlockSpec((tm, width), lambda i, *_: (jnp.maximum(i - np_tiles, 0), 0))]


def _mod_specs(mod, k, l, tm, n_prompt, tp):
    full, bs = mod
    bp = full.shape[2] - bs
    assert bs % bp == 0 and bp % 8 == 0
    return [pl.BlockSpec((1, 1, bp, D_MODEL), lambda i, *_: (l, k, bs // bp, 0)),
            pl.BlockSpec((1, 1, bs, D_MODEL), lambda i, *_: (l, k, 0, 0))]


def _prompt_mod(p_ref, tm, tp):
    b = jnp.minimum(pl.program_id(0) * tm // tp, p_ref.shape[2] - 1)
    return p_ref[0, 0, pl.ds(b, 1), :]


def _sample_mod(s_ref, tm):
    v = s_ref[0, 0]
    return jnp.tile(v, (tm // v.shape[0], 1))


def _mod_val(p_ref, s_ref, is_prompt, tm, tp):
    return jnp.where(is_prompt, _prompt_mod(p_ref, tm, tp), _sample_mod(s_ref, tm))


def _dual_val(p_ref, s_ref, is_prompt):
    return jnp.where(is_prompt, p_ref[...], s_ref[...])


def _ada_kernel(c_ref, w_ref, b_ref, o_ref):
    c = c_ref[...]
    o_ref[0, 0] = _dot(_silu(c).astype(BF16), w_ref[0].astype(BF16)) + b_ref[0]


def _ada(c_all, w_ada, b_ada):
    nb = c_all.shape[0]
    return pl.pallas_call(
        _ada_kernel,
        grid=(DEPTH, 6),
        in_specs=[pl.BlockSpec((nb, D_MODEL), lambda l, k: (0, 0)),
                  pl.BlockSpec((1, D_MODEL, D_MODEL), lambda l, k: (l, 0, k)),
                  pl.BlockSpec((1, 1, D_MODEL), lambda l, k: (l, 0, k))],
        out_specs=pl.BlockSpec((1, 1, nb, D_MODEL), lambda l, k: (l, k, 0, 0)),
        out_shape=jax.ShapeDtypeStruct((DEPTH, 6, nb, D_MODEL), F32),
        compiler_params=_cparams(("arbitrary", "arbitrary")),
        name="ada_mod",
    )(c_all, w_ada, b_ada.reshape(DEPTH, 1, 6 * D_MODEL))


def _inproj_kernel(xp, xs, scp, scs, shp, shs, g_ref, w_ref, o_ref, *, np_tiles, tp):
    is_p = pl.program_id(0) < np_tiles
    tm = o_ref.shape[0]
    h = (_rms(_dual_val(xp, xs, is_p)) * g_ref[0] * (1.0 + _mod_val(scp, scs, is_p, tm, tp))
         + _mod_val(shp, shs, is_p, tm, tp))
    o_ref[...] = _dot_nt(h.astype(BF16), w_ref[0])


def _inproj(x, mod, k_sc, k_sh, g, w, l, tp):
    n_prompt, ns = x[0].shape[0], x[1].shape[0]
    tm = TOKEN_TILE
    ms = _mod_specs(mod, k_sc, l, tm, n_prompt, tp) + _mod_specs(mod, k_sh, l, tm, n_prompt, tp)
    return pl.pallas_call(
        functools.partial(_inproj_kernel, np_tiles=n_prompt // tm, tp=tp),
        grid=((n_prompt + ns) // tm,),
        in_specs=_dual_specs(tm, n_prompt // tm, D_MODEL) + ms + [_layer_spec(g, l), _layer_spec(w, l)],
        out_specs=_row_spec(tm, P_W),
        out_shape=jax.ShapeDtypeStruct((n_prompt + ns, P_W), F32),
        compiler_params=_cparams(("arbitrary",)),
        name="norm_inproj",
    )(*x, *([mod[0]] * 4), g, w)


def _lane_bcast(a, c):
    return jnp.broadcast_to(a[:, c:c + 1], a.shape)


def _pair_lanes(a, c0, c1):
    lm0 = lax.broadcasted_iota(I32, (a.shape[0], LANES), 1) < HEAD_DIM
    return jnp.where(lm0, a[:, c0:c0 + 1], a[:, c1:c1 + 1])


def _ret_factors(ptab, tril, mask, last_fn):
    rows = tril.shape[0]
    lane = lax.broadcasted_iota(I32, (rows, LANES), 1)
    lm0 = lane < HEAD_DIM
    cum = _dot01(tril, jnp.where((lane >= C_RET) & (lane < C_ML), ptab[2:3], 0.0))
    xt = cum.T
    cum_last = last_fn(cum)
    out = []
    for p in range(RET_HEADS // 2):
        c0, c1 = C_RET + 2 * p, C_RET + 2 * p + 1
        b0, b1 = _lane_bcast(cum, c0), _lane_bcast(cum, c1)
        bp = jnp.where(lm0, b0, b1)
        wp = jnp.exp(_pair_lanes(cum_last, c0, c1) - bp)
        out.append((jnp.exp(jnp.where(mask, b0 - xt[c0:c0 + 1, :], -jnp.inf)),
                    jnp.exp(jnp.where(mask, b1 - xt[c1:c1 + 1, :], -jnp.inf)), jnp.exp(bp), wp))
    return out


def _mixer_core(z, us, retb, mlb, small, cos, sin, ptab, cw, cb, dsk, gs, gr, gm,
                mask, tril, seg, mprev, last_fn, st, y_ref, ret_factors):
    rows = small.shape[0]
    lane = lax.broadcasted_iota(I32, (rows, LANES), 1)
    lm0 = lane < HEAD_DIM

    pre = small + ptab[0:1]
    a_neg = -jnp.exp(ptab[1:2])
    dt = _softplus(pre)
    logf = -_softplus(-pre)
    la = jnp.where(lane < C_RET, dt * a_neg,
                   jnp.where(lane < C_ML, ptab[2:3], jnp.where(lane < C_ML + 4, logf, 0.0)))
    cum = _dot01(tril, la)
    ic = pltpu.roll(pre, 4, 1)
    mlm = (lane >= C_ML) & (lane < C_ML + 4)
    d = jnp.where(mlm, ic - cum, NEG)
    m_t = cum + jnp.maximum(mprev, _cummax_rows(d, seg))
    xt = jnp.where(mlm, d, cum).T
    colv = cum - m_t
    cum_last, m_last = last_fn(cum), last_fn(m_t)
    decq = jnp.where(mlm, jnp.exp(cum_last + mprev - m_last), jnp.exp(cum_last))
    yield

    def factors(c0, c1, kind):
        if kind == "ret":
            return ret_factors((c0 - C_RET) // 2)
        if kind == "ssd":
            b0, b1 = _lane_bcast(cum, c0), _lane_bcast(cum, c1)
            bp = jnp.where(lm0, b0, b1)
            wp = jnp.exp(_pair_lanes(cum_last, c0, c1) - bp)
            return (jnp.exp(jnp.where(mask, b0 - xt[c0:c0 + 1, :], -jnp.inf)),
                    jnp.exp(jnp.where(mask, b1 - xt[c1:c1 + 1, :], -jnp.inf)), jnp.exp(bp), wp)
        a0, a1 = _lane_bcast(colv, c0), _lane_bcast(colv, c1)
        eqp = jnp.exp(jnp.where(lm0, a0, a1) + _pair_lanes(mprev, c0, c1))
        wp = jnp.exp(jnp.where(lm0, _lane_bcast(d, c0), _lane_bcast(d, c1))
                     + _pair_lanes(cum_last, c0, c1) - _pair_lanes(m_last, c0, c1))
        return (jnp.exp(jnp.where(mask, a0 + xt[c0:c0 + 1, :], -jnp.inf)),
                jnp.exp(jnp.where(mask, a1 + xt[c1:c1 + 1, :], -jnp.inf)), eqp, wp)

    def pair(idx, inputs, c0, c1, kind, finish):
        qp, kp, vp = inputs()
        q0 = jnp.where(lm0, qp, 0.0)
        q1 = jnp.where(lm0, 0.0, qp)
        sc = _dot_nt(jnp.concatenate([q0, q1], axis=0).astype(BF16), kp.astype(BF16))
        yield
        d0, d1, eqp, wp = factors(c0, c1, kind)
        v01 = jnp.concatenate([jnp.where(lm0, vp, 0.0), jnp.where(lm0, 0.0, vp)], axis=0).astype(BF16)
        kw = kp * wp
        yield
        carried, qn = st.step(idx, qp, eqp, kw, vp, decq, c0, c1, kind == "ml")
        yield
        s0, s1 = sc[0:rows] * d0, sc[rows:2 * rows] * d1
        intra = _dot(jnp.concatenate([s0, s1], axis=1).astype(BF16), v01)
        yield
        finish(intra + carried, s0, s1, qn)

    def head_norm(o):
        o2 = o * o
        ms0 = jnp.sum(jnp.where(lm0, o2, 0.0), axis=-1, keepdims=True) * (1.0 / HEAD_DIM)
        ms1 = jnp.sum(jnp.where(lm0, 0.0, o2), axis=-1, keepdims=True) * (1.0 / HEAD_DIM)
        return o * jnp.where(lm0, lax.rsqrt(ms0 + EPS), lax.rsqrt(ms1 + EPS))

    conv = cb + us[0] * cw[0:1] + us[1] * cw[1:2] + us[2] * cw[2:3] + us[3] * cw[3:4]
    xc = _silu(conv)
    bb = xc[:, SSD_WIDTH:SSD_WIDTH + LANES]
    cc = xc[:, SSD_WIDTH + LANES:SSD_WIDTH + 2 * LANES]
    br = pltpu.roll(bb, HEAD_DIM, 1)
    cr = pltpu.roll(cc, HEAD_DIM, 1)
    ys = [None] * 4
    pairs = []
    for p in range(4):
        c0, c1 = C_SSD + 2 * p, C_SSD + 2 * p + 1
        sl = slice(LANES * p, LANES * (p + 1))

        def ssd_inputs(p=p, c0=c0, c1=c1, sl=sl):
            if p < 2:
                kp, qp = jnp.where(lm0, bb, br), jnp.where(lm0, cc, cr)
            else:
                kp, qp = jnp.where(lm0, br, bb), jnp.where(lm0, cr, cc)
            dtp = jnp.where(lm0, dt[:, c0:c0 + 1], dt[:, c1:c1 + 1])
            return qp, kp, xc[:, sl] * dtp

        def ssd_finish(o, s0, s1, qn, p=p, sl=sl):
            ys[p] = (o + dsk[:, sl] * xc[:, sl]) * _silu(z[:, sl])

        pairs.append(pair(p, ssd_inputs, c0, c1, "ssd", ssd_finish))

    for p in range(2):
        c0, c1 = C_RET + 2 * p, C_RET + 2 * p + 1
        sl = slice(LANES * p, LANES * (p + 1))

        def ret_inputs(p=p, sl=sl):
            qp = _rope(retb[:, LANES * p:LANES * (p + 1)], cos[:, sl], sin[:, sl], lane)
            kp = _rope(retb[:, RET_WIDTH + LANES * p:RET_WIDTH + LANES * (p + 1)], cos[:, sl], sin[:, sl], lane)
            return qp, kp * (HEAD_DIM ** -0.5), retb[:, 2 * RET_WIDTH + LANES * p:2 * RET_WIDTH + LANES * (p + 1)]

        def ret_finish(o, s0, s1, qn, p=p, sl=sl):
            gp = retb[:, 3 * RET_WIDTH + LANES * p:3 * RET_WIDTH + LANES * (p + 1)]
            y = head_norm(o) * gr[:, sl] * _silu(gp)
            y_ref[:, SSD_WIDTH + LANES * p:SSD_WIDTH + LANES * (p + 1)] = y.astype(y_ref.dtype)

        pairs.append(pair(4 + p, ret_inputs, c0, c1, "ret", ret_finish))

    for p in range(2):
        c0, c1 = C_ML + 2 * p, C_ML + 2 * p + 1
        sl = slice(LANES * p, LANES * (p + 1))

        def ml_inputs(p=p):
            return (mlb[:, LANES * p:LANES * (p + 1)],
                    mlb[:, ML_WIDTH + LANES * p:ML_WIDTH + LANES * (p + 1)] * (HEAD_DIM ** -0.5),
                    mlb[:, 2 * ML_WIDTH + LANES * p:2 * ML_WIDTH + LANES * (p + 1)])

        def ml_finish(num, s0, s1, qn, p=p, c0=c0, c1=c1, sl=sl):
            op = mlb[:, 3 * ML_WIDTH + LANES * p:3 * ML_WIDTH + LANES * (p + 1)]
            inter0 = jnp.exp(colv[:, c0:c0 + 1] + mprev[:, c0:c0 + 1])
            inter1 = jnp.exp(colv[:, c1:c1 + 1] + mprev[:, c1:c1 + 1])
            den0 = jnp.sum(s0, axis=-1, keepdims=True) + qn[0] * inter0
            den1 = jnp.sum(s1, axis=-1, keepdims=True) + qn[1] * inter1
            dn0 = jnp.maximum(jnp.abs(den0), jnp.exp(-m_t[:, c0:c0 + 1]))
            dn1 = jnp.maximum(jnp.abs(den1), jnp.exp(-m_t[:, c1:c1 + 1]))
            hh = num / jnp.where(lm0, dn0, dn1)
            y = head_norm(hh) * gm[:, sl] * jax.nn.sigmoid(op)
            off = SSD_WIDTH + RET_WIDTH + LANES * p
            y_ref[:, off:off + LANES] = y.astype(y_ref.dtype)

        pairs.append(pair(6 + p, ml_inputs, c0, c1, "ml", ml_finish))

    live = list(pairs)
    while live:
        for g in list(live):
            try:
                next(g)
            except StopIteration:
                live.remove(g)
        yield

    for g in range(SSD_GROUPS):
        ya, yb = ys[2 * g], ys[2 * g + 1]
        ms = (jnp.sum(ya * ya, axis=-1, keepdims=True)
              + jnp.sum(yb * yb, axis=-1, keepdims=True)) * (1.0 / (2 * LANES))
        r = lax.rsqrt(ms + EPS)
        for j, yv in ((2 * g, ya), (2 * g + 1, yb)):
            sl = slice(LANES * j, LANES * (j + 1))
            y_ref[:, sl] = (yv * r * gs[:, sl]).astype(y_ref.dtype)
    return m_t


def _run_interleaved(gens):
    out = [None] * len(gens)
    live = list(range(len(gens)))
    while live:
        for k in list(live):
            try:
                next(gens[k])
            except StopIteration as stop:
                out[k] = stop.value
                live.remove(k)
    return out


def _half_rows():
    return lax.broadcasted_iota(I32, (LANES, HEAD_DIM), 0) < HEAD_DIM


class _CarriedState:
    def __init__(self, sv, nrow):
        self.sv, self.nrow = sv, nrow

    def step(self, idx, qp, eqp, kw, vp, decq, c0, c1, ml):
        lm0 = lax.broadcasted_iota(I32, (1, LANES), 1) < HEAD_DIM
        drow = _pair_lanes(decq[0:1, :], c0, c1)
        s_old = self.sv[idx]
        carried = _dot(qp.astype(BF16), s_old.astype(BF16)) * eqp
        u = _dot_tn(kw.astype(BF16), vp.astype(BF16))
        same_head = ((lax.broadcasted_iota(I32, (LANES, LANES), 0) < HEAD_DIM)
                     == (lax.broadcasted_iota(I32, (LANES, LANES), 1) < HEAD_DIM))
        self.sv[idx] = s_old * drow + jnp.where(same_head, u, 0.0)
        qn = None
        if ml:
            p = idx - 6
            n_old = self.nrow[p:p + 1, :]
            qn_l = qp * n_old
            qn = (jnp.sum(jnp.where(lm0, qn_l, 0.0), axis=-1, keepdims=True),
                  jnp.sum(jnp.where(lm0, 0.0, qn_l), axis=-1, keepdims=True))
            self.nrow[p:p + 1, :] = n_old * drow + jnp.sum(kw, axis=0, keepdims=True)
        return carried, qn


def _mixer_prompt_kernel(*refs):
    sq = SEQ_PER_STEP
    proj_refs = refs[:sq]
    (cos_ref, sin_ref, ptab_ref, cw_ref, cb_ref, dsk_ref, gs_ref, gr_ref, gm_ref,
     y_ref, sv_o, conv_o, n_o, m_o) = refs[sq:sq + 14]
    scr = refs[sq + 14:]
    sv, nrow, mrow, cbuf = scr[0:sq], scr[sq:2 * sq], scr[2 * sq:3 * sq], scr[3 * sq:4 * sq]
    ret_cache = scr[4 * sq]
    ci = pl.program_id(1)
    rows = proj_refs[0].shape[1]
    ri = lax.broadcasted_iota(I32, (rows, rows), 0)
    cj = lax.broadcasted_iota(I32, (rows, rows), 1)
    mask = cj <= ri
    tril = jnp.where(mask, 1.0, 0.0).astype(BF16)
    last_row = lambda a: a[rows - 1:rows, :]

    @pl.when(ci == 0)
    def _():
        for s in range(sq):
            sv[s][...] = jnp.zeros_like(sv[s])
            nrow[s][...] = jnp.zeros_like(nrow[s])
            mrow[s][...] = jnp.zeros_like(mrow[s])
            cbuf[s][0:8, :] = jnp.zeros((8, SSD_XBC), F32)
        for p, fs in enumerate(_ret_factors(ptab_ref[0], tril, mask, last_row)):
            for k, f in enumerate(fs):
                ret_cache[4 * p + k] = f

    ret_factors = lambda p: tuple(ret_cache[4 * p + k] for k in range(4))
    gens = []
    for s in range(sq):
        proj_ref = proj_refs[s]
        cbuf[s][8:8 + rows, :] = proj_ref[0, :, P_XBC:P_RET]
        us = [cbuf[s][pl.ds(5 + k, rows), :] for k in range(SSD_CONV)]
        gens.append(_mixer_core(
            proj_ref[0, :, P_Z:P_XBC], us, proj_ref[0, :, P_RET:P_ML], proj_ref[0, :, P_ML:P_SM],
            proj_ref[0, :, P_SM:P_W], cos_ref[...], sin_ref[...], ptab_ref[0], cw_ref[0], cb_ref[0],
            dsk_ref[0], gs_ref[0], gr_ref[0], gm_ref[0],
            mask, tril, rows, mrow[s][0:1, :], last_row,
            _CarriedState(sv[s], nrow[s]), y_ref.at[s], ret_factors))
    for s, m_t in enumerate(_run_interleaved(gens)):
        mrow[s][0:1, :] = m_t[rows - 1:rows, :]
        cbuf[s][0:8, :] = cbuf[s][rows:rows + 8, :]

    @pl.when(ci == pl.num_programs(1) - 1)
    def _():
        top = _half_rows()
        for s in range(sq):
            for idx in range(N_PAIRS):
                bd = sv[s][idx]
                sv_o[s, idx] = jnp.where(top, bd[:, :HEAD_DIM], bd[:, HEAD_DIM:])
            conv_o[s] = cbuf[s][0:8, :]
            n_o[s] = nrow[s][...]
            m_o[s] = mrow[s][...]


def _mixer_prompt(proj, nb, t, cos, sin, consts, l):
    rows = math.gcd(t, CHUNK)
    nc = t // rows
    sq = SEQ_PER_STEP
    assert nb % sq == 0

    def proj_spec(s):
        return pl.BlockSpec((1, rows, P_W), lambda b, c: (0, (sq * b + s) * nc + c, 0))

    def seq_spec(*tail):
        return pl.BlockSpec((sq,) + tail, lambda b, c: (b,) + (0,) * len(tail))

    outs = pl.pallas_call(
        _mixer_prompt_kernel,
        grid=(nb // sq, nc),
        in_specs=[proj_spec(s) for s in range(sq)]
                 + [pl.BlockSpec((rows, RET_WIDTH), lambda b, c: (c, 0)),
                    pl.BlockSpec((rows, RET_WIDTH), lambda b, c: (c, 0))] + [_layer_spec(a, l) for a in consts],
        out_specs=[pl.BlockSpec((sq, rows, D_MODEL), lambda b, c: (b, c, 0)),
                   seq_spec(N_PAIRS, LANES, HEAD_DIM), seq_spec(8, SSD_XBC), seq_spec(8, LANES), seq_spec(8, LANES)],
        out_shape=[jax.ShapeDtypeStruct((nb, t, D_MODEL), BF16),
                   jax.ShapeDtypeStruct((nb, N_PAIRS, LANES, HEAD_DIM), F32),
                   jax.ShapeDtypeStruct((nb, 8, SSD_XBC), F32),
                   jax.ShapeDtypeStruct((nb, 8, LANES), F32),
                   jax.ShapeDtypeStruct((nb, 8, LANES), F32)],
        scratch_shapes=[pltpu.VMEM((N_PAIRS, LANES, LANES), F32) for _ in range(sq)]
                       + [pltpu.VMEM((8, LANES), F32) for _ in range(2 * sq)]
                       + [pltpu.VMEM((rows + 8, SSD_XBC), F32) for _ in range(sq)]
                       + [pltpu.VMEM((2 * RET_HEADS, rows, LANES), F32)],
        compiler_params=_cparams(("arbitrary", "arbitrary")),
        name="mixer_prompt",
    )(*([proj[None]] * sq), cos, sin, *consts)
    y, sv, conv, n, m = outs
    sv = sv.reshape(nb, 2 * N_PAIRS, HEAD_DIM, HEAD_DIM)
    states = (sv[:, :8], conv[:, 5:8], sv[:, 8:12], sv[:, 12:16],
              n[:, 0:2].reshape(nb, ML_HEADS, HEAD_DIM), m[:, 0, C_ML:C_ML + 4])
    return y.reshape(nb * t, D_MODEL), states


def _lane_recurrence(s_in, s_out, q_s, k_s, v_s, decay):
    n_tok = len(decay)
    nvb = HEAD_DIM // 8
    nb = q_s.shape[-1]
    dec8 = [jnp.broadcast_to(d, (8, nb)) for d in decay]

    def body(k, acc):
        acc = [list(a) for a in acc]
        qk = [q_s[t, pl.ds(k, 1), :] for t in range(n_tok)]
        kk = [k_s[t, pl.ds(k, 1), :] for t in range(n_tok)]
        for vb in range(nvb):
            rows = pl.ds(8 * vb, 8)
            s = s_in[k, rows, :]
            for t in range(n_tok):
                s = s * dec8[t] + kk[t] * v_s[t, rows, :]
                acc[t][vb] = acc[t][vb] + qk[t] * s
            s_out[k, rows, :] = s
        return tuple(tuple(a) for a in acc)

    init = tuple(tuple(jnp.zeros((8, nb), F32) for _ in range(nvb)) for _ in range(n_tok))
    acc = lax.fori_loop(0, HEAD_DIM, body, init, unroll=2)
    return [jnp.concatenate(list(a), axis=0) for a in acc]


def _conv_T(u_ref, tail_ref, w_ref, b_ref, state_ref):
    n_tok = u_ref.shape[0]
    full = [tail_ref[0, j] for j in range(SSD_CONV - 1)] + [u_ref[t] for t in range(n_tok)]
    outs = []
    for t in range(n_tok):
        acc = b_ref[0]
        for tap in range(SSD_CONV):
            acc = acc + full[t + tap] * w_ref[0, tap]
        outs.append(_silu(acc))
    for j in range(SSD_CONV - 1):
        state_ref[j] = full[n_tok + j]
    return outs


def _ssd_T_kernel(z_ref, xs_ref, b_ref, c_ref, sm_ref, tx_ref, tb_ref, tc_ref, wx_ref, wb_ref, wc_ref,
                  bx_ref, bb_ref, bc_ref, dtb_ref, alog_ref, dsk_ref, g_ref, s_ref,
                  y_ref, so_ref, cx_ref, cb_ref, cc_ref, q_s, k_s, v_s):
    h = pl.program_id(0)
    n_tok = xs_ref.shape[0]
    xs = _conv_T(xs_ref, tx_ref, wx_ref, bx_ref, cx_ref)
    bm = _conv_T(b_ref, tb_ref, wb_ref, bb_ref, cb_ref)
    cm = _conv_T(c_ref, tc_ref, wc_ref, bc_ref, cc_ref)
    a_neg = -jnp.exp(alog_ref[0, 0])
    decay = []
    for t in range(n_tok):
        dt = _softplus(sm_ref[t, pl.ds(C_SSD + h, 1), :] + dtb_ref[0, 0])
        decay.append(jnp.exp(dt * a_neg))
        q_s[t] = cm[t]
        k_s[t] = bm[t]
        v_s[t] = xs[t] * dt
    o = _lane_recurrence(s_ref.at[0, 0], so_ref.at[0, 0], q_s, k_s, v_s, decay)
    hh = h % (SSD_HEADS // SSD_GROUPS)
    row0 = pl.multiple_of(hh * HEAD_DIM, HEAD_DIM)
    for t in range(n_tok):
        y_ref[t, pl.ds(row0, HEAD_DIM), :] = (o[t] + dsk_ref[0, 0] * xs[t]) * _silu(z_ref[t])

    @pl.when(hh == SSD_HEADS // SSD_GROUPS - 1)
    def _():
        for t in range(n_tok):
            blk = y_ref[t]
            ms = jnp.mean(blk * blk, axis=0, keepdims=True)
            y_ref[t] = blk * lax.rsqrt(ms + EPS) * g_ref[0]


def _ret_T_kernel(q_ref, k_ref, v_ref, g_ref, cos_ref, sin_ref, gam_ref, gn_ref, s_ref,
                  y_ref, so_ref, q_s, k_s, v_s):
    n_tok = q_ref.shape[0]
    half = HEAD_DIM // 2

    def rope(x, t):
        x1, x2 = x[:half], x[half:]
        return jnp.concatenate([x1 * cos_ref[t] - x2 * sin_ref[t], x1 * sin_ref[t] + x2 * cos_ref[t]], axis=0)

    for t in range(n_tok):
        q_s[t] = rope(q_ref[t], t)
        k_s[t] = rope(k_ref[t], t) * (HEAD_DIM ** -0.5)
        v_s[t] = v_ref[t]
    o = _lane_recurrence(s_ref.at[0, 0], so_ref.at[0, 0], q_s, k_s, v_s, [gam_ref[0]] * n_tok)
    for t in range(n_tok):
        ms = jnp.mean(o[t] * o[t], axis=0, keepdims=True)
        y_ref[t] = o[t] * lax.rsqrt(ms + EPS) * gn_ref[0] * _silu(g_ref[t])


def _ml_T_kernel(q_ref, k_ref, v_ref, o_ref, sm_ref, bi_ref, bf_ref, gn_ref, c_ref, n_ref, m_ref,
                 y_ref, co_ref, no_ref, mo_ref, q_s, k_s, v_s):
    h = pl.program_id(0)
    n_tok = q_ref.shape[0]
    m = m_ref[0, 0]
    n = n_ref[0, 0]
    decay, qn, m_all = [], [], []
    for t in range(n_tok):
        i_t = sm_ref[t, pl.ds(C_RET + h, 1), :] + bi_ref[0, 0]
        logf = -_softplus(-(sm_ref[t, pl.ds(C_ML + h, 1), :] + bf_ref[0, 0]))
        m_new = jnp.maximum(logf + m, i_t)
        f_t = jnp.exp(logf + m - m_new)
        kw = k_ref[t] * (HEAD_DIM ** -0.5) * jnp.exp(i_t - m_new)
        n = n * f_t + kw
        q_s[t] = q_ref[t]
        k_s[t] = kw
        v_s[t] = v_ref[t]
        decay.append(f_t)
        qn.append(jnp.sum(q_ref[t] * n, axis=0, keepdims=True))
        m_all.append(m_new)
        m = m_new
    num = _lane_recurrence(c_ref.at[0, 0], co_ref.at[0, 0], q_s, k_s, v_s, decay)
    no_ref[0] = n
    mo_ref[0] = m
    for t in range(n_tok):
        hh = num[t] / jnp.maximum(jnp.abs(qn[t]), jnp.exp(-m_all[t]))
        ms = jnp.mean(hh * hh, axis=0, keepdims=True)
        y_ref[t] = hh * lax.rsqrt(ms + EPS) * gn_ref[0] * jax.nn.sigmoid(o_ref[t])


def _mixer_sample_T(projT, convT, cwb, cbb, ssdT, retT, cT, nT, mT, tabs, cos, sin, l):
    n_tok, _, nb = projT.shape
    dtb, alog, dsk, gs, gam, gr, bi, bf, gm = tabs
    hb = lambda col: col // HEAD_DIM
    rep = SSD_HEADS // SSD_GROUPS
    proj_blk = lambda first, div=1: pl.BlockSpec((n_tok, HEAD_DIM, nb), lambda h: (0, first + h // div, 0))
    small = pl.BlockSpec((n_tok, LANES, nb), lambda h: (0, P_SM // LANES, 0))
    lay4 = lambda a, first=0, div=1: pl.BlockSpec((1,) + a.shape[1:2] + (HEAD_DIM, a.shape[-1]),
                                                  lambda h: (l, 0, first + h // div, 0))
    chan = lambda a, first=0, div=1, n=1: pl.BlockSpec((1, n * HEAD_DIM, 1), lambda h: (l, first + h // div, 0))
    head4 = lambda a: pl.BlockSpec((1, 1) + a.shape[2:], lambda h: (l, h) + (0,) * (a.ndim - 2))
    vm = lambda: pltpu.VMEM((n_tok, HEAD_DIM, nb), F32)
    state_spec = pl.BlockSpec((1, 1, HEAD_DIM, HEAD_DIM, nb), lambda h: (0, h, 0, 0, 0))
    stacked = lambda heads: jax.ShapeDtypeStruct((1, heads, HEAD_DIM, HEAD_DIM, nb), F32)
    b0, c0 = hb(SSD_WIDTH), hb(SSD_WIDTH + SSD_GROUPS * SSD_STATE)

    y_ssd, ssd_n, cx, cb, cc = pl.pallas_call(
        _ssd_T_kernel,
        grid=(SSD_HEADS,),
        in_specs=[proj_blk(hb(P_Z)), proj_blk(hb(P_XBC)), proj_blk(hb(P_XBC) + b0, rep),
                  proj_blk(hb(P_XBC) + c0, rep), small,
                  lay4(convT), lay4(convT, b0, rep), lay4(convT, c0, rep),
                  lay4(cwb), lay4(cwb, b0, rep), lay4(cwb, c0, rep),
                  chan(cbb), chan(cbb, b0, rep), chan(cbb, c0, rep),
                  head4(dtb), head4(alog), head4(dsk), chan(gs, 0, rep, rep), head4(ssdT)],
        out_specs=[pl.BlockSpec((n_tok, rep * HEAD_DIM, nb), lambda h: (0, h // rep, 0)), state_spec,
                   pl.BlockSpec((SSD_CONV - 1, HEAD_DIM, nb), lambda h: (0, h, 0)),
                   pl.BlockSpec((SSD_CONV - 1, HEAD_DIM, nb), lambda h: (0, h // rep, 0)),
                   pl.BlockSpec((SSD_CONV - 1, HEAD_DIM, nb), lambda h: (0, h // rep, 0))],
        out_shape=[jax.ShapeDtypeStruct((n_tok, SSD_WIDTH, nb), F32),
                   stacked(SSD_HEADS),
                   jax.ShapeDtypeStruct((SSD_CONV - 1, SSD_WIDTH, nb), F32),
                   jax.ShapeDtypeStruct((SSD_CONV - 1, SSD_GROUPS * SSD_STATE, nb), F32),
                   jax.ShapeDtypeStruct((SSD_CONV - 1, SSD_GROUPS * SSD_STATE, nb), F32)],
        scratch_shapes=[vm(), vm(), vm()],
        compiler_params=_cparams(("arbitrary",)),
        name="sample_ssd",
    )(projT, projT, projT, projT, projT, convT, convT, convT, cwb, cwb, cwb, cbb, cbb, cbb,
      dtb, alog, dsk, gs, ssdT)

    y_ret, ret_n = pl.pallas_call(
        _ret_T_kernel,
        grid=(RET_HEADS,),
        in_specs=[proj_blk(hb(P_RET)), proj_blk(hb(P_RET + RET_WIDTH)), proj_blk(hb(P_RET + 2 * RET_WIDTH)),
                  proj_blk(hb(P_RET + 3 * RET_WIDTH)), _const_spec(cos), _const_spec(sin),
                  pl.BlockSpec((1, 1, 1), lambda h: (h, 0, 0)), chan(gr), head4(retT)],
        out_specs=[pl.BlockSpec((n_tok, HEAD_DIM, nb), lambda h: (0, h, 0)), state_spec],
        out_shape=[jax.ShapeDtypeStruct((n_tok, RET_WIDTH, nb), F32),
                   stacked(RET_HEADS)],
        scratch_shapes=[vm(), vm(), vm()],
        compiler_params=_cparams(("arbitrary",)),
        name="sample_ret",
    )(projT, projT, projT, projT, cos, sin, gam, gr, retT)

    y_ml, c_n, n_n, m_n = pl.pallas_call(
        _ml_T_kernel,
        grid=(ML_HEADS,),
        in_specs=[proj_blk(hb(P_ML)), proj_blk(hb(P_ML + ML_WIDTH)), proj_blk(hb(P_ML + 2 * ML_WIDTH)),
                  proj_blk(hb(P_ML + 3 * ML_WIDTH)), small, head4(bi), head4(bf),
                  chan(gm), head4(cT), head4(nT), head4(mT)],
        out_specs=[pl.BlockSpec((n_tok, HEAD_DIM, nb), lambda h: (0, h, 0)), state_spec,
                   pl.BlockSpec((1, HEAD_DIM, nb), lambda h: (h, 0, 0)),
                   pl.BlockSpec((1, 1, nb), lambda h: (h, 0, 0))],
        out_shape=[jax.ShapeDtypeStruct((n_tok, ML_WIDTH, nb), F32),
                   stacked(ML_HEADS),
                   jax.ShapeDtypeStruct((ML_HEADS, HEAD_DIM, nb), F32),
                   jax.ShapeDtypeStruct((ML_HEADS, 1, nb), F32)],
        scratch_shapes=[vm(), vm(), vm()],
        compiler_params=_cparams(("arbitrary",)),
        name="sample_mlstm",
    )(projT, projT, projT, projT, projT, bi, bf, gm, cT, nT, mT)

    y_t = jnp.concatenate([y_ssd, y_ret, y_ml], axis=1)
    conv_n = jnp.concatenate([cx, cb, cc], axis=1)
    return y_t, (ssd_n, conv_n, ret_n, c_n, n_n, m_n)


def _route(logits):
    lane = lax.broadcasted_iota(I32, logits.shape, 1)
    gmask = (lane >= N_EXPERTS) & (lane < N_EXPERTS + EXPERT_GROUPS)
    gl = jnp.where(gmask, logits, -jnp.inf)
    ge = jnp.exp(gl - jnp.max(gl, axis=-1, keepdims=True))
    gprob = ge / jnp.sum(ge, axis=-1, keepdims=True)
    g_w = jnp.max(gprob, axis=-1, keepdims=True)
    g_idx = jnp.min(jnp.where(gmask & (gprob == g_w), lane - N_EXPERTS, LANES), axis=-1, keepdims=True)
    emask = (lane < N_EXPERTS) & ((lane >> 3) == g_idx)
    el = jnp.where(emask, logits, -jnp.inf)
    ee = jnp.exp(el - jnp.max(el, axis=-1, keepdims=True))
    eprob = ee / jnp.sum(ee, axis=-1, keepdims=True)
    p1 = jnp.max(jnp.where(emask, eprob, -1.0), axis=-1, keepdims=True)
    i1 = jnp.min(jnp.where(emask & (eprob == p1), lane, LANES), axis=-1, keepdims=True)
    rest = emask & (lane != i1)
    p2 = jnp.max(jnp.where(rest, eprob, -1.0), axis=-1, keepdims=True)
    i2 = jnp.min(jnp.where(rest & (eprob == p2), lane, LANES), axis=-1, keepdims=True)
    tot = p1 + p2
    lo, hi = jnp.minimum(i1, i2) & 7, jnp.maximum(i1, i2) & 7
    key = g_idx * PAIRS_PER_GROUP + ((lo * (13 - lo)) >> 1) + hi - 1
    return g_w * (p1 / tot), g_w * (p2 / tot), i1, i2, key


def _outproj_kernel(yp, ys, xp, xs, gtp, gts, scp, scs, shp, shs, g_ref, w_ref, wr_ref, br_ref,
                    x1_ref, hx_ref, info_ref, cnt_ref, carry, *, np_tiles, tp):
    i = pl.program_id(0)
    is_p = i < np_tiles

    @pl.when(i == 0)
    def _():
        carry[...] = jnp.zeros_like(carry)

    tm = x1_ref.shape[0]
    nblk = OUTPROJ_ROW_BLOCKS
    rb = tm // nblk
    blocks = [slice(k * rb, (k + 1) * rb) for k in range(nblk)]

    def body(x_ref, y_ref, gt, sc, sh):
        mod = lambda m, b: m if m.shape[0] == 1 else m[b]
        x1s = [x_ref[b, :] + mod(gt, b) * _dot(y_ref[b, :], w_ref[0]) for b in blocks]
        for b, x1 in zip(blocks, x1s):
            x1_ref[b, :] = x1
        h2bs = [(_rms(x1) * g_ref[0] * (1.0 + mod(sc, b)) + mod(sh, b)).astype(BF16)
                for b, x1 in zip(blocks, x1s)]
        routes = [_route(_dot(h2b, wr_ref[0]) + br_ref[0]) for h2b in h2bs]

        tril = (lax.broadcasted_iota(I32, (rb, rb), 1) <= lax.broadcasted_iota(I32, (rb, rb), 0))
        tril = jnp.where(tril, 1.0, 0.0).astype(BF16)
        onehots = [lax.broadcasted_iota(I32, (rb, N_CLASS), 1) == r[4] for r in routes]
        uptos = [_dot(tril, jnp.where(oh, 1.0, 0.0).astype(BF16)) for oh in onehots]
        seen = carry[0:1, :]
        ranks = []
        for oh, upto in zip(onehots, uptos):
            ranks.append(jnp.sum(jnp.where(oh, upto - 1.0 + seen, 0.0), axis=-1, keepdims=True))
            seen = seen + upto[rb - 1:rb, :]
        carry[0:1, :] = seen
        cnt_ref[...] = jnp.broadcast_to(seen, cnt_ref.shape)

        lane = lax.broadcasted_iota(I32, (rb, LANES), 1)
        for b, h2b, (w1, w2, i1, i2, key), rank in zip(blocks, h2bs, routes, ranks):
            info = jnp.zeros((rb, LANES), F32)
            for c, v in enumerate((w1, w2, i1.astype(F32), i2.astype(F32), key.astype(F32), rank)):
                info = jnp.where(lane == c, v, info)
            hx_ref[b, 0:D_MODEL] = h2b.astype(F32)
            hx_ref[b, D_MODEL:H_EXT] = info
            info_ref[b, :] = info[:, 0:8]

    @pl.when(is_p)
    def _():
        body(xp, yp, _prompt_mod(gtp, tm, tp), _prompt_mod(scp, tm, tp), _prompt_mod(shp, tm, tp))

    @pl.when(jnp.logical_not(is_p))
    def _():
        body(xs, ys, _sample_mod(gts, tm), _sample_mod(scs, tm), _sample_mod(shs, tm))


def _outproj(y, x, mod, k_gt, k_sc, k_sh, g, w, wr, br, l, tp):
    n_prompt, ns = x[0].shape[0], x[1].shape[0]
    n = n_prompt + ns
    tm = TOKEN_TILE
    npt = n_prompt // tm
    ms = [s for k in (k_gt, k_sc, k_sh) for s in _mod_specs(mod, k, l, tm, n_prompt, tp)]
    return pl.pallas_call(
        functools.partial(_outproj_kernel, np_tiles=npt, tp=tp),
        grid=(n // tm,),
        in_specs=_dual_specs(tm, npt, D_MODEL) + _dual_specs(tm, npt, D_MODEL) + ms
                 + [_layer_spec(a, l) for a in (g, w, wr, br)],
        out_specs=[_row_spec(tm, D_MODEL), _row_spec(tm, H_EXT), _row_spec(tm, 8),
                   pl.BlockSpec((8, N_CLASS), lambda i: (0, 0))],
        out_shape=[jax.ShapeDtypeStruct((n, D_MODEL), F32),
                   jax.ShapeDtypeStruct((n, H_EXT), F32),
                   jax.ShapeDtypeStruct((n, 8), F32),
                   jax.ShapeDtypeStruct((8, N_CLASS), F32)],
        scratch_shapes=[pltpu.VMEM((8, N_CLASS), F32)],
        compiler_params=_cparams(("arbitrary",)),
        name="outproj_router",
    )(*y, *x, *([mod[0]] * 6), g, w, wr, br)


def _scatter_kernel(pos_ref, x_ref, o_hbm, buf, sem):
    i, n = pl.program_id(0), pl.num_programs(0)
    tm = x_ref.shape[0]
    slot = i % 2

    def wait_slot(s):
        pltpu.make_async_copy(buf.at[s], o_hbm.at[pl.ds(0, tm), :], sem.at[s]).wait()

    @pl.when(i >= 2)
    def _():
        wait_slot(slot)

    buf[slot] = x_ref[...]

    def body(r, c):
        dst = pos_ref[i * tm + r]
        pltpu.make_async_copy(buf.at[slot, pl.ds(r, 1), :], o_hbm.at[pl.ds(dst, 1), :], sem.at[slot]).start()
        return c
    lax.fori_loop(0, tm, body, 0, unroll=16)

    @pl.when(i == n - 1)
    def _():
        wait_slot(slot)

        @pl.when(n >= 2)
        def _():
            wait_slot(1 - slot)


def _scatter_rows(x, pos):
    n, width = x.shape
    tm = TOKEN_TILE
    return pl.pallas_call(
        _scatter_kernel,
        grid_spec=pltpu.PrefetchScalarGridSpec(
            num_scalar_prefetch=1, grid=(n // tm,),
            in_specs=[_row_spec(tm, width)],
            out_specs=pl.BlockSpec(memory_space=pl.ANY),
            scratch_shapes=[pltpu.VMEM((2, tm, width), x.dtype), pltpu.SemaphoreType.DMA((2,))]),
        out_shape=jax.ShapeDtypeStruct((n, width), x.dtype),
        compiler_params=_cparams(("arbitrary",)),
        name="scatter_rows",
    )(pos, x)


def _gather_tiles(idx_ref, src_hbm, buf, sem, tm):
    i, n = pl.program_id(0), pl.num_programs(0)

    def issue(tile, slot):
        def body(r, c):
            row = idx_ref[tile * tm + r]
            pltpu.make_async_copy(src_hbm.at[pl.ds(row, 1), :], buf.at[slot, pl.ds(r, 1), :],
                                  sem.at[slot]).start()
            return c
        lax.fori_loop(0, tm, body, 0, unroll=16)

    @pl.when(i == 0)
    def _():
        issue(0, 0)

    @pl.when(i + 1 < n)
    def _():
        issue(i + 1, (i + 1) % 2)

    slot = i % 2
    pltpu.make_async_copy(src_hbm.at[pl.ds(0, tm), :], buf.at[slot], sem.at[slot]).wait()
    return slot


def _experts_kernel(tile_ref, grp_ref, flag_ref, hx_ref, wgu_ref, wd_ref, o_ref):
    j = pl.program_id(0)
    flags = flag_ref[j]
    valid = (flags & 1) != 0

    @pl.when(valid & ((flags & 2) != 0))
    def _():
        o_ref[...] = jnp.zeros_like(o_ref)

    @pl.when(valid)
    def _():
        h = hx_ref[:, 0:D_MODEL].astype(BF16)
        r = hx_ref[:, D_MODEL:H_EXT]
        w1, w2, i1, i2 = r[:, 0:1], r[:, 1:2], r[:, 2:3], r[:, 3:4]
        base = grp_ref[j] * EXPERTS_PER_GROUP
        for e in range(EXPERTS_PER_GROUP):
            @pl.when(((flags >> (8 + e)) & 1) != 0)
            def _():
                eid = (base + e).astype(F32)
                ge = jnp.where(i1 == eid, w1, 0.0) + jnp.where(i2 == eid, w2, 0.0)
                nblk = EXPERT_ROW_BLOCKS
                rb = h.shape[0] // nblk
                aus = [_dot(h[k * rb:(k + 1) * rb], wgu_ref[0, 0, e]) for k in range(nblk)]
                acts = [(_silu(au[:, :EXPERT_FF]) * au[:, EXPERT_FF:]).astype(BF16) for au in aus]
                yes = [_dot(act, wd_ref[0, 0, e]) for act in acts]
                for k in range(nblk):
                    o_ref[k * rb:(k + 1) * rb, :] += ge[k * rb:(k + 1) * rb] * yes[k]


def _experts(hx_sorted, tile, grp, flags, wgu, wd, l):
    n = hx_sorted.shape[0]
    tm = MOE_TILE
    return pl.pallas_call(
        _experts_kernel,
        grid_spec=pltpu.PrefetchScalarGridSpec(
            num_scalar_prefetch=3, grid=(tile.shape[0],),
            in_specs=[pl.BlockSpec((tm, H_EXT), lambda j, t, g, f: (t[j], 0)),
                      pl.BlockSpec((1, 1, EXPERTS_PER_GROUP, D_MODEL, 2 * EXPERT_FF),
                                   lambda j, t, g, f: (l, g[j], 0, 0, 0)),
                      pl.BlockSpec((1, 1, EXPERTS_PER_GROUP, EXPERT_FF, D_MODEL),
                                   lambda j, t, g, f: (l, g[j], 0, 0, 0))],
            out_specs=pl.BlockSpec((tm, D_MODEL), lambda j, t, g, f: (t[j], 0))),
        out_shape=jax.ShapeDtypeStruct((n, D_MODEL), F32),
        compiler_params=_cparams(("arbitrary",)),
        name="moe_experts",
    )(tile, grp, flags, hx_sorted, wgu, wd)


def _combine_kernel(pos_ref, y_hbm, x1_ref, gtp, gts, gf_ref, op_ref, os_ref, buf, sem, *, np_tiles, tp, final):
    slot = _gather_tiles(pos_ref, y_hbm, buf, sem, x1_ref.shape[0])
    is_p = pl.program_id(0) < np_tiles
    x2 = x1_ref[...] + _mod_val(gtp, gts, is_p, x1_ref.shape[0], tp) * buf[slot]
    out = _rms(x2) * gf_ref[...] if final else x2

    @pl.when(is_p)
    def _():
        op_ref[...] = out

    @pl.when(jnp.logical_not(is_p))
    def _():
        os_ref[...] = out


def _combine(y_sorted, pos, x1, mod, k_gt, gf, l, n_prompt, tp, final):
    n = x1.shape[0]
    tm = TOKEN_TILE
    npt = n_prompt // tm
    return pl.pallas_call(
        functools.partial(_combine_kernel, np_tiles=npt, tp=tp, final=final),
        grid_spec=pltpu.PrefetchScalarGridSpec(
            num_scalar_prefetch=1, grid=(n // tm,),
            in_specs=[pl.BlockSpec(memory_space=pl.ANY), _row_spec(tm, D_MODEL)]
                     + _mod_specs(mod, k_gt, l, tm, n_prompt, tp) + [_const_spec(gf)],
            out_specs=_dual_specs(tm, npt, D_MODEL),
            scratch_shapes=[pltpu.VMEM((2, tm, D_MODEL), F32), pltpu.SemaphoreType.DMA((2,))]),
        out_shape=[jax.ShapeDtypeStruct((n_prompt, D_MODEL), F32),
                   jax.ShapeDtypeStruct((n - n_prompt, D_MODEL), F32)],
        compiler_params=_cparams(("arbitrary",)),
        name="moe_combine",
    )(pos, y_sorted, x1, mod[0], mod[0], gf)


def _class_members():
    table = np.zeros((N_CLASS, N_EXPERTS), dtype=bool)
    for grp in range(EXPERT_GROUPS):
        for lo in range(EXPERTS_PER_GROUP):
            for hi in range(lo + 1, EXPERTS_PER_GROUP):
                cls = grp * PAIRS_PER_GROUP + ((lo * (13 - lo)) >> 1) + hi - 1
                table[cls, grp * EXPERTS_PER_GROUP + lo] = True
                table[cls, grp * EXPERTS_PER_GROUP + hi] = True
    return table


def _routing_tables(cnt, key, rank, n_tiles, tm):
    n_items = n_tiles + EXPERT_GROUPS - 1
    c = jnp.arange(N_CLASS, dtype=I32)
    start_c = jnp.sum(jnp.where(c[:, None] < c[None, :], cnt[:, None], 0), axis=0)
    pos = jnp.sum(jnp.where(key[:, None] == c[None, :], start_c[None, :], 0), axis=1) + rank

    g = jnp.arange(EXPERT_GROUPS, dtype=I32)
    gend = jnp.sum(jnp.where(c[None, :] < PAIRS_PER_GROUP * (g[:, None] + 1), cnt[None, :], 0), axis=1)
    t0 = jnp.arange(n_tiles, dtype=I32) * tm
    t1 = t0 + (tm - 1)
    gfirst = jnp.sum((gend[None, :] <= t0[:, None]).astype(I32), axis=1)
    glast = jnp.sum((gend[None, :] <= t1[:, None]).astype(I32), axis=1)
    per_tile = glast - gfirst + 1
    tt = jnp.arange(n_tiles, dtype=I32)
    start_t = jnp.sum(jnp.where(tt[:, None] < tt[None, :], per_tile[:, None], 0), axis=0)
    total = jnp.sum(per_tile)

    j = jnp.arange(n_items, dtype=I32)
    valid = j < total
    tile = jnp.sum((start_t[None, :] <= j[:, None]).astype(I32), axis=1) - 1
    tile = jnp.where(valid, tile, n_tiles - 1)
    sel = tile[:, None] == tt[None, :]
    pick = lambda v: jnp.sum(jnp.where(sel, v[None, :], 0), axis=1)
    grp = jnp.where(valid, pick(gfirst) + (j - pick(start_t)), glast[n_tiles - 1])
    first = valid & (j == pick(start_t))

    in_tile = (cnt[None, :] > 0) & (start_c[None, :] <= t1[:, None]) & ((start_c + cnt)[None, :] > t0[:, None])
    e = jnp.arange(N_EXPERTS, dtype=I32)
    member = jnp.asarray(_class_members())
    present_t = jnp.any(in_tile[:, :, None] & member[None, :, :], axis=1)
    present_j = jnp.any(sel[:, :, None] & present_t[None, :, :], axis=1)
    eg = e[None, :] - grp[:, None] * EXPERTS_PER_GROUP
    bits = jnp.sum(jnp.where(present_j & (eg >= 0) & (eg < EXPERTS_PER_GROUP),
                             1 << (8 + jnp.clip(eg, 0, EXPERTS_PER_GROUP - 1)), 0), axis=1)
    flags = valid.astype(I32) | (first.astype(I32) << 1) | bits
    return pos.astype(I32), tile.astype(I32), grp.astype(I32), flags.astype(I32)


def _moe(hx, info, counts, x1, mod, k_gt, gf, wgu, wd, l, n_prompt, tp, final):
    n = hx.shape[0]
    tm = MOE_TILE
    pos, tile, grp, flags = _routing_tables(counts[0].astype(I32), info[:, 4].astype(I32),
                                            info[:, 5].astype(I32), n // tm, tm)
    y_sorted = _experts(_scatter_rows(hx, pos), tile, grp, flags, wgu, wd, l)
    return _combine(y_sorted, pos, x1, mod, k_gt, gf, l, n_prompt, tp, final)


def _rope_tables(pos):
    half = HEAD_DIM // 2
    inv = ROPE_BASE ** (-jnp.arange(half, dtype=F32) / half)
    ang = pos.astype(F32)[:, None] * inv[None, :]
    cos, sin = jnp.cos(ang), jnp.sin(ang)
    cos_t = jnp.tile(jnp.concatenate([cos, cos], axis=-1), (1, RET_HEADS))
    sin_t = jnp.tile(jnp.concatenate([-sin, sin], axis=-1), (1, RET_HEADS))
    return cos_t, sin_t


def _mixer_consts(w_in, conv_w, conv_b, dt_bias, a_log, d_skip, g_ssd_norm, g_ret_norm,
                  b_mlstm_i, b_mlstm_f, g_mlstm_norm):
    nl = w_in.shape[0]
    w_t = jnp.transpose(w_in, (0, 2, 1))
    w_p = jnp.concatenate([w_t[:, 0:1280], w_t[:, 1288:3336], w_t[:, 1280:1288], w_t[:, 3336:3344],
                           jnp.zeros((nl, LANES - 16, D_MODEL), F32)], axis=1).astype(BF16)
    pad = lambda v: jnp.pad(v, ((0, 0), (0, LANES - v.shape[1])))
    log_gamma = jnp.log(1.0 - 2.0 ** (-5.0 - jnp.arange(RET_HEADS, dtype=F32)))
    lg = jnp.broadcast_to(jnp.concatenate([jnp.zeros((8,), F32), log_gamma])[None, :], (nl, 12))
    ptab = jnp.stack([pad(jnp.concatenate([dt_bias, b_mlstm_i, b_mlstm_f], axis=1)), pad(a_log), pad(lg)]
                     + [jnp.zeros((nl, LANES), F32)] * 5, axis=1)
    consts = (ptab, conv_w, conv_b[:, None, :], jnp.repeat(d_skip, HEAD_DIM, axis=1)[:, None, :],
              g_ssd_norm[:, None, :], g_ret_norm[:, None, :], g_mlstm_norm[:, None, :])
    return w_p, consts


def kernel(x_prompt, x_sample, state_ssd, state_ssd_conv, state_ret, state_mlstm_c, state_mlstm_n,
           state_mlstm_m, c_prompt, c_sample, w_ada, b_ada, g_norm1, g_norm2, w_in, conv_w, conv_b,
           dt_bias, a_log, d_skip, g_ssd_norm, g_ret_norm, b_mlstm_i, b_mlstm_f, g_mlstm_norm, w_out,
           w_router_group, b_router_group, w_router_expert, b_router_expert, w_gate_up, w_down, g_final):
    bp, tp, _ = x_prompt.shape
    bs, ts, _ = x_sample.shape
    n_srows = bs * ts
    n_prompt = bp * tp

    mod4 = _ada(jnp.concatenate([c_sample, c_prompt], axis=0), w_ada, b_ada)
    mod = (mod4, bs)
    sh1, sc1, gt1, sh2, sc2, gt2 = range(6)

    cos_p, sin_p = _rope_tables(jnp.arange(tp, dtype=I32))
    lanes = lambda v: v[..., None]
    half = HEAD_DIM // 2
    ang = (PAST_LEN + jnp.arange(ts, dtype=I32)).astype(F32)[:, None] * (
        ROPE_BASE ** (-jnp.arange(half, dtype=F32) / half))[None, :]
    cos_s, sin_s = lanes(jnp.cos(ang)), lanes(jnp.sin(ang))
    gamma = 1.0 - 2.0 ** (-5.0 - jnp.arange(RET_HEADS, dtype=F32))
    tabs = (lanes(dt_bias[:, :, None]), lanes(a_log[:, :, None]), lanes(d_skip[:, :, None]), lanes(g_ssd_norm),
            lanes(gamma[:, None]), lanes(g_ret_norm), lanes(b_mlstm_i[:, :, None]), lanes(b_mlstm_f[:, :, None]),
            lanes(g_mlstm_norm))
    cwb, cbb = lanes(conv_w), lanes(conv_b)
    ssd_t = jnp.transpose(state_ssd, (0, 2, 3, 4, 1))
    ret_t = jnp.transpose(state_ret, (0, 2, 3, 4, 1))
    c_t = jnp.transpose(state_mlstm_c, (0, 2, 3, 4, 1))
    n_t = jnp.transpose(state_mlstm_n, (0, 2, 3, 1))
    m_t = jnp.transpose(state_mlstm_m, (0, 2, 1))[:, :, None, :]
    conv_t = jnp.transpose(state_ssd_conv, (0, 2, 3, 1))

    w_p, consts = _mixer_consts(w_in, conv_w, conv_b, dt_bias, a_log, d_skip, g_ssd_norm, g_ret_norm,
                                b_mlstm_i, b_mlstm_f, g_mlstm_norm)
    w_o = w_out.astype(BF16)
    zpad = LANES - N_EXPERTS - EXPERT_GROUPS
    wr = jnp.concatenate([w_router_expert, w_router_group, jnp.zeros((DEPTH, D_MODEL, zpad), F32)],
                         axis=2).astype(BF16)
    br = jnp.concatenate([b_router_expert, b_router_group, jnp.zeros((DEPTH, zpad), F32)], axis=1)[:, None, :]
    wgu = w_gate_up.astype(BF16).reshape(DEPTH, EXPERT_GROUPS, EXPERTS_PER_GROUP, D_MODEL, 2 * EXPERT_FF)
    wd = w_down.astype(BF16).reshape(DEPTH, EXPERT_GROUPS, EXPERTS_PER_GROUP, EXPERT_FF, D_MODEL)
    g1, g2, gf = g_norm1[:, None, :], g_norm2[:, None, :], g_final[None, :]

    x = (x_prompt.reshape(n_prompt, D_MODEL), jnp.transpose(x_sample, (1, 0, 2)).reshape(n_srows, D_MODEL))
    p_states, s_states = [], []
    for l in range(DEPTH):
        final = l == DEPTH - 1
        proj = _inproj(x, mod, sc1, sh1, g1, w_p, l, tp)

        ycat_p, st = _mixer_prompt(proj, bp, tp, cos_p, sin_p, consts, l)
        p_states.append(st)

        proj_t = jnp.transpose(proj[n_prompt:].reshape(ts, bs, P_W), (0, 2, 1))
        y_t, st_t = _mixer_sample_T(proj_t, conv_t, cwb, cbb, ssd_t, ret_t, c_t, n_t, m_t, tabs, cos_s, sin_s, l)
        s_states.append(st_t)
        ycat_s = jnp.transpose(y_t, (0, 2, 1)).reshape(n_srows, D_MODEL).astype(BF16)

        x1, hx, info, counts = _outproj((ycat_p, ycat_s), x, mod, gt1, sc2, sh2, g2, w_o, wr, br, l, tp)
        x = _moe(hx, info, counts, x1, mod, gt2, gf, wgu, wd, l, n_prompt, tp, final)

    y_prompt = x[0].reshape(bp, tp, D_MODEL)
    y_sample = jnp.transpose(x[1].reshape(ts, bs, D_MODEL), (1, 0, 2))
    p_st = [jnp.stack([s[i] for s in p_states], axis=0) for i in range(6)]
    s_t = [jnp.concatenate([s[i] for s in s_states], axis=0) if i in (0, 2, 3)
           else jnp.stack([s[i] for s in s_states], axis=0) for i in range(6)]
    s_st = [jnp.transpose(s_t[0], (0, 4, 1, 2, 3)), jnp.transpose(s_t[1], (0, 3, 1, 2)),
            jnp.transpose(s_t[2], (0, 4, 1, 2, 3)), jnp.transpose(s_t[3], (0, 4, 1, 2, 3)),
            jnp.transpose(s_t[4], (0, 3, 1, 2)), jnp.transpose(s_t[5][:, :, 0, :], (0, 2, 1))]
    return (y_prompt, y_sample, *p_st, *s_st)
```

```python
import functools
import math

import jax
import jax.numpy as jnp
import numpy as np
from jax import lax
from jax.experimental import pallas as pl
from jax.experimental.pallas import tpu as pltpu

F32 = jnp.float32
BF16 = jnp.bfloat16
I32 = jnp.int32

D_MODEL = 1024
DEPTH = 2
PAST_LEN = 16384
SSD_HEADS = 8
SSD_WIDTH = 512
SSD_GROUPS = 2
SSD_STATE = 64
SSD_CONV = 4
SSD_XBC = 768
RET_HEADS = 4
RET_WIDTH = 256
ML_HEADS = 4
ML_WIDTH = 256
HEAD_DIM = 64
ROPE_BASE = 10000.0
EPS = 1e-6
EXPERT_GROUPS = 4
EXPERTS_PER_GROUP = 8
N_EXPERTS = 32
EXPERT_FF = 256

LANES = 128
CHUNK = 128
SEQ_PER_STEP = 4
P_Z, P_XBC, P_RET, P_ML, P_SM, P_W = 0, 512, 1280, 2304, 3328, 3456
C_SSD, C_RET, C_ML = 0, 8, 12
N_PAIRS = 8
NEG = -1e30
TOKEN_TILE = 512
MOE_TILE = 256
OUTPROJ_ROW_BLOCKS = 2
EXPERT_ROW_BLOCKS = 2
VMEM_LIMIT = 56 * 1024 * 1024
H_EXT = D_MODEL + LANES
PAIRS_PER_GROUP = 32
N_CLASS = EXPERT_GROUPS * PAIRS_PER_GROUP


def _cparams(sem):
    return pltpu.CompilerParams(dimension_semantics=sem, vmem_limit_bytes=VMEM_LIMIT)


def _split3(x):
    x1 = x.astype(BF16)
    r = x - x1.astype(F32)
    x2 = r.astype(BF16)
    r = r - x2.astype(F32)
    return x1, x2, r.astype(BF16)


def _dot01(m01, x):
    return sum(jnp.dot(m01, p, preferred_element_type=F32) for p in _split3(x))


def _dot(a, b):
    return jnp.dot(a, b, preferred_element_type=F32)


def _dot_nt(a, b):
    return lax.dot_general(a, b, (((1,), (1,)), ((), ())), preferred_element_type=F32)


def _dot_tn(a, b):
    return lax.dot_general(a, b, (((0,), (0,)), ((), ())), preferred_element_type=F32)


def _softplus(x):
    return jnp.maximum(x, 0.0) + jnp.log1p(jnp.exp(-jnp.abs(x)))


def _silu(x):
    return x * jax.nn.sigmoid(x)


def _rms(x):
    return x * lax.rsqrt(jnp.mean(x * x, axis=-1, keepdims=True) + EPS)


def _cummax_rows(x, seg):
    t = lax.broadcasted_iota(I32, x.shape, 0) & (seg - 1)
    s = 1
    while s < seg:
        x = jnp.maximum(x, jnp.where(t >= s, pltpu.roll(x, s, 0), NEG))
        s *= 2
    return x


def _rope(x, cos, sin_signed, lane):
    swapped = jnp.where((lane & 63) < 32, pltpu.roll(x, 96, 1), pltpu.roll(x, 32, 1))
    return x * cos + swapped * sin_signed


def _row_spec(tm, width):
    return pl.BlockSpec((tm, width), lambda i, *_: (i, 0))


def _const_spec(a):
    nd = a.ndim
    return pl.BlockSpec(a.shape, lambda *_: (0,) * nd)


def _layer_spec(a, l):
    nd = a.ndim
    return pl.BlockSpec((1,) + a.shape[1:], lambda *_: (l,) + (0,) * (nd - 1))


def _dual_specs(tm, np_tiles, width):
    return [pl.BlockSpec((tm, width), lambda i, *_: (jnp.minimum(i, np_tiles - 1), 0)),
            pl.BlockSpec((tm, width), lambda i, *_: (jnp.maximum(i - np_tiles, 0), 0))]


def _mod_specs(mod, k, l, tm, n_prompt, tp):
    full, bs = mod
    bp = full.shape[2] - bs
    assert bs % bp == 0 and bp % 8 == 0
    return [pl.BlockSpec((1, 1, bp, D_MODEL), lambda i, *_: (l, k, bs // bp, 0)),
            pl.BlockSpec((1, 1, bs, D_MODEL), lambda i, *_: (l, k, 0, 0))]


def _prompt_mod(p_ref, tm, tp):
    b = jnp.minimum(pl.program_id(0) * tm // tp, p_ref.shape[2] - 1)
    return p_ref[0, 0, pl.ds(b, 1), :]


def _sample_mod(s_ref, tm):
    v = s_ref[0, 0]
    return jnp.tile(v, (tm // v.shape[0], 1))


def _mod_val(p_ref, s_ref, is_prompt, tm, tp):
    return jnp.where(is_prompt, _prompt_mod(p_ref, tm, tp), _sample_mod(s_ref, tm))


def _dual_val(p_ref, s_ref, is_prompt):
    return jnp.where(is_prompt, p_ref[...], s_ref[...])


def _ada_kernel(c_ref, w_ref, b_ref, o_ref):
    c = c_ref[...]
    o_ref[0, 0] = _dot(_silu(c).astype(BF16), w_ref[0].astype(BF16)) + b_ref[0]


def _ada(c_all, w_ada, b_ada):
    nb = c_all.shape[0]
    return pl.pallas_call(
        _ada_kernel,
        grid=(DEPTH, 6),
        in_specs=[pl.BlockSpec((nb, D_MODEL), lambda l, k: (0, 0)),
                  pl.BlockSpec((1, D_MODEL, D_MODEL), lambda l, k: (l, 0, k)),
                  pl.BlockSpec((1, 1, D_MODEL), lambda l, k: (l, 0, k))],
        out_specs=pl.BlockSpec((1, 1, nb, D_MODEL), lambda l, k: (l, k, 0, 0)),
        out_shape=jax.ShapeDtypeStruct((DEPTH, 6, nb, D_MODEL), F32),
        compiler_params=_cparams(("arbitrary", "arbitrary")),
        name="ada_mod",
    )(c_all, w_ada, b_ada.reshape(DEPTH, 1, 6 * D_MODEL))


def _inproj_kernel(xp, xs, scp, scs, shp, shs, g_ref, w_ref, o_ref, *, np_tiles, tp):
    is_p = pl.program_id(0) < np_tiles
    tm = o_ref.shape[0]
    h = (_rms(_dual_val(xp, xs, is_p)) * g_ref[0] * (1.0 + _mod_val(scp, scs, is_p, tm, tp))
         + _mod_val(shp, shs, is_p, tm, tp))
    o_ref[...] = _dot_nt(h.astype(BF16), w_ref[0])


def _inproj(x, mod, k_sc, k_sh, g, w, l, tp):
    n_prompt, ns = x[0].shape[0], x[1].shape[0]
    tm = TOKEN_TILE
    ms = _mod_specs(mod, k_sc, l, tm, n_prompt, tp) + _mod_specs(mod, k_sh, l, tm, n_prompt, tp)
    return pl.pallas_call(
        functools.partial(_inproj_kernel, np_tiles=n_prompt // tm, tp=tp),
        grid=((n_prompt + ns) // tm,),
        in_specs=_dual_specs(tm, n_prompt // tm, D_MODEL) + ms + [_layer_spec(g, l), _layer_spec(w, l)],
        out_specs=_row_spec(tm, P_W),
        out_shape=jax.ShapeDtypeStruct((n_prompt + ns, P_W), F32),
        compiler_params=_cparams(("arbitrary",)),
        name="norm_inproj",
    )(*x, *([mod[0]] * 4), g, w)


def _lane_bcast(a, c):
    return jnp.broadcast_to(a[:, c:c + 1], a.shape)


def _pair_lanes(a, c0, c1):
    lm0 = lax.broadcasted_iota(I32, (a.shape[0], LANES), 1) < HEAD_DIM
    return jnp.where(lm0, a[:, c0:c0 + 1], a[:, c1:c1 + 1])


def _ret_factors(ptab, tril, mask, last_fn):
    rows = tril.shape[0]
    lane = lax.broadcasted_iota(I32, (rows, LANES), 1)
    lm0 = lane < HEAD_DIM
    cum = _dot01(tril, jnp.where((lane >= C_RET) & (lane < C_ML), ptab[2:3], 0.0))
    xt = cum.T
    cum_last = last_fn(cum)
    out = []
    for p in range(RET_HEADS // 2):
        c0, c1 = C_RET + 2 * p, C_RET + 2 * p + 1
        b0, b1 = _lane_bcast(cum, c0), _lane_bcast(cum, c1)
        bp = jnp.where(lm0, b0, b1)
        wp = jnp.exp(_pair_lanes(cum_last, c0, c1) - bp)
        out.append((jnp.exp(jnp.where(mask, b0 - xt[c0:c0 + 1, :], -jnp.inf)),
                    jnp.exp(jnp.where(mask, b1 - xt[c1:c1 + 1, :], -jnp.inf)), jnp.exp(bp), wp))
    return out


def _mixer_core(z, us, retb, mlb, small, cos, sin, ptab, cw, cb, dsk, gs, gr, gm,
                mask, tril, seg, mprev, last_fn, st, y_ref, ret_factors):
    rows = small.shape[0]
    lane = lax.broadcasted_iota(I32, (rows, LANES), 1)
    lm0 = lane < HEAD_DIM

    pre = small + ptab[0:1]
    a_neg = -jnp.exp(ptab[1:2])
    dt = _softplus(pre)
    logf = -_softplus(-pre)
    la = jnp.where(lane < C_RET, dt * a_neg,
                   jnp.where(lane < C_ML, ptab[2:3], jnp.where(lane < C_ML + 4, logf, 0.0)))
    cum = _dot01(tril, la)
    ic = pltpu.roll(pre, 4, 1)
    mlm = (lane >= C_ML) & (lane < C_ML + 4)
    d = jnp.where(mlm, ic - cum, NEG)
    m_t = cum + jnp.maximum(mprev, _cummax_rows(d, seg))
    xt = jnp.where(mlm, d, cum).T
    colv = cum - m_t
    cum_last, m_last = last_fn(cum), last_fn(m_t)
    decq = jnp.where(mlm, jnp.exp(cum_last + mprev - m_last), jnp.exp(cum_last))
    yield

    def factors(c0, c1, kind):
        if kind == "ret":
            return ret_factors((c0 - C_RET) // 2)
        if kind == "ssd":
            b0, b1 = _lane_bcast(cum, c0), _lane_bcast(cum, c1)
            bp = jnp.where(lm0, b0, b1)
            wp = jnp.exp(_pair_lanes(cum_last, c0, c1) - bp)
            return (jnp.exp(jnp.where(mask, b0 - xt[c0:c0 + 1, :], -jnp.inf)),
                    jnp.exp(jnp.where(mask, b1 - xt[c1:c1 + 1, :], -jnp.inf)), jnp.exp(bp), wp)
        a0, a1 = _lane_bcast(colv, c0), _lane_bcast(colv, c1)
        eqp = jnp.exp(jnp.where(lm0, a0, a1) + _pair_lanes(mprev, c0, c1))
        wp = jnp.exp(jnp.where(lm0, _lane_bcast(d, c0), _lane_bcast(d, c1))
                     + _pair_lanes(cum_last, c0, c1) - _pair_lanes(m_last, c0, c1))
        return (jnp.exp(jnp.where(mask, a0 + xt[c0:c0 + 1, :], -jnp.inf)),
                jnp.exp(jnp.where(mask, a1 + xt[c1:c1 + 1, :], -jnp.inf)), eqp, wp)

    def pair(idx, inputs, c0, c1, kind, finish):
        qp, kp, vp = inputs()
        q0 = jnp.where(lm0, qp, 0.0)
        q1 = jnp.where(lm0, 0.0, qp)
        sc = _dot_nt(jnp.concatenate([q0, q1], axis=0).astype(BF16), kp.astype(BF16))
        yield
        d0, d1, eqp, wp = factors(c0, c1, kind)
        v01 = jnp.concatenate([jnp.where(lm0, vp, 0.0), jnp.where(lm0, 0.0, vp)], axis=0).astype(BF16)
        kw = kp * wp
        yield
        carried, qn = st.step(idx, qp, eqp, kw, vp, decq, c0, c1, kind == "ml")
        yield
        s0, s1 = sc[0:rows] * d0, sc[rows:2 * rows] * d1
        intra = _dot(jnp.concatenate([s0, s1], axis=1).astype(BF16), v01)
        yield
        finish(intra + carried, s0, s1, qn)

    def head_norm(o):
        o2 = o * o
        ms0 = jnp.sum(jnp.where(lm0, o2, 0.0), axis=-1, keepdims=True) * (1.0 / HEAD_DIM)
        ms1 = jnp.sum(jnp.where(lm0, 0.0, o2), axis=-1, keepdims=True) * (1.0 / HEAD_DIM)
        return o * jnp.where(lm0, lax.rsqrt(ms0 + EPS), lax.rsqrt(ms1 + EPS))

    conv = cb + us[0] * cw[0:1] + us[1] * cw[1:2] + us[2] * cw[2:3] + us[3] * cw[3:4]
    xc = _silu(conv)
    bb = xc[:, SSD_WIDTH:SSD_WIDTH + LANES]
    cc = xc[:, SSD_WIDTH + LANES:SSD_WIDTH + 2 * LANES]
    br = pltpu.roll(bb, HEAD_DIM, 1)
    cr = pltpu.roll(cc, HEAD_DIM, 1)
    ys = [None] * 4
    pairs = []
    for p in range(4):
        c0, c1 = C_SSD + 2 * p, C_SSD + 2 * p + 1
        sl = slice(LANES * p, LANES * (p + 1))

        def ssd_inputs(p=p, c0=c0, c1=c1, sl=sl):
            if p < 2:
                kp, qp = jnp.where(lm0, bb, br), jnp.where(lm0, cc, cr)
            else:
                kp, qp = jnp.where(lm0, br, bb), jnp.where(lm0, cr, cc)
            dtp = jnp.where(lm0, dt[:, c0:c0 + 1], dt[:, c1:c1 + 1])
            return qp, kp, xc[:, sl] * dtp

        def ssd_finish(o, s0, s1, qn, p=p, sl=sl):
            ys[p] = (o + dsk[:, sl] * xc[:, sl]) * _silu(z[:, sl])

        pairs.append(pair(p, ssd_inputs, c0, c1, "ssd", ssd_finish))

    for p in range(2):
        c0, c1 = C_RET + 2 * p, C_RET + 2 * p + 1
        sl = slice(LANES * p, LANES * (p + 1))

        def ret_inputs(p=p, sl=sl):
            qp = _rope(retb[:, LANES * p:LANES * (p + 1)], cos[:, sl], sin[:, sl], lane)
            kp = _rope(retb[:, RET_WIDTH + LANES * p:RET_WIDTH + LANES * (p + 1)], cos[:, sl], sin[:, sl], lane)
            return qp, kp * (HEAD_DIM ** -0.5), retb[:, 2 * RET_WIDTH + LANES * p:2 * RET_WIDTH + LANES * (p + 1)]

        def ret_finish(o, s0, s1, qn, p=p, sl=sl):
            gp = retb[:, 3 * RET_WIDTH + LANES * p:3 * RET_WIDTH + LANES * (p + 1)]
            y = head_norm(o) * gr[:, sl] * _silu(gp)
            y_ref[:, SSD_WIDTH + LANES * p:SSD_WIDTH + LANES * (p + 1)] = y.astype(y_ref.dtype)

        pairs.append(pair(4 + p, ret_inputs, c0, c1, "ret", ret_finish))

    for p in range(2):
        c0, c1 = C_ML + 2 * p, C_ML + 2 * p + 1
        sl = slice(LANES * p, LANES * (p + 1))

        def ml_inputs(p=p):
            return (mlb[:, LANES * p:LANES * (p + 1)],
                    mlb[:, ML_WIDTH + LANES * p:ML_WIDTH + LANES * (p + 1)] * (HEAD_DIM ** -0.5),
                    mlb[:, 2 * ML_WIDTH + LANES * p:2 * ML_WIDTH + LANES * (p + 1)])

        def ml_finish(num, s0, s1, qn, p=p, c0=c0, c1=c1, sl=sl):
            op = mlb[:, 3 * ML_WIDTH + LANES * p:3 * ML_WIDTH + LANES * (p + 1)]
            inter0 = jnp.exp(colv[:, c0:c0 + 1] + mprev[:, c0:c0 + 1])
            inter1 = jnp.exp(colv[:, c1:c1 + 1] + mprev[:, c1:c1 + 1])
            den0 = jnp.sum(s0, axis=-1, keepdims=True) + qn[0] * inter0
            den1 = jnp.sum(s1, axis=-1, keepdims=True) + qn[1] * inter1
            dn0 = jnp.maximum(jnp.abs(den0), jnp.exp(-m_t[:, c0:c0 + 1]))
            dn1 = jnp.maximum(jnp.abs(den1), jnp.exp(-m_t[:, c1:c1 + 1]))
            hh = num / jnp.where(lm0, dn0, dn1)
            y = head_norm(hh) * gm[:, sl] * jax.nn.sigmoid(op)
            off = SSD_WIDTH + RET_WIDTH + LANES * p
            y_ref[:, off:off + LANES] = y.astype(y_ref.dtype)

        pairs.append(pair(6 + p, ml_inputs, c0, c1, "ml", ml_finish))

    live = list(pairs)
    while live:
        for g in list(live):
            try:
                next(g)
            except StopIteration:
                live.remove(g)
        yield

    for g in range(SSD_GROUPS):
        ya, yb = ys[2 * g], ys[2 * g + 1]
        ms = (jnp.sum(ya * ya, axis=-1, keepdims=True)
              + jnp.sum(yb * yb, axis=-1, keepdims=True)) * (1.0 / (2 * LANES))
        r = lax.rsqrt(ms + EPS)
        for j, yv in ((2 * g, ya), (2 * g + 1, yb)):
            sl = slice(LANES * j, LANES * (j + 1))
            y_ref[:, sl] = (yv * r * gs[:, sl]).astype(y_ref.dtype)
    return m_t


def _run_interleaved(gens):
    out = [None] * len(gens)
    live = list(range(len(gens)))
    while live:
        for k in list(live):
            try:
                next(gens[k])
            except StopIteration as stop:
                out[k] = stop.value
                live.remove(k)
    return out


def _half_rows():
    return lax.broadcasted_iota(I32, (LANES, HEAD_DIM), 0) < HEAD_DIM


class _CarriedState:
    def __init__(self, sv, nrow):
        self.sv, self.nrow = sv, nrow

    def step(self, idx, qp, eqp, kw, vp, decq, c0, c1, ml):
        lm0 = lax.broadcasted_iota(I32, (1, LANES), 1) < HEAD_DIM
        drow = _pair_lanes(decq[0:1, :], c0, c1)
        s_old = self.sv[idx]
        carried = _dot(qp.astype(BF16), s_old.astype(BF16)) * eqp
        u = _dot_tn(kw.astype(BF16), vp.astype(BF16))
        same_head = ((lax.broadcasted_iota(I32, (LANES, LANES), 0) < HEAD_DIM)
                     == (lax.broadcasted_iota(I32, (LANES, LANES), 1) < HEAD_DIM))
        self.sv[idx] = s_old * drow + jnp.where(same_head, u, 0.0)
        qn = None
        if ml:
            p = idx - 6
            n_old = self.nrow[p:p + 1, :]
            qn_l = qp * n_old
            qn = (jnp.sum(jnp.where(lm0, qn_l, 0.0), axis=-1, keepdims=True),
                  jnp.sum(jnp.where(lm0, 0.0, qn_l), axis=-1, keepdims=True))
            self.nrow[p:p + 1, :] = n_old * drow + jnp.sum(kw, axis=0, keepdims=True)
        return carried, qn


def _mixer_prompt_kernel(*refs):
    sq = SEQ_PER_STEP
    proj_refs = refs[:sq]
    (cos_ref, sin_ref, ptab_ref, cw_ref, cb_ref, dsk_ref, gs_ref, gr_ref, gm_ref,
     y_ref, sv_o, conv_o, n_o, m_o) = refs[sq:sq + 14]
    scr = refs[sq + 14:]
    sv, nrow, mrow, cbuf = scr[0:sq], scr[sq:2 * sq], scr[2 * sq:3 * sq], scr[3 * sq:4 * sq]
    ret_cache = scr[4 * sq]
    ci = pl.program_id(1)
    rows = proj_refs[0].shape[1]
    ri = lax.broadcasted_iota(I32, (rows, rows), 0)
    cj = lax.broadcasted_iota(I32, (rows, rows), 1)
    mask = cj <= ri
    tril = jnp.where(mask, 1.0, 0.0).astype(BF16)
    last_row = lambda a: a[rows - 1:rows, :]

    @pl.when(ci == 0)
    def _():
        for s in range(sq):
            sv[s][...] = jnp.zeros_like(sv[s])
            nrow[s][...] = jnp.zeros_like(nrow[s])
            mrow[s][...] = jnp.zeros_like(mrow[s])
            cbuf[s][0:8, :] = jnp.zeros((8, SSD_XBC), F32)
        for p, fs in enumerate(_ret_factors(ptab_ref[0], tril, mask, last_row)):
            for k, f in enumerate(fs):
                ret_cache[4 * p + k] = f

    ret_factors = lambda p: tuple(ret_cache[4 * p + k] for k in range(4))
    gens = []
    for s in range(sq):
        proj_ref = proj_refs[s]
        cbuf[s][8:8 + rows, :] = proj_ref[0, :, P_XBC:P_RET]
        us = [cbuf[s][pl.ds(5 + k, rows), :] for k in range(SSD_CONV)]
        gens.append(_mixer_core(
            proj_ref[0, :, P_Z:P_XBC], us, proj_ref[0, :, P_RET:P_ML], proj_ref[0, :, P_ML:P_SM],
            proj_ref[0, :, P_SM:P_W], cos_ref[...], sin_ref[...], ptab_ref[0], cw_ref[0], cb_ref[0],
            dsk_ref[0], gs_ref[0], gr_ref[0], gm_ref[0],
            mask, tril, rows, mrow[s][0:1, :], last_row,
            _CarriedState(sv[s], nrow[s]), y_ref.at[s], ret_factors))
    for s, m_t in enumerate(_run_interleaved(gens)):
        mrow[s][0:1, :] = m_t[rows - 1:rows, :]
        cbuf[s][0:8, :] = cbuf[s][rows:rows + 8, :]

    @pl.when(ci == pl.num_programs(1) - 1)
    def _():
        top = _half_rows()
        for s in range(sq):
            for idx in range(N_PAIRS):
                bd = sv[s][idx]
                sv_o[s, idx] = jnp.where(top, bd[:, :HEAD_DIM], bd[:, HEAD_DIM:])
            conv_o[s] = cbuf[s][0:8, :]
            n_o[s] = nrow[s][...]
            m_o[s] = mrow[s][...]


def _mixer_prompt(proj, nb, t, cos, sin, consts, l):
    rows = math.gcd(t, CHUNK)
    nc = t // rows
    sq = SEQ_PER_STEP
    assert nb % sq == 0

    def proj_spec(s):
        return pl.BlockSpec((1, rows, P_W), lambda b, c: (0, (sq * b + s) * nc + c, 0))

    def seq_spec(*tail):
        return pl.BlockSpec((sq,) + tail, lambda b, c: (b,) + (0,) * len(tail))

    outs = pl.pallas_call(
        _mixer_prompt_kernel,
        grid=(nb // sq, nc),
        in_specs=[proj_spec(s) for s in range(sq)]
                 + [pl.BlockSpec((rows, RET_WIDTH), lambda b, c: (c, 0)),
                    pl.BlockSpec((rows, RET_WIDTH), lambda b, c: (c, 0))] + [_layer_spec(a, l) for a in consts],
        out_specs=[pl.BlockSpec((sq, rows, D_MODEL), lambda b, c: (b, c, 0)),
                   seq_spec(N_PAIRS, LANES, HEAD_DIM), seq_spec(8, SSD_XBC), seq_spec(8, LANES), seq_spec(8, LANES)],
        out_shape=[jax.ShapeDtypeStruct((nb, t, D_MODEL), BF16),
                   jax.ShapeDtypeStruct((nb, N_PAIRS, LANES, HEAD_DIM), F32),
                   jax.ShapeDtypeStruct((nb, 8, SSD_XBC), F32),
                   jax.ShapeDtypeStruct((nb, 8, LANES), F32),
                   jax.ShapeDtypeStruct((nb, 8, LANES), F32)],
        scratch_shapes=[pltpu.VMEM((N_PAIRS, LANES, LANES), F32) for _ in range(sq)]
                       + [pltpu.VMEM((8, LANES), F32) for _ in range(2 * sq)]
                       + [pltpu.VMEM((rows + 8, SSD_XBC), F32) for _ in range(sq)]
                       + [pltpu.VMEM((2 * RET_HEADS, rows, LANES), F32)],
        compiler_params=_cparams(("arbitrary", "arbitrary")),
        name="mixer_prompt",
    )(*([proj[None]] * sq), cos, sin, *consts)
    y, sv, conv, n, m = outs
    sv = sv.reshape(nb, 2 * N_PAIRS, HEAD_DIM, HEAD_DIM)
    states = (sv[:, :8], conv[:, 5:8], sv[:, 8:12], sv[:, 12:16],
              n[:, 0:2].reshape(nb, ML_HEADS, HEAD_DIM), m[:, 0, C_ML:C_ML + 4])
    return y.reshape(nb * t, D_MODEL), states


def _lane_recurrence(s_in, s_out, q_s, k_s, v_s, decay):
    n_tok = len(decay)
    nvb = HEAD_DIM // 8
    nb = q_s.shape[-1]
    dec8 = [jnp.broadcast_to(d, (8, nb)) for d in decay]

    def body(k, acc):
        acc = [list(a) for a in acc]
        qk = [q_s[t, pl.ds(k, 1), :] for t in range(n_tok)]
        kk = [k_s[t, pl.ds(k, 1), :] for t in range(n_tok)]
        for vb in range(nvb):
            rows = pl.ds(8 * vb, 8)
            s = s_in[k, rows, :]
            for t in range(n_tok):
                s = s * dec8[t] + kk[t] * v_s[t, rows, :]
                acc[t][vb] = acc[t][vb] + qk[t] * s
            s_out[k, rows, :] = s
        return tuple(tuple(a) for a in acc)

    init = tuple(tuple(jnp.zeros((8, nb), F32) for _ in range(nvb)) for _ in range(n_tok))
    acc = lax.fori_loop(0, HEAD_DIM, body, init, unroll=2)
    return [jnp.concatenate(list(a), axis=0) for a in acc]


def _conv_T(u_ref, tail_ref, w_ref, b_ref, state_ref):
    n_tok = u_ref.shape[0]
    full = [tail_ref[0, j] for j in range(SSD_CONV - 1)] + [u_ref[t] for t in range(n_tok)]
    outs = []
    for t in range(n_tok):
        acc = b_ref[0]
        for tap in range(SSD_CONV):
            acc = acc + full[t + tap] * w_ref[0, tap]
        outs.append(_silu(acc))
    for j in range(SSD_CONV - 1):
        state_ref[j] = full[n_tok + j]
    return outs


def _ssd_T_kernel(z_ref, xs_ref, b_ref, c_ref, sm_ref, tx_ref, tb_ref, tc_ref, wx_ref, wb_ref, wc_ref,
                  bx_ref, bb_ref, bc_ref, dtb_ref, alog_ref, dsk_ref, g_ref, s_ref,
                  y_ref, so_ref, cx_ref, cb_ref, cc_ref, q_s, k_s, v_s):
    h = pl.program_id(0)
    n_tok = xs_ref.shape[0]
    xs = _conv_T(xs_ref, tx_ref, wx_ref, bx_ref, cx_ref)
    bm = _conv_T(b_ref, tb_ref, wb_ref, bb_ref, cb_ref)
    cm = _conv_T(c_ref, tc_ref, wc_ref, bc_ref, cc_ref)
    a_neg = -jnp.exp(alog_ref[0, 0])
    decay = []
    for t in range(n_tok):
        dt = _softplus(sm_ref[t, pl.ds(C_SSD + h, 1), :] + dtb_ref[0, 0])
        decay.append(jnp.exp(dt * a_neg))
        q_s[t] = cm[t]
        k_s[t] = bm[t]
        v_s[t] = xs[t] * dt
    o = _lane_recurrence(s_ref.at[0, 0], so_ref.at[0, 0], q_s, k_s, v_s, decay)
    hh = h % (SSD_HEADS // SSD_GROUPS)
    row0 = pl.multiple_of(hh * HEAD_DIM, HEAD_DIM)
    for t in range(n_tok):
        y_ref[t, pl.ds(row0, HEAD_DIM), :] = (o[t] + dsk_ref[0, 0] * xs[t]) * _silu(z_ref[t])

    @pl.when(hh == SSD_HEADS // SSD_GROUPS - 1)
    def _():
        for t in range(n_tok):
            blk = y_ref[t]
            ms = jnp.mean(blk * blk, axis=0, keepdims=True)
            y_ref[t] = blk * lax.rsqrt(ms + EPS) * g_ref[0]


def _ret_T_kernel(q_ref, k_ref, v_ref, g_ref, cos_ref, sin_ref, gam_ref, gn_ref, s_ref,
                  y_ref, so_ref, q_s, k_s, v_s):
    n_tok = q_ref.shape[0]
    half = HEAD_DIM // 2

    def rope(x, t):
        x1, x2 = x[:half], x[half:]
        return jnp.concatenate([x1 * cos_ref[t] - x2 * sin_ref[t], x1 * sin_ref[t] + x2 * cos_ref[t]], axis=0)

    for t in range(n_tok):
        q_s[t] = rope(q_ref[t], t)
        k_s[t] = rope(k_ref[t], t) * (HEAD_DIM ** -0.5)
        v_s[t] = v_ref[t]
    o = _lane_recurrence(s_ref.at[0, 0], so_ref.at[0, 0], q_s, k_s, v_s, [gam_ref[0]] * n_tok)
    for t in range(n_tok):
        ms = jnp.mean(o[t] * o[t], axis=0, keepdims=True)
        y_ref[t] = o[t] * lax.rsqrt(ms + EPS) * gn_ref[0] * _silu(g_ref[t])


def _ml_T_kernel(q_ref, k_ref, v_ref, o_ref, sm_ref, bi_ref, bf_ref, gn_ref, c_ref, n_ref, m_ref,
                 y_ref, co_ref, no_ref, mo_ref, q_s, k_s, v_s):
    h = pl.program_id(0)
    n_tok = q_ref.shape[0]
    m = m_ref[0, 0]
    n = n_ref[0, 0]
    decay, qn, m_all = [], [], []
    for t in range(n_tok):
        i_t = sm_ref[t, pl.ds(C_RET + h, 1), :] + bi_ref[0, 0]
        logf = -_softplus(-(sm_ref[t, pl.ds(C_ML + h, 1), :] + bf_ref[0, 0]))
        m_new = jnp.maximum(logf + m, i_t)
        f_t = jnp.exp(logf + m - m_new)
        kw = k_ref[t] * (HEAD_DIM ** -0.5) * jnp.exp(i_t - m_new)
        n = n * f_t + kw
        q_s[t] = q_ref[t]
        k_s[t] = kw
        v_s[t] = v_ref[t]
        decay.append(f_t)
        qn.append(jnp.sum(q_ref[t] * n, axis=0, keepdims=True))
        m_all.append(m_new)
        m = m_new
    num = _lane_recurrence(c_ref.at[0, 0], co_ref.at[0, 0], q_s, k_s, v_s, decay)
    no_ref[0] = n
    mo_ref[0] = m
    for t in range(n_tok):
        hh = num[t] / jnp.maximum(jnp.abs(qn[t]), jnp.exp(-m_all[t]))
        ms = jnp.mean(hh * hh, axis=0, keepdims=True)
        y_ref[t] = hh * lax.rsqrt(ms + EPS) * gn_ref[0] * jax.nn.sigmoid(o_ref[t])


def _mixer_sample_T(projT, convT, cwb, cbb, ssdT, retT, cT, nT, mT, tabs, cos, sin, l):
    n_tok, _, nb = projT.shape
    dtb, alog, dsk, gs, gam, gr, bi, bf, gm = tabs
    hb = lambda col: col // HEAD_DIM
    rep = SSD_HEADS // SSD_GROUPS
    proj_blk = lambda first, div=1: pl.BlockSpec((n_tok, HEAD_DIM, nb), lambda h: (0, first + h // div, 0))
    small = pl.BlockSpec((n_tok, LANES, nb), lambda h: (0, P_SM // LANES, 0))
    lay4 = lambda a, first=0, div=1: pl.BlockSpec((1,) + a.shape[1:2] + (HEAD_DIM, a.shape[-1]),
                                                  lambda h: (l, 0, first + h // div, 0))
    chan = lambda a, first=0, div=1, n=1: pl.BlockSpec((1, n * HEAD_DIM, 1), lambda h: (l, first + h // div, 0))
    head4 = lambda a: pl.BlockSpec((1, 1) + a.shape[2:], lambda h: (l, h) + (0,) * (a.ndim - 2))
    vm = lambda: pltpu.VMEM((n_tok, HEAD_DIM, nb), F32)
    state_spec = pl.BlockSpec((1, 1, HEAD_DIM, HEAD_DIM, nb), lambda h: (0, h, 0, 0, 0))
    stacked = lambda heads: jax.ShapeDtypeStruct((1, heads, HEAD_DIM, HEAD_DIM, nb), F32)
    b0, c0 = hb(SSD_WIDTH), hb(SSD_WIDTH + SSD_GROUPS * SSD_STATE)

    y_ssd, ssd_n, cx, cb, cc = pl.pallas_call(
        _ssd_T_kernel,
        grid=(SSD_HEADS,),
        in_specs=[proj_blk(hb(P_Z)), proj_blk(hb(P_XBC)), proj_blk(hb(P_XBC) + b0, rep),
                  proj_blk(hb(P_XBC) + c0, rep), small,
                  lay4(convT), lay4(convT, b0, rep), lay4(convT, c0, rep),
                  lay4(cwb), lay4(cwb, b0, rep), lay4(cwb, c0, rep),
                  chan(cbb), chan(cbb, b0, rep), chan(cbb, c0, rep),
                  head4(dtb), head4(alog), head4(dsk), chan(gs, 0, rep, rep), head4(ssdT)],
        out_specs=[pl.BlockSpec((n_tok, rep * HEAD_DIM, nb), lambda h: (0, h // rep, 0)), state_spec,
                   pl.BlockSpec((SSD_CONV - 1, HEAD_DIM, nb), lambda h: (0, h, 0)),
                   pl.BlockSpec((SSD_CONV - 1, HEAD_DIM, nb), lambda h: (0, h // rep, 0)),
                   pl.BlockSpec((SSD_CONV - 1, HEAD_DIM, nb), lambda h: (0, h // rep, 0))],
        out_shape=[jax.ShapeDtypeStruct((n_tok, SSD_WIDTH, nb), F32),
                   stacked(SSD_HEADS),
                   jax.ShapeDtypeStruct((SSD_CONV - 1, SSD_WIDTH, nb), F32),
                   jax.ShapeDtypeStruct((SSD_CONV - 1, SSD_GROUPS * SSD_STATE, nb), F32),
                   jax.ShapeDtypeStruct((SSD_CONV - 1, SSD_GROUPS * SSD_STATE, nb), F32)],
        scratch_shapes=[vm(), vm(), vm()],
        compiler_params=_cparams(("arbitrary",)),
        name="sample_ssd",
    )(projT, projT, projT, projT, projT, convT, convT, convT, cwb, cwb, cwb, cbb, cbb, cbb,
      dtb, alog, dsk, gs, ssdT)

    y_ret, ret_n = pl.pallas_call(
        _ret_T_kernel,
        grid=(RET_HEADS,),
        in_specs=[proj_blk(hb(P_RET)), proj_blk(hb(P_RET + RET_WIDTH)), proj_blk(hb(P_RET + 2 * RET_WIDTH)),
                  proj_blk(hb(P_RET + 3 * RET_WIDTH)), _const_spec(cos), _const_spec(sin),
                  pl.BlockSpec((1, 1, 1), lambda h: (h, 0, 0)), chan(gr), head4(retT)],
        out_specs=[pl.BlockSpec((n_tok, HEAD_DIM, nb), lambda h: (0, h, 0)), state_spec],
        out_shape=[jax.ShapeDtypeStruct((n_tok, RET_WIDTH, nb), F32),
                   stacked(RET_HEADS)],
        scratch_shapes=[vm(), vm(), vm()],
        compiler_params=_cparams(("arbitrary",)),
        name="sample_ret",
    )(projT, projT, projT, projT, cos, sin, gam, gr, retT)

    y_ml, c_n, n_n, m_n = pl.pallas_call(
        _ml_T_kernel,
        grid=(ML_HEADS,),
        in_specs=[proj_blk(hb(P_ML)), proj_blk(hb(P_ML + ML_WIDTH)), proj_blk(hb(P_ML + 2 * ML_WIDTH)),
                  proj_blk(hb(P_ML + 3 * ML_WIDTH)), small, head4(bi), head4(bf),
                  chan(gm), head4(cT), head4(nT), head4(mT)],
        out_specs=[pl.BlockSpec((n_tok, HEAD_DIM, nb), lambda h: (0, h, 0)), state_spec,
                   pl.BlockSpec((1, HEAD_DIM, nb), lambda h: (h, 0, 0)),
                   pl.BlockSpec((1, 1, nb), lambda h: (h, 0, 0))],
        out_shape=[jax.ShapeDtypeStruct((n_tok, ML_WIDTH, nb), F32),
                   stacked(ML_HEADS),
                   jax.ShapeDtypeStruct((ML_HEADS, HEAD_DIM, nb), F32),
                   jax.ShapeDtypeStruct((ML_HEADS, 1, nb), F32)],
        scratch_shapes=[vm(), vm(), vm()],
        compiler_params=_cparams(("arbitrary",)),
        name="sample_mlstm",
    )(projT, projT, projT, projT, projT, bi, bf, gm, cT, nT, mT)

    y_t = jnp.concatenate([y_ssd, y_ret, y_ml], axis=1)
    conv_n = jnp.concatenate([cx, cb, cc], axis=1)
    return y_t, (ssd_n, conv_n, ret_n, c_n, n_n, m_n)


def _route(logits):
    lane = lax.broadcasted_iota(I32, logits.shape, 1)
    gmask = (lane >= N_EXPERTS) & (lane < N_EXPERTS + EXPERT_GROUPS)
    gl = jnp.where(gmask, logits, -jnp.inf)
    ge = jnp.exp(gl - jnp.max(gl, axis=-1, keepdims=True))
    gprob = ge / jnp.sum(ge, axis=-1, keepdims=True)
    g_w = jnp.max(gprob, axis=-1, keepdims=True)
    g_idx = jnp.min(jnp.where(gmask & (gprob == g_w), lane - N_EXPERTS, LANES), axis=-1, keepdims=True)
    emask = (lane < N_EXPERTS) & ((lane >> 3) == g_idx)
    el = jnp.where(emask, logits, -jnp.inf)
    ee = jnp.exp(el - jnp.max(el, axis=-1, keepdims=True))
    eprob = ee / jnp.sum(ee, axis=-1, keepdims=True)
    p1 = jnp.max(jnp.where(emask, eprob, -1.0), axis=-1, keepdims=True)
    i1 = jnp.min(jnp.where(emask & (eprob == p1), lane, LANES), axis=-1, keepdims=True)
    rest = emask & (lane != i1)
    p2 = jnp.max(jnp.where(rest, eprob, -1.0), axis=-1, keepdims=True)
    i2 = jnp.min(jnp.where(rest & (eprob == p2), lane, LANES), axis=-1, keepdims=True)
    tot = p1 + p2
    lo, hi = jnp.minimum(i1, i2) & 7, jnp.maximum(i1, i2) & 7
    key = g_idx * PAIRS_PER_GROUP + ((lo * (13 - lo)) >> 1) + hi - 1
    return g_w * (p1 / tot), g_w * (p2 / tot), i1, i2, key


def _outproj_kernel(yp, ys, xp, xs, gtp, gts, scp, scs, shp, shs, g_ref, w_ref, wr_ref, br_ref,
                    x1_ref, hx_ref, info_ref, cnt_ref, carry, *, np_tiles, tp):
    i = pl.program_id(0)
    is_p = i < np_tiles

    @pl.when(i == 0)
    def _():
        carry[...] = jnp.zeros_like(carry)

    tm = x1_ref.shape[0]
    nblk = OUTPROJ_ROW_BLOCKS
    rb = tm // nblk
    blocks = [slice(k * rb, (k + 1) * rb) for k in range(nblk)]

    def body(x_ref, y_ref, gt, sc, sh):
        mod = lambda m, b: m if m.shape[0] == 1 else m[b]
        x1s = [x_ref[b, :] + mod(gt, b) * _dot(y_ref[b, :], w_ref[0]) for b in blocks]
        for b, x1 in zip(blocks, x1s):
            x1_ref[b, :] = x1
        h2bs = [(_rms(x1) * g_ref[0] * (1.0 + mod(sc, b)) + mod(sh, b)).astype(BF16)
                for b, x1 in zip(blocks, x1s)]
        routes = [_route(_dot(h2b, wr_ref[0]) + br_ref[0]) for h2b in h2bs]

        tril = (lax.broadcasted_iota(I32, (rb, rb), 1) <= lax.broadcasted_iota(I32, (rb, rb), 0))
        tril = jnp.where(tril, 1.0, 0.0).astype(BF16)
        onehots = [lax.broadcasted_iota(I32, (rb, N_CLASS), 1) == r[4] for r in routes]
        uptos = [_dot(tril, jnp.where(oh, 1.0, 0.0).astype(BF16)) for oh in onehots]
        seen = carry[0:1, :]
        ranks = []
        for oh, upto in zip(onehots, uptos):
            ranks.append(jnp.sum(jnp.where(oh, upto - 1.0 + seen, 0.0), axis=-1, keepdims=True))
            seen = seen + upto[rb - 1:rb, :]
        carry[0:1, :] = seen
        cnt_ref[...] = jnp.broadcast_to(seen, cnt_ref.shape)

        lane = lax.broadcasted_iota(I32, (rb, LANES), 1)
        for b, h2b, (w1, w2, i1, i2, key), rank in zip(blocks, h2bs, routes, ranks):
            info = jnp.zeros((rb, LANES), F32)
            for c, v in enumerate((w1, w2, i1.astype(F32), i2.astype(F32), key.astype(F32), rank)):
                info = jnp.where(lane == c, v, info)
            hx_ref[b, 0:D_MODEL] = h2b.astype(F32)
            hx_ref[b, D_MODEL:H_EXT] = info
            info_ref[b, :] = info[:, 0:8]

    @pl.when(is_p)
    def _():
        body(xp, yp, _prompt_mod(gtp, tm, tp), _prompt_mod(scp, tm, tp), _prompt_mod(shp, tm, tp))

    @pl.when(jnp.logical_not(is_p))
    def _():
        body(xs, ys, _sample_mod(gts, tm), _sample_mod(scs, tm), _sample_mod(shs, tm))


def _outproj(y, x, mod, k_gt, k_sc, k_sh, g, w, wr, br, l, tp):
    n_prompt, ns = x[0].shape[0], x[1].shape[0]
    n = n_prompt + ns
    tm = TOKEN_TILE
    npt = n_prompt // tm
    ms = [s for k in (k_gt, k_sc, k_sh) for s in _mod_specs(mod, k, l, tm, n_prompt, tp)]
    return pl.pallas_call(
        functools.partial(_outproj_kernel, np_tiles=npt, tp=tp),
        grid=(n // tm,),
        in_specs=_dual_specs(tm, npt, D_MODEL) + _dual_specs(tm, npt, D_MODEL) + ms
                 + [_layer_spec(a, l) for a in (g, w, wr, br)],
        out_specs=[_row_spec(tm, D_MODEL), _row_spec(tm, H_EXT), _row_spec(tm, 8),
                   pl.BlockSpec((8, N_CLASS), lambda i: (0, 0))],
        out_shape=[jax.ShapeDtypeStruct((n, D_MODEL), F32),
                   jax.ShapeDtypeStruct((n, H_EXT), F32),
                   jax.ShapeDtypeStruct((n, 8), F32),
                   jax.ShapeDtypeStruct((8, N_CLASS), F32)],
        scratch_shapes=[pltpu.VMEM((8, N_CLASS), F32)],
        compiler_params=_cparams(("arbitrary",)),
        name="outproj_router",
    )(*y, *x, *([mod[0]] * 6), g, w, wr, br)


def _scatter_kernel(pos_ref, x_ref, o_hbm, buf, sem):
    i, n = pl.program_id(0), pl.num_programs(0)
    tm = x_ref.shape[0]
    slot = i % 2

    def wait_slot(s):
        pltpu.make_async_copy(buf.at[s], o_hbm.at[pl.ds(0, tm), :], sem.at[s]).wait()

    @pl.when(i >= 2)
    def _():
        wait_slot(slot)

    buf[slot] = x_ref[...]

    def body(r, c):
        dst = pos_ref[i * tm + r]
        pltpu.make_async_copy(buf.at[slot, pl.ds(r, 1), :], o_hbm.at[pl.ds(dst, 1), :], sem.at[slot]).start()
        return c
    lax.fori_loop(0, tm, body, 0, unroll=16)

    @pl.when(i == n - 1)
    def _():
        wait_slot(slot)

        @pl.when(n >= 2)
        def _():
            wait_slot(1 - slot)


def _scatter_rows(x, pos):
    n, width = x.shape
    tm = TOKEN_TILE
    return pl.pallas_call(
        _scatter_kernel,
        grid_spec=pltpu.PrefetchScalarGridSpec(
            num_scalar_prefetch=1, grid=(n // tm,),
            in_specs=[_row_spec(tm, width)],
            out_specs=pl.BlockSpec(memory_space=pl.ANY),
            scratch_shapes=[pltpu.VMEM((2, tm, width), x.dtype), pltpu.SemaphoreType.DMA((2,))]),
        out_shape=jax.ShapeDtypeStruct((n, width), x.dtype),
        compiler_params=_cparams(("arbitrary",)),
        name="scatter_rows",
    )(pos, x)


def _gather_tiles(idx_ref, src_hbm, buf, sem, tm):
    i, n = pl.program_id(0), pl.num_programs(0)

    def issue(tile, slot):
        def body(r, c):
            row = idx_ref[tile * tm + r]
            pltpu.make_async_copy(src_hbm.at[pl.ds(row, 1), :], buf.at[slot, pl.ds(r, 1), :],
                                  sem.at[slot]).start()
            return c
        lax.fori_loop(0, tm, body, 0, unroll=16)

    @pl.when(i == 0)
    def _():
        issue(0, 0)

    @pl.when(i + 1 < n)
    def _():
        issue(i + 1, (i + 1) % 2)

    slot = i % 2
    pltpu.make_async_copy(src_hbm.at[pl.ds(0, tm), :], buf.at[slot], sem.at[slot]).wait()
    return slot


def _experts_kernel(tile_ref, grp_ref, flag_ref, hx_ref, wgu_f32, wd_f32, o_ref, wgu_ref, wd_ref):
    j = pl.program_id(0)
    flags = flag_ref[j]
    valid = (flags & 1) != 0

    @pl.when((flags & 4) != 0)
    def _():
        for e in range(EXPERTS_PER_GROUP):
            wgu_ref[e] = wgu_f32[0, 0, e].astype(BF16)
            wd_ref[e] = wd_f32[0, 0, e].astype(BF16)

    @pl.when(valid & ((flags & 2) != 0))
    def _():
        o_ref[...] = jnp.zeros_like(o_ref)

    @pl.when(valid)
    def _():
        h = hx_ref[:, 0:D_MODEL].astype(BF16)
        r = hx_ref[:, D_MODEL:H_EXT]
        w1, w2, i1, i2 = r[:, 0:1], r[:, 1:2], r[:, 2:3], r[:, 3:4]
        base = grp_ref[j] * EXPERTS_PER_GROUP
        for e in range(EXPERTS_PER_GROUP):
            @pl.when(((flags >> (8 + e)) & 1) != 0)
            def _():
                eid = (base + e).astype(F32)
                ge = jnp.where(i1 == eid, w1, 0.0) + jnp.where(i2 == eid, w2, 0.0)
                nblk = EXPERT_ROW_BLOCKS
                rb = h.shape[0] // nblk
                aus = [_dot(h[k * rb:(k + 1) * rb], wgu_ref[e]) for k in range(nblk)]
                acts = [(_silu(au[:, :EXPERT_FF]) * au[:, EXPERT_FF:]).astype(BF16) for au in aus]
                yes = [_dot(act, wd_ref[e]) for act in acts]
                for k in range(nblk):
                    o_ref[k * rb:(k + 1) * rb, :] += ge[k * rb:(k + 1) * rb] * yes[k]


def _experts(hx_sorted, tile, grp, flags, wgu, wd, l):
    n = hx_sorted.shape[0]
    tm = MOE_TILE
    return pl.pallas_call(
        _experts_kernel,
        grid_spec=pltpu.PrefetchScalarGridSpec(
            num_scalar_prefetch=3, grid=(tile.shape[0],),
            in_specs=[pl.BlockSpec((tm, H_EXT), lambda j, t, g, f: (t[j], 0)),
                      pl.BlockSpec((1, 1, EXPERTS_PER_GROUP, D_MODEL, 2 * EXPERT_FF),
                                   lambda j, t, g, f: (l, g[j], 0, 0, 0), pipeline_mode=pl.Buffered(1)),
                      pl.BlockSpec((1, 1, EXPERTS_PER_GROUP, EXPERT_FF, D_MODEL),
                                   lambda j, t, g, f: (l, g[j], 0, 0, 0), pipeline_mode=pl.Buffered(1))],
            out_specs=pl.BlockSpec((tm, D_MODEL), lambda j, t, g, f: (t[j], 0)),
            scratch_shapes=[pltpu.VMEM((EXPERTS_PER_GROUP, D_MODEL, 2 * EXPERT_FF), BF16),
                            pltpu.VMEM((EXPERTS_PER_GROUP, EXPERT_FF, D_MODEL), BF16)]),
        out_shape=jax.ShapeDtypeStruct((n, D_MODEL), F32),
        compiler_params=_cparams(("arbitrary",)),
        name="moe_experts",
    )(tile, grp, flags, hx_sorted, wgu, wd)


def _combine_kernel(pos_ref, y_hbm, x1_ref, gtp, gts, gf_ref, op_ref, os_ref, buf, sem, *, np_tiles, tp, final):
    slot = _gather_tiles(pos_ref, y_hbm, buf, sem, x1_ref.shape[0])
    is_p = pl.program_id(0) < np_tiles
    x2 = x1_ref[...] + _mod_val(gtp, gts, is_p, x1_ref.shape[0], tp) * buf[slot]
    out = _rms(x2) * gf_ref[...] if final else x2

    @pl.when(is_p)
    def _():
        op_ref[...] = out

    @pl.when(jnp.logical_not(is_p))
    def _():
        os_ref[...] = out


def _combine(y_sorted, pos, x1, mod, k_gt, gf, l, n_prompt, tp, final):
    n = x1.shape[0]
    tm = TOKEN_TILE
    npt = n_prompt // tm
    return pl.pallas_call(
        functools.partial(_combine_kernel, np_tiles=npt, tp=tp, final=final),
        grid_spec=pltpu.PrefetchScalarGridSpec(
            num_scalar_prefetch=1, grid=(n // tm,),
            in_specs=[pl.BlockSpec(memory_space=pl.ANY), _row_spec(tm, D_MODEL)]
                     + _mod_specs(mod, k_gt, l, tm, n_prompt, tp) + [_const_spec(gf)],
            out_specs=_dual_specs(tm, npt, D_MODEL),
            scratch_shapes=[pltpu.VMEM((2, tm, D_MODEL), F32), pltpu.SemaphoreType.DMA((2,))]),
        out_shape=[jax.ShapeDtypeStruct((n_prompt, D_MODEL), F32),
                   jax.ShapeDtypeStruct((n - n_prompt, D_MODEL), F32)],
        compiler_params=_cparams(("arbitrary",)),
        name="moe_combine",
    )(pos, y_sorted, x1, mod[0], mod[0], gf)


def _class_members():
    table = np.zeros((N_CLASS, N_EXPERTS), dtype=bool)
    for grp in range(EXPERT_GROUPS):
        for lo in range(EXPERTS_PER_GROUP):
            for hi in range(lo + 1, EXPERTS_PER_GROUP):
                cls = grp * PAIRS_PER_GROUP + ((lo * (13 - lo)) >> 1) + hi - 1
                table[cls, grp * EXPERTS_PER_GROUP + lo] = True
                table[cls, grp * EXPERTS_PER_GROUP + hi] = True
    return table


def _routing_tables(cnt, key, rank, n_tiles, tm):
    n_items = n_tiles + EXPERT_GROUPS - 1
    c = jnp.arange(N_CLASS, dtype=I32)
    start_c = jnp.sum(jnp.where(c[:, None] < c[None, :], cnt[:, None], 0), axis=0)
    pos = jnp.sum(jnp.where(key[:, None] == c[None, :], start_c[None, :], 0), axis=1) + rank

    g = jnp.arange(EXPERT_GROUPS, dtype=I32)
    gend = jnp.sum(jnp.where(c[None, :] < PAIRS_PER_GROUP * (g[:, None] + 1), cnt[None, :], 0), axis=1)
    t0 = jnp.arange(n_tiles, dtype=I32) * tm
    t1 = t0 + (tm - 1)
    gfirst = jnp.sum((gend[None, :] <= t0[:, None]).astype(I32), axis=1)
    glast = jnp.sum((gend[None, :] <= t1[:, None]).astype(I32), axis=1)
    per_tile = glast - gfirst + 1
    tt = jnp.arange(n_tiles, dtype=I32)
    start_t = jnp.sum(jnp.where(tt[:, None] < tt[None, :], per_tile[:, None], 0), axis=0)
    total = jnp.sum(per_tile)

    j = jnp.arange(n_items, dtype=I32)
    valid = j < total
    tile = jnp.sum((start_t[None, :] <= j[:, None]).astype(I32), axis=1) - 1
    tile = jnp.where(valid, tile, n_tiles - 1)
    sel = tile[:, None] == tt[None, :]
    pick = lambda v: jnp.sum(jnp.where(sel, v[None, :], 0), axis=1)
    grp = jnp.where(valid, pick(gfirst) + (j - pick(start_t)), glast[n_tiles - 1])
    first = valid & (j == pick(start_t))

    in_tile = (cnt[None, :] > 0) & (start_c[None, :] <= t1[:, None]) & ((start_c + cnt)[None, :] > t0[:, None])
    e = jnp.arange(N_EXPERTS, dtype=I32)
    member = jnp.asarray(_class_members())
    present_t = jnp.any(in_tile[:, :, None] & member[None, :, :], axis=1)
    present_j = jnp.any(sel[:, :, None] & present_t[None, :, :], axis=1)
    eg = e[None, :] - grp[:, None] * EXPERTS_PER_GROUP
    bits = jnp.sum(jnp.where(present_j & (eg >= 0) & (eg < EXPERTS_PER_GROUP),
                             1 << (8 + jnp.clip(eg, 0, EXPERTS_PER_GROUP - 1)), 0), axis=1)
    new_grp = jnp.concatenate([jnp.ones((1,), bool), grp[1:] != grp[:-1]])
    flags = valid.astype(I32) | (first.astype(I32) << 1) | (new_grp.astype(I32) << 2) | bits
    return pos.astype(I32), tile.astype(I32), grp.astype(I32), flags.astype(I32)


def _moe(hx, info, counts, x1, mod, k_gt, gf, wgu, wd, l, n_prompt, tp, final):
    n = hx.shape[0]
    tm = MOE_TILE
    pos, tile, grp, flags = _routing_tables(counts[0].astype(I32), info[:, 4].astype(I32),
                                            info[:, 5].astype(I32), n // tm, tm)
    y_sorted = _experts(_scatter_rows(hx, pos), tile, grp, flags, wgu, wd, l)
    return _combine(y_sorted, pos, x1, mod, k_gt, gf, l, n_prompt, tp, final)


def _rope_tables(pos):
    half = HEAD_DIM // 2
    inv = ROPE_BASE ** (-jnp.arange(half, dtype=F32) / half)
    ang = pos.astype(F32)[:, None] * inv[None, :]
    cos, sin = jnp.cos(ang), jnp.sin(ang)
    cos_t = jnp.tile(jnp.concatenate([cos, cos], axis=-1), (1, RET_HEADS))
    sin_t = jnp.tile(jnp.concatenate([-sin, sin], axis=-1), (1, RET_HEADS))
    return cos_t, sin_t


def _mixer_consts(w_in, conv_w, conv_b, dt_bias, a_log, d_skip, g_ssd_norm, g_ret_norm,
                  b_mlstm_i, b_mlstm_f, g_mlstm_norm):
    nl = w_in.shape[0]
    w_t = jnp.transpose(w_in, (0, 2, 1))
    w_p = jnp.concatenate([w_t[:, 0:1280], w_t[:, 1288:3336], w_t[:, 1280:1288], w_t[:, 3336:3344],
                           jnp.zeros((nl, LANES - 16, D_MODEL), F32)], axis=1).astype(BF16)
    pad = lambda v: jnp.pad(v, ((0, 0), (0, LANES - v.shape[1])))
    log_gamma = jnp.log(1.0 - 2.0 ** (-5.0 - jnp.arange(RET_HEADS, dtype=F32)))
    lg = jnp.broadcast_to(jnp.concatenate([jnp.zeros((8,), F32), log_gamma])[None, :], (nl, 12))
    ptab = jnp.stack([pad(jnp.concatenate([dt_bias, b_mlstm_i, b_mlstm_f], axis=1)), pad(a_log), pad(lg)]
                     + [jnp.zeros((nl, LANES), F32)] * 5, axis=1)
    consts = (ptab, conv_w, conv_b[:, None, :], jnp.repeat(d_skip, HEAD_DIM, axis=1)[:, None, :],
              g_ssd_norm[:, None, :], g_ret_norm[:, None, :], g_mlstm_norm[:, None, :])
    return w_p, consts


def kernel(x_prompt, x_sample, state_ssd, state_ssd_conv, state_ret, state_mlstm_c, state_mlstm_n,
           state_mlstm_m, c_prompt, c_sample, w_ada, b_ada, g_norm1, g_norm2, w_in, conv_w, conv_b,
           dt_bias, a_log, d_skip, g_ssd_norm, g_ret_norm, b_mlstm_i, b_mlstm_f, g_mlstm_norm, w_out,
           w_router_group, b_router_group, w_router_expert, b_router_expert, w_gate_up, w_down, g_final):
    bp, tp, _ = x_prompt.shape
    bs, ts, _ = x_sample.shape
    n_srows = bs * ts
    n_prompt = bp * tp

    mod4 = _ada(jnp.concatenate([c_sample, c_prompt], axis=0), w_ada, b_ada)
    mod = (mod4, bs)
    sh1, sc1, gt1, sh2, sc2, gt2 = range(6)

    cos_p, sin_p = _rope_tables(jnp.arange(tp, dtype=I32))
    lanes = lambda v: v[..., None]
    half = HEAD_DIM // 2
    ang = (PAST_LEN + jnp.arange(ts, dtype=I32)).astype(F32)[:, None] * (
        ROPE_BASE ** (-jnp.arange(half, dtype=F32) / half))[None, :]
    cos_s, sin_s = lanes(jnp.cos(ang)), lanes(jnp.sin(ang))
    gamma = 1.0 - 2.0 ** (-5.0 - jnp.arange(RET_HEADS, dtype=F32))
    tabs = (lanes(dt_bias[:, :, None]), lanes(a_log[:, :, None]), lanes(d_skip[:, :, None]), lanes(g_ssd_norm),
            lanes(gamma[:, None]), lanes(g_ret_norm), lanes(b_mlstm_i[:, :, None]), lanes(b_mlstm_f[:, :, None]),
            lanes(g_mlstm_norm))
    cwb, cbb = lanes(conv_w), lanes(conv_b)
    ssd_t = jnp.transpose(state_ssd, (0, 2, 3, 4, 1))
    ret_t = jnp.transpose(state_ret, (0, 2, 3, 4, 1))
    c_t = jnp.transpose(state_mlstm_c, (0, 2, 3, 4, 1))
    n_t = jnp.transpose(state_mlstm_n, (0, 2, 3, 1))
    m_t = jnp.transpose(state_mlstm_m, (0, 2, 1))[:, :, None, :]
    conv_t = jnp.transpose(state_ssd_conv, (0, 2, 3, 1))

    w_p, consts = _mixer_consts(w_in, conv_w, conv_b, dt_bias, a_log, d_skip, g_ssd_norm, g_ret_norm,
                                b_mlstm_i, b_mlstm_f, g_mlstm_norm)
    w_o = w_out.astype(BF16)
    zpad = LANES - N_EXPERTS - EXPERT_GROUPS
    wr = jnp.concatenate([w_router_expert, w_router_group, jnp.zeros((DEPTH, D_MODEL, zpad), F32)],
                         axis=2).astype(BF16)
    br = jnp.concatenate([b_router_expert, b_router_group, jnp.zeros((DEPTH, zpad), F32)], axis=1)[:, None, :]
    wgu = w_gate_up.reshape(DEPTH, EXPERT_GROUPS, EXPERTS_PER_GROUP, D_MODEL, 2 * EXPERT_FF)
    wd = w_down.reshape(DEPTH, EXPERT_GROUPS, EXPERTS_PER_GROUP, EXPERT_FF, D_MODEL)
    g1, g2, gf = g_norm1[:, None, :], g_norm2[:, None, :], g_final[None, :]

    x = (x_prompt.reshape(n_prompt, D_MODEL), jnp.transpose(x_sample, (1, 0, 2)).reshape(n_srows, D_MODEL))
    p_states, s_states = [], []
    for l in range(DEPTH):
        final = l == DEPTH - 1
        proj = _inproj(x, mod, sc1, sh1, g1, w_p, l, tp)

        ycat_p, st = _mixer_prompt(proj, bp, tp, cos_p, sin_p, consts, l)
        p_states.append(st)

        proj_t = jnp.transpose(proj[n_prompt:].reshape(ts, bs, P_W), (0, 2, 1))
        y_t, st_t = _mixer_sample_T(proj_t, conv_t, cwb, cbb, ssd_t, ret_t, c_t, n_t, m_t, tabs, cos_s, sin_s, l)
        s_states.append(st_t)
        ycat_s = jnp.transpose(y_t, (0, 2, 1)).reshape(n_srows, D_MODEL).astype(BF16)

        x1, hx, info, counts = _outproj((ycat_p, ycat_s), x, mod, gt1, sc2, sh2, g2, w_o, wr, br, l, tp)
        x = _moe(hx, info, counts, x1, mod, gt2, gf, wgu, wd, l, n_prompt, tp, final)

    y_prompt = x[0].reshape(bp, tp, D_MODEL)
    y_sample = jnp.transpose(x[1].reshape(ts, bs, D_MODEL), (1, 0, 2))
    p_st = [jnp.stack([s[i] for s in p_states], axis=0) for i in range(6)]
    s_t = [jnp.concatenate([s[i] for s in s_states], axis=0) if i in (0, 2, 3)
           else jnp.stack([s[i] for s in s_states], axis=0) for i in range(6)]
    s_st = [jnp.transpose(s_t[0], (0, 4, 1, 2, 3)), jnp.transpose(s_t[1], (0, 3, 1, 2)),
            jnp.transpose(s_t[2], (0, 4, 1, 2, 3)), jnp.transpose(s_t[3], (0, 4, 1, 2, 3)),
            jnp.transpose(s_t[4], (0, 3, 1, 2)), jnp.transpose(s_t[5][:, :, 0, :], (0, 2, 1))]
    return (y_prompt, y_sample, *p_st, *s_st)
```

```python
import functools
import math

import jax
import jax.numpy as jnp
import numpy as np
from jax import lax
from jax.experimental import pallas as pl
from jax.experimental.pallas import tpu as pltpu

F32 = jnp.float32
BF16 = jnp.bfloat16
I32 = jnp.int32

D_MODEL = 1024
DEPTH = 2
PAST_LEN = 16384
SSD_HEADS = 8
SSD_WIDTH = 512
SSD_GROUPS = 2
SSD_STATE = 64
SSD_CONV = 4
SSD_XBC = 768
RET_HEADS = 4
RET_WIDTH = 256
ML_HEADS = 4
ML_WIDTH = 256
HEAD_DIM = 64
ROPE_BASE = 10000.0
EPS = 1e-6
EXPERT_GROUPS = 4
EXPERTS_PER_GROUP = 8
N_EXPERTS = 32
EXPERT_FF = 256

LANES = 128
CHUNK = 128
SEQ_PER_STEP = 4
P_Z, P_XBC, P_RET, P_ML, P_SM, P_W = 0, 512, 1280, 2304, 3328, 3456
C_SSD, C_RET, C_ML = 0, 8, 12
N_PAIRS = 8
NEG = -1e30
TOKEN_TILE = 512
MOE_TILE = 256
OUTPROJ_ROW_BLOCKS = 2
EXPERT_ROW_BLOCKS = 2
VMEM_LIMIT = 56 * 1024 * 1024
H_EXT = D_MODEL + LANES
PAIRS_PER_GROUP = 32
N_CLASS = EXPERT_GROUPS * PAIRS_PER_GROUP


def _cparams(sem):
    return pltpu.CompilerParams(dimension_semantics=sem, vmem_limit_bytes=VMEM_LIMIT)


def _split3(x):
    x1 = x.astype(BF16)
    r = x - x1.astype(F32)
    x2 = r.astype(BF16)
    r = r - x2.astype(F32)
    return x1, x2, r.astype(BF16)


def _dot01(m01, x):
    return sum(jnp.dot(m01, p, preferred_element_type=F32) for p in _split3(x))


def _dot(a, b):
    return jnp.dot(a, b, preferred_element_type=F32)


def _dot_nt(a, b):
    return lax.dot_general(a, b, (((1,), (1,)), ((), ())), preferred_element_type=F32)


def _dot_tn(a, b):
    return lax.dot_general(a, b, (((0,), (0,)), ((), ())), preferred_element_type=F32)


def _softplus(x):
    return jnp.maximum(x, 0.0) + jnp.log1p(jnp.exp(-jnp.abs(x)))


def _silu(x):
    return x * jax.nn.sigmoid(x)


def _rms(x):
    return x * lax.rsqrt(jnp.mean(x * x, axis=-1, keepdims=True) + EPS)


def _cummax_rows(x, seg):
    t = lax.broadcasted_iota(I32, x.shape, 0) & (seg - 1)
    s = 1
    while s < seg:
        x = jnp.maximum(x, jnp.where(t >= s, pltpu.roll(x, s, 0), NEG))
        s *= 2
    return x


def _rope(x, cos, sin_signed, lane):
    swapped = jnp.where((lane & 63) < 32, pltpu.roll(x, 96, 1), pltpu.roll(x, 32, 1))
    return x * cos + swapped * sin_signed


def _row_spec(tm, width):
    return pl.BlockSpec((tm, width), lambda i, *_: (i, 0))


def _const_spec(a):
    nd = a.ndim
    return pl.BlockSpec(a.shape, lambda *_: (0,) * nd)


def _layer_spec(a, l):
    nd = a.ndim
    return pl.BlockSpec((1,) + a.shape[1:], lambda *_: (l,) + (0,) * (nd - 1))


def _dual_specs(tm, np_tiles, width):
    return [pl.BlockSpec((tm, width), lambda i, *_: (jnp.minimum(i, np_tiles - 1), 0)),
            pl.BlockSpec((tm, width), lambda i, *_: (jnp.maximum(i - np_tiles, 0), 0))]


def _mod_specs(mod, k, l, tm, n_prompt, tp):
    full, bs = mod
    bp = full.shape[2] - bs
    assert bs % bp == 0 and bp % 8 == 0
    return [pl.BlockSpec((1, 1, bp, D_MODEL), lambda i, *_: (l, k, bs // bp, 0)),
            pl.BlockSpec((1, 1, bs, D_MODEL), lambda i, *_: (l, k, 0, 0))]


def _prompt_mod(p_ref, tm, tp):
    b = jnp.minimum(pl.program_id(0) * tm // tp, p_ref.shape[2] - 1)
    return p_ref[0, 0, pl.ds(b, 1), :]


def _sample_mod(s_ref, tm):
    v = s_ref[0, 0]
    return jnp.tile(v, (tm // v.shape[0], 1))


def _mod_val(p_ref, s_ref, is_prompt, tm, tp):
    return jnp.where(is_prompt, _prompt_mod(p_ref, tm, tp), _sample_mod(s_ref, tm))


def _dual_val(p_ref, s_ref, is_prompt):
    return jnp.where(is_prompt, p_ref[...], s_ref[...])


def _ada_kernel(c_ref, w_ref, b_ref, o_ref):
    c = c_ref[...]
    o_ref[0, 0] = _dot(_silu(c).astype(BF16), w_ref[0].astype(BF16)) + b_ref[0]


def _ada(c_all, w_ada, b_ada):
    nb = c_all.shape[0]
    return pl.pallas_call(
        _ada_kernel,
        grid=(DEPTH, 6),
        in_specs=[pl.BlockSpec((nb, D_MODEL), lambda l, k: (0, 0)),
                  pl.BlockSpec((1, D_MODEL, D_MODEL), lambda l, k: (l, 0, k)),
                  pl.BlockSpec((1, 1, D_MODEL), lambda l, k: (l, 0, k))],
        out_specs=pl.BlockSpec((1, 1, nb, D_MODEL), lambda l, k: (l, k, 0, 0)),
        out_shape=jax.ShapeDtypeStruct((DEPTH, 6, nb, D_MODEL), F32),
        compiler_params=_cparams(("arbitrary", "arbitrary")),
        name="ada_mod",
    )(c_all, w_ada, b_ada.reshape(DEPTH, 1, 6 * D_MODEL))


def _inproj_kernel(xp, xs, scp, scs, shp, shs, g_ref, w_ref, o_ref, *, np_tiles, tp):
    is_p = pl.program_id(0) < np_tiles
    tm = o_ref.shape[0]
    h = (_rms(_dual_val(xp, xs, is_p)) * g_ref[0] * (1.0 + _mod_val(scp, scs, is_p, tm, tp))
         + _mod_val(shp, shs, is_p, tm, tp))
    o_ref[...] = _dot_nt(h.astype(BF16), w_ref[0])


def _inproj(x, mod, k_sc, k_sh, g, w, l, tp):
    n_prompt, ns = x[0].shape[0], x[1].shape[0]
    tm = TOKEN_TILE
    ms = _mod_specs(mod, k_sc, l, tm, n_prompt, tp) + _mod_specs(mod, k_sh, l, tm, n_prompt, tp)
    return pl.pallas_call(
        functools.partial(_inproj_kernel, np_tiles=n_prompt // tm, tp=tp),
        grid=((n_prompt + ns) // tm,),
        in_specs=_dual_specs(tm, n_prompt // tm, D_MODEL) + ms + [_layer_spec(g, l), _layer_spec(w, l)],
        out_specs=_row_spec(tm, P_W),
        out_shape=jax.ShapeDtypeStruct((n_prompt + ns, P_W), F32),
        compiler_params=_cparams(("arbitrary",)),
        name="norm_inproj",
    )(*x, *([mod[0]] * 4), g, w)


def _lane_bcast(a, c):
    return jnp.broadcast_to(a[:, c:c + 1], a.shape)


def _pair_lanes(a, c0, c1):
    lm0 = lax.broadcasted_iota(I32, (a.shape[0], LANES), 1) < HEAD_DIM
    return jnp.where(lm0, a[:, c0:c0 + 1], a[:, c1:c1 + 1])


def _ret_factors(ptab, tril, mask, last_fn):
    rows = tril.shape[0]
    lane = lax.broadcasted_iota(I32, (rows, LANES), 1)
    lm0 = lane < HEAD_DIM
    cum = _dot01(tril, jnp.where((lane >= C_RET) & (lane < C_ML), ptab[2:3], 0.0))
    xt = cum.T
    cum_last = last_fn(cum)
    out = []
    for p in range(RET_HEADS // 2):
        c0, c1 = C_RET + 2 * p, C_RET + 2 * p + 1
        b0, b1 = _lane_bcast(cum, c0), _lane_bcast(cum, c1)
        bp = jnp.where(lm0, b0, b1)
        wp = jnp.exp(_pair_lanes(cum_last, c0, c1) - bp)
        out.append((jnp.exp(jnp.where(mask, b0 - xt[c0:c0 + 1, :], -jnp.inf)),
                    jnp.exp(jnp.where(mask, b1 - xt[c1:c1 + 1, :], -jnp.inf)), jnp.exp(bp), wp))
    return out


def _mixer_core(z, us, retb, mlb, small, cos, sin, ptab, cw, cb, dsk, gs, gr, gm,
                mask, tril, seg, mprev, last_fn, st, y_ref, ret_factors):
    rows = small.shape[0]
    lane = lax.broadcasted_iota(I32, (rows, LANES), 1)
    lm0 = lane < HEAD_DIM

    pre = small + ptab[0:1]
    a_neg = -jnp.exp(ptab[1:2])
    dt = _softplus(pre)
    logf = -_softplus(-pre)
    la = jnp.where(lane < C_RET, dt * a_neg,
                   jnp.where(lane < C_ML, ptab[2:3], jnp.where(lane < C_ML + 4, logf, 0.0)))
    cum = _dot01(tril, la)
    ic = pltpu.roll(pre, 4, 1)
    mlm = (lane >= C_ML) & (lane < C_ML + 4)
    d = jnp.where(mlm, ic - cum, NEG)
    m_t = cum + jnp.maximum(mprev, _cummax_rows(d, seg))
    xt = jnp.where(mlm, d, cum).T
    colv = cum - m_t
    cum_last, m_last = last_fn(cum), last_fn(m_t)
    decq = jnp.where(mlm, jnp.exp(cum_last + mprev - m_last), jnp.exp(cum_last))
    yield

    def factors(c0, c1, kind):
        if kind == "ret":
            return ret_factors((c0 - C_RET) // 2)
        if kind == "ssd":
            b0, b1 = _lane_bcast(cum, c0), _lane_bcast(cum, c1)
            bp = jnp.where(lm0, b0, b1)
            wp = jnp.exp(_pair_lanes(cum_last, c0, c1) - bp)
            return (jnp.exp(jnp.where(mask, b0 - xt[c0:c0 + 1, :], -jnp.inf)),
                    jnp.exp(jnp.where(mask, b1 - xt[c1:c1 + 1, :], -jnp.inf)), jnp.exp(bp), wp)
        a0, a1 = _lane_bcast(colv, c0), _lane_bcast(colv, c1)
        eqp = jnp.exp(jnp.where(lm0, a0, a1) + _pair_lanes(mprev, c0, c1))
        wp = jnp.exp(jnp.where(lm0, _lane_bcast(d, c0), _lane_bcast(d, c1))
                     + _pair_lanes(cum_last, c0, c1) - _pair_lanes(m_last, c0, c1))
        return (jnp.exp(jnp.where(mask, a0 + xt[c0:c0 + 1, :], -jnp.inf)),
                jnp.exp(jnp.where(mask, a1 + xt[c1:c1 + 1, :], -jnp.inf)), eqp, wp)

    def pair(idx, inputs, c0, c1, kind, finish):
        qp, kp, vp = inputs()
        q0 = jnp.where(lm0, qp, 0.0)
        q1 = jnp.where(lm0, 0.0, qp)
        sc = _dot_nt(jnp.concatenate([q0, q1], axis=0).astype(BF16), kp.astype(BF16))
        yield
        d0, d1, eqp, wp = factors(c0, c1, kind)
        v01 = jnp.concatenate([jnp.where(lm0, vp, 0.0), jnp.where(lm0, 0.0, vp)], axis=0).astype(BF16)
        kw = kp * wp
        yield
        carried, qn = st.step(idx, qp, eqp, kw, vp, decq, c0, c1, kind == "ml")
        yield
        s0, s1 = sc[0:rows] * d0, sc[rows:2 * rows] * d1
        intra = _dot(jnp.concatenate([s0, s1], axis=1).astype(BF16), v01)
        yield
        finish(intra + carried, s0, s1, qn)

    def head_norm(o):
        o2 = o * o
        ms0 = jnp.sum(jnp.where(lm0, o2, 0.0), axis=-1, keepdims=True) * (1.0 / HEAD_DIM)
        ms1 = jnp.sum(jnp.where(lm0, 0.0, o2), axis=-1, keepdims=True) * (1.0 / HEAD_DIM)
        return o * jnp.where(lm0, lax.rsqrt(ms0 + EPS), lax.rsqrt(ms1 + EPS))

    conv = cb + us[0] * cw[0:1] + us[1] * cw[1:2] + us[2] * cw[2:3] + us[3] * cw[3:4]
    xc = _silu(conv)
    bb = xc[:, SSD_WIDTH:SSD_WIDTH + LANES]
    cc = xc[:, SSD_WIDTH + LANES:SSD_WIDTH + 2 * LANES]
    br = pltpu.roll(bb, HEAD_DIM, 1)
    cr = pltpu.roll(cc, HEAD_DIM, 1)
    ys = [None] * 4
    pairs = []
    for p in range(4):
        c0, c1 = C_SSD + 2 * p, C_SSD + 2 * p + 1
        sl = slice(LANES * p, LANES * (p + 1))

        def ssd_inputs(p=p, c0=c0, c1=c1, sl=sl):
            if p < 2:
                kp, qp = jnp.where(lm0, bb, br), jnp.where(lm0, cc, cr)
            else:
                kp, qp = jnp.where(lm0, br, bb), jnp.where(lm0, cr, cc)
            dtp = jnp.where(lm0, dt[:, c0:c0 + 1], dt[:, c1:c1 + 1])
            return qp, kp, xc[:, sl] * dtp

        def ssd_finish(o, s0, s1, qn, p=p, sl=sl):
            ys[p] = (o + dsk[:, sl] * xc[:, sl]) * _silu(z[:, sl])

        pairs.append(pair(p, ssd_inputs, c0, c1, "ssd", ssd_finish))

    for p in range(2):
        c0, c1 = C_RET + 2 * p, C_RET + 2 * p + 1
        sl = slice(LANES * p, LANES * (p + 1))

        def ret_inputs(p=p, sl=sl):
            qp = _rope(retb[:, LANES * p:LANES * (p + 1)], cos[:, sl], sin[:, sl], lane)
            kp = _rope(retb[:, RET_WIDTH + LANES * p:RET_WIDTH + LANES * (p + 1)], cos[:, sl], sin[:, sl], lane)
            return qp, kp * (HEAD_DIM ** -0.5), retb[:, 2 * RET_WIDTH + LANES * p:2 * RET_WIDTH + LANES * (p + 1)]

        def ret_finish(o, s0, s1, qn, p=p, sl=sl):
            gp = retb[:, 3 * RET_WIDTH + LANES * p:3 * RET_WIDTH + LANES * (p + 1)]
            y = head_norm(o) * gr[:, sl] * _silu(gp)
            y_ref[:, SSD_WIDTH + LANES * p:SSD_WIDTH + LANES * (p + 1)] = y.astype(y_ref.dtype)

        pairs.append(pair(4 + p, ret_inputs, c0, c1, "ret", ret_finish))

    for p in range(2):
        c0, c1 = C_ML + 2 * p, C_ML + 2 * p + 1
        sl = slice(LANES * p, LANES * (p + 1))

        def ml_inputs(p=p):
            return (mlb[:, LANES * p:LANES * (p + 1)],
                    mlb[:, ML_WIDTH + LANES * p:ML_WIDTH + LANES * (p + 1)] * (HEAD_DIM ** -0.5),
                    mlb[:, 2 * ML_WIDTH + LANES * p:2 * ML_WIDTH + LANES * (p + 1)])

        def ml_finish(num, s0, s1, qn, p=p, c0=c0, c1=c1, sl=sl):
            op = mlb[:, 3 * ML_WIDTH + LANES * p:3 * ML_WIDTH + LANES * (p + 1)]
            inter0 = jnp.exp(colv[:, c0:c0 + 1] + mprev[:, c0:c0 + 1])
            inter1 = jnp.exp(colv[:, c1:c1 + 1] + mprev[:, c1:c1 + 1])
            den0 = jnp.sum(s0, axis=-1, keepdims=True) + qn[0] * inter0
            den1 = jnp.sum(s1, axis=-1, keepdims=True) + qn[1] * inter1
            dn0 = jnp.maximum(jnp.abs(den0), jnp.exp(-m_t[:, c0:c0 + 1]))
            dn1 = jnp.maximum(jnp.abs(den1), jnp.exp(-m_t[:, c1:c1 + 1]))
            hh = num / jnp.where(lm0, dn0, dn1)
            y = head_norm(hh) * gm[:, sl] * jax.nn.sigmoid(op)
            off = SSD_WIDTH + RET_WIDTH + LANES * p
            y_ref[:, off:off + LANES] = y.astype(y_ref.dtype)

        pairs.append(pair(6 + p, ml_inputs, c0, c1, "ml", ml_finish))

    live = list(pairs)
    while live:
        for g in list(live):
            try:
                next(g)
            except StopIteration:
                live.remove(g)
        yield

    for g in range(SSD_GROUPS):
        ya, yb = ys[2 * g], ys[2 * g + 1]
        ms = (jnp.sum(ya * ya, axis=-1, keepdims=True)
              + jnp.sum(yb * yb, axis=-1, keepdims=True)) * (1.0 / (2 * LANES))
        r = lax.rsqrt(ms + EPS)
        for j, yv in ((2 * g, ya), (2 * g + 1, yb)):
            sl = slice(LANES * j, LANES * (j + 1))
            y_ref[:, sl] = (yv * r * gs[:, sl]).astype(y_ref.dtype)
    return m_t


def _run_interleaved(gens):
    out = [None] * len(gens)
    live = list(range(len(gens)))
    while live:
        for k in list(live):
            try:
                next(gens[k])
            except StopIteration as stop:
                out[k] = stop.value
                live.remove(k)
    return out


def _half_rows():
    return lax.broadcasted_iota(I32, (LANES, HEAD_DIM), 0) < HEAD_DIM


class _CarriedState:
    def __init__(self, sv, nrow):
        self.sv, self.nrow = sv, nrow

    def step(self, idx, qp, eqp, kw, vp, decq, c0, c1, ml):
        lm0 = lax.broadcasted_iota(I32, (1, LANES), 1) < HEAD_DIM
        drow = _pair_lanes(decq[0:1, :], c0, c1)
        s_old = self.sv[idx]
        carried = _dot(qp.astype(BF16), s_old.astype(BF16)) * eqp
        u = _dot_tn(kw.astype(BF16), vp.astype(BF16))
        same_head = ((lax.broadcasted_iota(I32, (LANES, LANES), 0) < HEAD_DIM)
                     == (lax.broadcasted_iota(I32, (LANES, LANES), 1) < HEAD_DIM))
        self.sv[idx] = s_old * drow + jnp.where(same_head, u, 0.0)
        qn = None
        if ml:
            p = idx - 6
            n_old = self.nrow[p:p + 1, :]
            qn_l = qp * n_old
            qn = (jnp.sum(jnp.where(lm0, qn_l, 0.0), axis=-1, keepdims=True),
                  jnp.sum(jnp.where(lm0, 0.0, qn_l), axis=-1, keepdims=True))
            self.nrow[p:p + 1, :] = n_old * drow + jnp.sum(kw, axis=0, keepdims=True)
        return carried, qn


def _mixer_prompt_kernel(*refs):
    sq = SEQ_PER_STEP
    proj_refs = refs[:sq]
    (cos_ref, sin_ref, ptab_ref, cw_ref, cb_ref, dsk_ref, gs_ref, gr_ref, gm_ref,
     y_ref, sv_o, conv_o, n_o, m_o) = refs[sq:sq + 14]
    scr = refs[sq + 14:]
    sv, nrow, mrow, cbuf = scr[0:sq], scr[sq:2 * sq], scr[2 * sq:3 * sq], scr[3 * sq:4 * sq]
    ret_cache = scr[4 * sq]
    ci = pl.program_id(1)
    rows = proj_refs[0].shape[1]
    ri = lax.broadcasted_iota(I32, (rows, rows), 0)
    cj = lax.broadcasted_iota(I32, (rows, rows), 1)
    mask = cj <= ri
    tril = jnp.where(mask, 1.0, 0.0).astype(BF16)
    last_row = lambda a: a[rows - 1:rows, :]

    @pl.when(ci == 0)
    def _():
        for s in range(sq):
            sv[s][...] = jnp.zeros_like(sv[s])
            nrow[s][...] = jnp.zeros_like(nrow[s])
            mrow[s][...] = jnp.zeros_like(mrow[s])
            cbuf[s][0:8, :] = jnp.zeros((8, SSD_XBC), F32)
        for p, fs in enumerate(_ret_factors(ptab_ref[0], tril, mask, last_row)):
            for k, f in enumerate(fs):
                ret_cache[4 * p + k] = f

    ret_factors = lambda p: tuple(ret_cache[4 * p + k] for k in range(4))
    gens = []
    for s in range(sq):
        proj_ref = proj_refs[s]
        cbuf[s][8:8 + rows, :] = proj_ref[0, :, P_XBC:P_RET]
        us = [cbuf[s][pl.ds(5 + k, rows), :] for k in range(SSD_CONV)]
        gens.append(_mixer_core(
            proj_ref[0, :, P_Z:P_XBC], us, proj_ref[0, :, P_RET:P_ML], proj_ref[0, :, P_ML:P_SM],
            proj_ref[0, :, P_SM:P_W], cos_ref[...], sin_ref[...], ptab_ref[0], cw_ref[0], cb_ref[0],
            dsk_ref[0], gs_ref[0], gr_ref[0], gm_ref[0],
            mask, tril, rows, mrow[s][0:1, :], last_row,
            _CarriedState(sv[s], nrow[s]), y_ref.at[s], ret_factors))
    for s, m_t in enumerate(_run_interleaved(gens)):
        mrow[s][0:1, :] = m_t[rows - 1:rows, :]
        cbuf[s][0:8, :] = cbuf[s][rows:rows + 8, :]

    @pl.when(ci == pl.num_programs(1) - 1)
    def _():
        top = _half_rows()
        for s in range(sq):
            for idx in range(N_PAIRS):
                bd = sv[s][idx]
                sv_o[s, idx] = jnp.where(top, bd[:, :HEAD_DIM], bd[:, HEAD_DIM:])
            conv_o[s] = cbuf[s][0:8, :]
            n_o[s] = nrow[s][...]
            m_o[s] = mrow[s][...]


def _mixer_prompt(proj, nb, t, cos, sin, consts, l):
    rows = math.gcd(t, CHUNK)
    nc = t // rows
    sq = SEQ_PER_STEP
    assert nb % sq == 0

    def proj_spec(s):
        return pl.BlockSpec((1, rows, P_W), lambda b, c: (0, (sq * b + s) * nc + c, 0))

    def seq_spec(*tail):
        return pl.BlockSpec((sq,) + tail, lambda b, c: (b,) + (0,) * len(tail))

    outs = pl.pallas_call(
        _mixer_prompt_kernel,
        grid=(nb // sq, nc),
        in_specs=[proj_spec(s) for s in range(sq)]
                 + [pl.BlockSpec((rows, RET_WIDTH), lambda b, c: (c, 0)),
                    pl.BlockSpec((rows, RET_WIDTH), lambda b, c: (c, 0))] + [_layer_spec(a, l) for a in consts],
        out_specs=[pl.BlockSpec((sq, rows, D_MODEL), lambda b, c: (b, c, 0)),
                   seq_spec(N_PAIRS, LANES, HEAD_DIM), seq_spec(8, SSD_XBC), seq_spec(8, LANES), seq_spec(8, LANES)],
        out_shape=[jax.ShapeDtypeStruct((nb, t, D_MODEL), BF16),
                   jax.ShapeDtypeStruct((nb, N_PAIRS, LANES, HEAD_DIM), F32),
                   jax.ShapeDtypeStruct((nb, 8, SSD_XBC), F32),
                   jax.ShapeDtypeStruct((nb, 8, LANES), F32),
                   jax.ShapeDtypeStruct((nb, 8, LANES), F32)],
        scratch_shapes=[pltpu.VMEM((N_PAIRS, LANES, LANES), F32) for _ in range(sq)]
                       + [pltpu.VMEM((8, LANES), F32) for _ in range(2 * sq)]
                       + [pltpu.VMEM((rows + 8, SSD_XBC), F32) for _ in range(sq)]
                       + [pltpu.VMEM((2 * RET_HEADS, rows, LANES), F32)],
        compiler_params=_cparams(("arbitrary", "arbitrary")),
        name="mixer_prompt",
    )(*([proj[None]] * sq), cos, sin, *consts)
    y, sv, conv, n, m = outs
    sv = sv.reshape(nb, 2 * N_PAIRS, HEAD_DIM, HEAD_DIM)
    states = (sv[:, :8], conv[:, 5:8], sv[:, 8:12], sv[:, 12:16],
              n[:, 0:2].reshape(nb, ML_HEADS, HEAD_DIM), m[:, 0, C_ML:C_ML + 4])
    return y.reshape(nb * t, D_MODEL), states


def _lane_recurrence(s_in, s_out, q_s, k_s, v_s, decay):
    n_tok = len(decay)
    nvb = HEAD_DIM // 8
    nb = q_s.shape[-1]
    dec8 = [jnp.broadcast_to(d, (8, nb)) for d in decay]

    def body(k, acc):
        acc = [list(a) for a in acc]
        qk = [q_s[t, pl.ds(k, 1), :] for t in range(n_tok)]
        kk = [k_s[t, pl.ds(k, 1), :] for t in range(n_tok)]
        for vb in range(nvb):
            rows = pl.ds(8 * vb, 8)
            s = s_in[k, rows, :]
            for t in range(n_tok):
                s = s * dec8[t] + kk[t] * v_s[t, rows, :]
                acc[t][vb] = acc[t][vb] + qk[t] * s
            s_out[k, rows, :] = s
        return tuple(tuple(a) for a in acc)

    init = tuple(tuple(jnp.zeros((8, nb), F32) for _ in range(nvb)) for _ in range(n_tok))
    acc = lax.fori_loop(0, HEAD_DIM, body, init, unroll=2)
    return [jnp.concatenate(list(a), axis=0) for a in acc]


def _conv_T(u_ref, tail_ref, w_ref, b_ref, state_ref):
    n_tok = u_ref.shape[0]
    full = [tail_ref[0, j] for j in range(SSD_CONV - 1)] + [u_ref[t] for t in range(n_tok)]
    outs = []
    for t in range(n_tok):
        acc = b_ref[0]
        for tap in range(SSD_CONV):
            acc = acc + full[t + tap] * w_ref[0, tap]
        outs.append(_silu(acc))
    for j in range(SSD_CONV - 1):
        state_ref[j] = full[n_tok + j]
    return outs


def _ssd_T_kernel(z_ref, xs_ref, b_ref, c_ref, sm_ref, tx_ref, tb_ref, tc_ref, wx_ref, wb_ref, wc_ref,
                  bx_ref, bb_ref, bc_ref, dtb_ref, alog_ref, dsk_ref, g_ref, s_ref,
                  y_ref, so_ref, cx_ref, cb_ref, cc_ref, q_s, k_s, v_s):
    h = pl.program_id(0)
    n_tok = xs_ref.shape[0]
    xs = _conv_T(xs_ref, tx_ref, wx_ref, bx_ref, cx_ref)
    bm = _conv_T(b_ref, tb_ref, wb_ref, bb_ref, cb_ref)
    cm = _conv_T(c_ref, tc_ref, wc_ref, bc_ref, cc_ref)
    a_neg = -jnp.exp(alog_ref[0, 0])
    decay = []
    for t in range(n_tok):
        dt = _softplus(sm_ref[t, pl.ds(C_SSD + h, 1), :] + dtb_ref[0, 0])
        decay.append(jnp.exp(dt * a_neg))
        q_s[t] = cm[t]
        k_s[t] = bm[t]
        v_s[t] = xs[t] * dt
    o = _lane_recurrence(s_ref.at[0, 0], so_ref.at[0, 0], q_s, k_s, v_s, decay)
    hh = h % (SSD_HEADS // SSD_GROUPS)
    row0 = pl.multiple_of(hh * HEAD_DIM, HEAD_DIM)
    for t in range(n_tok):
        y_ref[t, pl.ds(row0, HEAD_DIM), :] = (o[t] + dsk_ref[0, 0] * xs[t]) * _silu(z_ref[t])

    @pl.when(hh == SSD_HEADS // SSD_GROUPS - 1)
    def _():
        for t in range(n_tok):
            blk = y_ref[t]
            ms = jnp.mean(blk * blk, axis=0, keepdims=True)
            y_ref[t] = blk * lax.rsqrt(ms + EPS) * g_ref[0]


def _ret_T_kernel(q_ref, k_ref, v_ref, g_ref, cos_ref, sin_ref, gam_ref, gn_ref, s_ref,
                  y_ref, so_ref, q_s, k_s, v_s):
    n_tok = q_ref.shape[0]
    half = HEAD_DIM // 2

    def rope(x, t):
        x1, x2 = x[:half], x[half:]
        return jnp.concatenate([x1 * cos_ref[t] - x2 * sin_ref[t], x1 * sin_ref[t] + x2 * cos_ref[t]], axis=0)

    for t in range(n_tok):
        q_s[t] = rope(q_ref[t], t)
        k_s[t] = rope(k_ref[t], t) * (HEAD_DIM ** -0.5)
        v_s[t] = v_ref[t]
    o = _lane_recurrence(s_ref.at[0, 0], so_ref.at[0, 0], q_s, k_s, v_s, [gam_ref[0]] * n_tok)
    for t in range(n_tok):
        ms = jnp.mean(o[t] * o[t], axis=0, keepdims=True)
        y_ref[t] = o[t] * lax.rsqrt(ms + EPS) * gn_ref[0] * _silu(g_ref[t])


def _ml_T_kernel(q_ref, k_ref, v_ref, o_ref, sm_ref, bi_ref, bf_ref, gn_ref, c_ref, n_ref, m_ref,
                 y_ref, co_ref, no_ref, mo_ref, q_s, k_s, v_s):
    h = pl.program_id(0)
    n_tok = q_ref.shape[0]
    m = m_ref[0, 0]
    n = n_ref[0, 0]
    decay, qn, m_all = [], [], []
    for t in range(n_tok):
        i_t = sm_ref[t, pl.ds(C_RET + h, 1), :] + bi_ref[0, 0]
        logf = -_softplus(-(sm_ref[t, pl.ds(C_ML + h, 1), :] + bf_ref[0, 0]))
        m_new = jnp.maximum(logf + m, i_t)
        f_t = jnp.exp(logf + m - m_new)
        kw = k_ref[t] * (HEAD_DIM ** -0.5) * jnp.exp(i_t - m_new)
        n = n * f_t + kw
        q_s[t] = q_ref[t]
        k_s[t] = kw
        v_s[t] = v_ref[t]
        decay.append(f_t)
        qn.append(jnp.sum(q_ref[t] * n, axis=0, keepdims=True))
        m_all.append(m_new)
        m = m_new
    num = _lane_recurrence(c_ref.at[0, 0], co_ref.at[0, 0], q_s, k_s, v_s, decay)
    no_ref[0] = n
    mo_ref[0] = m
    for t in range(n_tok):
        hh = num[t] / jnp.maximum(jnp.abs(qn[t]), jnp.exp(-m_all[t]))
        ms = jnp.mean(hh * hh, axis=0, keepdims=True)
        y_ref[t] = hh * lax.rsqrt(ms + EPS) * gn_ref[0] * jax.nn.sigmoid(o_ref[t])


def _mixer_sample_T(projT, convT, cwb, cbb, ssdT, retT, cT, nT, mT, tabs, cos, sin, l):
    n_tok, _, nb = projT.shape
    dtb, alog, dsk, gs, gam, gr, bi, bf, gm = tabs
    hb = lambda col: col // HEAD_DIM
    rep = SSD_HEADS // SSD_GROUPS
    proj_blk = lambda first, div=1: pl.BlockSpec((n_tok, HEAD_DIM, nb), lambda h: (0, first + h // div, 0))
    small = pl.BlockSpec((n_tok, LANES, nb), lambda h: (0, P_SM // LANES, 0))
    lay4 = lambda a, first=0, div=1: pl.BlockSpec((1,) + a.shape[1:2] + (HEAD_DIM, a.shape[-1]),
                                                  lambda h: (l, 0, first + h // div, 0))
    chan = lambda a, first=0, div=1, n=1: pl.BlockSpec((1, n * HEAD_DIM, 1), lambda h: (l, first + h // div, 0))
    head4 = lambda a: pl.BlockSpec((1, 1) + a.shape[2:], lambda h: (l, h) + (0,) * (a.ndim - 2))
    vm = lambda: pltpu.VMEM((n_tok, HEAD_DIM, nb), F32)
    state_spec = pl.BlockSpec((1, 1, HEAD_DIM, HEAD_DIM, nb), lambda h: (0, h, 0, 0, 0))
    stacked = lambda heads: jax.ShapeDtypeStruct((1, heads, HEAD_DIM, HEAD_DIM, nb), F32)
    b0, c0 = hb(SSD_WIDTH), hb(SSD_WIDTH + SSD_GROUPS * SSD_STATE)

    y_ssd, ssd_n, cx, cb, cc = pl.pallas_call(
        _ssd_T_kernel,
        grid=(SSD_HEADS,),
        in_specs=[proj_blk(hb(P_Z)), proj_blk(hb(P_XBC)), proj_blk(hb(P_XBC) + b0, rep),
                  proj_blk(hb(P_XBC) + c0, rep), small,
                  lay4(convT), lay4(convT, b0, rep), lay4(convT, c0, rep),
                  lay4(cwb), lay4(cwb, b0, rep), lay4(cwb, c0, rep),
                  chan(cbb), chan(cbb, b0, rep), chan(cbb, c0, rep),
                  head4(dtb), head4(alog), head4(dsk), chan(gs, 0, rep, rep), head4(ssdT)],
        out_specs=[pl.BlockSpec((n_tok, rep * HEAD_DIM, nb), lambda h: (0, h // rep, 0)), state_spec,
                   pl.BlockSpec((SSD_CONV - 1, HEAD_DIM, nb), lambda h: (0, h, 0)),
                   pl.BlockSpec((SSD_CONV - 1, HEAD_DIM, nb), lambda h: (0, h // rep, 0)),
                   pl.BlockSpec((SSD_CONV - 1, HEAD_DIM, nb), lambda h: (0, h // rep, 0))],
        out_shape=[jax.ShapeDtypeStruct((n_tok, SSD_WIDTH, nb), F32),
                   stacked(SSD_HEADS),
                   jax.ShapeDtypeStruct((SSD_CONV - 1, SSD_WIDTH, nb), F32),
                   jax.ShapeDtypeStruct((SSD_CONV - 1, SSD_GROUPS * SSD_STATE, nb), F32),
                   jax.ShapeDtypeStruct((SSD_CONV - 1, SSD_GROUPS * SSD_STATE, nb), F32)],
        scratch_shapes=[vm(), vm(), vm()],
        compiler_params=_cparams(("arbitrary",)),
        name="sample_ssd",
    )(projT, projT, projT, projT, projT, convT, convT, convT, cwb, cwb, cwb, cbb, cbb, cbb,
      dtb, alog, dsk, gs, ssdT)

    y_ret, ret_n = pl.pallas_call(
        _ret_T_kernel,
        grid=(RET_HEADS,),
        in_specs=[proj_blk(hb(P_RET)), proj_blk(hb(P_RET + RET_WIDTH)), proj_blk(hb(P_RET + 2 * RET_WIDTH)),
                  proj_blk(hb(P_RET + 3 * RET_WIDTH)), _const_spec(cos), _const_spec(sin),
                  pl.BlockSpec((1, 1, 1), lambda h: (h, 0, 0)), chan(gr), head4(retT)],
        out_specs=[pl.BlockSpec((n_tok, HEAD_DIM, nb), lambda h: (0, h, 0)), state_spec],
        out_shape=[jax.ShapeDtypeStruct((n_tok, RET_WIDTH, nb), F32),
                   stacked(RET_HEADS)],
        scratch_shapes=[vm(), vm(), vm()],
        compiler_params=_cparams(("arbitrary",)),
        name="sample_ret",
    )(projT, projT, projT, projT, cos, sin, gam, gr, retT)

    y_ml, c_n, n_n, m_n = pl.pallas_call(
        _ml_T_kernel,
        grid=(ML_HEADS,),
        in_specs=[proj_blk(hb(P_ML)), proj_blk(hb(P_ML + ML_WIDTH)), proj_blk(hb(P_ML + 2 * ML_WIDTH)),
                  proj_blk(hb(P_ML + 3 * ML_WIDTH)), small, head4(bi), head4(bf),
                  chan(gm), head4(cT), head4(nT), head4(mT)],
        out_specs=[pl.BlockSpec((n_tok, HEAD_DIM, nb), lambda h: (0, h, 0)), state_spec,
                   pl.BlockSpec((1, HEAD_DIM, nb), lambda h: (h, 0, 0)),
                   pl.BlockSpec((1, 1, nb), lambda h: (h, 0, 0))],
        out_shape=[jax.ShapeDtypeStruct((n_tok, ML_WIDTH, nb), F32),
                   stacked(ML_HEADS),
                   jax.ShapeDtypeStruct((ML_HEADS, HEAD_DIM, nb), F32),
                   jax.ShapeDtypeStruct((ML_HEADS, 1, nb), F32)],
        scratch_shapes=[vm(), vm(), vm()],
        compiler_params=_cparams(("arbitrary",)),
        name="sample_mlstm",
    )(projT, projT, projT, projT, projT, bi, bf, gm, cT, nT, mT)

    y_t = jnp.concatenate([y_ssd, y_ret, y_ml], axis=1)
    conv_n = jnp.concatenate([cx, cb, cc], axis=1)
    return y_t, (ssd_n, conv_n, ret_n, c_n, n_n, m_n)


def _route(logits):
    lane = lax.broadcasted_iota(I32, logits.shape, 1)
    gmask = (lane >= N_EXPERTS) & (lane < N_EXPERTS + EXPERT_GROUPS)
    gl = jnp.where(gmask, logits, -jnp.inf)
    ge = jnp.exp(gl - jnp.max(gl, axis=-1, keepdims=True))
    gprob = ge / jnp.sum(ge, axis=-1, keepdims=True)
    g_w = jnp.max(gprob, axis=-1, keepdims=True)
    g_idx = jnp.min(jnp.where(gmask & (gprob == g_w), lane - N_EXPERTS, LANES), axis=-1, keepdims=True)
    emask = (lane < N_EXPERTS) & ((lane >> 3) == g_idx)
    el = jnp.where(emask, logits, -jnp.inf)
    ee = jnp.exp(el - jnp.max(el, axis=-1, keepdims=True))
    eprob = ee / jnp.sum(ee, axis=-1, keepdims=True)
    p1 = jnp.max(jnp.where(emask, eprob, -1.0), axis=-1, keepdims=True)
    i1 = jnp.min(jnp.where(emask & (eprob == p1), lane, LANES), axis=-1, keepdims=True)
    rest = emask & (lane != i1)
    p2 = jnp.max(jnp.where(rest, eprob, -1.0), axis=-1, keepdims=True)
    i2 = jnp.min(jnp.where(rest & (eprob == p2), lane, LANES), axis=-1, keepdims=True)
    tot = p1 + p2
    lo, hi = jnp.minimum(i1, i2) & 7, jnp.maximum(i1, i2) & 7
    key = g_idx * PAIRS_PER_GROUP + ((lo * (13 - lo)) >> 1) + hi - 1
    return g_w * (p1 / tot), g_w * (p2 / tot), i1, i2, key


def _outproj_kernel(yp, ys, xp, xs, gtp, gts, scp, scs, shp, shs, g_ref, w_ref, wr_ref, br_ref,
                    x1_ref, hx_ref, info_ref, cnt_ref, carry, *, np_tiles, tp):
    i = pl.program_id(0)
    is_p = i < np_tiles

    @pl.when(i == 0)
    def _():
        carry[...] = jnp.zeros_like(carry)

    tm = x1_ref.shape[0]
    nblk = OUTPROJ_ROW_BLOCKS
    rb = tm // nblk
    blocks = [slice(k * rb, (k + 1) * rb) for k in range(nblk)]

    def body(x_ref, y_ref, gt, sc, sh):
        mod = lambda m, b: m if m.shape[0] == 1 else m[b]
        x1s = [x_ref[b, :] + mod(gt, b) * _dot(y_ref[b, :], w_ref[0]) for b in blocks]
        for b, x1 in zip(blocks, x1s):
            x1_ref[b, :] = x1
        h2bs = [(_rms(x1) * g_ref[0] * (1.0 + mod(sc, b)) + mod(sh, b)).astype(BF16)
                for b, x1 in zip(blocks, x1s)]
        routes = [_route(_dot(h2b, wr_ref[0]) + br_ref[0]) for h2b in h2bs]

        tril = (lax.broadcasted_iota(I32, (rb, rb), 1) <= lax.broadcasted_iota(I32, (rb, rb), 0))
        tril = jnp.where(tril, 1.0, 0.0).astype(BF16)
        onehots = [lax.broadcasted_iota(I32, (rb, N_CLASS), 1) == r[4] for r in routes]
        uptos = [_dot(tril, jnp.where(oh, 1.0, 0.0).astype(BF16)) for oh in onehots]
        seen = carry[0:1, :]
        ranks = []
        for oh, upto in zip(onehots, uptos):
            ranks.append(jnp.sum(jnp.where(oh, upto - 1.0 + seen, 0.0), axis=-1, keepdims=True))
            seen = seen + upto[rb - 1:rb, :]
        carry[0:1, :] = seen
        cnt_ref[...] = jnp.broadcast_to(seen, cnt_ref.shape)

        lane = lax.broadcasted_iota(I32, (rb, LANES), 1)
        for b, h2b, (w1, w2, i1, i2, key), rank in zip(blocks, h2bs, routes, ranks):
            info = jnp.zeros((rb, LANES), F32)
            for c, v in enumerate((w1, w2, i1.astype(F32), i2.astype(F32), key.astype(F32), rank)):
                info = jnp.where(lane == c, v, info)
            hx_ref[b, 0:D_MODEL] = h2b.astype(F32)
            hx_ref[b, D_MODEL:H_EXT] = info
            info_ref[b, :] = info[:, 0:8]

    @pl.when(is_p)
    def _():
        body(xp, yp, _prompt_mod(gtp, tm, tp), _prompt_mod(scp, tm, tp), _prompt_mod(shp, tm, tp))

    @pl.when(jnp.logical_not(is_p))
    def _():
        body(xs, ys, _sample_mod(gts, tm), _sample_mod(scs, tm), _sample_mod(shs, tm))


def _outproj(y, x, mod, k_gt, k_sc, k_sh, g, w, wr, br, l, tp):
    n_prompt, ns = x[0].shape[0], x[1].shape[0]
    n = n_prompt + ns
    tm = TOKEN_TILE
    npt = n_prompt // tm
    ms = [s for k in (k_gt, k_sc, k_sh) for s in _mod_specs(mod, k, l, tm, n_prompt, tp)]
    return pl.pallas_call(
        functools.partial(_outproj_kernel, np_tiles=npt, tp=tp),
        grid=(n // tm,),
        in_specs=_dual_specs(tm, npt, D_MODEL) + _dual_specs(tm, npt, D_MODEL) + ms
                 + [_layer_spec(a, l) for a in (g, w, wr, br)],
        out_specs=[_row_spec(tm, D_MODEL), _row_spec(tm, H_EXT), _row_spec(tm, 8),
                   pl.BlockSpec((8, N_CLASS), lambda i: (0, 0))],
        out_shape=[jax.ShapeDtypeStruct((n, D_MODEL), F32),
                   jax.ShapeDtypeStruct((n, H_EXT), F32),
                   jax.ShapeDtypeStruct((n, 8), F32),
                   jax.ShapeDtypeStruct((8, N_CLASS), F32)],
        scratch_shapes=[pltpu.VMEM((8, N_CLASS), F32)],
        compiler_params=_cparams(("arbitrary",)),
        name="outproj_router",
    )(*y, *x, *([mod[0]] * 6), g, w, wr, br)


def _scatter_kernel(pos_ref, x_ref, o_hbm, buf, sem):
    i, n = pl.program_id(0), pl.num_programs(0)
    tm = x_ref.shape[0]
    slot = i % 2

    def wait_slot(s):
        pltpu.make_async_copy(buf.at[s], o_hbm.at[pl.ds(0, tm), :], sem.at[s]).wait()

    @pl.when(i >= 2)
    def _():
        wait_slot(slot)

    buf[slot] = x_ref[...]

    def body(r2, c):
        for prio in range(2):
            r = 2 * r2 + prio
            dst = pos_ref[i * tm + r]
            pltpu.make_async_copy(buf.at[slot, pl.ds(r, 1), :], o_hbm.at[pl.ds(dst, 1), :],
                                  sem.at[slot]).start(priority=prio)
        return c
    lax.fori_loop(0, tm // 2, body, 0, unroll=8)

    @pl.when(i == n - 1)
    def _():
        wait_slot(slot)

        @pl.when(n >= 2)
        def _():
            wait_slot(1 - slot)


def _scatter_rows(x, pos):
    n, width = x.shape
    tm = TOKEN_TILE
    return pl.pallas_call(
        _scatter_kernel,
        grid_spec=pltpu.PrefetchScalarGridSpec(
            num_scalar_prefetch=1, grid=(n // tm,),
            in_specs=[_row_spec(tm, width)],
            out_specs=pl.BlockSpec(memory_space=pl.ANY),
            scratch_shapes=[pltpu.VMEM((2, tm, width), x.dtype), pltpu.SemaphoreType.DMA((2,))]),
        out_shape=jax.ShapeDtypeStruct((n, width), x.dtype),
        compiler_params=_cparams(("arbitrary",)),
        name="scatter_rows",
    )(pos, x)


def _gather_tiles(idx_ref, src_hbm, buf, sem, tm):
    i, n = pl.program_id(0), pl.num_programs(0)

    def issue(tile, slot):
        def body(r2, c):
            for prio in range(2):
                r = 2 * r2 + prio
                row = idx_ref[tile * tm + r]
                pltpu.make_async_copy(src_hbm.at[pl.ds(row, 1), :], buf.at[slot, pl.ds(r, 1), :],
                                      sem.at[slot]).start(priority=prio)
            return c
        lax.fori_loop(0, tm // 2, body, 0, unroll=8)

    @pl.when(i == 0)
    def _():
        issue(0, 0)

    @pl.when(i + 1 < n)
    def _():
        issue(i + 1, (i + 1) % 2)

    slot = i % 2
    pltpu.make_async_copy(src_hbm.at[pl.ds(0, tm), :], buf.at[slot], sem.at[slot]).wait()
    return slot


def _experts_kernel(tile_ref, grp_ref, flag_ref, hx_ref, wgu_f32, wd_f32, o_ref, wgu_ref, wd_ref):
    j = pl.program_id(0)
    flags = flag_ref[j]
    valid = (flags & 1) != 0

    @pl.when((flags & 4) != 0)
    def _():
        for e in range(EXPERTS_PER_GROUP):
            wgu_ref[e] = wgu_f32[0, 0, e].astype(BF16)
            wd_ref[e] = wd_f32[0, 0, e].astype(BF16)

    @pl.when(valid & ((flags & 2) != 0))
    def _():
        o_ref[...] = jnp.zeros_like(o_ref)

    @pl.when(valid)
    def _():
        h = hx_ref[:, 0:D_MODEL].astype(BF16)
        r = hx_ref[:, D_MODEL:H_EXT]
        w1, w2, i1, i2 = r[:, 0:1], r[:, 1:2], r[:, 2:3], r[:, 3:4]
        base = grp_ref[j] * EXPERTS_PER_GROUP
        for e in range(EXPERTS_PER_GROUP):
            @pl.when(((flags >> (8 + e)) & 1) != 0)
            def _():
                eid = (base + e).astype(F32)
                ge = jnp.where(i1 == eid, w1, 0.0) + jnp.where(i2 == eid, w2, 0.0)
                nblk = EXPERT_ROW_BLOCKS
                rb = h.shape[0] // nblk
                aus = [_dot(h[k * rb:(k + 1) * rb], wgu_ref[e]) for k in range(nblk)]
                acts = [(_silu(au[:, :EXPERT_FF]) * au[:, EXPERT_FF:]).astype(BF16) for au in aus]
                yes = [_dot(act, wd_ref[e]) for act in acts]
                for k in range(nblk):
                    o_ref[k * rb:(k + 1) * rb, :] += ge[k * rb:(k + 1) * rb] * yes[k]


def _experts(hx_sorted, tile, grp, flags, wgu, wd, l):
    n = hx_sorted.shape[0]
    tm = MOE_TILE
    return pl.pallas_call(
        _experts_kernel,
        grid_spec=pltpu.PrefetchScalarGridSpec(
            num_scalar_prefetch=3, grid=(tile.shape[0],),
            in_specs=[pl.BlockSpec((tm, H_EXT), lambda j, t, g, f: (t[j], 0)),
                      pl.BlockSpec((1, 1, EXPERTS_PER_GROUP, D_MODEL, 2 * EXPERT_FF),
                                   lambda j, t, g, f: (l, g[j], 0, 0, 0), pipeline_mode=pl.Buffered(1)),
                      pl.BlockSpec((1, 1, EXPERTS_PER_GROUP, EXPERT_FF, D_MODEL),
                                   lambda j, t, g, f: (l, g[j], 0, 0, 0), pipeline_mode=pl.Buffered(1))],
            out_specs=pl.BlockSpec((tm, D_MODEL), lambda j, t, g, f: (t[j], 0)),
            scratch_shapes=[pltpu.VMEM((EXPERTS_PER_GROUP, D_MODEL, 2 * EXPERT_FF), BF16),
                            pltpu.VMEM((EXPERTS_PER_GROUP, EXPERT_FF, D_MODEL), BF16)]),
        out_shape=jax.ShapeDtypeStruct((n, D_MODEL), F32),
        compiler_params=_cparams(("arbitrary",)),
        name="moe_experts",
    )(tile, grp, flags, hx_sorted, wgu, wd)


def _combine_kernel(pos_ref, y_hbm, x1_ref, gtp, gts, gf_ref, op_ref, os_ref, buf, sem, *, np_tiles, tp, final):
    slot = _gather_tiles(pos_ref, y_hbm, buf, sem, x1_ref.shape[0])
    is_p = pl.program_id(0) < np_tiles
    x2 = x1_ref[...] + _mod_val(gtp, gts, is_p, x1_ref.shape[0], tp) * buf[slot]
    out = _rms(x2) * gf_ref[...] if final else x2

    @pl.when(is_p)
    def _():
        op_ref[...] = out

    @pl.when(jnp.logical_not(is_p))
    def _():
        os_ref[...] = out


def _combine(y_sorted, pos, x1, mod, k_gt, gf, l, n_prompt, tp, final):
    n = x1.shape[0]
    tm = TOKEN_TILE
    npt = n_prompt // tm
    return pl.pallas_call(
        functools.partial(_combine_kernel, np_tiles=npt, tp=tp, final=final),
        grid_spec=pltpu.PrefetchScalarGridSpec(
            num_scalar_prefetch=1, grid=(n // tm,),
            in_specs=[pl.BlockSpec(memory_space=pl.ANY), _row_spec(tm, D_MODEL)]
                     + _mod_specs(mod, k_gt, l, tm, n_prompt, tp) + [_const_spec(gf)],
            out_specs=_dual_specs(tm, npt, D_MODEL),
            scratch_shapes=[pltpu.VMEM((2, tm, D_MODEL), F32), pltpu.SemaphoreType.DMA((2,))]),
        out_shape=[jax.ShapeDtypeStruct((n_prompt, D_MODEL), F32),
                   jax.ShapeDtypeStruct((n - n_prompt, D_MODEL), F32)],
        compiler_params=_cparams(("arbitrary",)),
        name="moe_combine",
    )(pos, y_sorted, x1, mod[0], mod[0], gf)


def _class_members():
    table = np.zeros((N_CLASS, N_EXPERTS), dtype=bool)
    for grp in range(EXPERT_GROUPS):
        for lo in range(EXPERTS_PER_GROUP):
            for hi in range(lo + 1, EXPERTS_PER_GROUP):
                cls = grp * PAIRS_PER_GROUP + ((lo * (13 - lo)) >> 1) + hi - 1
                table[cls, grp * EXPERTS_PER_GROUP + lo] = True
                table[cls, grp * EXPERTS_PER_GROUP + hi] = True
    return table


def _routing_tables(cnt, key, rank, n_tiles, tm):
    n_items = n_tiles + EXPERT_GROUPS - 1
    c = jnp.arange(N_CLASS, dtype=I32)
    start_c = jnp.sum(jnp.where(c[:, None] < c[None, :], cnt[:, None], 0), axis=0)
    pos = jnp.sum(jnp.where(key[:, None] == c[None, :], start_c[None, :], 0), axis=1) + rank

    g = jnp.arange(EXPERT_GROUPS, dtype=I32)
    gend = jnp.sum(jnp.where(c[None, :] < PAIRS_PER_GROUP * (g[:, None] + 1), cnt[None, :], 0), axis=1)
    t0 = jnp.arange(n_tiles, dtype=I32) * tm
    t1 = t0 + (tm - 1)
    gfirst = jnp.sum((gend[None, :] <= t0[:, None]).astype(I32), axis=1)
    glast = jnp.sum((gend[None, :] <= t1[:, None]).astype(I32), axis=1)
    per_tile = glast - gfirst + 1
    tt = jnp.arange(n_tiles, dtype=I32)
    start_t = jnp.sum(jnp.where(tt[:, None] < tt[None, :], per_tile[:, None], 0), axis=0)
    total = jnp.sum(per_tile)

    j = jnp.arange(n_items, dtype=I32)
    valid = j < total
    tile = jnp.sum((start_t[None, :] <= j[:, None]).astype(I32), axis=1) - 1
    tile = jnp.where(valid, tile, n_tiles - 1)
    sel = tile[:, None] == tt[None, :]
    pick = lambda v: jnp.sum(jnp.where(sel, v[None, :], 0), axis=1)
    grp = jnp.where(valid, pick(gfirst) + (j - pick(start_t)), glast[n_tiles - 1])
    first = valid & (j == pick(start_t))

    in_tile = (cnt[None, :] > 0) & (start_c[None, :] <= t1[:, None]) & ((start_c + cnt)[None, :] > t0[:, None])
    e = jnp.arange(N_EXPERTS, dtype=I32)
    member = jnp.asarray(_class_members())
    present_t = jnp.any(in_tile[:, :, None] & member[None, :, :], axis=1)
    present_j = jnp.any(sel[:, :, None] & present_t[None, :, :], axis=1)
    eg = e[None, :] - grp[:, None] * EXPERTS_PER_GROUP
    bits = jnp.sum(jnp.where(present_j & (eg >= 0) & (eg < EXPERTS_PER_GROUP),
                             1 << (8 + jnp.clip(eg, 0, EXPERTS_PER_GROUP - 1)), 0), axis=1)
    new_grp = jnp.concatenate([jnp.ones((1,), bool), grp[1:] != grp[:-1]])
    flags = valid.astype(I32) | (first.astype(I32) << 1) | (new_grp.astype(I32) << 2) | bits
    return pos.astype(I32), tile.astype(I32), grp.astype(I32), flags.astype(I32)


def _moe(hx, info, counts, x1, mod, k_gt, gf, wgu, wd, l, n_prompt, tp, final):
    n = hx.shape[0]
    tm = MOE_TILE
    pos, tile, grp, flags = _routing_tables(counts[0].astype(I32), info[:, 4].astype(I32),
                                            info[:, 5].astype(I32), n // tm, tm)
    y_sorted = _experts(_scatter_rows(hx, pos), tile, grp, flags, wgu, wd, l)
    return _combine(y_sorted, pos, x1, mod, k_gt, gf, l, n_prompt, tp, final)


def _rope_tables(pos):
    half = HEAD_DIM // 2
    inv = ROPE_BASE ** (-jnp.arange(half, dtype=F32) / half)
    ang = pos.astype(F32)[:, None] * inv[None, :]
    cos, sin = jnp.cos(ang), jnp.sin(ang)
    cos_t = jnp.tile(jnp.concatenate([cos, cos], axis=-1), (1, RET_HEADS))
    sin_t = jnp.tile(jnp.concatenate([-sin, sin], axis=-1), (1, RET_HEADS))
    return cos_t, sin_t


def _mixer_consts(w_in, conv_w, conv_b, dt_bias, a_log, d_skip, g_ssd_norm, g_ret_norm,
                  b_mlstm_i, b_mlstm_f, g_mlstm_norm):
    nl = w_in.shape[0]
    w_t = jnp.transpose(w_in, (0, 2, 1))
    w_p = jnp.concatenate([w_t[:, 0:1280], w_t[:, 1288:3336], w_t[:, 1280:1288], w_t[:, 3336:3344],
                           jnp.zeros((nl, LANES - 16, D_MODEL), F32)], axis=1).astype(BF16)
    pad = lambda v: jnp.pad(v, ((0, 0), (0, LANES - v.shape[1])))
    log_gamma = jnp.log(1.0 - 2.0 ** (-5.0 - jnp.arange(RET_HEADS, dtype=F32)))
    lg = jnp.broadcast_to(jnp.concatenate([jnp.zeros((8,), F32), log_gamma])[None, :], (nl, 12))
    ptab = jnp.stack([pad(jnp.concatenate([dt_bias, b_mlstm_i, b_mlstm_f], axis=1)), pad(a_log), pad(lg)]
                     + [jnp.zeros((nl, LANES), F32)] * 5, axis=1)
    consts = (ptab, conv_w, conv_b[:, None, :], jnp.repeat(d_skip, HEAD_DIM, axis=1)[:, None, :],
              g_ssd_norm[:, None, :], g_ret_norm[:, None, :], g_mlstm_norm[:, None, :])
    return w_p, consts


def kernel(x_prompt, x_sample, state_ssd, state_ssd_conv, state_ret, state_mlstm_c, state_mlstm_n,
           state_mlstm_m, c_prompt, c_sample, w_ada, b_ada, g_norm1, g_norm2, w_in, conv_w, conv_b,
           dt_bias, a_log, d_skip, g_ssd_norm, g_ret_norm, b_mlstm_i, b_mlstm_f, g_mlstm_norm, w_out,
           w_router_group, b_router_group, w_router_expert, b_router_expert, w_gate_up, w_down, g_final):
    bp, tp, _ = x_prompt.shape
    bs, ts, _ = x_sample.shape
    n_srows = bs * ts
    n_prompt = bp * tp

    mod4 = _ada(jnp.concatenate([c_sample, c_prompt], axis=0), w_ada, b_ada)
    mod = (mod4, bs)
    sh1, sc1, gt1, sh2, sc2, gt2 = range(6)

    cos_p, sin_p = _rope_tables(jnp.arange(tp, dtype=I32))
    lanes = lambda v: v[..., None]
    half = HEAD_DIM // 2
    ang = (PAST_LEN + jnp.arange(ts, dtype=I32)).astype(F32)[:, None] * (
        ROPE_BASE ** (-jnp.arange(half, dtype=F32) / half))[None, :]
    cos_s, sin_s = lanes(jnp.cos(ang)), lanes(jnp.sin(ang))
    gamma = 1.0 - 2.0 ** (-5.0 - jnp.arange(RET_HEADS, dtype=F32))
    tabs = (lanes(dt_bias[:, :, None]), lanes(a_log[:, :, None]), lanes(d_skip[:, :, None]), lanes(g_ssd_norm),
            lanes(gamma[:, None]), lanes(g_ret_norm), lanes(b_mlstm_i[:, :, None]), lanes(b_mlstm_f[:, :, None]),
            lanes(g_mlstm_norm))
    cwb, cbb = lanes(conv_w), lanes(conv_b)
    ssd_t = jnp.transpose(state_ssd, (0, 2, 3, 4, 1))
    ret_t = jnp.transpose(state_ret, (0, 2, 3, 4, 1))
    c_t = jnp.transpose(state_mlstm_c, (0, 2, 3, 4, 1))
    n_t = jnp.transpose(state_mlstm_n, (0, 2, 3, 1))
    m_t = jnp.transpose(state_mlstm_m, (0, 2, 1))[:, :, None, :]
    conv_t = jnp.transpose(state_ssd_conv, (0, 2, 3, 1))

    w_p, consts = _mixer_consts(w_in, conv_w, conv_b, dt_bias, a_log, d_skip, g_ssd_norm, g_ret_norm,
                                b_mlstm_i, b_mlstm_f, g_mlstm_norm)
    w_o = w_out.astype(BF16)
    zpad = LANES - N_EXPERTS - EXPERT_GROUPS
    wr = jnp.concatenate([w_router_expert, w_router_group, jnp.zeros((DEPTH, D_MODEL, zpad), F32)],
                         axis=2).astype(BF16)
    br = jnp.concatenate([b_router_expert, b_router_group, jnp.zeros((DEPTH, zpad), F32)], axis=1)[:, None, :]
    wgu = w_gate_up.reshape(DEPTH, EXPERT_GROUPS, EXPERTS_PER_GROUP, D_MODEL, 2 * EXPERT_FF)
    wd = w_down.reshape(DEPTH, EXPERT_GROUPS, EXPERTS_PER_GROUP, EXPERT_FF, D_MODEL)
    g1, g2, gf = g_norm1[:, None, :], g_norm2[:, None, :], g_final[None, :]

    x = (x_prompt.reshape(n_prompt, D_MODEL), jnp.transpose(x_sample, (1, 0, 2)).reshape(n_srows, D_MODEL))
    p_states, s_states = [], []
    for l in range(DEPTH):
        final = l == DEPTH - 1
        proj = _inproj(x, mod, sc1, sh1, g1, w_p, l, tp)

        ycat_p, st = _mixer_prompt(proj, bp, tp, cos_p, sin_p, consts, l)
        p_states.append(st)

        proj_t = jnp.transpose(proj[n_prompt:].reshape(ts, bs, P_W), (0, 2, 1))
        y_t, st_t = _mixer_sample_T(proj_t, conv_t, cwb, cbb, ssd_t, ret_t, c_t, n_t, m_t, tabs, cos_s, sin_s, l)
        s_states.append(st_t)
        ycat_s = jnp.transpose(y_t, (0, 2, 1)).reshape(n_srows, D_MODEL).astype(BF16)

        x1, hx, info, counts = _outproj((ycat_p, ycat_s), x, mod, gt1, sc2, sh2, g2, w_o, wr, br, l, tp)
        x = _moe(hx, info, counts, x1, mod, gt2, gf, wgu, wd, l, n_prompt, tp, final)

    y_prompt = x[0].reshape(bp, tp, D_MODEL)
    y_sample = jnp.transpose(x[1].reshape(ts, bs, D_MODEL), (1, 0, 2))
    p_st = [jnp.stack([s[i] for s in p_states], axis=0) for i in range(6)]
    s_t = [jnp.concatenate([s[i] for s in s_states], axis=0) if i in (0, 2, 3)
           else jnp.stack([s[i] for s in s_states], axis=0) for i in range(6)]
    s_st = [jnp.transpose(s_t[0], (0, 4, 1, 2, 3)), jnp.transpose(s_t[1], (0, 3, 1, 2)),
            jnp.transpose(s_t[2], (0, 4, 1, 2, 3)), jnp.transpose(s_t[3], (0, 4, 1, 2, 3)),
            jnp.transpose(s_t[4], (0, 3, 1, 2)), jnp.transpose(s_t[5][:, :, 0, :], (0, 2, 1))]
    return (y_prompt, y_sample, *p_st, *s_st)
```
